```python
import math
import jax, jax.numpy as jnp
from jax import lax
import numpy as np

D_MODEL = 1024
BATCH = 8
SEQ = 4096
DEPTH = 1

D_RNN = 1024
RNN_BLOCKS = 16
RNN_BLOCK = D_RNN // RNN_BLOCKS
CONV_W = 4
RGLRU_C = 8.0
N_HEADS = 8
HEAD_DIM = 64
V_DIM = 2 * HEAD_DIM
ATT_QK = N_HEADS * 2 * HEAD_DIM
ATT_V = N_HEADS * V_DIM
Q_BLOCK = 128
N_BRANCH = 2
D_IN = 2 * D_RNN + 2 * ATT_QK + ATT_V + N_BRANCH * D_MODEL
NUM_BUCKETS = 32
MAX_EXACT = NUM_BUCKETS // 2
MAX_DISTANCE = 128
N_EXPERTS = 32
TOP_K = 4
D_FF = D_MODEL
SWIGLU_LIMIT = 7.0
SWIGLU_ALPHA = 1.702
MOE_BLOCK = 128
EPS = 1e-6
NEG_INF = -1e30

kernel_name = 'hybrid_rglru_diffattn_moe_block'


def rms_norm(x, g):
    xf = x.astype(jnp.float32)
    y = xf * lax.rsqrt(jnp.mean(xf * xf, axis=-1, keepdims=True) + EPS)
    return (y * g.astype(jnp.float32)).astype(x.dtype)


def lambda_init(layer):
    return 0.8 - 0.6 * math.exp(-0.3 * layer)


def rel_bucket(q_pos, k_pos):
    n = jnp.maximum(q_pos[:, None] - k_pos[None, :], 0)
    nf = jnp.maximum(n, MAX_EXACT).astype(jnp.float32)
    large = MAX_EXACT + (jnp.log(nf / MAX_EXACT) / math.log(MAX_DISTANCE / MAX_EXACT) * (NUM_BUCKETS - MAX_EXACT)).astype(jnp.int32)
    large = jnp.minimum(large, NUM_BUCKETS - 1)
    return jnp.where(n < MAX_EXACT, n, large)


def _lru_combine(c1, c2):
    a1, b1 = c1
    a2, b2 = c2
    return a1 * a2, a2 * b1 + b2


def rglru_branch(xr, gate, conv_w, conv_b, w_a, b_a, w_x, b_x, lru_lambda):
    B, S, _ = xr.shape
    xc = lax.conv_general_dilated(xr, conv_w[:, None, :].astype(xr.dtype), window_strides=(1,), padding=[(CONV_W - 1, 0)], dimension_numbers=('NWC', 'WIO', 'NWC'), feature_group_count=D_RNN) + conv_b
    xb = xc.reshape(B, S, RNN_BLOCKS, RNN_BLOCK)
    r = jax.nn.sigmoid(jnp.einsum('bsnc,ncd->bsnd', xb, w_a) + b_a).reshape(B, S, D_RNN)
    i = jax.nn.sigmoid(jnp.einsum('bsnc,ncd->bsnd', xb, w_x) + b_x).reshape(B, S, D_RNN)
    log_a = -RGLRU_C * r.astype(jnp.float32) * jax.nn.softplus(-lru_lambda.astype(jnp.float32))
    a = jnp.exp(log_a)
    mult = jnp.sqrt(-jnp.expm1(2.0 * log_a))
    u = mult * (i * xc).astype(jnp.float32)
    _, h = lax.associative_scan(_lru_combine, (a, u), axis=1)
    return h.astype(xr.dtype) * jax.nn.gelu(gate, approximate=True)


def diff_attention(q, k, v, lam, lam_init, subln_g, rel_table):
    B, S = q.shape[0], q.shape[1]
    qt = q.transpose(3, 0, 2, 1, 4)
    kt = k.transpose(3, 0, 2, 1, 4)
    vt = v.transpose(0, 2, 1, 3)
    k_pos = jnp.arange(S, dtype=jnp.int32)
    scale = HEAD_DIM ** -0.5

    def block(blk):
        start = blk * Q_BLOCK
        qb = lax.dynamic_slice_in_dim(qt, start, Q_BLOCK, axis=3)
        q_pos = start + jnp.arange(Q_BLOCK, dtype=jnp.int32)
        bias = rel_table[rel_bucket(q_pos, k_pos)].transpose(2, 0, 1).astype(jnp.float32)
        mask = k_pos[None, :] <= q_pos[:, None]
        s = jnp.einsum('nbhqd,nbhkd->nbhqk', qb, kt).astype(jnp.float32) * scale + bias
        p = jax.nn.softmax(jnp.where(mask, s, NEG_INF), axis=-1)
        w = p[0] - lam * p[1]
        return jnp.einsum('bhqk,bhkd->bhqd', w.astype(v.dtype), vt)

    o = lax.map(block, jnp.arange(S // Q_BLOCK, dtype=jnp.int32))
    o = o.transpose(1, 0, 3, 2, 4).reshape(B, S, N_HEADS, V_DIM)
    o = rms_norm(o, subln_g) * (1.0 - lam_init)
    return o.reshape(B, S, ATT_V)


def moe_ffn(h, w_router, b_router, w_gate_e, b_gate_e, w_up_e, b_up_e, w_down_e, b_down_e):
    B, S, D = h.shape
    T = B * S
    TK = T * TOP_K
    xt = h.reshape(T, D)
    logits = (xt @ w_router + b_router).astype(jnp.float32)
    top_val, top_idx = lax.top_k(logits, TOP_K)
    weights = jax.nn.softmax(top_val, axis=-1)
    flat_e = top_idx.reshape(-1)
    flat_tok = jnp.arange(TK, dtype=jnp.int32) // TOP_K
    flat_w = weights.reshape(-1)
    order = jnp.argsort(flat_e)
    s_e, s_tok, s_w = flat_e[order], flat_tok[order], flat_w[order]
    counts = jnp.bincount(flat_e, length=N_EXPERTS)
    starts = jnp.cumsum(counts) - counts
    padded = (counts + MOE_BLOCK - 1) // MOE_BLOCK * MOE_BLOCK
    ends_p = jnp.cumsum(padded)
    starts_p = ends_p - padded
    dest = starts_p[s_e] + (jnp.arange(TK, dtype=jnp.int32) - starts[s_e])
    n_blocks = TK // MOE_BLOCK + N_EXPERTS
    P = n_blocks * MOE_BLOCK
    slot_tok = jnp.full((P,), T, dtype=jnp.int32).at[dest].set(s_tok)
    slot_w = jnp.zeros((P,), jnp.float32).at[dest].set(s_w)
    block_expert = jnp.minimum(jnp.searchsorted(ends_p, jnp.arange(n_blocks, dtype=jnp.int32) * MOE_BLOCK, side='right'), N_EXPERTS - 1)
    x_slots = xt.at[slot_tok].get(mode='fill', fill_value=0).reshape(n_blocks, MOE_BLOCK, D)

    def expert_block(args):
        xb, e = args
        g = xb @ w_gate_e[e] + b_gate_e[e]
        u = xb @ w_up_e[e] + b_up_e[e]
        g = jnp.minimum(g, SWIGLU_LIMIT)
        u = jnp.clip(u, -SWIGLU_LIMIT, SWIGLU_LIMIT)
        act = (u + 1.0) * (g * jax.nn.sigmoid(SWIGLU_ALPHA * g))
        return act @ w_down_e[e] + b_down_e[e]

    y_slots = lax.map(expert_block, (x_slots, block_expert)).reshape(P, D)
    y = jnp.zeros((T, D), h.dtype).at[slot_tok].add(slot_w[:, None].astype(h.dtype) * y_slots, mode='drop')
    return y.reshape(B, S, D)


def setup_inputs(seed: int = 0) -> dict:
    key = jax.random.key(seed)
    ks = jax.random.split(key, 32)
    L = DEPTH

    def nrm(k, shape, scale):
        return jax.random.normal(k, shape, jnp.float32) * scale

    a0 = jax.random.uniform(ks[9], (L, D_RNN), jnp.float32, 0.9, 0.999)
    return {
        'x': nrm(ks[0], (BATCH, SEQ, D_MODEL), 1.0),
        'norm_mix_g': 1.0 + nrm(ks[1], (L, D_MODEL), 0.05),
        'w_in': nrm(ks[2], (L, D_MODEL, D_IN), D_MODEL ** -0.5),
        'conv_w': nrm(ks[3], (L, CONV_W, D_RNN), CONV_W ** -0.5),
        'conv_b': nrm(ks[4], (L, D_RNN), 0.02),
        'w_rg_a': nrm(ks[5], (L, RNN_BLOCKS, RNN_BLOCK, RNN_BLOCK), RNN_BLOCK ** -0.5),
        'b_rg_a': nrm(ks[6], (L, RNN_BLOCKS, RNN_BLOCK), 0.02),
        'w_rg_x': nrm(ks[7], (L, RNN_BLOCKS, RNN_BLOCK, RNN_BLOCK), RNN_BLOCK ** -0.5),
        'b_rg_x': nrm(ks[8], (L, RNN_BLOCKS, RNN_BLOCK), 0.02),
        'lru_lambda': jnp.log(a0) - jnp.log1p(-a0),
        'diff_lambda_q1': nrm(ks[10], (L, HEAD_DIM), 0.1),
        'diff_lambda_k1': nrm(ks[11], (L, HEAD_DIM), 0.1),
        'diff_lambda_q2': nrm(ks[12], (L, HEAD_DIM), 0.1),
        'diff_lambda_k2': nrm(ks[13], (L, HEAD_DIM), 0.1),
        'subln_g': 1.0 + nrm(ks[14], (L, V_DIM), 0.05),
        'rel_bias_table': nrm(ks[15], (NUM_BUCKETS, N_HEADS), 0.3),
        'w_proj_rnn': nrm(ks[16], (L, D_RNN, D_MODEL), D_RNN ** -0.5),
        'w_proj_att': nrm(ks[17], (L, ATT_V, D_MODEL), ATT_V ** -0.5),
        'w_out': nrm(ks[18], (L, D_MODEL, D_MODEL), D_MODEL ** -0.5),
        'norm_ffn_g': 1.0 + nrm(ks[19], (L, D_MODEL), 0.05),
        'w_router': nrm(ks[20], (L, D_MODEL, N_EXPERTS), D_MODEL ** -0.5),
        'b_router': nrm(ks[21], (L, N_EXPERTS), 0.01),
        'w_gate_e': nrm(ks[22], (L, N_EXPERTS, D_MODEL, D_FF), D_MODEL ** -0.5),
        'b_gate_e': nrm(ks[23], (L, N_EXPERTS, D_FF), 0.02),
        'w_up_e': nrm(ks[24], (L, N_EXPERTS, D_MODEL, D_FF), D_MODEL ** -0.5),
        'b_up_e': nrm(ks[25], (L, N_EXPERTS, D_FF), 0.02),
        'w_down_e': nrm(ks[26], (L, N_EXPERTS, D_FF, D_MODEL), D_FF ** -0.5),
        'b_down_e': nrm(ks[27], (L, N_EXPERTS, D_MODEL), 0.02),
        'norm_final_g': 1.0 + nrm(ks[28], (D_MODEL,), 0.05),
    }


def reference(x, norm_mix_g, w_in, conv_w, conv_b, w_rg_a, b_rg_a, w_rg_x, b_rg_x, lru_lambda, diff_lambda_q1, diff_lambda_k1, diff_lambda_q2, diff_lambda_k2, subln_g, rel_bias_table, w_proj_rnn, w_proj_att, w_out, norm_ffn_g, w_router, b_router, w_gate_e, b_gate_e, w_up_e, b_up_e, w_down_e, b_down_e, norm_final_g):
    B, S, D = x.shape
    o0 = 0
    o1 = o0 + D_RNN
    o2 = o1 + D_RNN
    o3 = o2 + ATT_QK
    o4 = o3 + ATT_QK
    o5 = o4 + ATT_V
    for l in range(DEPTH):
        h = rms_norm(x, norm_mix_g[l])
        proj = h @ w_in[l]
        xr, gr = proj[..., o0:o1], proj[..., o1:o2]
        q = proj[..., o2:o3].reshape(B, S, N_HEADS, 2, HEAD_DIM)
        k = proj[..., o3:o4].reshape(B, S, N_HEADS, 2, HEAD_DIM)
        v = proj[..., o4:o5].reshape(B, S, N_HEADS, V_DIM)
        gates = jax.nn.sigmoid(proj[..., o5:].reshape(B, S, N_BRANCH, D_MODEL))

        y_rnn = rglru_branch(xr, gr, conv_w[l], conv_b[l], w_rg_a[l], b_rg_a[l], w_rg_x[l], b_rg_x[l], lru_lambda[l])
        lam_init = lambda_init(l)
        lam = (jnp.exp(jnp.sum(diff_lambda_q1[l].astype(jnp.float32) * diff_lambda_k1[l].astype(jnp.float32)))
               - jnp.exp(jnp.sum(diff_lambda_q2[l].astype(jnp.float32) * diff_lambda_k2[l].astype(jnp.float32))) + lam_init)
        y_att = diff_attention(q, k, v, lam, lam_init, subln_g[l], rel_bias_table)

        merged = gates[..., 0, :] * (y_rnn @ w_proj_rnn[l]) + gates[..., 1, :] * (y_att @ w_proj_att[l])
        x = x + merged @ w_out[l]
        hf = rms_norm(x, norm_ffn_g[l])
        x = x + moe_ffn(hf, w_router[l], b_router[l], w_gate_e[l], b_gate_e[l], w_up_e[l], b_up_e[l], w_down_e[l], b_down_e[l])
    return rms_norm(x, norm_final_g)
```

```python
import functools
import math

import jax
import jax.numpy as jnp
from jax import lax
from jax.experimental import pallas as pl
from jax.experimental.pallas import tpu as pltpu

F32 = jnp.float32
BF16 = jnp.bfloat16
I32 = jnp.int32

D_MODEL = 1024
D_RNN = 1024
RNN_BLOCKS = 16
RNN_BLOCK = D_RNN // RNN_BLOCKS
CONV_W = 4
RGLRU_C = 8.0
N_HEADS = 8
HEAD_DIM = 64
V_DIM = 2 * HEAD_DIM
ATT_QK = N_HEADS * 2 * HEAD_DIM
ATT_V = N_HEADS * V_DIM
D_IN = 2 * D_RNN + 2 * ATT_QK + ATT_V + 2 * D_MODEL
NUM_BUCKETS = 32
MAX_EXACT = NUM_BUCKETS // 2
MAX_DISTANCE = 128
N_EXPERTS = 32
TOP_K = 4
D_FF = D_MODEL
SWIGLU_LIMIT = 7.0
SWIGLU_ALPHA = 1.702
EPS = 1e-6
NEG_INF = -1e30

LANES = 128
SUBLANES = 8
MXU_DIM = 256
VMEM_LIMIT = 56 * 1024 * 1024

COL_XR, COL_GR, COL_Q, COL_K, COL_V, COL_G0, COL_G1 = range(7)


def _params(sem, vmem=VMEM_LIMIT):
    return pltpu.CompilerParams(dimension_semantics=sem, vmem_limit_bytes=vmem)


def _inproj_kernel(x_ref, g_ref, w_ref, o_ref, h_scr):
    @pl.when(pl.program_id(1) == 0)
    def _():
        x = x_ref[...]
        inv = lax.rsqrt(jnp.mean(x * x, axis=-1, keepdims=True) + EPS)
        h_scr[...] = (x * inv * g_ref[...]).astype(BF16)

    o_ref[...] = jnp.dot(h_scr[...], w_ref[...], preferred_element_type=F32).astype(o_ref.dtype)


def _inproj(x2d, g, w_bf16):
    T, D = x2d.shape
    N = w_bf16.shape[1]
    tm = min(1024, T)
    tn = 1024
    return pl.pallas_call(
        _inproj_kernel,
        grid=(T // tm, N // tn),
        in_specs=[
            pl.BlockSpec((tm, D), lambda i, j: (i, 0)),
            pl.BlockSpec((1, D), lambda i, j: (0, 0)),
            pl.BlockSpec((D, tn), lambda i, j: (0, j)),
        ],
        out_specs=pl.BlockSpec((tm, tn), lambda i, j: (i, j)),
        out_shape=jax.ShapeDtypeStruct((T, N), BF16),
        scratch_shapes=[pltpu.VMEM((tm, D), BF16)],
        compiler_params=_params(("parallel", "arbitrary")),
        name="inproj",
    )(x2d, g.reshape(1, D), w_bf16)


def _rglru_kernel(xr_ref, gr_ref, cw_ref, cb_ref, wa_ref, ba_ref, wx_ref, bx_ref, lam_ref,
                  o_ref, xbuf, a_scr, u_scr, h_scr):
    ts = xr_ref.shape[0]
    pad = SUBLANES

    @pl.when(pl.program_id(1) == 0)
    def _():
        xbuf[0:pad, :] = jnp.zeros((pad, D_RNN), F32)
        h_scr[...] = jnp.zeros_like(h_scr)

    x = xr_ref[...].astype(F32)
    xbuf[pad:pad + ts, :] = x
    cw = cw_ref[...]
    xc = (cw[3:4, :] * x
          + cw[2:3, :] * xbuf[pad - 1:pad - 1 + ts, :]
          + cw[1:2, :] * xbuf[pad - 2:pad - 2 + ts, :]
          + cw[0:1, :] * xbuf[pad - 3:pad - 3 + ts, :]) + cb_ref[...]
    xbuf[0:pad, :] = x[ts - pad:ts, :]

    xcb = xc.astype(BF16)
    nchunk = D_RNN // MXU_DIM
    r_pre = jnp.concatenate(
        [jnp.dot(xcb[:, c * MXU_DIM:(c + 1) * MXU_DIM], wa_ref[c], preferred_element_type=F32)
         for c in range(nchunk)], axis=-1)
    i_pre = jnp.concatenate(
        [jnp.dot(xcb[:, c * MXU_DIM:(c + 1) * MXU_DIM], wx_ref[c], preferred_element_type=F32)
         for c in range(nchunk)], axis=-1)
    r = jax.nn.sigmoid(r_pre + ba_ref[...])
    ig = jax.nn.sigmoid(i_pre + bx_ref[...])
    z = -lam_ref[...]
    softplus = jnp.maximum(z, 0.0) + jnp.log1p(jnp.exp(-jnp.abs(z)))
    log_a = (-RGLRU_C) * r * softplus
    a = jnp.exp(log_a)
    th = jnp.tanh(log_a)
    mult = jnp.sqrt(-2.0 * th / (1.0 - th))
    a_scr[...] = a
    u_scr[...] = mult * (ig * xc)

    row = lax.broadcasted_iota(I32, (SUBLANES, D_RNN), 0)

    def body(g, h):
        off = pl.multiple_of(g * SUBLANES, SUBLANES)
        a8 = a_scr[pl.ds(off, SUBLANES), :]
        u8 = u_scr[pl.ds(off, SUBLANES), :]
        for d in (1, 2, 4):
            a_sh = jnp.where(row >= d, pltpu.roll(a8, d, 0), 1.0)
            u_sh = jnp.where(row >= d, pltpu.roll(u8, d, 0), 0.0)
            u8 = a8 * u_sh + u8
            a8 = a8 * a_sh
        h8 = u8 + a8 * h
        u_scr[pl.ds(off, SUBLANES), :] = h8
        return h8[SUBLANES - 1:SUBLANES, :]

    h_scr[...] = lax.fori_loop(0, ts // SUBLANES, body, h_scr[...])
    gate = jax.nn.gelu(gr_ref[...].astype(F32), approximate=True)
    o_ref[...] = (u_scr[...] * gate).astype(o_ref.dtype)


def _block_diag_chunks(w):
    per = MXU_DIM // RNN_BLOCK
    w = w.reshape(D_RNN // MXU_DIM, per, RNN_BLOCK, RNN_BLOCK)
    eye = jnp.eye(per, dtype=w.dtype)
    out = jnp.einsum('gpcd,pq->gpcqd', w, eye)
    return out.reshape(D_RNN // MXU_DIM, MXU_DIM, MXU_DIM)


def _rglru(proj, B, S, conv_w, conv_b, w_a, b_a, w_x, b_x, lru_lambda):
    T = B * S
    ts = min(256, S)
    ns = S // ts
    wa = _block_diag_chunks(w_a).astype(BF16)
    wx = _block_diag_chunks(w_x).astype(BF16)
    nchunk = D_RNN // MXU_DIM
    row = lambda v: v.reshape(1, D_RNN)
    const2 = lambda b, s: (0, 0)
    return pl.pallas_call(
        _rglru_kernel,
        grid=(B, ns),
        in_specs=[
            pl.BlockSpec((ts, D_RNN), lambda b, s: (b * ns + s, COL_XR)),
            pl.BlockSpec((ts, D_RNN), lambda b, s: (b * ns + s, COL_GR)),
            pl.BlockSpec((CONV_W, D_RNN), const2),
            pl.BlockSpec((1, D_RNN), const2),
            pl.BlockSpec((nchunk, MXU_DIM, MXU_DIM), lambda b, s: (0, 0, 0)),
            pl.BlockSpec((1, D_RNN), const2),
            pl.BlockSpec((nchunk, MXU_DIM, MXU_DIM), lambda b, s: (0, 0, 0)),
            pl.BlockSpec((1, D_RNN), const2),
            pl.BlockSpec((1, D_RNN), const2),
        ],
        out_specs=pl.BlockSpec((ts, D_RNN), lambda b, s: (b * ns + s, 0)),
        out_shape=jax.ShapeDtypeStruct((T, D_RNN), BF16),
        scratch_shapes=[
            pltpu.VMEM((ts + SUBLANES, D_RNN), F32),
            pltpu.VMEM((ts, D_RNN), F32),
            pltpu.VMEM((ts, D_RNN), F32),
            pltpu.VMEM((1, D_RNN), F32),
        ],
        compiler_params=_params(("parallel", "arbitrary")),
        name="rglru",
    )(proj, proj, conv_w, row(conv_b), wa, row(b_a), wx, row(b_x), row(lru_lambda))


def _attn_kernel(q_ref, k_ref, v_ref, bias_ref, lq1_ref, lk1_ref, lq2_ref, lk2_ref, sg_ref,
                 o_ref, qs_scr, m_scr, l_scr, acc_scr, *, tq, lam_init):
    S = q_ref.shape[0]
    nq = S // tq
    scale = HEAD_DIM ** -0.5
    lane = lax.broadcasted_iota(I32, (tq, V_DIM), 1)
    lam = (jnp.exp(jnp.sum(lq1_ref[...] * lk1_ref[...], keepdims=True))
           - jnp.exp(jnp.sum(lq2_ref[...] * lk2_ref[...], keepdims=True)) + lam_init)

    def chunk(k0, bias):
        k = k_ref[pl.ds(k0, tq), :]
        v = v_ref[pl.ds(k0, tq), :]
        s = lax.dot_general(qs_scr[...], k, (((1,), (1,)), ((), ())), preferred_element_type=F32)
        if bias is not None:
            s = s + bias
        m_prev = m_scr[...]
        m_new = jnp.maximum(m_prev, jnp.max(s, axis=-1, keepdims=True))
        alpha = jnp.exp(m_prev - m_new)
        p = jnp.exp(s - m_new)
        l_scr[...] = alpha * l_scr[...] + jnp.sum(p, axis=-1, keepdims=True)
        acc_scr[...] = alpha * acc_scr[...] + jnp.dot(p.astype(BF16), v, preferred_element_type=F32)
        m_scr[...] = m_new

    def qblock(qi, carry):
        q0 = pl.multiple_of(qi * tq, tq)
        q = q_ref[pl.ds(q0, tq), :] * jnp.asarray(scale, BF16)
        zero = jnp.zeros_like(q)
        qs_scr[0:tq, :] = jnp.where(lane < HEAD_DIM, q, zero)
        qs_scr[tq:2 * tq, :] = jnp.where(lane >= HEAD_DIM, q, zero)
        m_scr[...] = jnp.full(m_scr.shape, NEG_INF, F32)
        l_scr[...] = jnp.zeros(l_scr.shape, F32)
        acc_scr[...] = jnp.zeros(acc_scr.shape, F32)

        def far(kj, c):
            chunk(pl.multiple_of(kj * tq, tq), None)
            return c

        lax.fori_loop(0, qi - 1, far, 0)

        @pl.when(qi >= 1)
        def _():
            chunk(pl.multiple_of((qi - 1) * tq, tq), bias_ref[1])

        chunk(q0, bias_ref[0])

        o_all = acc_scr[...] / l_scr[...]
        o = o_all[0:tq, :] - lam * o_all[tq:2 * tq, :]
        inv = lax.rsqrt(jnp.mean(o * o, axis=-1, keepdims=True) + EPS)
        y = (o * inv * sg_ref[...]) * (1.0 - lam_init)
        o_ref[pl.ds(q0, tq), :] = y.astype(o_ref.dtype)
        return carry

    lax.fori_loop(0, nq, qblock, 0)


def _rel_bucket(n):
    n = jnp.maximum(n, 0)
    nf = jnp.maximum(n, MAX_EXACT).astype(F32)
    large = MAX_EXACT + (jnp.log(nf / MAX_EXACT) / math.log(MAX_DISTANCE / MAX_EXACT)
                         * (NUM_BUCKETS - MAX_EXACT)).astype(I32)
    large = jnp.minimum(large, NUM_BUCKETS - 1)
    return jnp.where(n < MAX_EXACT, n, large)


def _bias_tiles(rel_table, tq):
    i = jnp.arange(tq, dtype=I32)[:, None]
    j = jnp.arange(tq, dtype=I32)[None, :]
    far_bias = rel_table[NUM_BUCKETS - 1].astype(F32)
    tiles = []
    for delta in (0, tq):
        n = i - j + delta
        b = rel_table[_rel_bucket(n)].astype(F32).transpose(2, 0, 1) - far_bias[:, None, None]
        b = jnp.where((n >= 0)[None], b, NEG_INF)
        tiles.append(jnp.concatenate([b, b], axis=1))
    return jnp.stack(tiles, axis=1)


def _attention(proj, B, S, lq1, lk1, lq2, lk2, subln_g, rel_table, lam_init):
    assert MAX_DISTANCE <= min(256, S)
    T = B * S
    tq = min(256, S)
    bias = _bias_tiles(rel_table, tq)
    vec = lambda v: v.reshape(1, -1).astype(F32)
    const2 = lambda b, h: (0, 0)
    kern = functools.partial(_attn_kernel, tq=tq, lam_init=lam_init)
    return pl.pallas_call(
        kern,
        grid=(B, N_HEADS),
        in_specs=[
            pl.BlockSpec((S, V_DIM), lambda b, h: (b, COL_Q * N_HEADS + h)),
            pl.BlockSpec((S, V_DIM), lambda b, h: (b, COL_K * N_HEADS + h)),
            pl.BlockSpec((S, V_DIM), lambda b, h: (b, COL_V * N_HEADS + h)),
            pl.BlockSpec((None, 2, 2 * tq, tq), lambda b, h: (h, 0, 0, 0)),
            pl.BlockSpec((1, HEAD_DIM), const2),
            pl.BlockSpec((1, HEAD_DIM), const2),
            pl.BlockSpec((1, HEAD_DIM), const2),
            pl.BlockSpec((1, HEAD_DIM), const2),
            pl.BlockSpec((1, V_DIM), const2),
        ],
        out_specs=pl.BlockSpec((S, V_DIM), lambda b, h: (b, h)),
        out_shape=jax.ShapeDtypeStruct((T, ATT_V), BF16),
        scratch_shapes=[
            pltpu.VMEM((2 * tq, V_DIM), BF16),
            pltpu.VMEM((2 * tq, 1), F32),
            pltpu.VMEM((2 * tq, 1), F32),
            pltpu.VMEM((2 * tq, V_DIM), F32),
        ],
        compiler_params=_params(("parallel", "arbitrary")),
        name="diff_attn",
    )(proj, proj, proj, bias, vec(lq1), vec(lk1), vec(lq2), vec(lk2), vec(subln_g))


def _merge_kernel(x_ref, yr_ref, ya_ref, g0_ref, g1_ref, wr_ref, wa_ref, wo_ref, gf_ref, wrt_ref, brt_ref,
                  x2_ref, hf_ref, idx_ref, wgt_ref):
    pr = jnp.dot(yr_ref[...], wr_ref[...], preferred_element_type=F32)
    pa = jnp.dot(ya_ref[...], wa_ref[...], preferred_element_type=F32)
    merged = (jax.nn.sigmoid(g0_ref[...].astype(F32)) * pr
              + jax.nn.sigmoid(g1_ref[...].astype(F32)) * pa)
    x2 = x_ref[...] + jnp.dot(merged.astype(BF16), wo_ref[...], preferred_element_type=F32)
    x2_ref[...] = x2
    inv = lax.rsqrt(jnp.mean(x2 * x2, axis=-1, keepdims=True) + EPS)
    hf = x2 * inv * gf_ref[...]
    hf_ref[...] = hf
    logits = jnp.dot(hf, wrt_ref[...], preferred_element_type=F32,
                     precision=lax.Precision.HIGHEST) + brt_ref[...]
    tm = logits.shape[0]
    lane = lax.broadcasted_iota(I32, (tm, LANES), 1)
    work = jnp.where(lane < N_EXPERTS, logits, -jnp.inf)
    idx_out = jnp.zeros((tm, LANES), I32)
    val_out = jnp.full((tm, LANES), -jnp.inf, F32)
    for k in range(TOP_K):
        mx = jnp.max(work, axis=-1, keepdims=True)
        sel = jnp.min(jnp.where(work == mx, lane, LANES), axis=-1, keepdims=True)
        idx_out = jnp.where(lane == k, sel, idx_out)
        val_out = jnp.where(lane == k, mx, val_out)
        work = jnp.where(lane == sel, -jnp.inf, work)
    e = jnp.exp(val_out - jnp.max(val_out, axis=-1, keepdims=True))
    idx_ref[...] = idx_out
    wgt_ref[...] = e / jnp.sum(e, axis=-1, keepdims=True)


def _merge_router(x2d, y_rnn, y_att, proj, w_pr, w_pa, w_o, g_ffn, w_router, b_router):
    T, D = x2d.shape
    tm = min(512, T)
    wrt = jnp.zeros((D, LANES), F32).at[:, :N_EXPERTS].set(w_router)
    brt = jnp.zeros((1, LANES), F32).at[0, :N_EXPERTS].set(b_router)
    rowblk = lambda c: pl.BlockSpec((tm, D), lambda i, c=c: (i, c))
    full = lambda a: pl.BlockSpec(a.shape, lambda i: (0,) * a.ndim)
    wr, wa, wo = w_pr.astype(BF16), w_pa.astype(BF16), w_o.astype(BF16)
    gf = g_ffn.reshape(1, D)
    return pl.pallas_call(
        _merge_kernel,
        grid=(T // tm,),
        in_specs=[rowblk(0), rowblk(0), rowblk(0), rowblk(COL_G0), rowblk(COL_G1),
                  full(wr), full(wa), full(wo), full(gf), full(wrt), full(brt)],
        out_specs=[rowblk(0), rowblk(0),
                   pl.BlockSpec((tm, LANES), lambda i: (i, 0)),
                   pl.BlockSpec((tm, LANES), lambda i: (i, 0))],
        out_shape=[jax.ShapeDtypeStruct((T, D), F32), jax.ShapeDtypeStruct((T, D), F32),
                   jax.ShapeDtypeStruct((T, LANES), I32), jax.ShapeDtypeStruct((T, LANES), F32)],
        compiler_params=_params(("parallel",)),
        name="merge_router",
    )(x2d, y_rnn, y_att, proj, proj, wr, wa, wo, gf, wrt, brt)


def _route_kernel(idx_ref, dest_ref, cnt_ref, cnt_scr, run_scr, start_scr, *, blk):
    ph = pl.program_id(0)
    i = pl.program_id(1)
    tt = idx_ref.shape[0]
    idx = idx_ref[...]
    lane = lax.broadcasted_iota(I32, (tt, LANES), 1)
    onehot = jnp.zeros((tt, LANES), F32)
    for k in range(TOP_K):
        onehot = onehot + (idx[:, k:k + 1] == lane).astype(F32)

    @pl.when((ph == 0) & (i == 0))
    def _():
        cnt_scr[...] = jnp.zeros_like(cnt_scr)

    @pl.when(ph == 0)
    def _():
        cnt_scr[...] += jnp.sum(onehot, axis=0, keepdims=True)

    @pl.when((ph == 1) & (i == 0))
    def _():
        cnt = jnp.broadcast_to(cnt_scr[...], (SUBLANES, LANES))
        padded = jnp.floor((cnt + (blk - 1)) / blk) * blk
        l8 = lax.broadcasted_iota(I32, (SUBLANES, LANES), 1)
        incl = padded
        d = 1
        while d < LANES:
            incl = incl + jnp.where(l8 >= d, pltpu.roll(incl, d, 1), 0.0)
            d *= 2
        start_scr[...] = (incl - padded)[0:1, :]
        run_scr[...] = jnp.zeros_like(run_scr)

    @pl.when(ph == 1)
    def _():
        r = lax.broadcasted_iota(I32, (tt, tt), 0)
        c = lax.broadcasted_iota(I32, (tt, tt), 1)
        tri = (c < r).astype(BF16)
        before = jnp.dot(tri, onehot.astype(BF16), preferred_element_type=F32)
        base = before + run_scr[...] + start_scr[...]
        out = jnp.zeros((tt, LANES), F32)
        for k in range(TOP_K):
            dk = jnp.sum(jnp.where(idx[:, k:k + 1] == lane, base, 0.0), axis=-1, keepdims=True)
            out = jnp.where(lane == k, dk, out)
        dest_ref[...] = out.astype(I32)
        run_scr[...] += jnp.sum(onehot, axis=0, keepdims=True)
        cnt_ref[...] = jnp.broadcast_to(cnt_scr[...], cnt_ref.shape)


def _route(top_idx, blk):
    T = top_idx.shape[0]
    tt = min(512, T)
    kern = functools.partial(_route_kernel, blk=blk)
    return pl.pallas_call(
        kern,
        grid=(2, T // tt),
        in_specs=[pl.BlockSpec((tt, LANES), lambda p, i: (i, 0))],
        out_specs=[pl.BlockSpec((tt, LANES), lambda p, i: (p * i, 0)),
                   pl.BlockSpec((SUBLANES, LANES), lambda p, i: (0, 0))],
        out_shape=[jax.ShapeDtypeStruct((T, LANES), I32),
                   jax.ShapeDtypeStruct((SUBLANES, LANES), F32)],
        scratch_shapes=[pltpu.VMEM((1, LANES), F32), pltpu.VMEM((1, LANES), F32),
                        pltpu.VMEM((1, LANES), F32)],
        compiler_params=_params(("arbitrary", "arbitrary")),
        name="route",
    )(top_idx)


def _dispatch_kernel(dest_ref, hf_ref, xs_in_ref, xs_ref, sem, *, tt):
    del xs_in_ref
    base = pl.program_id(0) * tt
    n = tt * TOP_K

    def row_copy(j):
        return pltpu.make_async_copy(hf_ref.at[pl.ds(base + j // TOP_K, 1), :],
                                     xs_ref.at[pl.ds(dest_ref[j], 1), :], sem)

    def start(j, c):
        row_copy(j).start()
        return c

    def wait(j, c):
        row_copy(j).wait()
        return c

    lax.fori_loop(0, n, start, 0)
    lax.fori_loop(0, n, wait, 0)


def _dispatch(dest_flat, hf, n_slots):
    T, D = hf.shape
    tt = min(512, T)
    kern = functools.partial(_dispatch_kernel, tt=tt)
    zeros = jnp.zeros((n_slots, D), F32)
    return pl.pallas_call(
        kern,
        grid=(T // tt,),
        in_specs=[pl.BlockSpec((tt * TOP_K,), lambda i: (i,), memory_space=pltpu.SMEM),
                  pl.BlockSpec(memory_space=pl.ANY),
                  pl.BlockSpec(memory_space=pl.ANY)],
        out_specs=pl.BlockSpec(memory_space=pl.ANY),
        out_shape=jax.ShapeDtypeStruct((n_slots, D), F32),
        scratch_shapes=[pltpu.SemaphoreType.DMA(())],
        input_output_aliases={2: 0},
        compiler_params=_params(("arbitrary",)),
        name="dispatch",
    )(dest_flat, hf, zeros)


def _expert_kernel(be_ref, bsrc_ref, act_ref, x_ref, wg_ref, bg_ref, wu_ref, bu_ref, wd_ref, bd_ref, o_ref):
    del be_ref, bsrc_ref

    @pl.when(act_ref[pl.program_id(0)] == 1)
    def _():
        x = x_ref[...].astype(BF16)
        g = jnp.dot(x, wg_ref[0], preferred_element_type=F32) + bg_ref[0]
        u = jnp.dot(x, wu_ref[0], preferred_element_type=F32) + bu_ref[0]
        g = jnp.minimum(g, SWIGLU_LIMIT)
        u = jnp.clip(u, -SWIGLU_LIMIT, SWIGLU_LIMIT)
        act = (u + 1.0) * (g * jax.nn.sigmoid(SWIGLU_ALPHA * g))
        o_ref[...] = jnp.dot(act.astype(BF16), wd_ref[0], preferred_element_type=F32) + bd_ref[0]


def _experts(x_slots, blk, block_expert, block_src, block_active, wg, bg, wu, bu, wd, bd):
    P, D = x_slots.shape
    nb = P // blk
    wspec = pl.BlockSpec((1, D, D_FF), lambda i, be, bs, ac: (be[i], 0, 0))
    bspec = pl.BlockSpec((1, 1, D_FF), lambda i, be, bs, ac: (be[i], 0, 0))
    xspec = pl.BlockSpec((blk, D), lambda i, be, bs, ac: (bs[i], 0))
    grid_spec = pltpu.PrefetchScalarGridSpec(
        num_scalar_prefetch=3,
        grid=(nb,),
        in_specs=[xspec, wspec, bspec, wspec, bspec, wspec, bspec],
        out_specs=xspec,
    )
    b3 = lambda b: b.reshape(N_EXPERTS, 1, -1)
    return pl.pallas_call(
        _expert_kernel,
        grid_spec=grid_spec,
        out_shape=jax.ShapeDtypeStruct((P, D), F32),
        compiler_params=_params(("arbitrary",)),
        name="experts",
    )(block_expert, block_src, block_active, x_slots,
      wg.astype(BF16), b3(bg), wu.astype(BF16), b3(bu), wd.astype(BF16), b3(bd))


def _combine_kernel(dest_ref, ys_ref, w_ref, x2_ref, g_ref, o_ref, buf, sem, *, tt):
    n = tt * TOP_K

    def row_copy(j):
        k = j % TOP_K
        t = j // TOP_K
        return pltpu.make_async_copy(ys_ref.at[pl.ds(dest_ref[j], 1), :],
                                     buf.at[pl.ds(k * tt + t, 1), :], sem)

    def start(j, c):
        row_copy(j).start()
        return c

    def wait(j, c):
        row_copy(j).wait()
        return c

    lax.fori_loop(0, n, start, 0)
    lax.fori_loop(0, n, wait, 0)
    w = w_ref[...]
    y = x2_ref[...]
    for k in range(TOP_K):
        y = y + w[:, k:k + 1] * buf[k * tt:(k + 1) * tt, :]
    inv = lax.rsqrt(jnp.mean(y * y, axis=-1, keepdims=True) + EPS)
    o_ref[...] = y * inv * g_ref[...]


def _combine(dest_flat, y_slots, top_w, x2, g_final):
    T, D = x2.shape
    tt = min(256, T)
    kern = functools.partial(_combine_kernel, tt=tt)
    return pl.pallas_call(
        kern,
        grid=(T // tt,),
        in_specs=[pl.BlockSpec((tt * TOP_K,), lambda i: (i,), memory_space=pltpu.SMEM),
                  pl.BlockSpec(memory_space=pl.ANY),
                  pl.BlockSpec((tt, LANES), lambda i: (i, 0)),
                  pl.BlockSpec((tt, D), lambda i: (i, 0)),
                  pl.BlockSpec((1, D), lambda i: (0, 0))],
        out_specs=pl.BlockSpec((tt, D), lambda i: (i, 0)),
        out_shape=jax.ShapeDtypeStruct((T, D), F32),
        scratch_shapes=[pltpu.VMEM((tt * TOP_K, D), F32), pltpu.SemaphoreType.DMA(())],
        compiler_params=_params(("arbitrary",)),
        name="combine",
    )(dest_flat, y_slots, top_w, x2, g_final.reshape(1, D))


def _moe_block_size(T):
    return min(512, max(SUBLANES, T * TOP_K // N_EXPERTS))


def _block_tables(counts, blk, nb):
    cnt = counts.astype(I32)
    padded = (cnt + blk - 1) // blk * blk
    ends = jnp.cumsum(padded)
    used = ends[-1] // blk
    starts = jnp.arange(nb, dtype=I32) * blk
    src = jnp.minimum(jnp.arange(nb, dtype=I32), jnp.maximum(used - 1, 0))
    expert = jnp.minimum(jnp.sum((starts[:, None] >= ends[None, :]).astype(I32), axis=1), N_EXPERTS - 1)
    expert = expert[src]
    active = (jnp.arange(nb, dtype=I32) < used).astype(I32)
    return expert, src, active


def kernel(x, norm_mix_g, w_in, conv_w, conv_b, w_rg_a, b_rg_a, w_rg_x, b_rg_x, lru_lambda, diff_lambda_q1, diff_lambda_k1, diff_lambda_q2, diff_lambda_k2, subln_g, rel_bias_table, w_proj_rnn, w_proj_att, w_out, norm_ffn_g, w_router, b_router, w_gate_e, b_gate_e, w_up_e, b_up_e, w_down_e, b_down_e, norm_final_g):
    B, S, D = x.shape
    T = B * S
    assert norm_mix_g.shape[0] == 1, "single-layer block: the final norm is fused into the MoE combine"
    l = 0
    xt = x.reshape(T, D)
    lam_init = 0.8 - 0.6 * math.exp(-0.3 * l)
    proj = _inproj(xt, norm_mix_g[l], w_in[l].astype(BF16))
    y_rnn = _rglru(proj, B, S, conv_w[l], conv_b[l], w_rg_a[l], b_rg_a[l], w_rg_x[l], b_rg_x[l],
                   lru_lambda[l])
    y_att = _attention(proj, B, S, diff_lambda_q1[l], diff_lambda_k1[l], diff_lambda_q2[l],
                       diff_lambda_k2[l], subln_g[l], rel_bias_table, lam_init)
    x2, hf, top_idx, top_w = _merge_router(xt, y_rnn, y_att, proj, w_proj_rnn[l], w_proj_att[l],
                                           w_out[l], norm_ffn_g[l], w_router[l], b_router[l])
    blk = _moe_block_size(T)
    nb = T * TOP_K // blk + N_EXPERTS
    dest, counts = _route(top_idx, blk)
    dest_flat = dest[:, :TOP_K].reshape(-1)
    expert, src, active = _block_tables(counts[0, :N_EXPERTS], blk, nb)
    x_slots = _dispatch(dest_flat, hf, nb * blk)
    y_slots = _experts(x_slots, blk, expert, src, active, w_gate_e[l], b_gate_e[l], w_up_e[l],
                       b_up_e[l], w_down_e[l], b_down_e[l])
    out = _combine(dest_flat, y_slots, top_w, x2, norm_final_g)
    return out.reshape(B, S, D)
```

```python
import functools
import math

import jax
import jax.numpy as jnp
from jax import lax
from jax.experimental import pallas as pl
from jax.experimental.pallas import tpu as pltpu

F32 = jnp.float32
BF16 = jnp.bfloat16
I32 = jnp.int32

D_MODEL = 1024
D_RNN = 1024
RNN_BLOCKS = 16
RNN_BLOCK = D_RNN // RNN_BLOCKS
CONV_W = 4
RGLRU_C = 8.0
N_HEADS = 8
HEAD_DIM = 64
V_DIM = 2 * HEAD_DIM
ATT_QK = N_HEADS * 2 * HEAD_DIM
ATT_V = N_HEADS * V_DIM
D_IN = 2 * D_RNN + 2 * ATT_QK + ATT_V + 2 * D_MODEL
NUM_BUCKETS = 32
MAX_EXACT = NUM_BUCKETS // 2
MAX_DISTANCE = 128
N_EXPERTS = 32
TOP_K = 4
D_FF = D_MODEL
SWIGLU_LIMIT = 7.0
SWIGLU_ALPHA = 1.702
EPS = 1e-6
NEG_INF = -1e30

LANES = 128
SUBLANES = 8
MXU_DIM = 256
VMEM_LIMIT = 56 * 1024 * 1024

COL_XR, COL_GR, COL_Q, COL_K, COL_V, COL_G0, COL_G1 = range(7)


def _params(sem, vmem=VMEM_LIMIT):
    return pltpu.CompilerParams(dimension_semantics=sem, vmem_limit_bytes=vmem)


ROW_TILES = D_MODEL // LANES
assert ROW_TILES == SUBLANES


def _store_token_major(ref, val):
    n = val.shape[0]
    for c in range(ROW_TILES):
        ref[pl.ds(c, n, stride=ROW_TILES), :] = val[:, c * LANES:(c + 1) * LANES]


def _load_token_major(ref, start, n):
    return jnp.concatenate(
        [ref[pl.ds(start * ROW_TILES + c, n, stride=ROW_TILES), :] for c in range(ROW_TILES)], axis=-1)


def _inproj_kernel(x_ref, g_ref, w_ref, o_ref, h_scr):
    @pl.when(pl.program_id(1) == 0)
    def _():
        x = x_ref[...]
        inv = lax.rsqrt(jnp.mean(x * x, axis=-1, keepdims=True) + EPS)
        h_scr[...] = (x * inv * g_ref[...]).astype(BF16)

    o_ref[...] = jnp.dot(h_scr[...], w_ref[...], preferred_element_type=F32).astype(o_ref.dtype)


def _inproj(x2d, g, w_bf16):
    T, D = x2d.shape
    N = w_bf16.shape[1]
    tm = min(1024, T)
    tn = 1024
    return pl.pallas_call(
        _inproj_kernel,
        grid=(T // tm, N // tn),
        in_specs=[
            pl.BlockSpec((tm, D), lambda i, j: (i, 0)),
            pl.BlockSpec((1, D), lambda i, j: (0, 0)),
            pl.BlockSpec((D, tn), lambda i, j: (0, j)),
        ],
        out_specs=pl.BlockSpec((tm, tn), lambda i, j: (i, j)),
        out_shape=jax.ShapeDtypeStruct((T, N), BF16),
        scratch_shapes=[pltpu.VMEM((tm, D), BF16)],
        compiler_params=_params(("parallel", "arbitrary")),
        name="inproj",
    )(x2d, g.reshape(1, D), w_bf16)


def _rglru_kernel(xr_ref, gr_ref, cw_ref, cb_ref, wa_ref, ba_ref, wx_ref, bx_ref, lam_ref,
                  o_ref, xbuf, a_scr, u_scr, h_scr):
    ts = xr_ref.shape[0]
    pad = SUBLANES

    @pl.when(pl.program_id(1) == 0)
    def _():
        xbuf[0:pad, :] = jnp.zeros((pad, D_RNN), F32)
        h_scr[...] = jnp.zeros_like(h_scr)

    x = xr_ref[...].astype(F32)
    xbuf[pad:pad + ts, :] = x
    cw = cw_ref[...]
    xc = (cw[3:4, :] * x
          + cw[2:3, :] * xbuf[pad - 1:pad - 1 + ts, :]
          + cw[1:2, :] * xbuf[pad - 2:pad - 2 + ts, :]
          + cw[0:1, :] * xbuf[pad - 3:pad - 3 + ts, :]) + cb_ref[...]
    xbuf[0:pad, :] = x[ts - pad:ts, :]

    xcb = xc.astype(BF16)
    nchunk = D_RNN // MXU_DIM
    r_pre = jnp.concatenate(
        [jnp.dot(xcb[:, c * MXU_DIM:(c + 1) * MXU_DIM], wa_ref[c], preferred_element_type=F32)
         for c in range(nchunk)], axis=-1)
    i_pre = jnp.concatenate(
        [jnp.dot(xcb[:, c * MXU_DIM:(c + 1) * MXU_DIM], wx_ref[c], preferred_element_type=F32)
         for c in range(nchunk)], axis=-1)
    r = jax.nn.sigmoid(r_pre + ba_ref[...])
    ig = jax.nn.sigmoid(i_pre + bx_ref[...])
    z = -lam_ref[...]
    softplus = jnp.maximum(z, 0.0) + jnp.log1p(jnp.exp(-jnp.abs(z)))
    log_a = (-RGLRU_C) * r * softplus
    a = jnp.exp(log_a)
    th = jnp.tanh(log_a)
    mult = jnp.sqrt(-2.0 * th / (1.0 - th))
    a_scr[...] = a
    u_scr[...] = mult * (ig * xc)

    row = lax.broadcasted_iota(I32, (SUBLANES, D_RNN), 0)

    def body(g, h):
        off = pl.multiple_of(g * SUBLANES, SUBLANES)
        a8 = a_scr[pl.ds(off, SUBLANES), :]
        u8 = u_scr[pl.ds(off, SUBLANES), :]
        for d in (1, 2, 4):
            a_sh = jnp.where(row >= d, pltpu.roll(a8, d, 0), 1.0)
            u_sh = jnp.where(row >= d, pltpu.roll(u8, d, 0), 0.0)
            u8 = a8 * u_sh + u8
            a8 = a8 * a_sh
        h8 = u8 + a8 * h
        u_scr[pl.ds(off, SUBLANES), :] = h8
        return h8[SUBLANES - 1:SUBLANES, :]

    h_scr[...] = lax.fori_loop(0, ts // SUBLANES, body, h_scr[...])
    gate = jax.nn.gelu(gr_ref[...].astype(F32), approximate=True)
    o_ref[...] = (u_scr[...] * gate).astype(o_ref.dtype)


def _block_diag_chunks(w):
    per = MXU_DIM // RNN_BLOCK
    w = w.reshape(D_RNN // MXU_DIM, per, RNN_BLOCK, RNN_BLOCK)
    eye = jnp.eye(per, dtype=w.dtype)
    out = jnp.einsum('gpcd,pq->gpcqd', w, eye)
    return out.reshape(D_RNN // MXU_DIM, MXU_DIM, MXU_DIM)


def _rglru(proj, B, S, conv_w, conv_b, w_a, b_a, w_x, b_x, lru_lambda):
    T = B * S
    ts = min(256, S)
    ns = S // ts
    wa = _block_diag_chunks(w_a).astype(BF16)
    wx = _block_diag_chunks(w_x).astype(BF16)
    nchunk = D_RNN // MXU_DIM
    row = lambda v: v.reshape(1, D_RNN)
    const2 = lambda b, s: (0, 0)
    return pl.pallas_call(
        _rglru_kernel,
        grid=(B, ns),
        in_specs=[
            pl.BlockSpec((ts, D_RNN), lambda b, s: (b * ns + s, COL_XR)),
            pl.BlockSpec((ts, D_RNN), lambda b, s: (b * ns + s, COL_GR)),
            pl.BlockSpec((CONV_W, D_RNN), const2),
            pl.BlockSpec((1, D_RNN), const2),
            pl.BlockSpec((nchunk, MXU_DIM, MXU_DIM), lambda b, s: (0, 0, 0)),
            pl.BlockSpec((1, D_RNN), const2),
            pl.BlockSpec((nchunk, MXU_DIM, MXU_DIM), lambda b, s: (0, 0, 0)),
            pl.BlockSpec((1, D_RNN), const2),
            pl.BlockSpec((1, D_RNN), const2),
        ],
        out_specs=pl.BlockSpec((ts, D_RNN), lambda b, s: (b * ns + s, 0)),
        out_shape=jax.ShapeDtypeStruct((T, D_RNN), BF16),
        scratch_shapes=[
            pltpu.VMEM((ts + SUBLANES, D_RNN), F32),
            pltpu.VMEM((ts, D_RNN), F32),
            pltpu.VMEM((ts, D_RNN), F32),
            pltpu.VMEM((1, D_RNN), F32),
        ],
        compiler_params=_params(("parallel", "arbitrary")),
        name="rglru",
    )(proj, proj, conv_w, row(conv_b), wa, row(b_a), wx, row(b_x), row(lru_lambda))


def _attn_kernel(q_ref, k_ref, v_ref, bias_ref, lq1_ref, lk1_ref, lq2_ref, lk2_ref, sg_ref,
                 o_ref, qs_scr, vx_scr, s_scr, m_scr, acc_scr, *, tq, lam_init):
    S = q_ref.shape[0]
    nq = S // tq
    scale = HEAD_DIM ** -0.5 * LOG2E
    lane = lax.broadcasted_iota(I32, (tq, V_DIM), 1)
    lam = (jnp.exp(jnp.sum(lq1_ref[...] * lk1_ref[...], keepdims=True))
           - jnp.exp(jnp.sum(lq2_ref[...] * lk2_ref[...], keepdims=True)) + lam_init)
    vx_scr[:, 0:V_DIM] = v_ref[...]
    vx_scr[:, V_DIM:2 * V_DIM] = jnp.ones((S, V_DIM), BF16)

    def scores(j, dst):
        k = k_ref[pl.ds(pl.multiple_of(j * tq, tq), tq), :]
        s_scr[dst] = lax.dot_general(qs_scr[...], k, (((1,), (1,)), ((), ())),
                                     preferred_element_type=F32)

    def step(j, src, bias_idx, prefetch):
        if prefetch:
            scores(j + 1, 1 - src)
        s = s_scr[src]
        if bias_idx is not None:
            b = bias_ref[bias_idx]
            s = s + jnp.concatenate([b, b], axis=0)
        vx = vx_scr[pl.ds(pl.multiple_of(j * tq, tq), tq), :]
        m_prev = m_scr[...]
        m_new = jnp.maximum(m_prev, jnp.max(s, axis=1)[:, None])
        p = jnp.exp2(s - jnp.tile(m_new, (1, tq // LANES)))
        alpha = jnp.exp2(m_prev - m_new)
        acc_scr[...] = (jnp.tile(alpha, (1, 2)) * acc_scr[...]
                        + jnp.dot(p.astype(BF16), vx, preferred_element_type=F32))
        m_scr[...] = m_new

    def qblock(qi, carry):
        q0 = pl.multiple_of(qi * tq, tq)
        q = (q_ref[pl.ds(q0, tq), :].astype(F32) * scale).astype(BF16)
        zero = jnp.zeros_like(q)
        qs_scr[0:tq, :] = jnp.where(lane < HEAD_DIM, q, zero)
        qs_scr[tq:2 * tq, :] = jnp.where(lane >= HEAD_DIM, q, zero)
        m_scr[...] = jnp.full(m_scr.shape, NEG_INF, F32)
        acc_scr[...] = jnp.zeros(acc_scr.shape, F32)
        scores(0, 0)
        nfar = jnp.maximum(qi - 1, 0)

        def far_pair(i, c):
            step(2 * i, 0, None, True)
            step(2 * i + 1, 1, None, True)
            return c

        lax.fori_loop(0, nfar // 2, far_pair, 0)
        odd = nfar % 2

        @pl.when(odd == 1)
        def _():
            step(nfar - 1, 0, None, True)

        @pl.when(qi == 0)
        def _():
            step(qi, 0, 0, False)

        for par in (0, 1):
            @pl.when((qi >= 1) & (odd == par))
            def _(par=par):
                step(qi - 1, par, 1, True)
                step(qi, 1 - par, 0, False)

        acc = acc_scr[...]
        o_all = acc[:, 0:V_DIM] / acc[:, V_DIM:2 * V_DIM]
        o = o_all[0:tq, :] - lam * o_all[tq:2 * tq, :]
        inv = lax.rsqrt(jnp.mean(o * o, axis=-1, keepdims=True) + EPS)
        y = (o * inv * sg_ref[...]) * (1.0 - lam_init)
        o_ref[pl.ds(q0, tq), :] = y.astype(o_ref.dtype)
        return carry

    lax.fori_loop(0, nq, qblock, 0)


def _rel_bucket(n):
    n = jnp.maximum(n, 0)
    nf = jnp.maximum(n, MAX_EXACT).astype(F32)
    large = MAX_EXACT + (jnp.log(nf / MAX_EXACT) / math.log(MAX_DISTANCE / MAX_EXACT)
                         * (NUM_BUCKETS - MAX_EXACT)).astype(I32)
    large = jnp.minimum(large, NUM_BUCKETS - 1)
    return jnp.where(n < MAX_EXACT, n, large)


def _bias_tiles(rel_table, tq):
    i = jnp.arange(tq, dtype=I32)[:, None]
    j = jnp.arange(tq, dtype=I32)[None, :]
    table = rel_table.astype(F32) - rel_table[NUM_BUCKETS - 1].astype(F32)[None, :]
    tiles = []
    for delta in (0, tq):
        n = i - j + delta
        onehot = (_rel_bucket(n)[:, :, None] == jnp.arange(NUM_BUCKETS, dtype=I32)).astype(F32)
        b = jnp.einsum('ijb,bh->hij', onehot, table, precision=lax.Precision.HIGHEST) * LOG2E
        tiles.append(jnp.where((n >= 0)[None], b, NEG_INF))
    return jnp.stack(tiles, axis=1)


ATT_TQ = 512
LOG2E = math.log2(math.e)


def _attention(proj, B, S, lq1, lk1, lq2, lk2, subln_g, rel_table, lam_init):
    T = B * S
    tq = min(ATT_TQ, S)
    assert V_DIM == LANES and MAX_DISTANCE <= tq and S % tq == 0
    bias = _bias_tiles(rel_table, tq)
    vec = lambda v: v.reshape(1, -1).astype(F32)
    const2 = lambda b, h: (0, 0)
    kern = functools.partial(_attn_kernel, tq=tq, lam_init=lam_init)
    return pl.pallas_call(
        kern,
        grid=(B, N_HEADS),
        in_specs=[
            pl.BlockSpec((S, V_DIM), lambda b, h: (b, COL_Q * N_HEADS + h)),
            pl.BlockSpec((S, V_DIM), lambda b, h: (b, COL_K * N_HEADS + h)),
            pl.BlockSpec((S, V_DIM), lambda b, h: (b, COL_V * N_HEADS + h)),
            pl.BlockSpec((None, 2, tq, tq), lambda b, h: (h, 0, 0, 0)),
            pl.BlockSpec((1, HEAD_DIM), const2),
            pl.BlockSpec((1, HEAD_DIM), const2),
            pl.BlockSpec((1, HEAD_DIM), const2),
            pl.BlockSpec((1, HEAD_DIM), const2),
            pl.BlockSpec((1, V_DIM), const2),
        ],
        out_specs=pl.BlockSpec((S, V_DIM), lambda b, h: (b, h)),
        out_shape=jax.ShapeDtypeStruct((T, ATT_V), BF16),
        scratch_shapes=[
            pltpu.VMEM((2 * tq, V_DIM), BF16),
            pltpu.VMEM((S, 2 * V_DIM), BF16),
            pltpu.VMEM((2, 2 * tq, tq), F32),
            pltpu.VMEM((2 * tq, LANES), F32),
            pltpu.VMEM((2 * tq, 2 * V_DIM), F32),
        ],
        compiler_params=_params(("parallel", "arbitrary")),
        name="diff_attn",
    )(proj, proj, proj, bias, vec(lq1), vec(lk1), vec(lq2), vec(lk2), vec(subln_g))


def _merge_kernel(x_ref, yr_ref, ya_ref, g0_ref, g1_ref, wr_ref, wa_ref, wo_ref, gf_ref, wrt_ref, brt_ref,
                  x2_ref, hf_ref, idx_ref, wgt_ref):
    pr = jnp.dot(yr_ref[...], wr_ref[...], preferred_element_type=F32)
    pa = jnp.dot(ya_ref[...], wa_ref[...], preferred_element_type=F32)
    merged = (jax.nn.sigmoid(g0_ref[...].astype(F32)) * pr
              + jax.nn.sigmoid(g1_ref[...].astype(F32)) * pa)
    x2 = x_ref[...] + jnp.dot(merged.astype(BF16), wo_ref[...], preferred_element_type=F32)
    x2_ref[...] = x2
    inv = lax.rsqrt(jnp.mean(x2 * x2, axis=-1, keepdims=True) + EPS)
    hf = x2 * inv * gf_ref[...]
    _store_token_major(hf_ref, hf)
    logits = jnp.dot(hf, wrt_ref[...], preferred_element_type=F32,
                     precision=lax.Precision.HIGHEST) + brt_ref[...]
    tm = logits.shape[0]
    lane = lax.broadcasted_iota(I32, (tm, LANES), 1)
    work = jnp.where(lane < N_EXPERTS, logits, -jnp.inf)
    idx_out = jnp.zeros((tm, LANES), I32)
    val_out = jnp.full((tm, LANES), -jnp.inf, F32)
    for k in range(TOP_K):
        mx = jnp.max(work, axis=-1, keepdims=True)
        sel = jnp.min(jnp.where(work == mx, lane, LANES), axis=-1, keepdims=True)
        idx_out = jnp.where(lane == k, sel, idx_out)
        val_out = jnp.where(lane == k, mx, val_out)
        work = jnp.where(lane == sel, -jnp.inf, work)
    e = jnp.exp(val_out - jnp.max(val_out, axis=-1, keepdims=True))
    idx_ref[...] = idx_out
    wgt_ref[...] = e / jnp.sum(e, axis=-1, keepdims=True)


def _merge_router(x2d, y_rnn, y_att, proj, w_pr, w_pa, w_o, g_ffn, w_router, b_router):
    T, D = x2d.shape
    tm = min(512, T)
    wrt = jnp.zeros((D, LANES), F32).at[:, :N_EXPERTS].set(w_router)
    brt = jnp.zeros((1, LANES), F32).at[0, :N_EXPERTS].set(b_router)
    rowblk = lambda c: pl.BlockSpec((tm, D), lambda i, c=c: (i, c))
    full = lambda a: pl.BlockSpec(a.shape, lambda i: (0,) * a.ndim)
    wr, wa, wo = w_pr.astype(BF16), w_pa.astype(BF16), w_o.astype(BF16)
    gf = g_ffn.reshape(1, D)
    return pl.pallas_call(
        _merge_kernel,
        grid=(T // tm,),
        in_specs=[rowblk(0), rowblk(0), rowblk(0), rowblk(COL_G0), rowblk(COL_G1),
                  full(wr), full(wa), full(wo), full(gf), full(wrt), full(brt)],
        out_specs=[rowblk(0), pl.BlockSpec((tm * ROW_TILES, LANES), lambda i: (i, 0)),
                   pl.BlockSpec((tm, LANES), lambda i: (i, 0)),
                   pl.BlockSpec((tm, LANES), lambda i: (i, 0))],
        out_shape=[jax.ShapeDtypeStruct((T, D), F32), jax.ShapeDtypeStruct((T * ROW_TILES, LANES), F32),
                   jax.ShapeDtypeStruct((T, LANES), I32), jax.ShapeDtypeStruct((T, LANES), F32)],
        compiler_params=_params(("parallel",)),
        name="merge_router",
    )(x2d, y_rnn, y_att, proj, proj, wr, wa, wo, gf, wrt, brt)


def _route_kernel(idx_ref, dest_ref, cnt_ref, cnt_scr, run_scr, start_scr, *, blk):
    ph = pl.program_id(0)
    i = pl.program_id(1)
    tt = idx_ref.shape[0]
    idx = idx_ref[...]
    lane = lax.broadcasted_iota(I32, (tt, LANES), 1)
    onehot = jnp.zeros((tt, LANES), F32)
    for k in range(TOP_K):
        onehot = onehot + (idx[:, k:k + 1] == lane).astype(F32)

    @pl.when((ph == 0) & (i == 0))
    def _():
        cnt_scr[...] = jnp.zeros_like(cnt_scr)

    @pl.when(ph == 0)
    def _():
        cnt_scr[...] += jnp.sum(onehot, axis=0, keepdims=True)

    @pl.when((ph == 1) & (i == 0))
    def _():
        cnt = jnp.broadcast_to(cnt_scr[...], (SUBLANES, LANES))
        padded = jnp.floor((cnt + (blk - 1)) / blk) * blk
        l8 = lax.broadcasted_iota(I32, (SUBLANES, LANES), 1)
        incl = padded
        d = 1
        while d < LANES:
            incl = incl + jnp.where(l8 >= d, pltpu.roll(incl, d, 1), 0.0)
            d *= 2
        start_scr[...] = (incl - padded)[0:1, :]
        run_scr[...] = jnp.zeros_like(run_scr)

    @pl.when(ph == 1)
    def _():
        r = lax.broadcasted_iota(I32, (tt, tt), 0)
        c = lax.broadcasted_iota(I32, (tt, tt), 1)
        tri = (c < r).astype(BF16)
        before = jnp.dot(tri, onehot.astype(BF16), preferred_element_type=F32)
        base = before + run_scr[...] + start_scr[...]
        out = jnp.zeros((tt, LANES), F32)
        for k in range(TOP_K):
            dk = jnp.sum(jnp.where(idx[:, k:k + 1] == lane, base, 0.0), axis=-1, keepdims=True)
            out = jnp.where(lane == k, dk, out)
        dest_ref[...] = out.astype(I32)
        run_scr[...] += jnp.sum(onehot, axis=0, keepdims=True)
        cnt_ref[...] = jnp.broadcast_to(cnt_scr[...], cnt_ref.shape)


def _route(top_idx, blk):
    T = top_idx.shape[0]
    tt = min(512, T)
    kern = functools.partial(_route_kernel, blk=blk)
    return pl.pallas_call(
        kern,
        grid=(2, T // tt),
        in_specs=[pl.BlockSpec((tt, LANES), lambda p, i: (i, 0))],
        out_specs=[pl.BlockSpec((tt, LANES), lambda p, i: (p * i, 0)),
                   pl.BlockSpec((SUBLANES, LANES), lambda p, i: (0, 0))],
        out_shape=[jax.ShapeDtypeStruct((T, LANES), I32),
                   jax.ShapeDtypeStruct((SUBLANES, LANES), F32)],
        scratch_shapes=[pltpu.VMEM((1, LANES), F32), pltpu.VMEM((1, LANES), F32),
                        pltpu.VMEM((1, LANES), F32)],
        compiler_params=_params(("arbitrary", "arbitrary")),
        name="route",
    )(top_idx)


DMA_UNROLL = 8


def _for_each_row(n, fn):
    def body(g, c):
        for u in range(DMA_UNROLL):
            fn(g * DMA_UNROLL + u)
        return c

    lax.fori_loop(0, n // DMA_UNROLL, body, 0)


def _row_tile(ref, row):
    return ref.at[pl.ds(pl.multiple_of(row * ROW_TILES, ROW_TILES), ROW_TILES), :]


def _dispatch_kernel(dest_ref, hf_ref, xs_in_ref, xs_ref, sem, *, tt):
    del xs_in_ref
    base = pl.program_id(0) * tt

    def row_copy(j):
        return pltpu.make_async_copy(_row_tile(hf_ref, base + j // TOP_K),
                                     _row_tile(xs_ref, dest_ref[j]), sem)

    _for_each_row(tt * TOP_K, lambda j: row_copy(j).start())
    _for_each_row(tt * TOP_K, lambda j: row_copy(j).wait())


def _dispatch(dest_flat, hf_tm, n_slots):
    T = hf_tm.shape[0] // ROW_TILES
    tt = min(512, T)
    kern = functools.partial(_dispatch_kernel, tt=tt)
    zeros = jnp.zeros((n_slots * ROW_TILES, LANES), F32)
    return pl.pallas_call(
        kern,
        grid=(T // tt,),
        in_specs=[pl.BlockSpec((tt * TOP_K,), lambda i: (i,), memory_space=pltpu.SMEM),
                  pl.BlockSpec(memory_space=pl.ANY),
                  pl.BlockSpec(memory_space=pl.ANY)],
        out_specs=pl.BlockSpec(memory_space=pl.ANY),
        out_shape=jax.ShapeDtypeStruct((n_slots * ROW_TILES, LANES), F32),
        scratch_shapes=[pltpu.SemaphoreType.DMA(())],
        input_output_aliases={2: 0},
        compiler_params=_params(("arbitrary",)),
        name="dispatch",
    )(dest_flat, hf_tm, zeros)


def _expert_kernel(be_ref, bsrc_ref, act_ref, x_ref, wg_ref, bg_ref, wu_ref, bu_ref, wd_ref, bd_ref, o_ref,
                   *, blk):
    del be_ref, bsrc_ref

    @pl.when(act_ref[pl.program_id(0)] == 1)
    def _():
        x = _load_token_major(x_ref, 0, blk).astype(BF16)
        g = jnp.dot(x, wg_ref[0], preferred_element_type=F32) + bg_ref[0]
        u = jnp.dot(x, wu_ref[0], preferred_element_type=F32) + bu_ref[0]
        g = jnp.minimum(g, SWIGLU_LIMIT)
        u = jnp.clip(u, -SWIGLU_LIMIT, SWIGLU_LIMIT)
        act = (u + 1.0) * (g * jax.nn.sigmoid(SWIGLU_ALPHA * g))
        y = jnp.dot(act.astype(BF16), wd_ref[0], preferred_element_type=F32) + bd_ref[0]
        _store_token_major(o_ref, y)


def _experts(x_slots, blk, block_expert, block_src, block_active, wg, bg, wu, bu, wd, bd):
    D = D_MODEL
    nb = x_slots.shape[0] // (blk * ROW_TILES)
    wspec = pl.BlockSpec((1, D, D_FF), lambda i, be, bs, ac: (be[i], 0, 0))
    bspec = pl.BlockSpec((1, 1, D_FF), lambda i, be, bs, ac: (be[i], 0, 0))
    xspec = pl.BlockSpec((blk * ROW_TILES, LANES), lambda i, be, bs, ac: (bs[i], 0))
    grid_spec = pltpu.PrefetchScalarGridSpec(
        num_scalar_prefetch=3,
        grid=(nb,),
        in_specs=[xspec, wspec, bspec, wspec, bspec, wspec, bspec],
        out_specs=xspec,
    )
    b3 = lambda b: b.reshape(N_EXPERTS, 1, -1)
    return pl.pallas_call(
        functools.partial(_expert_kernel, blk=blk),
        grid_spec=grid_spec,
        out_shape=jax.ShapeDtypeStruct(x_slots.shape, F32),
        compiler_params=_params(("arbitrary",)),
        name="experts",
    )(block_expert, block_src, block_active, x_slots,
      wg.astype(BF16), b3(bg), wu.astype(BF16), b3(bu), wd.astype(BF16), b3(bd))


def _combine_kernel(dest_ref, ys_ref, w_ref, x2_ref, g_ref, o_ref, buf, sem, *, tt):
    def row_copy(j):
        return pltpu.make_async_copy(_row_tile(ys_ref, dest_ref[j]),
                                     _row_tile(buf, (j % TOP_K) * tt + j // TOP_K), sem)

    _for_each_row(tt * TOP_K, lambda j: row_copy(j).start())
    _for_each_row(tt * TOP_K, lambda j: row_copy(j).wait())
    w = w_ref[...]
    y = x2_ref[...]
    for k in range(TOP_K):
        y = y + w[:, k:k + 1] * _load_token_major(buf, k * tt, tt)
    inv = lax.rsqrt(jnp.mean(y * y, axis=-1, keepdims=True) + EPS)
    o_ref[...] = y * inv * g_ref[...]


def _combine(dest_flat, y_slots, top_w, x2, g_final):
    T, D = x2.shape
    tt = min(256, T)
    kern = functools.partial(_combine_kernel, tt=tt)
    return pl.pallas_call(
        kern,
        grid=(T // tt,),
        in_specs=[pl.BlockSpec((tt * TOP_K,), lambda i: (i,), memory_space=pltpu.SMEM),
                  pl.BlockSpec(memory_space=pl.ANY),
                  pl.BlockSpec((tt, LANES), lambda i: (i, 0)),
                  pl.BlockSpec((tt, D), lambda i: (i, 0)),
                  pl.BlockSpec((1, D), lambda i: (0, 0))],
        out_specs=pl.BlockSpec((tt, D), lambda i: (i, 0)),
        out_shape=jax.ShapeDtypeStruct((T, D), F32),
        scratch_shapes=[pltpu.VMEM((tt * TOP_K * ROW_TILES, LANES), F32), pltpu.SemaphoreType.DMA(())],
        compiler_params=_params(("arbitrary",)),
        name="combine",
    )(dest_flat, y_slots, top_w, x2, g_final.reshape(1, D))


def _moe_block_size(T):
    return min(512, max(SUBLANES, T * TOP_K // N_EXPERTS))


def _block_tables(counts, blk, nb):
    cnt = counts.astype(I32)
    padded = (cnt + blk - 1) // blk * blk
    ends = jnp.cumsum(padded)
    used = ends[-1] // blk
    starts = jnp.arange(nb, dtype=I32) * blk
    src = jnp.minimum(jnp.arange(nb, dtype=I32), jnp.maximum(used - 1, 0))
    expert = jnp.minimum(jnp.sum((starts[:, None] >= ends[None, :]).astype(I32), axis=1), N_EXPERTS - 1)
    expert = expert[src]
    active = (jnp.arange(nb, dtype=I32) < used).astype(I32)
    return expert, src, active


def kernel(x, norm_mix_g, w_in, conv_w, conv_b, w_rg_a, b_rg_a, w_rg_x, b_rg_x, lru_lambda, diff_lambda_q1, diff_lambda_k1, diff_lambda_q2, diff_lambda_k2, subln_g, rel_bias_table, w_proj_rnn, w_proj_att, w_out, norm_ffn_g, w_router, b_router, w_gate_e, b_gate_e, w_up_e, b_up_e, w_down_e, b_down_e, norm_final_g):
    B, S, D = x.shape
    T = B * S
    assert norm_mix_g.shape[0] == 1, "single-layer block: the final norm is fused into the MoE combine"
    l = 0
    xt = x.reshape(T, D)
    lam_init = 0.8 - 0.6 * math.exp(-0.3 * l)
    proj = _inproj(xt, norm_mix_g[l], w_in[l].astype(BF16))
    y_rnn = _rglru(proj, B, S, conv_w[l], conv_b[l], w_rg_a[l], b_rg_a[l], w_rg_x[l], b_rg_x[l],
                   lru_lambda[l])
    y_att = _attention(proj, B, S, diff_lambda_q1[l], diff_lambda_k1[l], diff_lambda_q2[l],
                       diff_lambda_k2[l], subln_g[l], rel_bias_table, lam_init)
    x2, hf, top_idx, top_w = _merge_router(xt, y_rnn, y_att, proj, w_proj_rnn[l], w_proj_att[l],
                                           w_out[l], norm_ffn_g[l], w_router[l], b_router[l])
    blk = _moe_block_size(T)
    nb = T * TOP_K // blk + N_EXPERTS
    dest, counts = _route(top_idx, blk)
    dest_flat = dest[:, :TOP_K].reshape(-1)
    expert, src, active = _block_tables(counts[0, :N_EXPERTS], blk, nb)
    x_slots = _dispatch(dest_flat, hf, nb * blk)
    y_slots = _experts(x_slots, blk, expert, src, active, w_gate_e[l], b_gate_e[l], w_up_e[l],
                       b_up_e[l], w_down_e[l], b_down_e[l])
    out = _combine(dest_flat, y_slots, top_w, x2, norm_final_g)
    return out.reshape(B, S, D)
```

```python
import functools
import math

import jax
import jax.numpy as jnp
from jax import lax
from jax.experimental import pallas as pl
from jax.experimental.pallas import tpu as pltpu

F32 = jnp.float32
BF16 = jnp.bfloat16
I32 = jnp.int32

D_MODEL = 1024
D_RNN = 1024
RNN_BLOCKS = 16
RNN_BLOCK = D_RNN // RNN_BLOCKS
CONV_W = 4
RGLRU_C = 8.0
N_HEADS = 8
HEAD_DIM = 64
V_DIM = 2 * HEAD_DIM
ATT_QK = N_HEADS * 2 * HEAD_DIM
ATT_V = N_HEADS * V_DIM
D_IN = 2 * D_RNN + 2 * ATT_QK + ATT_V + 2 * D_MODEL
NUM_BUCKETS = 32
MAX_EXACT = NUM_BUCKETS // 2
MAX_DISTANCE = 128
N_EXPERTS = 32
TOP_K = 4
D_FF = D_MODEL
SWIGLU_LIMIT = 7.0
SWIGLU_ALPHA = 1.702
EPS = 1e-6
NEG_INF = -1e30

LANES = 128
SUBLANES = 8
MXU_DIM = 256
VMEM_LIMIT = 56 * 1024 * 1024

COL_XR, COL_GR, COL_Q, COL_K, COL_V, COL_G0, COL_G1 = range(7)


def _params(sem, vmem=VMEM_LIMIT):
    return pltpu.CompilerParams(dimension_semantics=sem, vmem_limit_bytes=vmem)


ROW_TILES = D_MODEL // LANES
assert ROW_TILES == SUBLANES


def _store_token_major(ref, val):
    n = val.shape[0]
    for c in range(ROW_TILES):
        ref[pl.ds(c, n, stride=ROW_TILES), :] = val[:, c * LANES:(c + 1) * LANES]


def _load_token_major(ref, start, n):
    return jnp.concatenate(
        [ref[pl.ds(start * ROW_TILES + c, n, stride=ROW_TILES), :] for c in range(ROW_TILES)], axis=-1)


def _inproj_kernel(x_ref, g_ref, w_ref, o_ref, h_scr):
    @pl.when(pl.program_id(1) == 0)
    def _():
        x = x_ref[...]
        inv = lax.rsqrt(jnp.mean(x * x, axis=-1, keepdims=True) + EPS)
        h_scr[...] = (x * inv * g_ref[...]).astype(BF16)

    o_ref[...] = jnp.dot(h_scr[...], w_ref[...], preferred_element_type=F32).astype(o_ref.dtype)


def _inproj(x2d, g, w_bf16):
    T, D = x2d.shape
    N = w_bf16.shape[1]
    tm = min(1024, T)
    tn = 1024
    return pl.pallas_call(
        _inproj_kernel,
        grid=(T // tm, N // tn),
        in_specs=[
            pl.BlockSpec((tm, D), lambda i, j: (i, 0)),
            pl.BlockSpec((1, D), lambda i, j: (0, 0)),
            pl.BlockSpec((D, tn), lambda i, j: (0, j)),
        ],
        out_specs=pl.BlockSpec((tm, tn), lambda i, j: (i, j)),
        out_shape=jax.ShapeDtypeStruct((T, N), BF16),
        scratch_shapes=[pltpu.VMEM((tm, D), BF16)],
        compiler_params=_params(("parallel", "arbitrary")),
        name="inproj",
    )(x2d, g.reshape(1, D), w_bf16)


def _rglru_kernel(xr_ref, gr_ref, cw_ref, cb_ref, wa_ref, ba_ref, wx_ref, bx_ref, lam_ref,
                  o_ref, xbuf, a_scr, u_scr, h_scr):
    ts = xr_ref.shape[0]
    pad = SUBLANES

    @pl.when(pl.program_id(1) == 0)
    def _():
        xbuf[0:pad, :] = jnp.zeros((pad, D_RNN), F32)
        h_scr[...] = jnp.zeros_like(h_scr)

    x = xr_ref[...].astype(F32)
    xbuf[pad:pad + ts, :] = x
    cw = cw_ref[...]
    xc = (cw[3:4, :] * x
          + cw[2:3, :] * xbuf[pad - 1:pad - 1 + ts, :]
          + cw[1:2, :] * xbuf[pad - 2:pad - 2 + ts, :]
          + cw[0:1, :] * xbuf[pad - 3:pad - 3 + ts, :]) + cb_ref[...]
    xbuf[0:pad, :] = x[ts - pad:ts, :]

    xcb = xc.astype(BF16)
    nchunk = D_RNN // MXU_DIM
    r_pre = jnp.concatenate(
        [jnp.dot(xcb[:, c * MXU_DIM:(c + 1) * MXU_DIM], wa_ref[c], preferred_element_type=F32)
         for c in range(nchunk)], axis=-1)
    i_pre = jnp.concatenate(
        [jnp.dot(xcb[:, c * MXU_DIM:(c + 1) * MXU_DIM], wx_ref[c], preferred_element_type=F32)
         for c in range(nchunk)], axis=-1)
    r = jax.nn.sigmoid(r_pre + ba_ref[...])
    ig = jax.nn.sigmoid(i_pre + bx_ref[...])
    z = -lam_ref[...]
    softplus = jnp.maximum(z, 0.0) + jnp.log1p(jnp.exp(-jnp.abs(z)))
    log_a = (-RGLRU_C) * r * softplus
    a = jnp.exp(log_a)
    th = jnp.tanh(log_a)
    mult = jnp.sqrt(-2.0 * th / (1.0 - th))
    a_scr[...] = a
    u_scr[...] = mult * (ig * xc)

    row = lax.broadcasted_iota(I32, (SUBLANES, D_RNN), 0)

    def body(g, h):
        off = pl.multiple_of(g * SUBLANES, SUBLANES)
        a8 = a_scr[pl.ds(off, SUBLANES), :]
        u8 = u_scr[pl.ds(off, SUBLANES), :]
        for d in (1, 2, 4):
            a_sh = jnp.where(row >= d, pltpu.roll(a8, d, 0), 1.0)
            u_sh = jnp.where(row >= d, pltpu.roll(u8, d, 0), 0.0)
            u8 = a8 * u_sh + u8
            a8 = a8 * a_sh
        h8 = u8 + a8 * h
        u_scr[pl.ds(off, SUBLANES), :] = h8
        return h8[SUBLANES - 1:SUBLANES, :]

    h_scr[...] = lax.fori_loop(0, ts // SUBLANES, body, h_scr[...])
    gate = jax.nn.gelu(gr_ref[...].astype(F32), approximate=True)
    o_ref[...] = (u_scr[...] * gate).astype(o_ref.dtype)


def _block_diag_chunks(w):
    per = MXU_DIM // RNN_BLOCK
    w = w.reshape(D_RNN // MXU_DIM, per, RNN_BLOCK, RNN_BLOCK)
    eye = jnp.eye(per, dtype=w.dtype)
    out = jnp.einsum('gpcd,pq->gpcqd', w, eye)
    return out.reshape(D_RNN // MXU_DIM, MXU_DIM, MXU_DIM)


def _rglru(proj, B, S, conv_w, conv_b, w_a, b_a, w_x, b_x, lru_lambda):
    T = B * S
    ts = min(256, S)
    ns = S // ts
    wa = _block_diag_chunks(w_a).astype(BF16)
    wx = _block_diag_chunks(w_x).astype(BF16)
    nchunk = D_RNN // MXU_DIM
    row = lambda v: v.reshape(1, D_RNN)
    const2 = lambda b, s: (0, 0)
    return pl.pallas_call(
        _rglru_kernel,
        grid=(B, ns),
        in_specs=[
            pl.BlockSpec((ts, D_RNN), lambda b, s: (b * ns + s, COL_XR)),
            pl.BlockSpec((ts, D_RNN), lambda b, s: (b * ns + s, COL_GR)),
            pl.BlockSpec((CONV_W, D_RNN), const2),
            pl.BlockSpec((1, D_RNN), const2),
            pl.BlockSpec((nchunk, MXU_DIM, MXU_DIM), lambda b, s: (0, 0, 0)),
            pl.BlockSpec((1, D_RNN), const2),
            pl.BlockSpec((nchunk, MXU_DIM, MXU_DIM), lambda b, s: (0, 0, 0)),
            pl.BlockSpec((1, D_RNN), const2),
            pl.BlockSpec((1, D_RNN), const2),
        ],
        out_specs=pl.BlockSpec((ts, D_RNN), lambda b, s: (b * ns + s, 0)),
        out_shape=jax.ShapeDtypeStruct((T, D_RNN), BF16),
        scratch_shapes=[
            pltpu.VMEM((ts + SUBLANES, D_RNN), F32),
            pltpu.VMEM((ts, D_RNN), F32),
            pltpu.VMEM((ts, D_RNN), F32),
            pltpu.VMEM((1, D_RNN), F32),
        ],
        compiler_params=_params(("parallel", "arbitrary")),
        name="rglru",
    )(proj, proj, conv_w, row(conv_b), wa, row(b_a), wx, row(b_x), row(lru_lambda))


def _attn_kernel(q_ref, k_ref, v_ref, bias_ref, lq1_ref, lk1_ref, lq2_ref, lk2_ref, sg_ref,
                 o_ref, qs_scr, vx_scr, s_scr, m_scr, acc_scr, *, tq, lam_init):
    S = q_ref.shape[0]
    nq = S // tq
    scale = HEAD_DIM ** -0.5 * LOG2E
    lane = lax.broadcasted_iota(I32, (tq, V_DIM), 1)
    lam = (jnp.exp(jnp.sum(lq1_ref[...] * lk1_ref[...], keepdims=True))
           - jnp.exp(jnp.sum(lq2_ref[...] * lk2_ref[...], keepdims=True)) + lam_init)
    vx_scr[:, 0:V_DIM] = v_ref[...]
    vx_scr[:, V_DIM:2 * V_DIM] = jnp.ones((S, V_DIM), BF16)

    def scores(j, dst):
        k = k_ref[pl.ds(pl.multiple_of(j * tq, tq), tq), :]
        s_scr[dst] = lax.dot_general(qs_scr[...], k, (((1,), (1,)), ((), ())),
                                     preferred_element_type=F32)

    def step(j, src, bias_idx, prefetch):
        if prefetch:
            scores(j + 1, 1 - src)
        s = s_scr[src]
        if bias_idx is not None:
            b = bias_ref[bias_idx]
            s = s + jnp.concatenate([b, b], axis=0)
        vx = vx_scr[pl.ds(pl.multiple_of(j * tq, tq), tq), :]
        m_prev = m_scr[...]
        m_new = jnp.maximum(m_prev, jnp.max(s, axis=1)[:, None])
        p = jnp.exp2(s - jnp.tile(m_new, (1, tq // LANES)))
        alpha = jnp.exp2(m_prev - m_new)
        acc_scr[...] = (jnp.tile(alpha, (1, 2)) * acc_scr[...]
                        + jnp.dot(p.astype(BF16), vx, preferred_element_type=F32))
        m_scr[...] = m_new

    def qblock(qi, carry):
        q0 = pl.multiple_of(qi * tq, tq)
        q = (q_ref[pl.ds(q0, tq), :].astype(F32) * scale).astype(BF16)
        zero = jnp.zeros_like(q)
        qs_scr[0:tq, :] = jnp.where(lane < HEAD_DIM, q, zero)
        qs_scr[tq:2 * tq, :] = jnp.where(lane >= HEAD_DIM, q, zero)
        m_scr[...] = jnp.full(m_scr.shape, NEG_INF, F32)
        acc_scr[...] = jnp.zeros(acc_scr.shape, F32)
        scores(0, 0)
        nfar = jnp.maximum(qi - 1, 0)

        def far_pair(i, c):
            step(2 * i, 0, None, True)
            step(2 * i + 1, 1, None, True)
            return c

        lax.fori_loop(0, nfar // 2, far_pair, 0)
        odd = nfar % 2

        @pl.when(odd == 1)
        def _():
            step(nfar - 1, 0, None, True)

        @pl.when(qi == 0)
        def _():
            step(qi, 0, 0, False)

        for par in (0, 1):
            @pl.when((qi >= 1) & (odd == par))
            def _(par=par):
                step(qi - 1, par, 1, True)
                step(qi, 1 - par, 0, False)

        acc = acc_scr[...]
        o_all = acc[:, 0:V_DIM] / acc[:, V_DIM:2 * V_DIM]
        o = o_all[0:tq, :] - lam * o_all[tq:2 * tq, :]
        inv = lax.rsqrt(jnp.mean(o * o, axis=-1, keepdims=True) + EPS)
        y = (o * inv * sg_ref[...]) * (1.0 - lam_init)
        o_ref[pl.ds(q0, tq), :] = y.astype(o_ref.dtype)
        return carry

    lax.fori_loop(0, nq, qblock, 0)


def _rel_bucket(n):
    n = jnp.maximum(n, 0)
    nf = jnp.maximum(n, MAX_EXACT).astype(F32)
    large = MAX_EXACT + (jnp.log(nf / MAX_EXACT) / math.log(MAX_DISTANCE / MAX_EXACT)
                         * (NUM_BUCKETS - MAX_EXACT)).astype(I32)
    large = jnp.minimum(large, NUM_BUCKETS - 1)
    return jnp.where(n < MAX_EXACT, n, large)


def _bias_tiles(rel_table, tq):
    i = jnp.arange(tq, dtype=I32)[:, None]
    j = jnp.arange(tq, dtype=I32)[None, :]
    table = rel_table.astype(F32) - rel_table[NUM_BUCKETS - 1].astype(F32)[None, :]
    tiles = []
    for delta in (0, tq):
        n = i - j + delta
        onehot = (_rel_bucket(n)[:, :, None] == jnp.arange(NUM_BUCKETS, dtype=I32)).astype(F32)
        b = jnp.einsum('ijb,bh->hij', onehot, table, precision=lax.Precision.HIGHEST) * LOG2E
        tiles.append(jnp.where((n >= 0)[None], b, NEG_INF))
    return jnp.stack(tiles, axis=1)


ATT_TQ = 512
LOG2E = math.log2(math.e)


def _attention(proj, B, S, lq1, lk1, lq2, lk2, subln_g, rel_table, lam_init):
    T = B * S
    tq = min(ATT_TQ, S)
    assert V_DIM == LANES and MAX_DISTANCE <= tq and S % tq == 0
    bias = _bias_tiles(rel_table, tq)
    vec = lambda v: v.reshape(1, -1).astype(F32)
    const2 = lambda b, h: (0, 0)
    kern = functools.partial(_attn_kernel, tq=tq, lam_init=lam_init)
    return pl.pallas_call(
        kern,
        grid=(B, N_HEADS),
        in_specs=[
            pl.BlockSpec((S, V_DIM), lambda b, h: (b, COL_Q * N_HEADS + h)),
            pl.BlockSpec((S, V_DIM), lambda b, h: (b, COL_K * N_HEADS + h)),
            pl.BlockSpec((S, V_DIM), lambda b, h: (b, COL_V * N_HEADS + h)),
            pl.BlockSpec((None, 2, tq, tq), lambda b, h: (h, 0, 0, 0)),
            pl.BlockSpec((1, HEAD_DIM), const2),
            pl.BlockSpec((1, HEAD_DIM), const2),
            pl.BlockSpec((1, HEAD_DIM), const2),
            pl.BlockSpec((1, HEAD_DIM), const2),
            pl.BlockSpec((1, V_DIM), const2),
        ],
        out_specs=pl.BlockSpec((S, V_DIM), lambda b, h: (b, h)),
        out_shape=jax.ShapeDtypeStruct((T, ATT_V), BF16),
        scratch_shapes=[
            pltpu.VMEM((2 * tq, V_DIM), BF16),
            pltpu.VMEM((S, 2 * V_DIM), BF16),
            pltpu.VMEM((2, 2 * tq, tq), F32),
            pltpu.VMEM((2 * tq, LANES), F32),
            pltpu.VMEM((2 * tq, 2 * V_DIM), F32),
        ],
        compiler_params=_params(("parallel", "arbitrary")),
        name="diff_attn",
    )(proj, proj, proj, bias, vec(lq1), vec(lk1), vec(lq2), vec(lk2), vec(subln_g))


def _merge_kernel(x_ref, yr_ref, ya_ref, g0_ref, g1_ref, wr_ref, wa_ref, wo_ref, gf_ref, wrt_ref, brt_ref,
                  x2_ref, hf_ref, idx_ref, wgt_ref):
    pr = jnp.dot(yr_ref[...], wr_ref[...], preferred_element_type=F32)
    pa = jnp.dot(ya_ref[...], wa_ref[...], preferred_element_type=F32)
    merged = (jax.nn.sigmoid(g0_ref[...].astype(F32)) * pr
              + jax.nn.sigmoid(g1_ref[...].astype(F32)) * pa)
    x2 = x_ref[...] + jnp.dot(merged.astype(BF16), wo_ref[...], preferred_element_type=F32)
    x2_ref[...] = x2
    inv = lax.rsqrt(jnp.mean(x2 * x2, axis=-1, keepdims=True) + EPS)
    hf = x2 * inv * gf_ref[...]
    _store_token_major(hf_ref, hf)
    logits = jnp.dot(hf, wrt_ref[...], preferred_element_type=F32,
                     precision=lax.Precision.HIGHEST) + brt_ref[...]
    tm = logits.shape[0]
    lane = lax.broadcasted_iota(I32, (tm, LANES), 1)
    work = jnp.where(lane < N_EXPERTS, logits, -jnp.inf)
    idx_out = jnp.zeros((tm, LANES), I32)
    val_out = jnp.full((tm, LANES), -jnp.inf, F32)
    for k in range(TOP_K):
        mx = jnp.max(work, axis=-1, keepdims=True)
        sel = jnp.min(jnp.where(work == mx, lane, LANES), axis=-1, keepdims=True)
        idx_out = jnp.where(lane == k, sel, idx_out)
        val_out = jnp.where(lane == k, mx, val_out)
        work = jnp.where(lane == sel, -jnp.inf, work)
    e = jnp.exp(val_out - jnp.max(val_out, axis=-1, keepdims=True))
    idx_ref[...] = idx_out
    wgt_ref[...] = e / jnp.sum(e, axis=-1, keepdims=True)


def _merge_router(x2d, y_rnn, y_att, proj, w_pr, w_pa, w_o, g_ffn, w_router, b_router):
    T, D = x2d.shape
    tm = min(512, T)
    wrt = jnp.zeros((D, LANES), F32).at[:, :N_EXPERTS].set(w_router)
    brt = jnp.zeros((1, LANES), F32).at[0, :N_EXPERTS].set(b_router)
    rowblk = lambda c: pl.BlockSpec((tm, D), lambda i, c=c: (i, c))
    full = lambda a: pl.BlockSpec(a.shape, lambda i: (0,) * a.ndim)
    wr, wa, wo = w_pr.astype(BF16), w_pa.astype(BF16), w_o.astype(BF16)
    gf = g_ffn.reshape(1, D)
    return pl.pallas_call(
        _merge_kernel,
        grid=(T // tm,),
        in_specs=[rowblk(0), rowblk(0), rowblk(0), rowblk(COL_G0), rowblk(COL_G1),
                  full(wr), full(wa), full(wo), full(gf), full(wrt), full(brt)],
        out_specs=[rowblk(0), pl.BlockSpec((tm * ROW_TILES, LANES), lambda i: (i, 0)),
                   pl.BlockSpec((tm, LANES), lambda i: (i, 0)),
                   pl.BlockSpec((tm, LANES), lambda i: (i, 0))],
        out_shape=[jax.ShapeDtypeStruct((T, D), F32), jax.ShapeDtypeStruct((T * ROW_TILES, LANES), F32),
                   jax.ShapeDtypeStruct((T, LANES), I32), jax.ShapeDtypeStruct((T, LANES), F32)],
        compiler_params=_params(("parallel",)),
        name="merge_router",
    )(x2d, y_rnn, y_att, proj, proj, wr, wa, wo, gf, wrt, brt)


def _route_kernel(idx_ref, dest_ref, cnt_ref, cnt_scr, run_scr, start_scr, *, blk):
    ph = pl.program_id(0)
    i = pl.program_id(1)
    tt = idx_ref.shape[0]
    idx = idx_ref[...]
    lane = lax.broadcasted_iota(I32, (tt, LANES), 1)
    onehot = jnp.zeros((tt, LANES), F32)
    for k in range(TOP_K):
        onehot = onehot + (idx[:, k:k + 1] == lane).astype(F32)

    @pl.when((ph == 0) & (i == 0))
    def _():
        cnt_scr[...] = jnp.zeros_like(cnt_scr)

    @pl.when(ph == 0)
    def _():
        cnt_scr[...] += jnp.sum(onehot, axis=0, keepdims=True)

    @pl.when((ph == 1) & (i == 0))
    def _():
        cnt = jnp.broadcast_to(cnt_scr[...], (SUBLANES, LANES))
        padded = jnp.floor((cnt + (blk - 1)) / blk) * blk
        l8 = lax.broadcasted_iota(I32, (SUBLANES, LANES), 1)
        incl = padded
        d = 1
        while d < LANES:
            incl = incl + jnp.where(l8 >= d, pltpu.roll(incl, d, 1), 0.0)
            d *= 2
        start_scr[...] = (incl - padded)[0:1, :]
        run_scr[...] = jnp.zeros_like(run_scr)

    @pl.when(ph == 1)
    def _():
        r = lax.broadcasted_iota(I32, (tt, tt), 0)
        c = lax.broadcasted_iota(I32, (tt, tt), 1)
        tri = (c < r).astype(BF16)
        before = jnp.dot(tri, onehot.astype(BF16), preferred_element_type=F32)
        base = before + run_scr[...] + start_scr[...]
        out = jnp.zeros((tt, LANES), F32)
        for k in range(TOP_K):
            dk = jnp.sum(jnp.where(idx[:, k:k + 1] == lane, base, 0.0), axis=-1, keepdims=True)
            out = jnp.where(lane == k, dk, out)
        dest_ref[...] = out.astype(I32)
        run_scr[...] += jnp.sum(onehot, axis=0, keepdims=True)
        cnt_ref[...] = jnp.broadcast_to(cnt_scr[...], cnt_ref.shape)


def _route(top_idx, blk):
    T = top_idx.shape[0]
    tt = min(512, T)
    kern = functools.partial(_route_kernel, blk=blk)
    return pl.pallas_call(
        kern,
        grid=(2, T // tt),
        in_specs=[pl.BlockSpec((tt, LANES), lambda p, i: (i, 0))],
        out_specs=[pl.BlockSpec((tt, LANES), lambda p, i: (p * i, 0)),
                   pl.BlockSpec((SUBLANES, LANES), lambda p, i: (0, 0))],
        out_shape=[jax.ShapeDtypeStruct((T, LANES), I32),
                   jax.ShapeDtypeStruct((SUBLANES, LANES), F32)],
        scratch_shapes=[pltpu.VMEM((1, LANES), F32), pltpu.VMEM((1, LANES), F32),
                        pltpu.VMEM((1, LANES), F32)],
        compiler_params=_params(("arbitrary", "arbitrary")),
        name="route",
    )(top_idx)


TOKEN_UNROLL = 2


def _for_each_token_k(n_tok, fn):
    def body(g, c):
        for u in range(TOKEN_UNROLL):
            for k in range(TOP_K):
                fn(g * TOKEN_UNROLL + u, k)
        return c

    lax.fori_loop(0, n_tok // TOKEN_UNROLL, body, 0)


def _row_tile(ref, row):
    return ref.at[pl.ds(pl.multiple_of(row * ROW_TILES, ROW_TILES), ROW_TILES), :]


def _dispatch_kernel(dest_ref, hf_ref, xs_in_ref, xs_ref, sem, *, tt):
    del xs_in_ref

    def row_copy(t, k):
        return pltpu.make_async_copy(_row_tile(hf_ref, t),
                                     _row_tile(xs_ref, dest_ref[t * TOP_K + k]), sem)

    _for_each_token_k(tt, lambda t, k: row_copy(t, k).start())
    _for_each_token_k(tt, lambda t, k: row_copy(t, k).wait())


def _dispatch(dest_flat, hf_tm, n_slots):
    T = hf_tm.shape[0] // ROW_TILES
    tt = min(512, T)
    kern = functools.partial(_dispatch_kernel, tt=tt)
    zeros = jnp.zeros((n_slots * ROW_TILES, LANES), F32)
    return pl.pallas_call(
        kern,
        grid=(T // tt,),
        in_specs=[pl.BlockSpec((tt * TOP_K,), lambda i: (i,), memory_space=pltpu.SMEM),
                  pl.BlockSpec((tt * ROW_TILES, LANES), lambda i: (i, 0)),
                  pl.BlockSpec(memory_space=pl.ANY)],
        out_specs=pl.BlockSpec(memory_space=pl.ANY),
        out_shape=jax.ShapeDtypeStruct((n_slots * ROW_TILES, LANES), F32),
        scratch_shapes=[pltpu.SemaphoreType.DMA(())],
        input_output_aliases={2: 0},
        compiler_params=_params(("arbitrary",)),
        name="dispatch",
    )(dest_flat, hf_tm, zeros)


def _expert_kernel(be_ref, bsrc_ref, act_ref, x_ref, wg_ref, bg_ref, wu_ref, bu_ref, wd_ref, bd_ref, o_ref,
                   *, blk):
    del be_ref, bsrc_ref

    @pl.when(act_ref[pl.program_id(0)] == 1)
    def _():
        x = _load_token_major(x_ref, 0, blk).astype(BF16)
        g = jnp.dot(x, wg_ref[0], preferred_element_type=F32) + bg_ref[0]
        u = jnp.dot(x, wu_ref[0], preferred_element_type=F32) + bu_ref[0]
        g = jnp.minimum(g, SWIGLU_LIMIT)
        u = jnp.clip(u, -SWIGLU_LIMIT, SWIGLU_LIMIT)
        act = (u + 1.0) * (g * jax.nn.sigmoid(SWIGLU_ALPHA * g))
        y = jnp.dot(act.astype(BF16), wd_ref[0], preferred_element_type=F32) + bd_ref[0]
        _store_token_major(o_ref, y)


def _experts(x_slots, blk, block_expert, block_src, block_active, wg, bg, wu, bu, wd, bd):
    D = D_MODEL
    nb = x_slots.shape[0] // (blk * ROW_TILES)
    wspec = pl.BlockSpec((1, D, D_FF), lambda i, be, bs, ac: (be[i], 0, 0))
    bspec = pl.BlockSpec((1, 1, D_FF), lambda i, be, bs, ac: (be[i], 0, 0))
    xspec = pl.BlockSpec((blk * ROW_TILES, LANES), lambda i, be, bs, ac: (bs[i], 0))
    grid_spec = pltpu.PrefetchScalarGridSpec(
        num_scalar_prefetch=3,
        grid=(nb,),
        in_specs=[xspec, wspec, bspec, wspec, bspec, wspec, bspec],
        out_specs=xspec,
    )
    b3 = lambda b: b.reshape(N_EXPERTS, 1, -1)
    return pl.pallas_call(
        functools.partial(_expert_kernel, blk=blk),
        grid_spec=grid_spec,
        out_shape=jax.ShapeDtypeStruct(x_slots.shape, F32),
        compiler_params=_params(("arbitrary",)),
        name="experts",
    )(block_expert, block_src, block_active, x_slots,
      wg.astype(BF16), b3(bg), wu.astype(BF16), b3(bu), wd.astype(BF16), b3(bd))


def _combine_kernel(dest_ref, ys_ref, w_ref, x2_ref, g_ref, o_ref, buf, sem, *, tt):
    def row_copy(t, k):
        return pltpu.make_async_copy(_row_tile(ys_ref, dest_ref[t * TOP_K + k]),
                                     _row_tile(buf, k * tt + t), sem)

    _for_each_token_k(tt, lambda t, k: row_copy(t, k).start())
    _for_each_token_k(tt, lambda t, k: row_copy(t, k).wait())
    w = w_ref[...]
    y = x2_ref[...]
    for k in range(TOP_K):
        y = y + w[:, k:k + 1] * _load_token_major(buf, k * tt, tt)
    inv = lax.rsqrt(jnp.mean(y * y, axis=-1, keepdims=True) + EPS)
    o_ref[...] = y * inv * g_ref[...]


def _combine(dest_flat, y_slots, top_w, x2, g_final):
    T, D = x2.shape
    tt = min(256, T)
    kern = functools.partial(_combine_kernel, tt=tt)
    return pl.pallas_call(
        kern,
        grid=(T // tt,),
        in_specs=[pl.BlockSpec((tt * TOP_K,), lambda i: (i,), memory_space=pltpu.SMEM),
                  pl.BlockSpec(memory_space=pl.ANY),
                  pl.BlockSpec((tt, LANES), lambda i: (i, 0)),
                  pl.BlockSpec((tt, D), lambda i: (i, 0)),
                  pl.BlockSpec((1, D), lambda i: (0, 0))],
        out_specs=pl.BlockSpec((tt, D), lambda i: (i, 0)),
        out_shape=jax.ShapeDtypeStruct((T, D), F32),
        scratch_shapes=[pltpu.VMEM((tt * TOP_K * ROW_TILES, LANES), F32), pltpu.SemaphoreType.DMA(())],
        compiler_params=_params(("arbitrary",)),
        name="combine",
    )(dest_flat, y_slots, top_w, x2, g_final.reshape(1, D))


def _moe_block_size(T):
    return min(512, max(SUBLANES, T * TOP_K // N_EXPERTS))


def _block_tables(counts, blk, nb):
    cnt = counts.astype(I32)
    padded = (cnt + blk - 1) // blk * blk
    ends = jnp.cumsum(padded)
    used = ends[-1] // blk
    starts = jnp.arange(nb, dtype=I32) * blk
    src = jnp.minimum(jnp.arange(nb, dtype=I32), jnp.maximum(used - 1, 0))
    expert = jnp.minimum(jnp.sum((starts[:, None] >= ends[None, :]).astype(I32), axis=1), N_EXPERTS - 1)
    expert = expert[src]
    active = (jnp.arange(nb, dtype=I32) < used).astype(I32)
    return expert, src, active


def kernel(x, norm_mix_g, w_in, conv_w, conv_b, w_rg_a, b_rg_a, w_rg_x, b_rg_x, lru_lambda, diff_lambda_q1, diff_lambda_k1, diff_lambda_q2, diff_lambda_k2, subln_g, rel_bias_table, w_proj_rnn, w_proj_att, w_out, norm_ffn_g, w_router, b_router, w_gate_e, b_gate_e, w_up_e, b_up_e, w_down_e, b_down_e, norm_final_g):
    B, S, D = x.shape
    T = B * S
    assert norm_mix_g.shape[0] == 1, "single-layer block: the final norm is fused into the MoE combine"
    l = 0
    xt = x.reshape(T, D)
    lam_init = 0.8 - 0.6 * math.exp(-0.3 * l)
    proj = _inproj(xt, norm_mix_g[l], w_in[l].astype(BF16))
    y_rnn = _rglru(proj, B, S, conv_w[l], conv_b[l], w_rg_a[l], b_rg_a[l], w_rg_x[l], b_rg_x[l],
                   lru_lambda[l])
    y_att = _attention(proj, B, S, diff_lambda_q1[l], diff_lambda_k1[l], diff_lambda_q2[l],
                       diff_lambda_k2[l], subln_g[l], rel_bias_table, lam_init)
    x2, hf, top_idx, top_w = _merge_router(xt, y_rnn, y_att, proj, w_proj_rnn[l], w_proj_att[l],
                                           w_out[l], norm_ffn_g[l], w_router[l], b_router[l])
    blk = _moe_block_size(T)
    nb = T * TOP_K // blk + N_EXPERTS
    dest, counts = _route(top_idx, blk)
    dest_flat = dest[:, :TOP_K].reshape(-1)
    expert, src, active = _block_tables(counts[0, :N_EXPERTS], blk, nb)
    x_slots = _dispatch(dest_flat, hf, nb * blk)
    y_slots = _experts(x_slots, blk, expert, src, active, w_gate_e[l], b_gate_e[l], w_up_e[l],
                       b_up_e[l], w_down_e[l], b_down_e[l])
    out = _combine(dest_flat, y_slots, top_w, x2, norm_final_g)
    return out.reshape(B, S, D)
```

```python
import functools
import math

import jax
import jax.numpy as jnp
from jax import lax
from jax.experimental import pallas as pl
from jax.experimental.pallas import tpu as pltpu

F32 = jnp.float32
BF16 = jnp.bfloat16
I32 = jnp.int32

D_MODEL = 1024
D_RNN = 1024
RNN_BLOCKS = 16
RNN_BLOCK = D_RNN // RNN_BLOCKS
CONV_W = 4
RGLRU_C = 8.0
N_HEADS = 8
HEAD_DIM = 64
V_DIM = 2 * HEAD_DIM
ATT_QK = N_HEADS * 2 * HEAD_DIM
ATT_V = N_HEADS * V_DIM
D_IN = 2 * D_RNN + 2 * ATT_QK + ATT_V + 2 * D_MODEL
NUM_BUCKETS = 32
MAX_EXACT = NUM_BUCKETS // 2
MAX_DISTANCE = 128
N_EXPERTS = 32
TOP_K = 4
D_FF = D_MODEL
SWIGLU_LIMIT = 7.0
SWIGLU_ALPHA = 1.702
EPS = 1e-6
NEG_INF = -1e30

LANES = 128
SUBLANES = 8
MXU_DIM = 256
VMEM_LIMIT = 56 * 1024 * 1024

COL_XR, COL_GR, COL_Q, COL_K, COL_V, COL_G0, COL_G1 = range(7)


def _params(sem, vmem=VMEM_LIMIT):
    return pltpu.CompilerParams(dimension_semantics=sem, vmem_limit_bytes=vmem)


ROW_TILES = D_MODEL // LANES
assert ROW_TILES == SUBLANES


def _store_token_major(ref, val):
    n = val.shape[0]
    for c in range(ROW_TILES):
        ref[pl.ds(c, n, stride=ROW_TILES), :] = val[:, c * LANES:(c + 1) * LANES]


def _load_token_major(ref, start, n):
    return jnp.concatenate(
        [ref[pl.ds(start * ROW_TILES + c, n, stride=ROW_TILES), :] for c in range(ROW_TILES)], axis=-1)


def _inproj_kernel(x_ref, g_ref, w_ref, o_ref, h_scr):
    @pl.when(pl.program_id(1) == 0)
    def _():
        x = x_ref[...]
        inv = lax.rsqrt(jnp.mean(x * x, axis=-1, keepdims=True) + EPS)
        h_scr[...] = (x * inv * g_ref[...]).astype(BF16)

    o_ref[...] = jnp.dot(h_scr[...], w_ref[...], preferred_element_type=F32).astype(o_ref.dtype)


def _inproj(x2d, g, w_bf16):
    T, D = x2d.shape
    N = w_bf16.shape[1]
    tm = min(1024, T)
    tn = 1024
    return pl.pallas_call(
        _inproj_kernel,
        grid=(T // tm, N // tn),
        in_specs=[
            pl.BlockSpec((tm, D), lambda i, j: (i, 0)),
            pl.BlockSpec((1, D), lambda i, j: (0, 0)),
            pl.BlockSpec((D, tn), lambda i, j: (0, j)),
        ],
        out_specs=pl.BlockSpec((tm, tn), lambda i, j: (i, j)),
        out_shape=jax.ShapeDtypeStruct((T, N), BF16),
        scratch_shapes=[pltpu.VMEM((tm, D), BF16)],
        compiler_params=_params(("parallel", "arbitrary")),
        name="inproj",
    )(x2d, g.reshape(1, D), w_bf16)


def _rglru_kernel(xr_ref, gr_ref, cw_ref, cb_ref, wa_ref, ba_ref, wx_ref, bx_ref, lam_ref,
                  o_ref, xbuf, a_scr, u_scr, h_scr):
    ts = xr_ref.shape[0]
    pad = SUBLANES

    @pl.when(pl.program_id(1) == 0)
    def _():
        xbuf[0:pad, :] = jnp.zeros((pad, D_RNN), F32)
        h_scr[...] = jnp.zeros_like(h_scr)

    x = xr_ref[...].astype(F32)
    xbuf[pad:pad + ts, :] = x
    cw = cw_ref[...]
    xc = (cw[3:4, :] * x
          + cw[2:3, :] * xbuf[pad - 1:pad - 1 + ts, :]
          + cw[1:2, :] * xbuf[pad - 2:pad - 2 + ts, :]
          + cw[0:1, :] * xbuf[pad - 3:pad - 3 + ts, :]) + cb_ref[...]
    xbuf[0:pad, :] = x[ts - pad:ts, :]

    xcb = xc.astype(BF16)
    nchunk = D_RNN // MXU_DIM
    r_pre = jnp.concatenate(
        [jnp.dot(xcb[:, c * MXU_DIM:(c + 1) * MXU_DIM], wa_ref[c], preferred_element_type=F32)
         for c in range(nchunk)], axis=-1)
    i_pre = jnp.concatenate(
        [jnp.dot(xcb[:, c * MXU_DIM:(c + 1) * MXU_DIM], wx_ref[c], preferred_element_type=F32)
         for c in range(nchunk)], axis=-1)
    r = jax.nn.sigmoid(r_pre + ba_ref[...])
    ig = jax.nn.sigmoid(i_pre + bx_ref[...])
    z = -lam_ref[...]
    softplus = jnp.maximum(z, 0.0) + jnp.log1p(jnp.exp(-jnp.abs(z)))
    log_a = (-RGLRU_C) * r * softplus
    a = jnp.exp(log_a)
    th = jnp.tanh(log_a)
    mult = jnp.sqrt(-2.0 * th / (1.0 - th))
    a_scr[...] = a
    u_scr[...] = mult * (ig * xc)

    row = lax.broadcasted_iota(I32, (SUBLANES, D_RNN), 0)

    def body(g, h):
        off = pl.multiple_of(g * SUBLANES, SUBLANES)
        a8 = a_scr[pl.ds(off, SUBLANES), :]
        u8 = u_scr[pl.ds(off, SUBLANES), :]
        for d in (1, 2, 4):
            a_sh = jnp.where(row >= d, pltpu.roll(a8, d, 0), 1.0)
            u_sh = jnp.where(row >= d, pltpu.roll(u8, d, 0), 0.0)
            u8 = a8 * u_sh + u8
            a8 = a8 * a_sh
        h8 = u8 + a8 * h
        u_scr[pl.ds(off, SUBLANES), :] = h8
        return h8[SUBLANES - 1:SUBLANES, :]

    h_scr[...] = lax.fori_loop(0, ts // SUBLANES, body, h_scr[...])
    gate = jax.nn.gelu(gr_ref[...].astype(F32), approximate=True)
    o_ref[...] = (u_scr[...] * gate).astype(o_ref.dtype)


def _block_diag_chunks(w):
    per = MXU_DIM // RNN_BLOCK
    w = w.reshape(D_RNN // MXU_DIM, per, RNN_BLOCK, RNN_BLOCK)
    eye = jnp.eye(per, dtype=w.dtype)
    out = jnp.einsum('gpcd,pq->gpcqd', w, eye)
    return out.reshape(D_RNN // MXU_DIM, MXU_DIM, MXU_DIM)


def _rglru(proj, B, S, conv_w, conv_b, w_a, b_a, w_x, b_x, lru_lambda):
    T = B * S
    ts = min(256, S)
    ns = S // ts
    wa = _block_diag_chunks(w_a).astype(BF16)
    wx = _block_diag_chunks(w_x).astype(BF16)
    nchunk = D_RNN // MXU_DIM
    row = lambda v: v.reshape(1, D_RNN)
    const2 = lambda b, s: (0, 0)
    return pl.pallas_call(
        _rglru_kernel,
        grid=(B, ns),
        in_specs=[
            pl.BlockSpec((ts, D_RNN), lambda b, s: (b * ns + s, COL_XR)),
            pl.BlockSpec((ts, D_RNN), lambda b, s: (b * ns + s, COL_GR)),
            pl.BlockSpec((CONV_W, D_RNN), const2),
            pl.BlockSpec((1, D_RNN), const2),
            pl.BlockSpec((nchunk, MXU_DIM, MXU_DIM), lambda b, s: (0, 0, 0)),
            pl.BlockSpec((1, D_RNN), const2),
            pl.BlockSpec((nchunk, MXU_DIM, MXU_DIM), lambda b, s: (0, 0, 0)),
            pl.BlockSpec((1, D_RNN), const2),
            pl.BlockSpec((1, D_RNN), const2),
        ],
        out_specs=pl.BlockSpec((ts, D_RNN), lambda b, s: (b * ns + s, 0)),
        out_shape=jax.ShapeDtypeStruct((T, D_RNN), BF16),
        scratch_shapes=[
            pltpu.VMEM((ts + SUBLANES, D_RNN), F32),
            pltpu.VMEM((ts, D_RNN), F32),
            pltpu.VMEM((ts, D_RNN), F32),
            pltpu.VMEM((1, D_RNN), F32),
        ],
        compiler_params=_params(("parallel", "arbitrary")),
        name="rglru",
    )(proj, proj, conv_w, row(conv_b), wa, row(b_a), wx, row(b_x), row(lru_lambda))


def _attn_kernel(q_ref, k_ref, v_ref, bias_ref, lq1_ref, lk1_ref, lq2_ref, lk2_ref, sg_ref,
                 o_ref, qs_scr, vx_scr, s_scr, m_scr, acc_scr, *, tq, lam_init):
    S = q_ref.shape[0]
    nq = S // tq
    scale = HEAD_DIM ** -0.5 * LOG2E
    lane = lax.broadcasted_iota(I32, (tq, V_DIM), 1)
    lam = (jnp.exp(jnp.sum(lq1_ref[...] * lk1_ref[...], keepdims=True))
           - jnp.exp(jnp.sum(lq2_ref[...] * lk2_ref[...], keepdims=True)) + lam_init)
    vx_scr[:, 0:V_DIM] = v_ref[...]
    vx_scr[:, V_DIM:2 * V_DIM] = jnp.ones((S, V_DIM), BF16)

    def scores(j, dst):
        k = k_ref[pl.ds(pl.multiple_of(j * tq, tq), tq), :]
        s_scr[dst] = lax.dot_general(qs_scr[...], k, (((1,), (1,)), ((), ())),
                                     preferred_element_type=F32)

    def step(j, src, bias_idx, prefetch):
        if prefetch:
            scores(j + 1, 1 - src)
        s = s_scr[src]
        if bias_idx is not None:
            b = bias_ref[bias_idx]
            s = s + jnp.concatenate([b, b], axis=0)
        vx = vx_scr[pl.ds(pl.multiple_of(j * tq, tq), tq), :]
        m_prev = m_scr[...]
        m_new = jnp.maximum(m_prev, jnp.max(s, axis=1)[:, None])
        p = jnp.exp2(s - jnp.tile(m_new, (1, tq // LANES)))
        alpha = jnp.exp2(m_prev - m_new)
        acc_scr[...] = (jnp.tile(alpha, (1, 2)) * acc_scr[...]
                        + jnp.dot(p.astype(BF16), vx, preferred_element_type=F32))
        m_scr[...] = m_new

    def qblock(qi, carry):
        q0 = pl.multiple_of(qi * tq, tq)
        q = (q_ref[pl.ds(q0, tq), :].astype(F32) * scale).astype(BF16)
        zero = jnp.zeros_like(q)
        qs_scr[0:tq, :] = jnp.where(lane < HEAD_DIM, q, zero)
        qs_scr[tq:2 * tq, :] = jnp.where(lane >= HEAD_DIM, q, zero)
        m_scr[...] = jnp.full(m_scr.shape, NEG_INF, F32)
        acc_scr[...] = jnp.zeros(acc_scr.shape, F32)
        scores(0, 0)
        nfar = jnp.maximum(qi - 1, 0)

        def far_pair(i, c):
            step(2 * i, 0, None, True)
            step(2 * i + 1, 1, None, True)
            return c

        lax.fori_loop(0, nfar // 2, far_pair, 0)
        odd = nfar % 2

        @pl.when(odd == 1)
        def _():
            step(nfar - 1, 0, None, True)

        @pl.when(qi == 0)
        def _():
            step(qi, 0, 0, False)

        for par in (0, 1):
            @pl.when((qi >= 1) & (odd == par))
            def _(par=par):
                step(qi - 1, par, 1, True)
                step(qi, 1 - par, 0, False)

        acc = acc_scr[...]
        o_all = acc[:, 0:V_DIM] / acc[:, V_DIM:2 * V_DIM]
        o = o_all[0:tq, :] - lam * o_all[tq:2 * tq, :]
        inv = lax.rsqrt(jnp.mean(o * o, axis=-1, keepdims=True) + EPS)
        y = (o * inv * sg_ref[...]) * (1.0 - lam_init)
        o_ref[pl.ds(q0, tq), :] = y.astype(o_ref.dtype)
        return carry

    lax.fori_loop(0, nq, qblock, 0)


def _rel_bucket(n):
    n = jnp.maximum(n, 0)
    nf = jnp.maximum(n, MAX_EXACT).astype(F32)
    large = MAX_EXACT + (jnp.log(nf / MAX_EXACT) / math.log(MAX_DISTANCE / MAX_EXACT)
                         * (NUM_BUCKETS - MAX_EXACT)).astype(I32)
    large = jnp.minimum(large, NUM_BUCKETS - 1)
    return jnp.where(n < MAX_EXACT, n, large)


def _bias_tiles(rel_table, tq):
    i = jnp.arange(tq, dtype=I32)[:, None]
    j = jnp.arange(tq, dtype=I32)[None, :]
    table = rel_table.astype(F32) - rel_table[NUM_BUCKETS - 1].astype(F32)[None, :]
    tiles = []
    for delta in (0, tq):
        n = i - j + delta
        onehot = (_rel_bucket(n)[:, :, None] == jnp.arange(NUM_BUCKETS, dtype=I32)).astype(F32)
        b = jnp.einsum('ijb,bh->hij', onehot, table, precision=lax.Precision.HIGHEST) * LOG2E
        tiles.append(jnp.where((n >= 0)[None], b, NEG_INF))
    return jnp.stack(tiles, axis=1)


ATT_TQ = 512
LOG2E = math.log2(math.e)


def _attention(proj, B, S, lq1, lk1, lq2, lk2, subln_g, rel_table, lam_init):
    T = B * S
    tq = min(ATT_TQ, S)
    assert V_DIM == LANES and MAX_DISTANCE <= tq and S % tq == 0
    bias = _bias_tiles(rel_table, tq)
    vec = lambda v: v.reshape(1, -1).astype(F32)
    const2 = lambda b, h: (0, 0)
    kern = functools.partial(_attn_kernel, tq=tq, lam_init=lam_init)
    return pl.pallas_call(
        kern,
        grid=(B, N_HEADS),
        in_specs=[
            pl.BlockSpec((S, V_DIM), lambda b, h: (b, COL_Q * N_HEADS + h)),
            pl.BlockSpec((S, V_DIM), lambda b, h: (b, COL_K * N_HEADS + h)),
            pl.BlockSpec((S, V_DIM), lambda b, h: (b, COL_V * N_HEADS + h)),
            pl.BlockSpec((None, 2, tq, tq), lambda b, h: (h, 0, 0, 0)),
            pl.BlockSpec((1, HEAD_DIM), const2),
            pl.BlockSpec((1, HEAD_DIM), const2),
            pl.BlockSpec((1, HEAD_DIM), const2),
            pl.BlockSpec((1, HEAD_DIM), const2),
            pl.BlockSpec((1, V_DIM), const2),
        ],
        out_specs=pl.BlockSpec((S, V_DIM), lambda b, h: (b, h)),
        out_shape=jax.ShapeDtypeStruct((T, ATT_V), BF16),
        scratch_shapes=[
            pltpu.VMEM((2 * tq, V_DIM), BF16),
            pltpu.VMEM((S, 2 * V_DIM), BF16),
            pltpu.VMEM((2, 2 * tq, tq), F32),
            pltpu.VMEM((2 * tq, LANES), F32),
            pltpu.VMEM((2 * tq, 2 * V_DIM), F32),
        ],
        compiler_params=_params(("parallel", "arbitrary")),
        name="diff_attn",
    )(proj, proj, proj, bias, vec(lq1), vec(lk1), vec(lq2), vec(lk2), vec(subln_g))


def _merge_kernel(x_ref, yr_ref, ya_ref, g0_ref, g1_ref, wr_ref, wa_ref, wo_ref, gf_ref, wrt_ref, brt_ref,
                  x2_ref, hf_ref, idx_ref, wgt_ref):
    pr = jnp.dot(yr_ref[...], wr_ref[...], preferred_element_type=F32)
    pa = jnp.dot(ya_ref[...], wa_ref[...], preferred_element_type=F32)
    merged = (jax.nn.sigmoid(g0_ref[...].astype(F32)) * pr
              + jax.nn.sigmoid(g1_ref[...].astype(F32)) * pa)
    x2 = x_ref[...] + jnp.dot(merged.astype(BF16), wo_ref[...], preferred_element_type=F32)
    x2_ref[...] = x2
    inv = lax.rsqrt(jnp.mean(x2 * x2, axis=-1, keepdims=True) + EPS)
    hf = x2 * inv * gf_ref[...]
    _store_token_major(hf_ref, hf)
    hi = hf.astype(BF16)
    lo = (hf - hi.astype(F32)).astype(BF16)
    hw = jnp.dot(hi, wrt_ref[...], preferred_element_type=F32)
    lw = jnp.dot(lo, wrt_ref[:, 0:LANES], preferred_element_type=F32)
    logits = hw[:, 0:LANES] + hw[:, LANES:2 * LANES] + lw + brt_ref[...]
    tm = logits.shape[0]
    lane = lax.broadcasted_iota(I32, (tm, LANES), 1)
    work = jnp.where(lane < N_EXPERTS, logits, -jnp.inf)
    idx_out = jnp.zeros((tm, LANES), I32)
    val_out = jnp.full((tm, LANES), -jnp.inf, F32)
    for k in range(TOP_K):
        mx = jnp.max(work, axis=-1, keepdims=True)
        sel = jnp.min(jnp.where(work == mx, lane, LANES), axis=-1, keepdims=True)
        idx_out = jnp.where(lane == k, sel, idx_out)
        val_out = jnp.where(lane == k, mx, val_out)
        work = jnp.where(lane == sel, -jnp.inf, work)
    e = jnp.exp(val_out - jnp.max(val_out, axis=-1, keepdims=True))
    idx_ref[...] = idx_out
    wgt_ref[...] = e / jnp.sum(e, axis=-1, keepdims=True)


def _merge_router(x2d, y_rnn, y_att, proj, w_pr, w_pa, w_o, g_ffn, w_router, b_router):
    T, D = x2d.shape
    tm = min(512, T)
    w_hi = w_router.astype(BF16)
    w_lo = (w_router - w_hi.astype(F32)).astype(BF16)
    wrt = (jnp.zeros((D, 2 * LANES), BF16).at[:, :N_EXPERTS].set(w_hi)
           .at[:, LANES:LANES + N_EXPERTS].set(w_lo))
    brt = jnp.zeros((1, LANES), F32).at[0, :N_EXPERTS].set(b_router)
    rowblk = lambda c: pl.BlockSpec((tm, D), lambda i, c=c: (i, c))
    full = lambda a: pl.BlockSpec(a.shape, lambda i: (0,) * a.ndim)
    wr, wa, wo = w_pr.astype(BF16), w_pa.astype(BF16), w_o.astype(BF16)
    gf = g_ffn.reshape(1, D)
    return pl.pallas_call(
        _merge_kernel,
        grid=(T // tm,),
        in_specs=[rowblk(0), rowblk(0), rowblk(0), rowblk(COL_G0), rowblk(COL_G1),
                  full(wr), full(wa), full(wo), full(gf), full(wrt), full(brt)],
        out_specs=[rowblk(0), pl.BlockSpec((tm * ROW_TILES, LANES), lambda i: (i, 0)),
                   pl.BlockSpec((tm, LANES), lambda i: (i, 0)),
                   pl.BlockSpec((tm, LANES), lambda i: (i, 0))],
        out_shape=[jax.ShapeDtypeStruct((T, D), F32), jax.ShapeDtypeStruct((T * ROW_TILES, LANES), F32),
                   jax.ShapeDtypeStruct((T, LANES), I32), jax.ShapeDtypeStruct((T, LANES), F32)],
        compiler_params=_params(("parallel",)),
        name="merge_router",
    )(x2d, y_rnn, y_att, proj, proj, wr, wa, wo, gf, wrt, brt)


def _route_kernel(idx_ref, dest_ref, cnt_ref, cnt_scr, run_scr, start_scr, *, blk):
    ph = pl.program_id(0)
    i = pl.program_id(1)
    tt = idx_ref.shape[0]
    idx = idx_ref[...]
    lane = lax.broadcasted_iota(I32, (tt, LANES), 1)
    onehot = jnp.zeros((tt, LANES), F32)
    for k in range(TOP_K):
        onehot = onehot + (idx[:, k:k + 1] == lane).astype(F32)

    @pl.when((ph == 0) & (i == 0))
    def _():
        cnt_scr[...] = jnp.zeros_like(cnt_scr)

    @pl.when(ph == 0)
    def _():
        cnt_scr[...] += jnp.sum(onehot, axis=0, keepdims=True)

    @pl.when((ph == 1) & (i == 0))
    def _():
        cnt = jnp.broadcast_to(cnt_scr[...], (SUBLANES, LANES))
        padded = jnp.floor((cnt + (blk - 1)) / blk) * blk
        l8 = lax.broadcasted_iota(I32, (SUBLANES, LANES), 1)
        incl = padded
        d = 1
        while d < LANES:
            incl = incl + jnp.where(l8 >= d, pltpu.roll(incl, d, 1), 0.0)
            d *= 2
        start_scr[...] = (incl - padded)[0:1, :]
        run_scr[...] = jnp.zeros_like(run_scr)

    @pl.when(ph == 1)
    def _():
        r = lax.broadcasted_iota(I32, (tt, tt), 0)
        c = lax.broadcasted_iota(I32, (tt, tt), 1)
        tri = (c < r).astype(BF16)
        before = jnp.dot(tri, onehot.astype(BF16), preferred_element_type=F32)
        base = before + run_scr[...] + start_scr[...]
        out = jnp.zeros((tt, LANES), F32)
        for k in range(TOP_K):
            dk = jnp.sum(jnp.where(idx[:, k:k + 1] == lane, base, 0.0), axis=-1, keepdims=True)
            out = jnp.where(lane == k, dk, out)
        dest_ref[...] = out.astype(I32)
        run_scr[...] += jnp.sum(onehot, axis=0, keepdims=True)
        cnt_ref[...] = jnp.broadcast_to(cnt_scr[...], cnt_ref.shape)


def _route(top_idx, blk):
    T = top_idx.shape[0]
    tt = min(512, T)
    kern = functools.partial(_route_kernel, blk=blk)
    return pl.pallas_call(
        kern,
        grid=(2, T // tt),
        in_specs=[pl.BlockSpec((tt, LANES), lambda p, i: (i, 0))],
        out_specs=[pl.BlockSpec((tt, LANES), lambda p, i: (p * i, 0)),
                   pl.BlockSpec((SUBLANES, LANES), lambda p, i: (0, 0))],
        out_shape=[jax.ShapeDtypeStruct((T, LANES), I32),
                   jax.ShapeDtypeStruct((SUBLANES, LANES), F32)],
        scratch_shapes=[pltpu.VMEM((1, LANES), F32), pltpu.VMEM((1, LANES), F32),
                        pltpu.VMEM((1, LANES), F32)],
        compiler_params=_params(("arbitrary", "arbitrary")),
        name="route",
    )(top_idx)


TOKEN_UNROLL = 2


def _for_each_token_k(n_tok, fn):
    def body(g, c):
        for u in range(TOKEN_UNROLL):
            for k in range(TOP_K):
                fn(g * TOKEN_UNROLL + u, k)
        return c

    lax.fori_loop(0, n_tok // TOKEN_UNROLL, body, 0)


def _row_tile(ref, row):
    return ref.at[pl.ds(pl.multiple_of(row * ROW_TILES, ROW_TILES), ROW_TILES), :]


def _dispatch_kernel(ends_ref, dest_ref, hf_ref, xs_ref, zero_buf, sem, zsem, *, tt, blk):
    @pl.when(pl.program_id(0) == 0)
    def _():
        zero_buf[...] = jnp.zeros_like(zero_buf)

        def tail_fill(e):
            prev_end = ends_ref[e - 1] if e > 0 else 0
            start = pl.multiple_of((ends_ref[e] - blk) * ROW_TILES, ROW_TILES)
            return ends_ref[e] > prev_end, pltpu.make_async_copy(
                zero_buf, xs_ref.at[pl.ds(start, blk * ROW_TILES), :], zsem)

        for e in range(N_EXPERTS):
            nonempty, cp = tail_fill(e)
            pl.when(nonempty)(cp.start)
        for e in range(N_EXPERTS):
            nonempty, cp = tail_fill(e)
            pl.when(nonempty)(cp.wait)

    def row_copy(t, k):
        return pltpu.make_async_copy(_row_tile(hf_ref, t),
                                     _row_tile(xs_ref, dest_ref[t * TOP_K + k]), sem)

    _for_each_token_k(tt, lambda t, k: row_copy(t, k).start())
    _for_each_token_k(tt, lambda t, k: row_copy(t, k).wait())


def _dispatch(ends, dest_flat, hf_tm, n_slots, blk):
    T = hf_tm.shape[0] // ROW_TILES
    tt = min(512, T)
    kern = functools.partial(_dispatch_kernel, tt=tt, blk=blk)
    grid_spec = pltpu.PrefetchScalarGridSpec(
        num_scalar_prefetch=1,
        grid=(T // tt,),
        in_specs=[pl.BlockSpec((tt * TOP_K,), lambda i, ends: (i,), memory_space=pltpu.SMEM),
                  pl.BlockSpec((tt * ROW_TILES, LANES), lambda i, ends: (i, 0))],
        out_specs=pl.BlockSpec(memory_space=pl.ANY),
        scratch_shapes=[pltpu.VMEM((blk * ROW_TILES, LANES), F32),
                        pltpu.SemaphoreType.DMA(()), pltpu.SemaphoreType.DMA(())],
    )
    return pl.pallas_call(
        kern,
        grid_spec=grid_spec,
        out_shape=jax.ShapeDtypeStruct((n_slots * ROW_TILES, LANES), F32),
        compiler_params=_params(("arbitrary",)),
        name="dispatch",
    )(ends, dest_flat, hf_tm)


def _expert_kernel(be_ref, bsrc_ref, act_ref, first_ref, x_ref, wg_ref, bg_ref, wu_ref, bu_ref, wd_ref, bd_ref,
                   o_ref, wg_scr, wu_scr, wd_scr, *, blk):
    del be_ref, bsrc_ref
    i = pl.program_id(0)

    @pl.when(first_ref[i] == 1)
    def _():
        wg_scr[...] = wg_ref[0].astype(BF16)
        wu_scr[...] = wu_ref[0].astype(BF16)
        wd_scr[...] = wd_ref[0].astype(BF16)

    @pl.when(act_ref[i] == 1)
    def _():
        x = _load_token_major(x_ref, 0, blk).astype(BF16)
        g = jnp.dot(x, wg_scr[...], preferred_element_type=F32) + bg_ref[0]
        u = jnp.dot(x, wu_scr[...], preferred_element_type=F32) + bu_ref[0]
        g = jnp.minimum(g, SWIGLU_LIMIT)
        u = jnp.clip(u, -SWIGLU_LIMIT, SWIGLU_LIMIT)
        act = (u + 1.0) * (g * jax.nn.sigmoid(SWIGLU_ALPHA * g))
        y = jnp.dot(act.astype(BF16), wd_scr[...], preferred_element_type=F32) + bd_ref[0]
        _store_token_major(o_ref, y)


def _experts(x_slots, blk, block_expert, block_src, block_active, block_first, wg, bg, wu, bu, wd, bd):
    D = D_MODEL
    nb = x_slots.shape[0] // (blk * ROW_TILES)
    wspec = pl.BlockSpec((1, D, D_FF), lambda i, be, bs, ac, fi: (be[i], 0, 0))
    bspec = pl.BlockSpec((1, 1, D_FF), lambda i, be, bs, ac, fi: (be[i], 0, 0))
    xspec = pl.BlockSpec((blk * ROW_TILES, LANES), lambda i, be, bs, ac, fi: (bs[i], 0))
    grid_spec = pltpu.PrefetchScalarGridSpec(
        num_scalar_prefetch=4,
        grid=(nb,),
        in_specs=[xspec, wspec, bspec, wspec, bspec, wspec, bspec],
        out_specs=xspec,
        scratch_shapes=[pltpu.VMEM((D, D_FF), BF16), pltpu.VMEM((D, D_FF), BF16),
                        pltpu.VMEM((D_FF, D), BF16)],
    )
    b3 = lambda b: b.reshape(N_EXPERTS, 1, -1)
    return pl.pallas_call(
        functools.partial(_expert_kernel, blk=blk),
        grid_spec=grid_spec,
        out_shape=jax.ShapeDtypeStruct(x_slots.shape, F32),
        compiler_params=_params(("arbitrary",)),
        name="experts",
    )(block_expert, block_src, block_active, block_first, x_slots,
      wg, b3(bg), wu, b3(bu), wd, b3(bd))


def _combine_kernel(dest_ref, ys_ref, w_ref, x2_ref, g_ref, o_ref, buf, sem, *, tt):
    def row_copy(t, k):
        return pltpu.make_async_copy(_row_tile(ys_ref, dest_ref[t * TOP_K + k]),
                                     _row_tile(buf, k * tt + t), sem)

    _for_each_token_k(tt, lambda t, k: row_copy(t, k).start())
    _for_each_token_k(tt, lambda t, k: row_copy(t, k).wait())
    w = w_ref[...]
    y = x2_ref[...]
    for k in range(TOP_K):
        y = y + w[:, k:k + 1] * _load_token_major(buf, k * tt, tt)
    inv = lax.rsqrt(jnp.mean(y * y, axis=-1, keepdims=True) + EPS)
    o_ref[...] = y * inv * g_ref[...]


def _combine(dest_flat, y_slots, top_w, x2, g_final):
    T, D = x2.shape
    tt = min(256, T)
    kern = functools.partial(_combine_kernel, tt=tt)
    return pl.pallas_call(
        kern,
        grid=(T // tt,),
        in_specs=[pl.BlockSpec((tt * TOP_K,), lambda i: (i,), memory_space=pltpu.SMEM),
                  pl.BlockSpec(memory_space=pl.ANY),
                  pl.BlockSpec((tt, LANES), lambda i: (i, 0)),
                  pl.BlockSpec((tt, D), lambda i: (i, 0)),
                  pl.BlockSpec((1, D), lambda i: (0, 0))],
        out_specs=pl.BlockSpec((tt, D), lambda i: (i, 0)),
        out_shape=jax.ShapeDtypeStruct((T, D), F32),
        scratch_shapes=[pltpu.VMEM((tt * TOP_K * ROW_TILES, LANES), F32), pltpu.SemaphoreType.DMA(())],
        compiler_params=_params(("arbitrary",)),
        name="combine",
    )(dest_flat, y_slots, top_w, x2, g_final.reshape(1, D))


def _moe_block_size(T):
    return min(512, max(SUBLANES, T * TOP_K // N_EXPERTS))


def _block_tables(counts, blk, nb):
    cnt = counts.astype(I32)
    padded = (cnt + blk - 1) // blk * blk
    ends = jnp.cumsum(padded)
    used = ends[-1] // blk
    starts = jnp.arange(nb, dtype=I32) * blk
    src = jnp.minimum(jnp.arange(nb, dtype=I32), jnp.maximum(used - 1, 0))
    expert = jnp.minimum(jnp.sum((starts[:, None] >= ends[None, :]).astype(I32), axis=1), N_EXPERTS - 1)
    expert = expert[src]
    active = (jnp.arange(nb, dtype=I32) < used).astype(I32)
    prev = jnp.concatenate([jnp.full((1,), -1, I32), expert[:-1]])
    first = active * (expert != prev).astype(I32)
    return ends.astype(I32), expert, src, active, first


def kernel(x, norm_mix_g, w_in, conv_w, conv_b, w_rg_a, b_rg_a, w_rg_x, b_rg_x, lru_lambda, diff_lambda_q1, diff_lambda_k1, diff_lambda_q2, diff_lambda_k2, subln_g, rel_bias_table, w_proj_rnn, w_proj_att, w_out, norm_ffn_g, w_router, b_router, w_gate_e, b_gate_e, w_up_e, b_up_e, w_down_e, b_down_e, norm_final_g):
    B, S, D = x.shape
    T = B * S
    assert norm_mix_g.shape[0] == 1, "single-layer block: the final norm is fused into the MoE combine"
    l = 0
    xt = x.reshape(T, D)
    lam_init = 0.8 - 0.6 * math.exp(-0.3 * l)
    proj = _inproj(xt, norm_mix_g[l], w_in[l].astype(BF16))
    y_rnn = _rglru(proj, B, S, conv_w[l], conv_b[l], w_rg_a[l], b_rg_a[l], w_rg_x[l], b_rg_x[l],
                   lru_lambda[l])
    y_att = _attention(proj, B, S, diff_lambda_q1[l], diff_lambda_k1[l], diff_lambda_q2[l],
                       diff_lambda_k2[l], subln_g[l], rel_bias_table, lam_init)
    x2, hf, top_idx, top_w = _merge_router(xt, y_rnn, y_att, proj, w_proj_rnn[l], w_proj_att[l],
                                           w_out[l], norm_ffn_g[l], w_router[l], b_router[l])
    blk = _moe_block_size(T)
    nb = T * TOP_K // blk + N_EXPERTS
    dest, counts = _route(top_idx, blk)
    dest_flat = dest[:, :TOP_K].reshape(-1)
    ends, expert, src, active, first = _block_tables(counts[0, :N_EXPERTS], blk, nb)
    x_slots = _dispatch(ends, dest_flat, hf, nb * blk, blk)
    y_slots = _experts(x_slots, blk, expert, src, active, first, w_gate_e[l], b_gate_e[l], w_up_e[l],
                       b_up_e[l], w_down_e[l], b_down_e[l])
    out = _combine(dest_flat, y_slots, top_w, x2, norm_final_g)
    return out.reshape(B, S, D)
```

```python
import functools
import math

import jax
import jax.numpy as jnp
from jax import lax
from jax.experimental import pallas as pl
from jax.experimental.pallas import tpu as pltpu

F32 = jnp.float32
BF16 = jnp.bfloat16
I32 = jnp.int32

D_MODEL = 1024
D_RNN = 1024
RNN_BLOCKS = 16
RNN_BLOCK = D_RNN // RNN_BLOCKS
CONV_W = 4
RGLRU_C = 8.0
N_HEADS = 8
HEAD_DIM = 64
V_DIM = 2 * HEAD_DIM
ATT_QK = N_HEADS * 2 * HEAD_DIM
ATT_V = N_HEADS * V_DIM
D_IN = 2 * D_RNN + 2 * ATT_QK + ATT_V + 2 * D_MODEL
NUM_BUCKETS = 32
MAX_EXACT = NUM_BUCKETS // 2
MAX_DISTANCE = 128
N_EXPERTS = 32
TOP_K = 4
D_FF = D_MODEL
SWIGLU_LIMIT = 7.0
SWIGLU_ALPHA = 1.702
EPS = 1e-6
NEG_INF = -1e30

LANES = 128
SUBLANES = 8
MXU_DIM = 256
VMEM_LIMIT = 56 * 1024 * 1024

COL_XR, COL_GR, COL_Q, COL_K, COL_V, COL_G0, COL_G1 = range(7)


def _params(sem, vmem=VMEM_LIMIT):
    return pltpu.CompilerParams(dimension_semantics=sem, vmem_limit_bytes=vmem)


ROW_TILES = D_MODEL // LANES
assert ROW_TILES == SUBLANES


def _store_token_major(ref, val):
    n = val.shape[0]
    for c in range(ROW_TILES):
        ref[pl.ds(c, n, stride=ROW_TILES), :] = val[:, c * LANES:(c + 1) * LANES]


def _load_token_major(ref, start, n):
    return jnp.concatenate(
        [ref[pl.ds(start * ROW_TILES + c, n, stride=ROW_TILES), :] for c in range(ROW_TILES)], axis=-1)


def _inproj_kernel(x_ref, g_ref, w_ref, o_ref, h_scr):
    @pl.when(pl.program_id(1) == 0)
    def _():
        x = x_ref[...]
        inv = lax.rsqrt(jnp.mean(x * x, axis=-1, keepdims=True) + EPS)
        h_scr[...] = (x * inv * g_ref[...]).astype(BF16)

    o_ref[...] = jnp.dot(h_scr[...], w_ref[...], preferred_element_type=F32).astype(o_ref.dtype)


def _inproj(x2d, g, w_bf16):
    T, D = x2d.shape
    N = w_bf16.shape[1]
    tm = min(1024, T)
    tn = 1024
    return pl.pallas_call(
        _inproj_kernel,
        grid=(T // tm, N // tn),
        in_specs=[
            pl.BlockSpec((tm, D), lambda i, j: (i, 0)),
            pl.BlockSpec((1, D), lambda i, j: (0, 0)),
            pl.BlockSpec((D, tn), lambda i, j: (0, j)),
        ],
        out_specs=pl.BlockSpec((tm, tn), lambda i, j: (i, j)),
        out_shape=jax.ShapeDtypeStruct((T, N), BF16),
        scratch_shapes=[pltpu.VMEM((tm, D), BF16)],
        compiler_params=_params(("parallel", "arbitrary")),
        name="inproj",
    )(x2d, g.reshape(1, D), w_bf16)


def _rglru_kernel(xr_ref, gr_ref, cw_ref, cb_ref, wa_ref, ba_ref, wx_ref, bx_ref, lam_ref,
                  o_ref, xbuf, a_scr, u_scr, h_scr):
    ts = xr_ref.shape[0]
    pad = SUBLANES

    @pl.when(pl.program_id(1) == 0)
    def _():
        xbuf[0:pad, :] = jnp.zeros((pad, D_RNN), F32)
        h_scr[...] = jnp.zeros_like(h_scr)

    x = xr_ref[...].astype(F32)
    xbuf[pad:pad + ts, :] = x
    cw = cw_ref[...]
    xc = (cw[3:4, :] * x
          + cw[2:3, :] * xbuf[pad - 1:pad - 1 + ts, :]
          + cw[1:2, :] * xbuf[pad - 2:pad - 2 + ts, :]
          + cw[0:1, :] * xbuf[pad - 3:pad - 3 + ts, :]) + cb_ref[...]
    xbuf[0:pad, :] = x[ts - pad:ts, :]

    xcb = xc.astype(BF16)
    nchunk = D_RNN // MXU_DIM
    r_pre = jnp.concatenate(
        [jnp.dot(xcb[:, c * MXU_DIM:(c + 1) * MXU_DIM], wa_ref[c], preferred_element_type=F32)
         for c in range(nchunk)], axis=-1)
    i_pre = jnp.concatenate(
        [jnp.dot(xcb[:, c * MXU_DIM:(c + 1) * MXU_DIM], wx_ref[c], preferred_element_type=F32)
         for c in range(nchunk)], axis=-1)
    r = jax.nn.sigmoid(r_pre + ba_ref[...])
    ig = jax.nn.sigmoid(i_pre + bx_ref[...])
    z = -lam_ref[...]
    softplus = jnp.maximum(z, 0.0) + jnp.log1p(jnp.exp(-jnp.abs(z)))
    log_a = (-RGLRU_C) * r * softplus
    a = jnp.exp(log_a)
    th = jnp.tanh(log_a)
    mult = jnp.sqrt(-2.0 * th / (1.0 - th))
    a_scr[...] = a
    u_scr[...] = mult * (ig * xc)

    row = lax.broadcasted_iota(I32, (SUBLANES, D_RNN), 0)

    def body(g, h):
        off = pl.multiple_of(g * SUBLANES, SUBLANES)
        a8 = a_scr[pl.ds(off, SUBLANES), :]
        u8 = u_scr[pl.ds(off, SUBLANES), :]
        for d in (1, 2, 4):
            a_sh = jnp.where(row >= d, pltpu.roll(a8, d, 0), 1.0)
            u_sh = jnp.where(row >= d, pltpu.roll(u8, d, 0), 0.0)
            u8 = a8 * u_sh + u8
            a8 = a8 * a_sh
        h8 = u8 + a8 * h
        u_scr[pl.ds(off, SUBLANES), :] = h8
        return h8[SUBLANES - 1:SUBLANES, :]

    h_scr[...] = lax.fori_loop(0, ts // SUBLANES, body, h_scr[...])
    gate = jax.nn.gelu(gr_ref[...].astype(F32), approximate=True)
    o_ref[...] = (u_scr[...] * gate).astype(o_ref.dtype)


def _block_diag_chunks(w):
    per = MXU_DIM // RNN_BLOCK
    w = w.reshape(D_RNN // MXU_DIM, per, RNN_BLOCK, RNN_BLOCK)
    eye = jnp.eye(per, dtype=w.dtype)
    out = jnp.einsum('gpcd,pq->gpcqd', w, eye)
    return out.reshape(D_RNN // MXU_DIM, MXU_DIM, MXU_DIM)


def _rglru(proj, B, S, conv_w, conv_b, w_a, b_a, w_x, b_x, lru_lambda):
    T = B * S
    ts = min(256, S)
    ns = S // ts
    wa = _block_diag_chunks(w_a).astype(BF16)
    wx = _block_diag_chunks(w_x).astype(BF16)
    nchunk = D_RNN // MXU_DIM
    row = lambda v: v.reshape(1, D_RNN)
    const2 = lambda b, s: (0, 0)
    return pl.pallas_call(
        _rglru_kernel,
        grid=(B, ns),
        in_specs=[
            pl.BlockSpec((ts, D_RNN), lambda b, s: (b * ns + s, COL_XR)),
            pl.BlockSpec((ts, D_RNN), lambda b, s: (b * ns + s, COL_GR)),
            pl.BlockSpec((CONV_W, D_RNN), const2),
            pl.BlockSpec((1, D_RNN), const2),
            pl.BlockSpec((nchunk, MXU_DIM, MXU_DIM), lambda b, s: (0, 0, 0)),
            pl.BlockSpec((1, D_RNN), const2),
            pl.BlockSpec((nchunk, MXU_DIM, MXU_DIM), lambda b, s: (0, 0, 0)),
            pl.BlockSpec((1, D_RNN), const2),
            pl.BlockSpec((1, D_RNN), const2),
        ],
        out_specs=pl.BlockSpec((ts, D_RNN), lambda b, s: (b * ns + s, 0)),
        out_shape=jax.ShapeDtypeStruct((T, D_RNN), BF16),
        scratch_shapes=[
            pltpu.VMEM((ts + SUBLANES, D_RNN), F32),
            pltpu.VMEM((ts, D_RNN), F32),
            pltpu.VMEM((ts, D_RNN), F32),
            pltpu.VMEM((1, D_RNN), F32),
        ],
        compiler_params=_params(("parallel", "arbitrary")),
        name="rglru",
    )(proj, proj, conv_w, row(conv_b), wa, row(b_a), wx, row(b_x), row(lru_lambda))


def _attn_kernel(q_ref, k_ref, v_ref, bias_ref, lq1_ref, lk1_ref, lq2_ref, lk2_ref, sg_ref,
                 o_ref, qs_scr, vx_scr, s_scr, m_scr, acc_scr, *, tq, lam_init):
    S = q_ref.shape[0]
    nq = S // tq
    scale = HEAD_DIM ** -0.5 * LOG2E
    lane = lax.broadcasted_iota(I32, (tq, V_DIM), 1)
    lam = (jnp.exp(jnp.sum(lq1_ref[...] * lk1_ref[...], keepdims=True))
           - jnp.exp(jnp.sum(lq2_ref[...] * lk2_ref[...], keepdims=True)) + lam_init)
    vx_scr[:, 0:V_DIM] = v_ref[...]
    vx_scr[:, V_DIM:2 * V_DIM] = jnp.ones((S, V_DIM), BF16)

    def scores(j, dst):
        k = k_ref[pl.ds(pl.multiple_of(j * tq, tq), tq), :]
        s_scr[dst] = lax.dot_general(qs_scr[...], k, (((1,), (1,)), ((), ())),
                                     preferred_element_type=F32)

    def step(j, src, bias_idx, prefetch):
        if prefetch:
            scores(j + 1, 1 - src)
        s = s_scr[src]
        if bias_idx is not None:
            b = bias_ref[bias_idx]
            s = s + jnp.concatenate([b, b], axis=0)
        vx = vx_scr[pl.ds(pl.multiple_of(j * tq, tq), tq), :]
        m_prev = m_scr[...]
        m_new = jnp.maximum(m_prev, jnp.max(s, axis=1)[:, None])
        p = jnp.exp2(s - jnp.tile(m_new, (1, tq // LANES)))
        alpha = jnp.exp2(m_prev - m_new)
        acc_scr[...] = (jnp.tile(alpha, (1, 2)) * acc_scr[...]
                        + jnp.dot(p.astype(BF16), vx, preferred_element_type=F32))
        m_scr[...] = m_new

    def qblock(qi, carry):
        q0 = pl.multiple_of(qi * tq, tq)
        q = (q_ref[pl.ds(q0, tq), :].astype(F32) * scale).astype(BF16)
        zero = jnp.zeros_like(q)
        qs_scr[0:tq, :] = jnp.where(lane < HEAD_DIM, q, zero)
        qs_scr[tq:2 * tq, :] = jnp.where(lane >= HEAD_DIM, q, zero)
        m_scr[...] = jnp.full(m_scr.shape, NEG_INF, F32)
        acc_scr[...] = jnp.zeros(acc_scr.shape, F32)
        scores(0, 0)
        nfar = jnp.maximum(qi - 1, 0)

        def far_pair(i, c):
            step(2 * i, 0, None, True)
            step(2 * i + 1, 1, None, True)
            return c

        lax.fori_loop(0, nfar // 2, far_pair, 0)
        odd = nfar % 2

        @pl.when(odd == 1)
        def _():
            step(nfar - 1, 0, None, True)

        @pl.when(qi == 0)
        def _():
            step(qi, 0, 0, False)

        for par in (0, 1):
            @pl.when((qi >= 1) & (odd == par))
            def _(par=par):
                step(qi - 1, par, 1, True)
                step(qi, 1 - par, 0, False)

        acc = acc_scr[...]
        o_all = acc[:, 0:V_DIM] / acc[:, V_DIM:2 * V_DIM]
        o = o_all[0:tq, :] - lam * o_all[tq:2 * tq, :]
        inv = lax.rsqrt(jnp.mean(o * o, axis=-1, keepdims=True) + EPS)
        y = (o * inv * sg_ref[...]) * (1.0 - lam_init)
        o_ref[pl.ds(q0, tq), :] = y.astype(o_ref.dtype)
        return carry

    lax.fori_loop(0, nq, qblock, 0)


def _rel_bucket(n):
    n = jnp.maximum(n, 0)
    nf = jnp.maximum(n, MAX_EXACT).astype(F32)
    large = MAX_EXACT + (jnp.log(nf / MAX_EXACT) / math.log(MAX_DISTANCE / MAX_EXACT)
                         * (NUM_BUCKETS - MAX_EXACT)).astype(I32)
    large = jnp.minimum(large, NUM_BUCKETS - 1)
    return jnp.where(n < MAX_EXACT, n, large)


def _bias_tiles(rel_table, tq):
    i = jnp.arange(tq, dtype=I32)[:, None]
    j = jnp.arange(tq, dtype=I32)[None, :]
    table = rel_table.astype(F32) - rel_table[NUM_BUCKETS - 1].astype(F32)[None, :]
    tiles = []
    for delta in (0, tq):
        n = i - j + delta
        onehot = (_rel_bucket(n)[:, :, None] == jnp.arange(NUM_BUCKETS, dtype=I32)).astype(F32)
        b = jnp.einsum('ijb,bh->hij', onehot, table, precision=lax.Precision.HIGHEST) * LOG2E
        tiles.append(jnp.where((n >= 0)[None], b, NEG_INF))
    return jnp.stack(tiles, axis=1)


ATT_TQ = 512
LOG2E = math.log2(math.e)


def _attention(proj, B, S, lq1, lk1, lq2, lk2, subln_g, rel_table, lam_init):
    T = B * S
    tq = min(ATT_TQ, S)
    assert V_DIM == LANES and MAX_DISTANCE <= tq and S % tq == 0
    bias = _bias_tiles(rel_table, tq)
    vec = lambda v: v.reshape(1, -1).astype(F32)
    const2 = lambda b, h: (0, 0)
    kern = functools.partial(_attn_kernel, tq=tq, lam_init=lam_init)
    return pl.pallas_call(
        kern,
        grid=(B, N_HEADS),
        in_specs=[
            pl.BlockSpec((S, V_DIM), lambda b, h: (b, COL_Q * N_HEADS + h)),
            pl.BlockSpec((S, V_DIM), lambda b, h: (b, COL_K * N_HEADS + h)),
            pl.BlockSpec((S, V_DIM), lambda b, h: (b, COL_V * N_HEADS + h)),
            pl.BlockSpec((None, 2, tq, tq), lambda b, h: (h, 0, 0, 0)),
            pl.BlockSpec((1, HEAD_DIM), const2),
            pl.BlockSpec((1, HEAD_DIM), const2),
            pl.BlockSpec((1, HEAD_DIM), const2),
            pl.BlockSpec((1, HEAD_DIM), const2),
            pl.BlockSpec((1, V_DIM), const2),
        ],
        out_specs=pl.BlockSpec((S, V_DIM), lambda b, h: (b, h)),
        out_shape=jax.ShapeDtypeStruct((T, ATT_V), BF16),
        scratch_shapes=[
            pltpu.VMEM((2 * tq, V_DIM), BF16),
            pltpu.VMEM((S, 2 * V_DIM), BF16),
            pltpu.VMEM((2, 2 * tq, tq), F32),
            pltpu.VMEM((2 * tq, LANES), F32),
            pltpu.VMEM((2 * tq, 2 * V_DIM), F32),
        ],
        compiler_params=_params(("parallel", "arbitrary")),
        name="diff_attn",
    )(proj, proj, proj, bias, vec(lq1), vec(lk1), vec(lq2), vec(lk2), vec(subln_g))


def _merge_kernel(x_ref, yr_ref, ya_ref, g0_ref, g1_ref, wr_ref, wa_ref, wo_ref, gf_ref, wrt_ref, brt_ref,
                  x2_ref, hf_ref, idx_ref, wgt_ref):
    pr = jnp.dot(yr_ref[...], wr_ref[...], preferred_element_type=F32)
    pa = jnp.dot(ya_ref[...], wa_ref[...], preferred_element_type=F32)
    merged = (jax.nn.sigmoid(g0_ref[...].astype(F32)) * pr
              + jax.nn.sigmoid(g1_ref[...].astype(F32)) * pa)
    x2 = x_ref[...] + jnp.dot(merged.astype(BF16), wo_ref[...], preferred_element_type=F32)
    x2_ref[...] = x2
    inv = lax.rsqrt(jnp.mean(x2 * x2, axis=-1, keepdims=True) + EPS)
    hf = x2 * inv * gf_ref[...]
    _store_token_major(hf_ref, hf)
    hi = hf.astype(BF16)
    lo = (hf - hi.astype(F32)).astype(BF16)
    hw = jnp.dot(hi, wrt_ref[...], preferred_element_type=F32)
    lw = jnp.dot(lo, wrt_ref[:, 0:LANES], preferred_element_type=F32)
    logits = hw[:, 0:LANES] + hw[:, LANES:2 * LANES] + lw + brt_ref[...]
    tm = logits.shape[0]
    lane = lax.broadcasted_iota(I32, (tm, LANES), 1)
    work = jnp.where(lane < N_EXPERTS, logits, -jnp.inf)
    idx_out = jnp.zeros((tm, LANES), I32)
    val_out = jnp.full((tm, LANES), -jnp.inf, F32)
    for k in range(TOP_K):
        mx = jnp.max(work, axis=-1, keepdims=True)
        sel = jnp.min(jnp.where(work == mx, lane, LANES), axis=-1, keepdims=True)
        idx_out = jnp.where(lane == k, sel, idx_out)
        val_out = jnp.where(lane == k, mx, val_out)
        work = jnp.where(lane == sel, -jnp.inf, work)
    e = jnp.exp(val_out - jnp.max(val_out, axis=-1, keepdims=True))
    idx_ref[...] = idx_out
    wgt_ref[...] = e / jnp.sum(e, axis=-1, keepdims=True)


def _merge_router(x2d, y_rnn, y_att, proj, w_pr, w_pa, w_o, g_ffn, w_router, b_router):
    T, D = x2d.shape
    tm = min(512, T)
    w_hi = w_router.astype(BF16)
    w_lo = (w_router - w_hi.astype(F32)).astype(BF16)
    wrt = (jnp.zeros((D, 2 * LANES), BF16).at[:, :N_EXPERTS].set(w_hi)
           .at[:, LANES:LANES + N_EXPERTS].set(w_lo))
    brt = jnp.zeros((1, LANES), F32).at[0, :N_EXPERTS].set(b_router)
    rowblk = lambda c: pl.BlockSpec((tm, D), lambda i, c=c: (i, c))
    full = lambda a: pl.BlockSpec(a.shape, lambda i: (0,) * a.ndim)
    wr, wa, wo = w_pr.astype(BF16), w_pa.astype(BF16), w_o.astype(BF16)
    gf = g_ffn.reshape(1, D)
    return pl.pallas_call(
        _merge_kernel,
        grid=(T // tm,),
        in_specs=[rowblk(0), rowblk(0), rowblk(0), rowblk(COL_G0), rowblk(COL_G1),
                  full(wr), full(wa), full(wo), full(gf), full(wrt), full(brt)],
        out_specs=[rowblk(0), pl.BlockSpec((tm * ROW_TILES, LANES), lambda i: (i, 0)),
                   pl.BlockSpec((tm, LANES), lambda i: (i, 0)),
                   pl.BlockSpec((tm, LANES), lambda i: (i, 0))],
        out_shape=[jax.ShapeDtypeStruct((T, D), F32), jax.ShapeDtypeStruct((T * ROW_TILES, LANES), F32),
                   jax.ShapeDtypeStruct((T, LANES), I32), jax.ShapeDtypeStruct((T, LANES), F32)],
        compiler_params=_params(("parallel",)),
        name="merge_router",
    )(x2d, y_rnn, y_att, proj, proj, wr, wa, wo, gf, wrt, brt)


def _route_kernel(idx_ref, dest_ref, cnt_ref, cnt_scr, run_scr, start_scr, *, blk):
    ph = pl.program_id(0)
    i = pl.program_id(1)
    tt = idx_ref.shape[0]
    idx = idx_ref[...]
    lane = lax.broadcasted_iota(I32, (tt, LANES), 1)
    onehot = jnp.zeros((tt, LANES), F32)
    for k in range(TOP_K):
        onehot = onehot + (idx[:, k:k + 1] == lane).astype(F32)

    @pl.when((ph == 0) & (i == 0))
    def _():
        cnt_scr[...] = jnp.zeros_like(cnt_scr)

    @pl.when(ph == 0)
    def _():
        cnt_scr[...] += jnp.sum(onehot, axis=0, keepdims=True)

    @pl.when((ph == 1) & (i == 0))
    def _():
        cnt = jnp.broadcast_to(cnt_scr[...], (SUBLANES, LANES))
        padded = jnp.floor((cnt + (blk - 1)) / blk) * blk
        l8 = lax.broadcasted_iota(I32, (SUBLANES, LANES), 1)
        incl = padded
        d = 1
        while d < LANES:
            incl = incl + jnp.where(l8 >= d, pltpu.roll(incl, d, 1), 0.0)
            d *= 2
        start_scr[...] = (incl - padded)[0:1, :]
        run_scr[...] = jnp.zeros_like(run_scr)

    @pl.when(ph == 1)
    def _():
        r = lax.broadcasted_iota(I32, (tt, tt), 0)
        c = lax.broadcasted_iota(I32, (tt, tt), 1)
        tri = (c < r).astype(BF16)
        before = jnp.dot(tri, onehot.astype(BF16), preferred_element_type=F32)
        base = before + run_scr[...] + start_scr[...]
        out = jnp.zeros((tt, LANES), F32)
        for k in range(TOP_K):
            dk = jnp.sum(jnp.where(idx[:, k:k + 1] == lane, base, 0.0), axis=-1, keepdims=True)
            out = jnp.where(lane == k, dk, out)
        dest_ref[...] = out.astype(I32)
        run_scr[...] += jnp.sum(onehot, axis=0, keepdims=True)
        cnt_ref[...] = jnp.broadcast_to(cnt_scr[...], cnt_ref.shape)


def _route(top_idx, blk):
    T = top_idx.shape[0]
    tt = min(512, T)
    kern = functools.partial(_route_kernel, blk=blk)
    return pl.pallas_call(
        kern,
        grid=(2, T // tt),
        in_specs=[pl.BlockSpec((tt, LANES), lambda p, i: (i, 0))],
        out_specs=[pl.BlockSpec((tt, LANES), lambda p, i: (p * i, 0)),
                   pl.BlockSpec((SUBLANES, LANES), lambda p, i: (0, 0))],
        out_shape=[jax.ShapeDtypeStruct((T, LANES), I32),
                   jax.ShapeDtypeStruct((SUBLANES, LANES), F32)],
        scratch_shapes=[pltpu.VMEM((1, LANES), F32), pltpu.VMEM((1, LANES), F32),
                        pltpu.VMEM((1, LANES), F32)],
        compiler_params=_params(("arbitrary", "arbitrary")),
        name="route",
    )(top_idx)


TOKEN_UNROLL = 2


def _for_each_token_k(n_tok, fn):
    def body(g, c):
        for u in range(TOKEN_UNROLL):
            for k in range(TOP_K):
                fn(g * TOKEN_UNROLL + u, k)
        return c

    lax.fori_loop(0, n_tok // TOKEN_UNROLL, body, 0)


def _row_tile(ref, row):
    return ref.at[pl.ds(pl.multiple_of(row * ROW_TILES, ROW_TILES), ROW_TILES), :]


def _dispatch_kernel(ends_ref, dest_ref, hf_ref, xs_ref, zero_buf, sem, zsem, *, tt, blk):
    @pl.when(pl.program_id(0) == 0)
    def _():
        zero_buf[...] = jnp.zeros_like(zero_buf)

        def tail_fill(e):
            prev_end = ends_ref[e - 1] if e > 0 else 0
            start = pl.multiple_of((ends_ref[e] - blk) * ROW_TILES, ROW_TILES)
            return ends_ref[e] > prev_end, pltpu.make_async_copy(
                zero_buf, xs_ref.at[pl.ds(start, blk * ROW_TILES), :], zsem)

        for e in range(N_EXPERTS):
            nonempty, cp = tail_fill(e)
            pl.when(nonempty)(cp.start)
        for e in range(N_EXPERTS):
            nonempty, cp = tail_fill(e)
            pl.when(nonempty)(cp.wait)

    def row_copy(t, k):
        return pltpu.make_async_copy(_row_tile(hf_ref, t),
                                     _row_tile(xs_ref, dest_ref[t * TOP_K + k]), sem)

    _for_each_token_k(tt, lambda t, k: row_copy(t, k).start(priority=k % 2))
    _for_each_token_k(tt, lambda t, k: row_copy(t, k).wait())


def _dispatch(ends, dest_flat, hf_tm, n_slots, blk):
    T = hf_tm.shape[0] // ROW_TILES
    tt = min(512, T)
    kern = functools.partial(_dispatch_kernel, tt=tt, blk=blk)
    grid_spec = pltpu.PrefetchScalarGridSpec(
        num_scalar_prefetch=1,
        grid=(T // tt,),
        in_specs=[pl.BlockSpec((tt * TOP_K,), lambda i, ends: (i,), memory_space=pltpu.SMEM),
                  pl.BlockSpec((tt * ROW_TILES, LANES), lambda i, ends: (i, 0))],
        out_specs=pl.BlockSpec(memory_space=pl.ANY),
        scratch_shapes=[pltpu.VMEM((blk * ROW_TILES, LANES), F32),
                        pltpu.SemaphoreType.DMA(()), pltpu.SemaphoreType.DMA(())],
    )
    return pl.pallas_call(
        kern,
        grid_spec=grid_spec,
        out_shape=jax.ShapeDtypeStruct((n_slots * ROW_TILES, LANES), F32),
        compiler_params=_params(("arbitrary",)),
        name="dispatch",
    )(ends, dest_flat, hf_tm)


def _expert_kernel(be_ref, bsrc_ref, act_ref, first_ref, x_ref, wg_ref, bg_ref, wu_ref, bu_ref, wd_ref, bd_ref,
                   o_ref, wg_scr, wu_scr, wd_scr, *, blk):
    del be_ref, bsrc_ref
    i = pl.program_id(0)

    @pl.when(first_ref[i] == 1)
    def _():
        wg_scr[...] = wg_ref[0].astype(BF16)
        wu_scr[...] = wu_ref[0].astype(BF16)
        wd_scr[...] = wd_ref[0].astype(BF16)

    @pl.when(act_ref[i] == 1)
    def _():
        x = _load_token_major(x_ref, 0, blk).astype(BF16)
        g = jnp.dot(x, wg_scr[...], preferred_element_type=F32) + bg_ref[0]
        u = jnp.dot(x, wu_scr[...], preferred_element_type=F32) + bu_ref[0]
        g = jnp.minimum(g, SWIGLU_LIMIT)
        u = jnp.clip(u, -SWIGLU_LIMIT, SWIGLU_LIMIT)
        act = (u + 1.0) * (g * jax.nn.sigmoid(SWIGLU_ALPHA * g))
        y = jnp.dot(act.astype(BF16), wd_scr[...], preferred_element_type=F32) + bd_ref[0]
        _store_token_major(o_ref, y)


def _experts(x_slots, blk, block_expert, block_src, block_active, block_first, wg, bg, wu, bu, wd, bd):
    D = D_MODEL
    nb = x_slots.shape[0] // (blk * ROW_TILES)
    wspec = pl.BlockSpec((1, D, D_FF), lambda i, be, bs, ac, fi: (be[i], 0, 0))
    bspec = pl.BlockSpec((1, 1, D_FF), lambda i, be, bs, ac, fi: (be[i], 0, 0))
    xspec = pl.BlockSpec((blk * ROW_TILES, LANES), lambda i, be, bs, ac, fi: (bs[i], 0))
    grid_spec = pltpu.PrefetchScalarGridSpec(
        num_scalar_prefetch=4,
        grid=(nb,),
        in_specs=[xspec, wspec, bspec, wspec, bspec, wspec, bspec],
        out_specs=xspec,
        scratch_shapes=[pltpu.VMEM((D, D_FF), BF16), pltpu.VMEM((D, D_FF), BF16),
                        pltpu.VMEM((D_FF, D), BF16)],
    )
    b3 = lambda b: b.reshape(N_EXPERTS, 1, -1)
    return pl.pallas_call(
        functools.partial(_expert_kernel, blk=blk),
        grid_spec=grid_spec,
        out_shape=jax.ShapeDtypeStruct(x_slots.shape, F32),
        compiler_params=_params(("arbitrary",)),
        name="experts",
    )(block_expert, block_src, block_active, block_first, x_slots,
      wg, b3(bg), wu, b3(bu), wd, b3(bd))


def _combine_kernel(dest_ref, dest_next_ref, ys_ref, w_ref, x2_ref, g_ref, o_ref, buf, sems, *, tt):
    i = pl.program_id(0)
    n = pl.num_programs(0)
    cur = i % 2

    def row_copy(idx_ref, slot, t, k):
        return pltpu.make_async_copy(_row_tile(ys_ref, idx_ref[t * TOP_K + k]),
                                     _row_tile(buf, (slot * TOP_K + k) * tt + t), sems.at[slot])

    def gather(idx_ref, slot):
        _for_each_token_k(tt, lambda t, k: row_copy(idx_ref, slot, t, k).start(priority=k % 2))

    @pl.when(i == 0)
    def _():
        gather(dest_ref, 0)

    @pl.when(i + 1 < n)
    def _():
        gather(dest_next_ref, 1 - cur)

    _for_each_token_k(tt, lambda t, k: row_copy(dest_ref, cur, t, k).wait())
    w = w_ref[...]
    y = x2_ref[...]
    for k in range(TOP_K):
        y = y + w[:, k:k + 1] * _load_token_major(buf, (cur * TOP_K + k) * tt, tt)
    inv = lax.rsqrt(jnp.mean(y * y, axis=-1, keepdims=True) + EPS)
    o_ref[...] = y * inv * g_ref[...]


def _combine(dest_flat, y_slots, top_w, x2, g_final):
    T, D = x2.shape
    tt = min(256, T)
    nt = T // tt
    kern = functools.partial(_combine_kernel, tt=tt)
    return pl.pallas_call(
        kern,
        grid=(nt,),
        in_specs=[pl.BlockSpec((tt * TOP_K,), lambda i: (i,), memory_space=pltpu.SMEM),
                  pl.BlockSpec((tt * TOP_K,), lambda i: (jnp.minimum(i + 1, nt - 1),),
                               memory_space=pltpu.SMEM),
                  pl.BlockSpec(memory_space=pl.ANY),
                  pl.BlockSpec((tt, LANES), lambda i: (i, 0)),
                  pl.BlockSpec((tt, D), lambda i: (i, 0)),
                  pl.BlockSpec((1, D), lambda i: (0, 0))],
        out_specs=pl.BlockSpec((tt, D), lambda i: (i, 0)),
        out_shape=jax.ShapeDtypeStruct((T, D), F32),
        scratch_shapes=[pltpu.VMEM((2 * TOP_K * tt * ROW_TILES, LANES), F32),
                        pltpu.SemaphoreType.DMA((2,))],
        compiler_params=_params(("arbitrary",)),
        name="combine",
    )(dest_flat, dest_flat, y_slots, top_w, x2, g_final.reshape(1, D))


def _moe_block_size(T):
    return min(512, max(SUBLANES, T * TOP_K // N_EXPERTS))


def _block_tables(counts, blk, nb):
    cnt = counts.astype(I32)
    padded = (cnt + blk - 1) // blk * blk
    ends = jnp.cumsum(padded)
    used = ends[-1] // blk
    starts = jnp.arange(nb, dtype=I32) * blk
    src = jnp.minimum(jnp.arange(nb, dtype=I32), jnp.maximum(used - 1, 0))
    expert = jnp.minimum(jnp.sum((starts[:, None] >= ends[None, :]).astype(I32), axis=1), N_EXPERTS - 1)
    expert = expert[src]
    active = (jnp.arange(nb, dtype=I32) < used).astype(I32)
    prev = jnp.concatenate([jnp.full((1,), -1, I32), expert[:-1]])
    first = active * (expert != prev).astype(I32)
    return ends.astype(I32), expert, src, active, first


def kernel(x, norm_mix_g, w_in, conv_w, conv_b, w_rg_a, b_rg_a, w_rg_x, b_rg_x, lru_lambda, diff_lambda_q1, diff_lambda_k1, diff_lambda_q2, diff_lambda_k2, subln_g, rel_bias_table, w_proj_rnn, w_proj_att, w_out, norm_ffn_g, w_router, b_router, w_gate_e, b_gate_e, w_up_e, b_up_e, w_down_e, b_down_e, norm_final_g):
    B, S, D = x.shape
    T = B * S
    assert norm_mix_g.shape[0] == 1, "single-layer block: the final norm is fused into the MoE combine"
    l = 0
    xt = x.reshape(T, D)
    lam_init = 0.8 - 0.6 * math.exp(-0.3 * l)
    proj = _inproj(xt, norm_mix_g[l], w_in[l].astype(BF16))
    y_rnn = _rglru(proj, B, S, conv_w[l], conv_b[l], w_rg_a[l], b_rg_a[l], w_rg_x[l], b_rg_x[l],
                   lru_lambda[l])
    y_att = _attention(proj, B, S, diff_lambda_q1[l], diff_lambda_k1[l], diff_lambda_q2[l],
                       diff_lambda_k2[l], subln_g[l], rel_bias_table, lam_init)
    x2, hf, top_idx, top_w = _merge_router(xt, y_rnn, y_att, proj, w_proj_rnn[l], w_proj_att[l],
                                           w_out[l], norm_ffn_g[l], w_router[l], b_router[l])
    blk = _moe_block_size(T)
    nb = T * TOP_K // blk + N_EXPERTS
    dest, counts = _route(top_idx, blk)
    dest_flat = dest[:, :TOP_K].reshape(-1)
    ends, expert, src, active, first = _block_tables(counts[0, :N_EXPERTS], blk, nb)
    x_slots = _dispatch(ends, dest_flat, hf, nb * blk, blk)
    y_slots = _experts(x_slots, blk, expert, src, active, first, w_gate_e[l], b_gate_e[l], w_up_e[l],
                       b_up_e[l], w_down_e[l], b_down_e[l])
    out = _combine(dest_flat, y_slots, top_w, x2, norm_final_g)
    return out.reshape(B, S, D)
```

```python
import functools
import math

import jax
import jax.numpy as jnp
from jax import lax
from jax.experimental import pallas as pl
from jax.experimental.pallas import tpu as pltpu

F32 = jnp.float32
BF16 = jnp.bfloat16
I32 = jnp.int32

D_MODEL = 1024
D_RNN = 1024
RNN_BLOCKS = 16
RNN_BLOCK = D_RNN // RNN_BLOCKS
CONV_W = 4
RGLRU_C = 8.0
N_HEADS = 8
HEAD_DIM = 64
V_DIM = 2 * HEAD_DIM
ATT_QK = N_HEADS * 2 * HEAD_DIM
ATT_V = N_HEADS * V_DIM
D_IN = 2 * D_RNN + 2 * ATT_QK + ATT_V + 2 * D_MODEL
NUM_BUCKETS = 32
MAX_EXACT = NUM_BUCKETS // 2
MAX_DISTANCE = 128
N_EXPERTS = 32
TOP_K = 4
D_FF = D_MODEL
SWIGLU_LIMIT = 7.0
SWIGLU_ALPHA = 1.702
EPS = 1e-6
NEG_INF = -1e30

LANES = 128
SUBLANES = 8
MXU_DIM = 256
VMEM_LIMIT = 56 * 1024 * 1024

COL_XR, COL_GR, COL_Q, COL_K, COL_V, COL_G0, COL_G1 = range(7)


def _params(sem, vmem=VMEM_LIMIT):
    return pltpu.CompilerParams(dimension_semantics=sem, vmem_limit_bytes=vmem)


ROW_TILES = D_MODEL // LANES
assert ROW_TILES == SUBLANES


def _store_token_major(ref, val):
    n = val.shape[0]
    for c in range(ROW_TILES):
        ref[pl.ds(c, n, stride=ROW_TILES), :] = val[:, c * LANES:(c + 1) * LANES]


def _load_token_major(ref, start, n):
    return jnp.concatenate(
        [ref[pl.ds(start * ROW_TILES + c, n, stride=ROW_TILES), :] for c in range(ROW_TILES)], axis=-1)


def _inproj_kernel(x_ref, g_ref, w_ref, o_ref, h_scr):
    @pl.when(pl.program_id(1) == 0)
    def _():
        x = x_ref[...]
        inv = lax.rsqrt(jnp.mean(x * x, axis=-1, keepdims=True) + EPS)
        h_scr[...] = (x * inv * g_ref[...]).astype(BF16)

    o_ref[...] = jnp.dot(h_scr[...], w_ref[...], preferred_element_type=F32).astype(o_ref.dtype)


def _inproj(x2d, g, w_bf16):
    T, D = x2d.shape
    N = w_bf16.shape[1]
    tm = min(1024, T)
    tn = 3584 if N % 3584 == 0 else 1024
    return pl.pallas_call(
        _inproj_kernel,
        grid=(T // tm, N // tn),
        in_specs=[
            pl.BlockSpec((tm, D), lambda i, j: (i, 0)),
            pl.BlockSpec((1, D), lambda i, j: (0, 0)),
            pl.BlockSpec((D, tn), lambda i, j: (0, j)),
        ],
        out_specs=pl.BlockSpec((tm, tn), lambda i, j: (i, j)),
        out_shape=jax.ShapeDtypeStruct((T, N), BF16),
        scratch_shapes=[pltpu.VMEM((tm, D), BF16)],
        compiler_params=_params(("parallel", "arbitrary")),
        name="inproj",
    )(x2d, g.reshape(1, D), w_bf16)


def _rglru_kernel(xr_ref, gr_ref, cw_ref, cb_ref, wa_ref, ba_ref, wx_ref, bx_ref, lam_ref,
                  o_ref, xbuf, tail_scr, a_scr, u_scr, h_scr):
    nb, ts, _ = xr_ref.shape
    pad = SUBLANES
    nslab = D_RNN // LANES

    @pl.when(pl.program_id(0) == 0)
    def _():
        tail_scr[...] = jnp.zeros_like(tail_scr)
        h_scr[...] = jnp.zeros_like(h_scr)

    cw = cw_ref[...]
    z = -lam_ref[...]
    softplus = jnp.maximum(z, 0.0) + jnp.log1p(jnp.exp(-jnp.abs(z)))
    sigmoid = lambda v: 0.5 * jnp.tanh(0.5 * v) + 0.5
    nchunk = D_RNN // MXU_DIM
    for b in range(nb):
        x = xr_ref[b].astype(F32)
        xbuf[0:pad, :] = tail_scr[b]
        xbuf[pad:pad + ts, :] = x
        xc = (cw[3:4, :] * x
              + cw[2:3, :] * xbuf[pad - 1:pad - 1 + ts, :]
              + cw[1:2, :] * xbuf[pad - 2:pad - 2 + ts, :]
              + cw[0:1, :] * xbuf[pad - 3:pad - 3 + ts, :]) + cb_ref[...]
        tail_scr[b] = x[ts - pad:ts, :]
        xcb = xc.astype(BF16)
        r_pre = jnp.concatenate(
            [jnp.dot(xcb[:, c * MXU_DIM:(c + 1) * MXU_DIM], wa_ref[c], preferred_element_type=F32)
             for c in range(nchunk)], axis=-1)
        i_pre = jnp.concatenate(
            [jnp.dot(xcb[:, c * MXU_DIM:(c + 1) * MXU_DIM], wx_ref[c], preferred_element_type=F32)
             for c in range(nchunk)], axis=-1)
        r = sigmoid(r_pre + ba_ref[...])
        ig = sigmoid(i_pre + bx_ref[...])
        a = jnp.exp((-RGLRU_C) * r * softplus)
        u = jnp.sqrt(1.0 - a * a) * (ig * xc)
        for c in range(nslab):
            a_scr[c, pl.ds(b, ts, stride=nb), :] = a[:, c * LANES:(c + 1) * LANES]
            u_scr[c, pl.ds(b, ts, stride=nb), :] = u[:, c * LANES:(c + 1) * LANES]

    def body(g, hs):
        for j in range(SCAN_UNROLL):
            off = pl.multiple_of((g * SCAN_UNROLL + j) * nb, nb)
            new = []
            for c in range(nslab):
                h = a_scr[c, pl.ds(off, nb), :] * hs[c] + u_scr[c, pl.ds(off, nb), :]
                u_scr[c, pl.ds(off, nb), :] = h
                new.append(h)
            hs = tuple(new)
        return hs

    hs = lax.fori_loop(0, ts // SCAN_UNROLL, body, tuple(h_scr[c] for c in range(nslab)))
    for c in range(nslab):
        h_scr[c] = hs[c]
    for b in range(nb):
        h = jnp.concatenate([u_scr[c, pl.ds(b, ts, stride=nb), :] for c in range(nslab)], axis=-1)
        gate = jax.nn.gelu(gr_ref[b].astype(F32), approximate=True)
        o_ref[b] = (h * gate).astype(o_ref.dtype)


SCAN_UNROLL = 8


def _block_diag_chunks(w):
    per = MXU_DIM // RNN_BLOCK
    w = w.reshape(D_RNN // MXU_DIM, per, RNN_BLOCK, RNN_BLOCK)
    eye = jnp.eye(per, dtype=w.dtype)
    out = jnp.einsum('gpcd,pq->gpcqd', w, eye)
    return out.reshape(D_RNN // MXU_DIM, MXU_DIM, MXU_DIM)


def _rglru(proj, B, S, conv_w, conv_b, w_a, b_a, w_x, b_x, lru_lambda):
    assert B <= SUBLANES, "all batch rows of a time step share one vreg in the scan"
    ts = min(256, S)
    ns = S // ts
    wa = _block_diag_chunks(w_a).astype(BF16)
    wx = _block_diag_chunks(w_x).astype(BF16)
    nchunk = D_RNN // MXU_DIM
    nslab = D_RNN // LANES
    row = lambda v: v.reshape(1, D_RNN)
    const2 = lambda s: (0, 0)
    proj3 = proj.reshape(B, S, proj.shape[-1])
    out = pl.pallas_call(
        _rglru_kernel,
        grid=(ns,),
        in_specs=[
            pl.BlockSpec((B, ts, D_RNN), lambda s: (0, s, COL_XR)),
            pl.BlockSpec((B, ts, D_RNN), lambda s: (0, s, COL_GR)),
            pl.BlockSpec((CONV_W, D_RNN), const2),
            pl.BlockSpec((1, D_RNN), const2),
            pl.BlockSpec((nchunk, MXU_DIM, MXU_DIM), lambda s: (0, 0, 0)),
            pl.BlockSpec((1, D_RNN), const2),
            pl.BlockSpec((nchunk, MXU_DIM, MXU_DIM), lambda s: (0, 0, 0)),
            pl.BlockSpec((1, D_RNN), const2),
            pl.BlockSpec((1, D_RNN), const2),
        ],
        out_specs=pl.BlockSpec((B, ts, D_RNN), lambda s: (0, s, 0)),
        out_shape=jax.ShapeDtypeStruct((B, S, D_RNN), BF16),
        scratch_shapes=[
            pltpu.VMEM((ts + SUBLANES, D_RNN), F32),
            pltpu.VMEM((B, SUBLANES, D_RNN), F32),
            pltpu.VMEM((nslab, ts * B, LANES), F32),
            pltpu.VMEM((nslab, ts * B, LANES), F32),
            pltpu.VMEM((nslab, B, LANES), F32),
        ],
        compiler_params=_params(("arbitrary",)),
        name="rglru",
    )(proj3, proj3, conv_w, row(conv_b), wa, row(b_a), wx, row(b_x), row(lru_lambda))
    return out.reshape(B * S, D_RNN)


def _attn_kernel(q_ref, k_ref, v_ref, bias_ref, lq1_ref, lk1_ref, lq2_ref, lk2_ref, sg_ref,
                 o_ref, qs_scr, vx_scr, s_scr, m_scr, acc_scr, *, tq, lam_init):
    S = q_ref.shape[0]
    nq = S // tq
    scale = HEAD_DIM ** -0.5 * LOG2E
    lane = lax.broadcasted_iota(I32, (tq, V_DIM), 1)
    lam = (jnp.exp(jnp.sum(lq1_ref[...] * lk1_ref[...], keepdims=True))
           - jnp.exp(jnp.sum(lq2_ref[...] * lk2_ref[...], keepdims=True)) + lam_init)
    vx_scr[:, 0:V_DIM] = v_ref[...]
    vx_scr[:, V_DIM:2 * V_DIM] = jnp.ones((S, V_DIM), BF16)

    def scores(j, dst, halves=1):
        w = tq // halves
        for part in range(halves):
            k = k_ref[pl.ds(pl.multiple_of(j * tq + part * w, w), w), :]
            s_scr[dst, :, part * w:(part + 1) * w] = lax.dot_general(
                qs_scr[...], k, (((1,), (1,)), ((), ())), preferred_element_type=F32)

    def step(j, src, bias_idx, prefetch):
        if prefetch:
            scores(j + 1, 1 - src)
        s = s_scr[src]
        if bias_idx is not None:
            b = bias_ref[bias_idx]
            s = s + jnp.concatenate([b, b], axis=0)
        vx = vx_scr[pl.ds(pl.multiple_of(j * tq, tq), tq), :]
        m_prev = m_scr[...]
        m_new = jnp.maximum(m_prev, jnp.max(s, axis=1)[:, None])
        p = jnp.exp2(s - jnp.tile(m_new, (1, tq // LANES)))
        alpha = jnp.exp2(m_prev - m_new)
        acc_scr[...] = (jnp.tile(alpha, (1, 2)) * acc_scr[...]
                        + jnp.dot(p.astype(BF16), vx, preferred_element_type=F32))
        m_scr[...] = m_new

    def qblock(qi, carry):
        q0 = pl.multiple_of(qi * tq, tq)
        q = (q_ref[pl.ds(q0, tq), :].astype(F32) * scale).astype(BF16)
        zero = jnp.zeros_like(q)
        qs_scr[0:tq, :] = jnp.where(lane < HEAD_DIM, q, zero)
        qs_scr[tq:2 * tq, :] = jnp.where(lane >= HEAD_DIM, q, zero)
        m_scr[...] = jnp.full(m_scr.shape, NEG_INF, F32)
        acc_scr[...] = jnp.zeros(acc_scr.shape, F32)
        scores(0, 0, halves=2)
        nfar = jnp.maximum(qi - 1, 0)

        def far_pair(i, c):
            step(2 * i, 0, None, True)
            step(2 * i + 1, 1, None, True)
            return c

        lax.fori_loop(0, nfar // 2, far_pair, 0)
        odd = nfar % 2

        @pl.when(odd == 1)
        def _():
            step(nfar - 1, 0, None, True)

        @pl.when(qi == 0)
        def _():
            step(qi, 0, 0, False)

        for par in (0, 1):
            @pl.when((qi >= 1) & (odd == par))
            def _(par=par):
                step(qi - 1, par, 1, True)
                step(qi, 1 - par, 0, False)

        acc = acc_scr[...]
        o_all = acc[:, 0:V_DIM] / acc[:, V_DIM:2 * V_DIM]
        o = o_all[0:tq, :] - lam * o_all[tq:2 * tq, :]
        inv = lax.rsqrt(jnp.mean(o * o, axis=-1, keepdims=True) + EPS)
        y = (o * inv * sg_ref[...]) * (1.0 - lam_init)
        o_ref[pl.ds(q0, tq), :] = y.astype(o_ref.dtype)
        return carry

    lax.fori_loop(0, nq, qblock, 0)


def _rel_bucket(n):
    n = jnp.maximum(n, 0)
    nf = jnp.maximum(n, MAX_EXACT).astype(F32)
    large = MAX_EXACT + (jnp.log(nf / MAX_EXACT) / math.log(MAX_DISTANCE / MAX_EXACT)
                         * (NUM_BUCKETS - MAX_EXACT)).astype(I32)
    large = jnp.minimum(large, NUM_BUCKETS - 1)
    return jnp.where(n < MAX_EXACT, n, large)


def _bias_tiles(rel_table, tq):
    i = jnp.arange(tq, dtype=I32)[:, None]
    j = jnp.arange(tq, dtype=I32)[None, :]
    table = rel_table.astype(F32) - rel_table[NUM_BUCKETS - 1].astype(F32)[None, :]
    tiles = []
    for delta in (0, tq):
        n = i - j + delta
        onehot = (_rel_bucket(n)[:, :, None] == jnp.arange(NUM_BUCKETS, dtype=I32)).astype(F32)
        b = jnp.einsum('ijb,bh->hij', onehot, table, precision=lax.Precision.HIGHEST) * LOG2E
        tiles.append(jnp.where((n >= 0)[None], b, NEG_INF))
    return jnp.stack(tiles, axis=1)


ATT_TQ = 512
LOG2E = math.log2(math.e)


def _attention(proj, B, S, lq1, lk1, lq2, lk2, subln_g, rel_table, lam_init):
    T = B * S
    tq = min(ATT_TQ, S)
    assert V_DIM == LANES and MAX_DISTANCE <= tq and S % tq == 0
    bias = _bias_tiles(rel_table, tq)
    vec = lambda v: v.reshape(1, -1).astype(F32)
    const2 = lambda b, h: (0, 0)
    kern = functools.partial(_attn_kernel, tq=tq, lam_init=lam_init)
    return pl.pallas_call(
        kern,
        grid=(B, N_HEADS),
        in_specs=[
            pl.BlockSpec((S, V_DIM), lambda b, h: (b, COL_Q * N_HEADS + h)),
            pl.BlockSpec((S, V_DIM), lambda b, h: (b, COL_K * N_HEADS + h)),
            pl.BlockSpec((S, V_DIM), lambda b, h: (b, COL_V * N_HEADS + h)),
            pl.BlockSpec((None, 2, tq, tq), lambda b, h: (h, 0, 0, 0)),
            pl.BlockSpec((1, HEAD_DIM), const2),
            pl.BlockSpec((1, HEAD_DIM), const2),
            pl.BlockSpec((1, HEAD_DIM), const2),
            pl.BlockSpec((1, HEAD_DIM), const2),
            pl.BlockSpec((1, V_DIM), const2),
        ],
        out_specs=pl.BlockSpec((S, V_DIM), lambda b, h: (b, h)),
        out_shape=jax.ShapeDtypeStruct((T, ATT_V), BF16),
        scratch_shapes=[
            pltpu.VMEM((2 * tq, V_DIM), BF16),
            pltpu.VMEM((S, 2 * V_DIM), BF16),
            pltpu.VMEM((2, 2 * tq, tq), F32),
            pltpu.VMEM((2 * tq, LANES), F32),
            pltpu.VMEM((2 * tq, 2 * V_DIM), F32),
        ],
        compiler_params=_params(("parallel", "arbitrary")),
        name="diff_attn",
    )(proj, proj, proj, bias, vec(lq1), vec(lk1), vec(lq2), vec(lk2), vec(subln_g))


def _merge_kernel(x_ref, yr_ref, ya_ref, g0_ref, g1_ref, wr_ref, wa_ref, wo_ref, gf_ref, wrt_ref, brt_ref,
                  x2_ref, hf_ref, idx_ref, wgt_ref):
    pr = jnp.dot(yr_ref[...], wr_ref[...], preferred_element_type=F32)
    pa = jnp.dot(ya_ref[...], wa_ref[...], preferred_element_type=F32)
    merged = (jax.nn.sigmoid(g0_ref[...].astype(F32)) * pr
              + jax.nn.sigmoid(g1_ref[...].astype(F32)) * pa)
    x2 = x_ref[...] + jnp.dot(merged.astype(BF16), wo_ref[...], preferred_element_type=F32)
    x2_ref[...] = x2
    inv = lax.rsqrt(jnp.mean(x2 * x2, axis=-1, keepdims=True) + EPS)
    hf = x2 * inv * gf_ref[...]
    _store_token_major(hf_ref, hf)
    hi = hf.astype(BF16)
    lo = (hf - hi.astype(F32)).astype(BF16)
    hw = jnp.dot(hi, wrt_ref[...], preferred_element_type=F32)
    lw = jnp.dot(lo, wrt_ref[:, 0:LANES], preferred_element_type=F32)
    logits = hw[:, 0:LANES] + hw[:, LANES:2 * LANES] + lw + brt_ref[...]
    tm = logits.shape[0]
    lane = lax.broadcasted_iota(I32, (tm, LANES), 1)
    work = jnp.where(lane < N_EXPERTS, logits, -jnp.inf)
    idx_out = jnp.zeros((tm, LANES), I32)
    val_out = jnp.full((tm, LANES), -jnp.inf, F32)
    for k in range(TOP_K):
        mx = jnp.max(work, axis=-1, keepdims=True)
        sel = jnp.min(jnp.where(work == mx, lane, LANES), axis=-1, keepdims=True)
        idx_out = jnp.where(lane == k, sel, idx_out)
        val_out = jnp.where(lane == k, mx, val_out)
        work = jnp.where(lane == sel, -jnp.inf, work)
    e = jnp.exp(val_out - jnp.max(val_out, axis=-1, keepdims=True))
    idx_ref[...] = idx_out
    wgt_ref[...] = e / jnp.sum(e, axis=-1, keepdims=True)


def _merge_router(x2d, y_rnn, y_att, proj, w_pr, w_pa, w_o, g_ffn, w_router, b_router):
    T, D = x2d.shape
    tm = min(512, T)
    w_hi = w_router.astype(BF16)
    w_lo = (w_router - w_hi.astype(F32)).astype(BF16)
    wrt = (jnp.zeros((D, 2 * LANES), BF16).at[:, :N_EXPERTS].set(w_hi)
           .at[:, LANES:LANES + N_EXPERTS].set(w_lo))
    brt = jnp.zeros((1, LANES), F32).at[0, :N_EXPERTS].set(b_router)
    rowblk = lambda c: pl.BlockSpec((tm, D), lambda i, c=c: (i, c))
    full = lambda a: pl.BlockSpec(a.shape, lambda i: (0,) * a.ndim)
    wr, wa, wo = w_pr.astype(BF16), w_pa.astype(BF16), w_o.astype(BF16)
    gf = g_ffn.reshape(1, D)
    return pl.pallas_call(
        _merge_kernel,
        grid=(T // tm,),
        in_specs=[rowblk(0), rowblk(0), rowblk(0), rowblk(COL_G0), rowblk(COL_G1),
                  full(wr), full(wa), full(wo), full(gf), full(wrt), full(brt)],
        out_specs=[rowblk(0), pl.BlockSpec((tm * ROW_TILES, LANES), lambda i: (i, 0)),
                   pl.BlockSpec((tm, LANES), lambda i: (i, 0)),
                   pl.BlockSpec((tm, LANES), lambda i: (i, 0))],
        out_shape=[jax.ShapeDtypeStruct((T, D), F32), jax.ShapeDtypeStruct((T * ROW_TILES, LANES), F32),
                   jax.ShapeDtypeStruct((T, LANES), I32), jax.ShapeDtypeStruct((T, LANES), F32)],
        compiler_params=_params(("parallel",)),
        name="merge_router",
    )(x2d, y_rnn, y_att, proj, proj, wr, wa, wo, gf, wrt, brt)


def _route_kernel(idx_ref, dest_ref, cnt_ref, cnt_scr, run_scr, start_scr, *, blk):
    ph = pl.program_id(0)
    i = pl.program_id(1)
    tt = idx_ref.shape[0]
    idx = idx_ref[...]
    lane = lax.broadcasted_iota(I32, (tt, LANES), 1)
    onehot = jnp.zeros((tt, LANES), F32)
    for k in range(TOP_K):
        onehot = onehot + (idx[:, k:k + 1] == lane).astype(F32)

    @pl.when((ph == 0) & (i == 0))
    def _():
        cnt_scr[...] = jnp.zeros_like(cnt_scr)

    @pl.when(ph == 0)
    def _():
        cnt_scr[...] += jnp.sum(onehot, axis=0, keepdims=True)

    @pl.when((ph == 1) & (i == 0))
    def _():
        cnt = jnp.broadcast_to(cnt_scr[...], (SUBLANES, LANES))
        padded = jnp.floor((cnt + (blk - 1)) / blk) * blk
        l8 = lax.broadcasted_iota(I32, (SUBLANES, LANES), 1)
        incl = padded
        d = 1
        while d < LANES:
            incl = incl + jnp.where(l8 >= d, pltpu.roll(incl, d, 1), 0.0)
            d *= 2
        start_scr[...] = (incl - padded)[0:1, :]
        run_scr[...] = jnp.zeros_like(run_scr)

    @pl.when(ph == 1)
    def _():
        r = lax.broadcasted_iota(I32, (tt, tt), 0)
        c = lax.broadcasted_iota(I32, (tt, tt), 1)
        tri = (c < r).astype(BF16)
        before = jnp.dot(tri, onehot.astype(BF16), preferred_element_type=F32)
        base = before + run_scr[...] + start_scr[...]
        out = jnp.zeros((tt, LANES), F32)
        for k in range(TOP_K):
            dk = jnp.sum(jnp.where(idx[:, k:k + 1] == lane, base, 0.0), axis=-1, keepdims=True)
            out = jnp.where(lane == k, dk, out)
        dest_ref[...] = out.astype(I32)
        run_scr[...] += jnp.sum(onehot, axis=0, keepdims=True)
        cnt_ref[...] = jnp.broadcast_to(cnt_scr[...], cnt_ref.shape)


def _route(top_idx, blk):
    T = top_idx.shape[0]
    tt = min(512, T)
    kern = functools.partial(_route_kernel, blk=blk)
    return pl.pallas_call(
        kern,
        grid=(2, T // tt),
        in_specs=[pl.BlockSpec((tt, LANES), lambda p, i: (i, 0))],
        out_specs=[pl.BlockSpec((tt, LANES), lambda p, i: (p * i, 0)),
                   pl.BlockSpec((SUBLANES, LANES), lambda p, i: (0, 0))],
        out_shape=[jax.ShapeDtypeStruct((T, LANES), I32),
                   jax.ShapeDtypeStruct((SUBLANES, LANES), F32)],
        scratch_shapes=[pltpu.VMEM((1, LANES), F32), pltpu.VMEM((1, LANES), F32),
                        pltpu.VMEM((1, LANES), F32)],
        compiler_params=_params(("arbitrary", "arbitrary")),
        name="route",
    )(top_idx)


TOKEN_UNROLL = 2


def _for_each_token_k(n_tok, fn):
    def body(g, c):
        for u in range(TOKEN_UNROLL):
            for k in range(TOP_K):
                fn(g * TOKEN_UNROLL + u, k)
        return c

    lax.fori_loop(0, n_tok // TOKEN_UNROLL, body, 0)


def _row_tile(ref, row):
    return ref.at[pl.ds(pl.multiple_of(row * ROW_TILES, ROW_TILES), ROW_TILES), :]


def _dispatch_kernel(ends_ref, dest_ref, hf_ref, xs_ref, zero_buf, sem, zsem, *, tt, blk):
    @pl.when(pl.program_id(0) == 0)
    def _():
        zero_buf[...] = jnp.zeros_like(zero_buf)

        def tail_fill(e):
            prev_end = ends_ref[e - 1] if e > 0 else 0
            start = pl.multiple_of((ends_ref[e] - blk) * ROW_TILES, ROW_TILES)
            return ends_ref[e] > prev_end, pltpu.make_async_copy(
                zero_buf, xs_ref.at[pl.ds(start, blk * ROW_TILES), :], zsem)

        for e in range(N_EXPERTS):
            nonempty, cp = tail_fill(e)
            pl.when(nonempty)(cp.start)
        for e in range(N_EXPERTS):
            nonempty, cp = tail_fill(e)
            pl.when(nonempty)(cp.wait)

    def row_copy(t, k):
        return pltpu.make_async_copy(_row_tile(hf_ref, t),
                                     _row_tile(xs_ref, dest_ref[t * TOP_K + k]), sem)

    _for_each_token_k(tt, lambda t, k: row_copy(t, k).start(priority=k % 2))
    _for_each_token_k(tt, lambda t, k: row_copy(t, k).wait())


def _dispatch(ends, dest_flat, hf_tm, n_slots, blk):
    T = hf_tm.shape[0] // ROW_TILES
    tt = min(512, T)
    kern = functools.partial(_dispatch_kernel, tt=tt, blk=blk)
    grid_spec = pltpu.PrefetchScalarGridSpec(
        num_scalar_prefetch=1,
        grid=(T // tt,),
        in_specs=[pl.BlockSpec((tt * TOP_K,), lambda i, ends: (i,), memory_space=pltpu.SMEM),
                  pl.BlockSpec((tt * ROW_TILES, LANES), lambda i, ends: (i, 0))],
        out_specs=pl.BlockSpec(memory_space=pl.ANY),
        scratch_shapes=[pltpu.VMEM((blk * ROW_TILES, LANES), F32),
                        pltpu.SemaphoreType.DMA(()), pltpu.SemaphoreType.DMA(())],
    )
    return pl.pallas_call(
        kern,
        grid_spec=grid_spec,
        out_shape=jax.ShapeDtypeStruct((n_slots * ROW_TILES, LANES), F32),
        compiler_params=_params(("arbitrary",)),
        name="dispatch",
    )(ends, dest_flat, hf_tm)


def _expert_kernel(be_ref, bsrc_ref, act_ref, first_ref, x_ref, wg_ref, bg_ref, wu_ref, bu_ref, wd_ref, bd_ref,
                   o_ref, wg_scr, wu_scr, wd_scr, *, blk):
    del be_ref, bsrc_ref
    i = pl.program_id(0)

    @pl.when(first_ref[i] == 1)
    def _():
        wg_scr[...] = wg_ref[0].astype(BF16)
        wu_scr[...] = wu_ref[0].astype(BF16)
        wd_scr[...] = wd_ref[0].astype(BF16)

    @pl.when(act_ref[i] == 1)
    def _():
        x = _load_token_major(x_ref, 0, blk).astype(BF16)
        g = jnp.dot(x, wg_scr[...], preferred_element_type=F32) + bg_ref[0]
        u = jnp.dot(x, wu_scr[...], preferred_element_type=F32) + bu_ref[0]
        g = jnp.minimum(g, SWIGLU_LIMIT)
        u = jnp.clip(u, -SWIGLU_LIMIT, SWIGLU_LIMIT)
        act = (u + 1.0) * (g * jax.nn.sigmoid(SWIGLU_ALPHA * g))
        y = jnp.dot(act.astype(BF16), wd_scr[...], preferred_element_type=F32) + bd_ref[0]
        _store_token_major(o_ref, y)


def _experts(x_slots, blk, block_expert, block_src, block_active, block_first, wg, bg, wu, bu, wd, bd):
    D = D_MODEL
    nb = x_slots.shape[0] // (blk * ROW_TILES)
    wspec = pl.BlockSpec((1, D, D_FF), lambda i, be, bs, ac, fi: (be[i], 0, 0))
    bspec = pl.BlockSpec((1, 1, D_FF), lambda i, be, bs, ac, fi: (be[i], 0, 0))
    xspec = pl.BlockSpec((blk * ROW_TILES, LANES), lambda i, be, bs, ac, fi: (bs[i], 0))
    grid_spec = pltpu.PrefetchScalarGridSpec(
        num_scalar_prefetch=4,
        grid=(nb,),
        in_specs=[xspec, wspec, bspec, wspec, bspec, wspec, bspec],
        out_specs=xspec,
        scratch_shapes=[pltpu.VMEM((D, D_FF), BF16), pltpu.VMEM((D, D_FF), BF16),
                        pltpu.VMEM((D_FF, D), BF16)],
    )
    b3 = lambda b: b.reshape(N_EXPERTS, 1, -1)
    return pl.pallas_call(
        functools.partial(_expert_kernel, blk=blk),
        grid_spec=grid_spec,
        out_shape=jax.ShapeDtypeStruct(x_slots.shape, F32),
        compiler_params=_params(("arbitrary",)),
        name="experts",
    )(block_expert, block_src, block_active, block_first, x_slots,
      wg, b3(bg), wu, b3(bu), wd, b3(bd))


def _combine_kernel(dest_ref, dest_next_ref, ys_ref, w_ref, x2_ref, g_ref, o_ref, buf, sems, *, tt):
    i = pl.program_id(0)
    n = pl.num_programs(0)
    cur = i % 2

    def row_copy(idx_ref, slot, t, k):
        return pltpu.make_async_copy(_row_tile(ys_ref, idx_ref[t * TOP_K + k]),
                                     _row_tile(buf, (slot * TOP_K + k) * tt + t), sems.at[slot])

    def gather(idx_ref, slot):
        _for_each_token_k(tt, lambda t, k: row_copy(idx_ref, slot, t, k).start(priority=k % 2))

    @pl.when(i == 0)
    def _():
        gather(dest_ref, 0)

    @pl.when(i + 1 < n)
    def _():
        gather(dest_next_ref, 1 - cur)

    _for_each_token_k(tt, lambda t, k: row_copy(dest_ref, cur, t, k).wait())
    w = w_ref[...]
    y = x2_ref[...]
    for k in range(TOP_K):
        y = y + w[:, k:k + 1] * _load_token_major(buf, (cur * TOP_K + k) * tt, tt)
    inv = lax.rsqrt(jnp.mean(y * y, axis=-1, keepdims=True) + EPS)
    o_ref[...] = y * inv * g_ref[...]


def _combine(dest_flat, y_slots, top_w, x2, g_final):
    T, D = x2.shape
    tt = min(256, T)
    nt = T // tt
    kern = functools.partial(_combine_kernel, tt=tt)
    return pl.pallas_call(
        kern,
        grid=(nt,),
        in_specs=[pl.BlockSpec((tt * TOP_K,), lambda i: (i,), memory_space=pltpu.SMEM),
                  pl.BlockSpec((tt * TOP_K,), lambda i: (jnp.minimum(i + 1, nt - 1),),
                               memory_space=pltpu.SMEM),
                  pl.BlockSpec(memory_space=pl.ANY),
                  pl.BlockSpec((tt, LANES), lambda i: (i, 0)),
                  pl.BlockSpec((tt, D), lambda i: (i, 0)),
                  pl.BlockSpec((1, D), lambda i: (0, 0))],
        out_specs=pl.BlockSpec((tt, D), lambda i: (i, 0)),
        out_shape=jax.ShapeDtypeStruct((T, D), F32),
        scratch_shapes=[pltpu.VMEM((2 * TOP_K * tt * ROW_TILES, LANES), F32),
                        pltpu.SemaphoreType.DMA((2,))],
        compiler_params=_params(("arbitrary",)),
        name="combine",
    )(dest_flat, dest_flat, y_slots, top_w, x2, g_final.reshape(1, D))


def _moe_block_size(T):
    return min(512, max(SUBLANES, T * TOP_K // N_EXPERTS))


def _block_tables(counts, blk, nb):
    cnt = counts.astype(I32)
    padded = (cnt + blk - 1) // blk * blk
    ends = jnp.cumsum(padded)
    used = ends[-1] // blk
    starts = jnp.arange(nb, dtype=I32) * blk
    src = jnp.minimum(jnp.arange(nb, dtype=I32), jnp.maximum(used - 1, 0))
    expert = jnp.minimum(jnp.sum((starts[:, None] >= ends[None, :]).astype(I32), axis=1), N_EXPERTS - 1)
    expert = expert[src]
    active = (jnp.arange(nb, dtype=I32) < used).astype(I32)
    prev = jnp.concatenate([jnp.full((1,), -1, I32), expert[:-1]])
    first = active * (expert != prev).astype(I32)
    return ends.astype(I32), expert, src, active, first


def kernel(x, norm_mix_g, w_in, conv_w, conv_b, w_rg_a, b_rg_a, w_rg_x, b_rg_x, lru_lambda, diff_lambda_q1, diff_lambda_k1, diff_lambda_q2, diff_lambda_k2, subln_g, rel_bias_table, w_proj_rnn, w_proj_att, w_out, norm_ffn_g, w_router, b_router, w_gate_e, b_gate_e, w_up_e, b_up_e, w_down_e, b_down_e, norm_final_g):
    B, S, D = x.shape
    T = B * S
    assert norm_mix_g.shape[0] == 1, "single-layer block: the final norm is fused into the MoE combine"
    l = 0
    xt = x.reshape(T, D)
    lam_init = 0.8 - 0.6 * math.exp(-0.3 * l)
    proj = _inproj(xt, norm_mix_g[l], w_in[l].astype(BF16))
    y_rnn = _rglru(proj, B, S, conv_w[l], conv_b[l], w_rg_a[l], b_rg_a[l], w_rg_x[l], b_rg_x[l],
                   lru_lambda[l])
    y_att = _attention(proj, B, S, diff_lambda_q1[l], diff_lambda_k1[l], diff_lambda_q2[l],
                       diff_lambda_k2[l], subln_g[l], rel_bias_table, lam_init)
    x2, hf, top_idx, top_w = _merge_router(xt, y_rnn, y_att, proj, w_proj_rnn[l], w_proj_att[l],
                                           w_out[l], norm_ffn_g[l], w_router[l], b_router[l])
    blk = _moe_block_size(T)
    nb = T * TOP_K // blk + N_EXPERTS
    dest, counts = _route(top_idx, blk)
    dest_flat = dest[:, :TOP_K].reshape(-1)
    ends, expert, src, active, first = _block_tables(counts[0, :N_EXPERTS], blk, nb)
    x_slots = _dispatch(ends, dest_flat, hf, nb * blk, blk)
    y_slots = _experts(x_slots, blk, expert, src, active, first, w_gate_e[l], b_gate_e[l], w_up_e[l],
                       b_up_e[l], w_down_e[l], b_down_e[l])
    out = _combine(dest_flat, y_slots, top_w, x2, norm_final_g)
    return out.reshape(B, S, D)
```

```python
import functools
import math

import jax
import jax.numpy as jnp
from jax import lax
from jax.experimental import pallas as pl
from jax.experimental.pallas import tpu as pltpu

F32 = jnp.float32
BF16 = jnp.bfloat16
I32 = jnp.int32

D_MODEL = 1024
D_RNN = 1024
RNN_BLOCKS = 16
RNN_BLOCK = D_RNN // RNN_BLOCKS
CONV_W = 4
RGLRU_C = 8.0
N_HEADS = 8
HEAD_DIM = 64
V_DIM = 2 * HEAD_DIM
ATT_QK = N_HEADS * 2 * HEAD_DIM
ATT_V = N_HEADS * V_DIM
D_IN = 2 * D_RNN + 2 * ATT_QK + ATT_V + 2 * D_MODEL
NUM_BUCKETS = 32
MAX_EXACT = NUM_BUCKETS // 2
MAX_DISTANCE = 128
N_EXPERTS = 32
TOP_K = 4
D_FF = D_MODEL
SWIGLU_LIMIT = 7.0
SWIGLU_ALPHA = 1.702
EPS = 1e-6
NEG_INF = -1e30

LANES = 128
SUBLANES = 8
MXU_DIM = 256
VMEM_LIMIT = 56 * 1024 * 1024

COL_XR, COL_GR, COL_Q, COL_K, COL_V, COL_G0, COL_G1 = range(7)


def _params(sem, vmem=VMEM_LIMIT):
    return pltpu.CompilerParams(dimension_semantics=sem, vmem_limit_bytes=vmem)


ROW_TILES = D_MODEL // LANES
assert ROW_TILES == SUBLANES


def _store_token_major(ref, val):
    n = val.shape[0]
    for c in range(ROW_TILES):
        ref[pl.ds(c, n, stride=ROW_TILES), :] = val[:, c * LANES:(c + 1) * LANES]


def _load_token_major(ref, start, n):
    return jnp.concatenate(
        [ref[pl.ds(start * ROW_TILES + c, n, stride=ROW_TILES), :] for c in range(ROW_TILES)], axis=-1)


def _inproj_kernel(x_ref, g_ref, w_ref, o_ref, h_scr):
    @pl.when(pl.program_id(1) == 0)
    def _():
        x = x_ref[...]
        inv = lax.rsqrt(jnp.mean(x * x, axis=-1, keepdims=True) + EPS)
        h_scr[...] = (x * inv * g_ref[...]).astype(BF16)

    o_ref[...] = jnp.dot(h_scr[...], w_ref[...], preferred_element_type=F32).astype(o_ref.dtype)


def _inproj(x2d, g, w_bf16):
    T, D = x2d.shape
    N = w_bf16.shape[1]
    tm = min(1024, T)
    tn = 3584 if N % 3584 == 0 else 1024
    return pl.pallas_call(
        _inproj_kernel,
        grid=(T // tm, N // tn),
        in_specs=[
            pl.BlockSpec((tm, D), lambda i, j: (i, 0)),
            pl.BlockSpec((1, D), lambda i, j: (0, 0)),
            pl.BlockSpec((D, tn), lambda i, j: (0, j)),
        ],
        out_specs=pl.BlockSpec((tm, tn), lambda i, j: (i, j)),
        out_shape=jax.ShapeDtypeStruct((T, N), BF16),
        scratch_shapes=[pltpu.VMEM((tm, D), BF16)],
        compiler_params=_params(("parallel", "arbitrary")),
        name="inproj",
    )(x2d, g.reshape(1, D), w_bf16)


def _rglru_kernel(xr_ref, gr_ref, cw_ref, cb_ref, wa_ref, ba_ref, wx_ref, bx_ref, lam_ref,
                  o_ref, xbuf, tail_scr, a_scr, u_scr, h_scr):
    nb, ts, _ = xr_ref.shape
    pad = SUBLANES
    nslab = D_RNN // LANES

    @pl.when(pl.program_id(0) == 0)
    def _():
        tail_scr[...] = jnp.zeros_like(tail_scr)
        h_scr[...] = jnp.zeros_like(h_scr)

    cw = cw_ref[...]
    z = -lam_ref[...]
    softplus = jnp.maximum(z, 0.0) + jnp.log1p(jnp.exp(-jnp.abs(z)))
    sigmoid = lambda v: 0.5 * jnp.tanh(0.5 * v) + 0.5
    nchunk = D_RNN // MXU_DIM
    for b in range(nb):
        x = xr_ref[b].astype(F32)
        xbuf[0:pad, :] = tail_scr[b]
        xbuf[pad:pad + ts, :] = x
        xc = (cw[3:4, :] * x
              + cw[2:3, :] * xbuf[pad - 1:pad - 1 + ts, :]
              + cw[1:2, :] * xbuf[pad - 2:pad - 2 + ts, :]
              + cw[0:1, :] * xbuf[pad - 3:pad - 3 + ts, :]) + cb_ref[...]
        tail_scr[b] = x[ts - pad:ts, :]
        xcb = xc.astype(BF16)
        r_pre = jnp.concatenate(
            [jnp.dot(xcb[:, c * MXU_DIM:(c + 1) * MXU_DIM], wa_ref[c], preferred_element_type=F32)
             for c in range(nchunk)], axis=-1)
        i_pre = jnp.concatenate(
            [jnp.dot(xcb[:, c * MXU_DIM:(c + 1) * MXU_DIM], wx_ref[c], preferred_element_type=F32)
             for c in range(nchunk)], axis=-1)
        r = sigmoid(r_pre + ba_ref[...])
        ig = sigmoid(i_pre + bx_ref[...])
        a = jnp.exp((-RGLRU_C) * r * softplus)
        u = jnp.sqrt(1.0 - a * a) * (ig * xc)
        for c in range(nslab):
            a_scr[c, pl.ds(b, ts, stride=nb), :] = a[:, c * LANES:(c + 1) * LANES]
            u_scr[c, pl.ds(b, ts, stride=nb), :] = u[:, c * LANES:(c + 1) * LANES]

    def body(g, hs):
        for j in range(SCAN_UNROLL):
            off = pl.multiple_of((g * SCAN_UNROLL + j) * nb, nb)
            new = []
            for c in range(nslab):
                h = a_scr[c, pl.ds(off, nb), :] * hs[c] + u_scr[c, pl.ds(off, nb), :]
                u_scr[c, pl.ds(off, nb), :] = h
                new.append(h)
            hs = tuple(new)
        return hs

    hs = lax.fori_loop(0, ts // SCAN_UNROLL, body, tuple(h_scr[c] for c in range(nslab)))
    for c in range(nslab):
        h_scr[c] = hs[c]
    for b in range(nb):
        h = jnp.concatenate([u_scr[c, pl.ds(b, ts, stride=nb), :] for c in range(nslab)], axis=-1)
        gate = jax.nn.gelu(gr_ref[b].astype(F32), approximate=True)
        o_ref[b] = (h * gate).astype(o_ref.dtype)


SCAN_UNROLL = 8


def _block_diag_chunks(w):
    per = MXU_DIM // RNN_BLOCK
    w = w.reshape(D_RNN // MXU_DIM, per, RNN_BLOCK, RNN_BLOCK)
    eye = jnp.eye(per, dtype=w.dtype)
    out = jnp.einsum('gpcd,pq->gpcqd', w, eye)
    return out.reshape(D_RNN // MXU_DIM, MXU_DIM, MXU_DIM)


def _rglru(proj, B, S, conv_w, conv_b, w_a, b_a, w_x, b_x, lru_lambda):
    assert B <= SUBLANES, "all batch rows of a time step share one vreg in the scan"
    ts = min(256, S)
    ns = S // ts
    wa = _block_diag_chunks(w_a).astype(BF16)
    wx = _block_diag_chunks(w_x).astype(BF16)
    nchunk = D_RNN // MXU_DIM
    nslab = D_RNN // LANES
    row = lambda v: v.reshape(1, D_RNN)
    const2 = lambda s: (0, 0)
    proj3 = proj.reshape(B, S, proj.shape[-1])
    out = pl.pallas_call(
        _rglru_kernel,
        grid=(ns,),
        in_specs=[
            pl.BlockSpec((B, ts, D_RNN), lambda s: (0, s, COL_XR)),
            pl.BlockSpec((B, ts, D_RNN), lambda s: (0, s, COL_GR)),
            pl.BlockSpec((CONV_W, D_RNN), const2),
            pl.BlockSpec((1, D_RNN), const2),
            pl.BlockSpec((nchunk, MXU_DIM, MXU_DIM), lambda s: (0, 0, 0)),
            pl.BlockSpec((1, D_RNN), const2),
            pl.BlockSpec((nchunk, MXU_DIM, MXU_DIM), lambda s: (0, 0, 0)),
            pl.BlockSpec((1, D_RNN), const2),
            pl.BlockSpec((1, D_RNN), const2),
        ],
        out_specs=pl.BlockSpec((B, ts, D_RNN), lambda s: (0, s, 0)),
        out_shape=jax.ShapeDtypeStruct((B, S, D_RNN), BF16),
        scratch_shapes=[
            pltpu.VMEM((ts + SUBLANES, D_RNN), F32),
            pltpu.VMEM((B, SUBLANES, D_RNN), F32),
            pltpu.VMEM((nslab, ts * B, LANES), F32),
            pltpu.VMEM((nslab, ts * B, LANES), F32),
            pltpu.VMEM((nslab, B, LANES), F32),
        ],
        compiler_params=_params(("arbitrary",)),
        name="rglru",
    )(proj3, proj3, conv_w, row(conv_b), wa, row(b_a), wx, row(b_x), row(lru_lambda))
    return out.reshape(B * S, D_RNN)


def _attn_kernel(q_ref, k_ref, v_ref, bias_ref, lq1_ref, lk1_ref, lq2_ref, lk2_ref, sg_ref,
                 o_ref, qs_scr, vx_scr, s_scr, m_scr, acc_scr, *, tq, lam_init):
    S = q_ref.shape[0]
    nq = S // tq
    scale = HEAD_DIM ** -0.5 * LOG2E
    lane = lax.broadcasted_iota(I32, (tq, V_DIM), 1)
    lam = (jnp.exp(jnp.sum(lq1_ref[...] * lk1_ref[...], keepdims=True))
           - jnp.exp(jnp.sum(lq2_ref[...] * lk2_ref[...], keepdims=True)) + lam_init)
    vx_scr[:, 0:V_DIM] = v_ref[...]
    vx_scr[:, V_DIM:2 * V_DIM] = jnp.ones((S, V_DIM), BF16)

    Z0 = 2

    def prep_q(qi, buf):
        q = (q_ref[pl.ds(pl.multiple_of(qi * tq, tq), tq), :].astype(F32) * scale).astype(BF16)
        zero = jnp.zeros_like(q)
        qs_scr[buf, 0:tq, :] = jnp.where(lane < HEAD_DIM, q, zero)
        qs_scr[buf, tq:2 * tq, :] = jnp.where(lane >= HEAD_DIM, q, zero)

    def scores(buf, j, dst):
        k = k_ref[pl.ds(pl.multiple_of(j * tq, tq), tq), :]
        s_scr[dst] = lax.dot_general(qs_scr[buf], k, (((1,), (1,)), ((), ())),
                                     preferred_element_type=F32)

    def step(j, src, bias_idx, prefetch):
        scores(*prefetch)
        s = s_scr[src]
        if bias_idx is not None:
            b = bias_ref[bias_idx]
            s = s + jnp.concatenate([b, b], axis=0)
        vx = vx_scr[pl.ds(pl.multiple_of(j * tq, tq), tq), :]
        m_prev = m_scr[...]
        m_new = jnp.maximum(m_prev, jnp.max(s, axis=1)[:, None])
        p = jnp.exp2(s - jnp.tile(m_new, (1, tq // LANES)))
        alpha = jnp.exp2(m_prev - m_new)
        acc_scr[...] = (jnp.tile(alpha, (1, 2)) * acc_scr[...]
                        + jnp.dot(p.astype(BF16), vx, preferred_element_type=F32))
        m_scr[...] = m_new

    def begin_block():
        m_scr[...] = jnp.full(m_scr.shape, NEG_INF, F32)
        acc_scr[...] = jnp.zeros(acc_scr.shape, F32)

    def diag_step(qi, par, src):
        nxt = jnp.minimum(qi + 1, nq - 1)
        prep_q(nxt, 1 - par)
        step(qi, src, 0, (1 - par, 0, Z0 + 1 - par))

    def end_block(qi):
        acc = acc_scr[...]
        o_all = acc[:, 0:V_DIM] / acc[:, V_DIM:2 * V_DIM]
        o = o_all[0:tq, :] - lam * o_all[tq:2 * tq, :]
        inv = lax.rsqrt(jnp.mean(o * o, axis=-1, keepdims=True) + EPS)
        y = (o * inv * sg_ref[...]) * (1.0 - lam_init)
        o_ref[pl.ds(pl.multiple_of(qi * tq, tq), tq), :] = y.astype(o_ref.dtype)

    def far_pairs(par, npairs):
        def body(m, c):
            step(2 * m + 1, 0, None, (par, 2 * m + 2, 1))
            step(2 * m + 2, 1, None, (par, 2 * m + 3, 0))
            return c

        lax.fori_loop(0, npairs, body, 0)

    prep_q(0, 0)
    scores(0, 0, Z0)
    begin_block()
    diag_step(0, 0, Z0)
    end_block(0)
    begin_block()
    step(0, Z0 + 1, 1, (1, 1, 0))
    diag_step(1, 1, 0)
    end_block(1)

    def block_pair(i, carry):
        qi = 2 * i
        begin_block()
        step(0, Z0, None, (0, 1, 0))
        far_pairs(0, i - 1)
        step(qi - 1, 0, 1, (0, qi, 1))
        diag_step(qi, 0, 1)
        end_block(qi)
        qi = 2 * i + 1
        begin_block()
        step(0, Z0 + 1, None, (1, 1, 0))
        far_pairs(1, i - 1)
        step(qi - 2, 0, None, (1, qi - 1, 1))
        step(qi - 1, 1, 1, (1, qi, 0))
        diag_step(qi, 1, 0)
        end_block(qi)
        return carry

    lax.fori_loop(1, nq // 2, block_pair, 0)


def _rel_bucket(n):
    n = jnp.maximum(n, 0)
    nf = jnp.maximum(n, MAX_EXACT).astype(F32)
    large = MAX_EXACT + (jnp.log(nf / MAX_EXACT) / math.log(MAX_DISTANCE / MAX_EXACT)
                         * (NUM_BUCKETS - MAX_EXACT)).astype(I32)
    large = jnp.minimum(large, NUM_BUCKETS - 1)
    return jnp.where(n < MAX_EXACT, n, large)


def _bias_tiles(rel_table, tq):
    i = jnp.arange(tq, dtype=I32)[:, None]
    j = jnp.arange(tq, dtype=I32)[None, :]
    table = rel_table.astype(F32) - rel_table[NUM_BUCKETS - 1].astype(F32)[None, :]
    tiles = []
    for delta in (0, tq):
        n = i - j + delta
        onehot = (_rel_bucket(n)[:, :, None] == jnp.arange(NUM_BUCKETS, dtype=I32)).astype(F32)
        b = jnp.einsum('ijb,bh->hij', onehot, table, precision=lax.Precision.HIGHEST) * LOG2E
        tiles.append(jnp.where((n >= 0)[None], b, NEG_INF))
    return jnp.stack(tiles, axis=1)


ATT_TQ = 512
LOG2E = math.log2(math.e)


def _attention(proj, B, S, lq1, lk1, lq2, lk2, subln_g, rel_table, lam_init):
    T = B * S
    tq = min(ATT_TQ, S // 2)
    assert V_DIM == LANES and MAX_DISTANCE <= tq and S % (2 * tq) == 0
    bias = _bias_tiles(rel_table, tq)
    vec = lambda v: v.reshape(1, -1).astype(F32)
    const2 = lambda b, h: (0, 0)
    kern = functools.partial(_attn_kernel, tq=tq, lam_init=lam_init)
    return pl.pallas_call(
        kern,
        grid=(B, N_HEADS),
        in_specs=[
            pl.BlockSpec((S, V_DIM), lambda b, h: (b, COL_Q * N_HEADS + h)),
            pl.BlockSpec((S, V_DIM), lambda b, h: (b, COL_K * N_HEADS + h)),
            pl.BlockSpec((S, V_DIM), lambda b, h: (b, COL_V * N_HEADS + h)),
            pl.BlockSpec((None, 2, tq, tq), lambda b, h: (h, 0, 0, 0)),
            pl.BlockSpec((1, HEAD_DIM), const2),
            pl.BlockSpec((1, HEAD_DIM), const2),
            pl.BlockSpec((1, HEAD_DIM), const2),
            pl.BlockSpec((1, HEAD_DIM), const2),
            pl.BlockSpec((1, V_DIM), const2),
        ],
        out_specs=pl.BlockSpec((S, V_DIM), lambda b, h: (b, h)),
        out_shape=jax.ShapeDtypeStruct((T, ATT_V), BF16),
        scratch_shapes=[
            pltpu.VMEM((2, 2 * tq, V_DIM), BF16),
            pltpu.VMEM((S, 2 * V_DIM), BF16),
            pltpu.VMEM((4, 2 * tq, tq), F32),
            pltpu.VMEM((2 * tq, LANES), F32),
            pltpu.VMEM((2 * tq, 2 * V_DIM), F32),
        ],
        compiler_params=_params(("parallel", "arbitrary")),
        name="diff_attn",
    )(proj, proj, proj, bias, vec(lq1), vec(lk1), vec(lq2), vec(lk2), vec(subln_g))


def _merge_kernel(x_ref, yr_ref, ya_ref, g0_ref, g1_ref, wr_ref, wa_ref, wo_ref, gf_ref, wrt_ref, brt_ref,
                  x2_ref, hf_ref, idx_ref, wgt_ref):
    pr = jnp.dot(yr_ref[...], wr_ref[...], preferred_element_type=F32)
    pa = jnp.dot(ya_ref[...], wa_ref[...], preferred_element_type=F32)
    merged = (jax.nn.sigmoid(g0_ref[...].astype(F32)) * pr
              + jax.nn.sigmoid(g1_ref[...].astype(F32)) * pa)
    x2 = x_ref[...] + jnp.dot(merged.astype(BF16), wo_ref[...], preferred_element_type=F32)
    x2_ref[...] = x2
    inv = lax.rsqrt(jnp.mean(x2 * x2, axis=-1, keepdims=True) + EPS)
    hf = x2 * inv * gf_ref[...]
    _store_token_major(hf_ref, hf)
    hi = hf.astype(BF16)
    lo = (hf - hi.astype(F32)).astype(BF16)
    hw = jnp.dot(hi, wrt_ref[...], preferred_element_type=F32)
    lw = jnp.dot(lo, wrt_ref[:, 0:LANES], preferred_element_type=F32)
    logits = hw[:, 0:LANES] + hw[:, LANES:2 * LANES] + lw + brt_ref[...]
    tm = logits.shape[0]
    lane = lax.broadcasted_iota(I32, (tm, LANES), 1)
    work = jnp.where(lane < N_EXPERTS, logits, -jnp.inf)
    idx_out = jnp.zeros((tm, LANES), I32)
    val_out = jnp.full((tm, LANES), -jnp.inf, F32)
    for k in range(TOP_K):
        mx = jnp.max(work, axis=-1, keepdims=True)
        sel = jnp.min(jnp.where(work == mx, lane, LANES), axis=-1, keepdims=True)
        idx_out = jnp.where(lane == k, sel, idx_out)
        val_out = jnp.where(lane == k, mx, val_out)
        work = jnp.where(lane == sel, -jnp.inf, work)
    e = jnp.exp(val_out - jnp.max(val_out, axis=-1, keepdims=True))
    idx_ref[...] = idx_out
    wgt_ref[...] = e / jnp.sum(e, axis=-1, keepdims=True)


def _merge_router(x2d, y_rnn, y_att, proj, w_pr, w_pa, w_o, g_ffn, w_router, b_router):
    T, D = x2d.shape
    tm = min(512, T)
    w_hi = w_router.astype(BF16)
    w_lo = (w_router - w_hi.astype(F32)).astype(BF16)
    wrt = (jnp.zeros((D, 2 * LANES), BF16).at[:, :N_EXPERTS].set(w_hi)
           .at[:, LANES:LANES + N_EXPERTS].set(w_lo))
    brt = jnp.zeros((1, LANES), F32).at[0, :N_EXPERTS].set(b_router)
    rowblk = lambda c: pl.BlockSpec((tm, D), lambda i, c=c: (i, c))
    full = lambda a: pl.BlockSpec(a.shape, lambda i: (0,) * a.ndim)
    wr, wa, wo = w_pr.astype(BF16), w_pa.astype(BF16), w_o.astype(BF16)
    gf = g_ffn.reshape(1, D)
    return pl.pallas_call(
        _merge_kernel,
        grid=(T // tm,),
        in_specs=[rowblk(0), rowblk(0), rowblk(0), rowblk(COL_G0), rowblk(COL_G1),
                  full(wr), full(wa), full(wo), full(gf), full(wrt), full(brt)],
        out_specs=[rowblk(0), pl.BlockSpec((tm * ROW_TILES, LANES), lambda i: (i, 0)),
                   pl.BlockSpec((tm, LANES), lambda i: (i, 0)),
                   pl.BlockSpec((tm, LANES), lambda i: (i, 0))],
        out_shape=[jax.ShapeDtypeStruct((T, D), F32), jax.ShapeDtypeStruct((T * ROW_TILES, LANES), F32),
                   jax.ShapeDtypeStruct((T, LANES), I32), jax.ShapeDtypeStruct((T, LANES), F32)],
        compiler_params=_params(("parallel",)),
        name="merge_router",
    )(x2d, y_rnn, y_att, proj, proj, wr, wa, wo, gf, wrt, brt)


def _route_kernel(idx_ref, dest_ref, cnt_ref, cnt_scr, run_scr, start_scr, *, blk):
    ph = pl.program_id(0)
    i = pl.program_id(1)
    tt = idx_ref.shape[0]
    idx = idx_ref[...]
    lane = lax.broadcasted_iota(I32, (tt, LANES), 1)
    onehot = jnp.zeros((tt, LANES), F32)
    for k in range(TOP_K):
        onehot = onehot + (idx[:, k:k + 1] == lane).astype(F32)

    @pl.when((ph == 0) & (i == 0))
    def _():
        cnt_scr[...] = jnp.zeros_like(cnt_scr)

    @pl.when(ph == 0)
    def _():
        cnt_scr[...] += jnp.sum(onehot, axis=0, keepdims=True)

    @pl.when((ph == 1) & (i == 0))
    def _():
        cnt = jnp.broadcast_to(cnt_scr[...], (SUBLANES, LANES))
        padded = jnp.floor((cnt + (blk - 1)) / blk) * blk
        l8 = lax.broadcasted_iota(I32, (SUBLANES, LANES), 1)
        incl = padded
        d = 1
        while d < LANES:
            incl = incl + jnp.where(l8 >= d, pltpu.roll(incl, d, 1), 0.0)
            d *= 2
        start_scr[...] = (incl - padded)[0:1, :]
        run_scr[...] = jnp.zeros_like(run_scr)

    @pl.when(ph == 1)
    def _():
        r = lax.broadcasted_iota(I32, (tt, tt), 0)
        c = lax.broadcasted_iota(I32, (tt, tt), 1)
        tri = (c < r).astype(BF16)
        before = jnp.dot(tri, onehot.astype(BF16), preferred_element_type=F32)
        base = before + run_scr[...] + start_scr[...]
        out = jnp.zeros((tt, LANES), F32)
        for k in range(TOP_K):
            dk = jnp.sum(jnp.where(idx[:, k:k + 1] == lane, base, 0.0), axis=-1, keepdims=True)
            out = jnp.where(lane == k, dk, out)
        dest_ref[...] = out.astype(I32)
        run_scr[...] += jnp.sum(onehot, axis=0, keepdims=True)
        cnt_ref[...] = jnp.broadcast_to(cnt_scr[...], cnt_ref.shape)


def _route(top_idx, blk):
    T = top_idx.shape[0]
    tt = min(512, T)
    kern = functools.partial(_route_kernel, blk=blk)
    return pl.pallas_call(
        kern,
        grid=(2, T // tt),
        in_specs=[pl.BlockSpec((tt, LANES), lambda p, i: (i, 0))],
        out_specs=[pl.BlockSpec((tt, LANES), lambda p, i: (p * i, 0)),
                   pl.BlockSpec((SUBLANES, LANES), lambda p, i: (0, 0))],
        out_shape=[jax.ShapeDtypeStruct((T, LANES), I32),
                   jax.ShapeDtypeStruct((SUBLANES, LANES), F32)],
        scratch_shapes=[pltpu.VMEM((1, LANES), F32), pltpu.VMEM((1, LANES), F32),
                        pltpu.VMEM((1, LANES), F32)],
        compiler_params=_params(("arbitrary", "arbitrary")),
        name="route",
    )(top_idx)


TOKEN_UNROLL = 2


def _for_each_token_k(n_tok, fn):
    def body(g, c):
        for u in range(TOKEN_UNROLL):
            for k in range(TOP_K):
                fn(g * TOKEN_UNROLL + u, k)
        return c

    lax.fori_loop(0, n_tok // TOKEN_UNROLL, body, 0)


def _row_tile(ref, row):
    return ref.at[pl.ds(pl.multiple_of(row * ROW_TILES, ROW_TILES), ROW_TILES), :]


def _dispatch_kernel(ends_ref, dest_ref, hf_ref, xs_ref, zero_buf, sem, zsem, *, tt, blk):
    @pl.when(pl.program_id(0) == 0)
    def _():
        zero_buf[...] = jnp.zeros_like(zero_buf)

        def tail_fill(e):
            prev_end = ends_ref[e - 1] if e > 0 else 0
            start = pl.multiple_of((ends_ref[e] - blk) * ROW_TILES, ROW_TILES)
            return ends_ref[e] > prev_end, pltpu.make_async_copy(
                zero_buf, xs_ref.at[pl.ds(start, blk * ROW_TILES), :], zsem)

        for e in range(N_EXPERTS):
            nonempty, cp = tail_fill(e)
            pl.when(nonempty)(cp.start)
        for e in range(N_EXPERTS):
            nonempty, cp = tail_fill(e)
            pl.when(nonempty)(cp.wait)

    def row_copy(t, k):
        return pltpu.make_async_copy(_row_tile(hf_ref, t),
                                     _row_tile(xs_ref, dest_ref[t * TOP_K + k]), sem)

    _for_each_token_k(tt, lambda t, k: row_copy(t, k).start(priority=k % 2))
    _for_each_token_k(tt, lambda t, k: row_copy(t, k).wait())


def _dispatch(ends, dest_flat, hf_tm, n_slots, blk):
    T = hf_tm.shape[0] // ROW_TILES
    tt = min(512, T)
    kern = functools.partial(_dispatch_kernel, tt=tt, blk=blk)
    grid_spec = pltpu.PrefetchScalarGridSpec(
        num_scalar_prefetch=1,
        grid=(T // tt,),
        in_specs=[pl.BlockSpec((tt * TOP_K,), lambda i, ends: (i,), memory_space=pltpu.SMEM),
                  pl.BlockSpec((tt * ROW_TILES, LANES), lambda i, ends: (i, 0))],
        out_specs=pl.BlockSpec(memory_space=pl.ANY),
        scratch_shapes=[pltpu.VMEM((blk * ROW_TILES, LANES), F32),
                        pltpu.SemaphoreType.DMA(()), pltpu.SemaphoreType.DMA(())],
    )
    return pl.pallas_call(
        kern,
        grid_spec=grid_spec,
        out_shape=jax.ShapeDtypeStruct((n_slots * ROW_TILES, LANES), F32),
        compiler_params=_params(("arbitrary",)),
        name="dispatch",
    )(ends, dest_flat, hf_tm)


def _expert_kernel(be_ref, bsrc_ref, act_ref, first_ref, x_ref, wg_ref, bg_ref, wu_ref, bu_ref, wd_ref, bd_ref,
                   o_ref, wg_scr, wu_scr, wd_scr, *, blk):
    del be_ref, bsrc_ref
    i = pl.program_id(0)

    @pl.when(first_ref[i] == 1)
    def _():
        wg_scr[...] = wg_ref[0].astype(BF16)
        wu_scr[...] = wu_ref[0].astype(BF16)
        wd_scr[...] = wd_ref[0].astype(BF16)

    @pl.when(act_ref[i] == 1)
    def _():
        x = _load_token_major(x_ref, 0, blk).astype(BF16)
        g = jnp.dot(x, wg_scr[...], preferred_element_type=F32) + bg_ref[0]
        u = jnp.dot(x, wu_scr[...], preferred_element_type=F32) + bu_ref[0]
        g = jnp.minimum(g, SWIGLU_LIMIT)
        u = jnp.clip(u, -SWIGLU_LIMIT, SWIGLU_LIMIT)
        act = (u + 1.0) * (g * jax.nn.sigmoid(SWIGLU_ALPHA * g))
        y = jnp.dot(act.astype(BF16), wd_scr[...], preferred_element_type=F32) + bd_ref[0]
        _store_token_major(o_ref, y)


def _experts(x_slots, blk, block_expert, block_src, block_active, block_first, wg, bg, wu, bu, wd, bd):
    D = D_MODEL
    nb = x_slots.shape[0] // (blk * ROW_TILES)
    wspec = pl.BlockSpec((1, D, D_FF), lambda i, be, bs, ac, fi: (be[i], 0, 0))
    bspec = pl.BlockSpec((1, 1, D_FF), lambda i, be, bs, ac, fi: (be[i], 0, 0))
    xspec = pl.BlockSpec((blk * ROW_TILES, LANES), lambda i, be, bs, ac, fi: (bs[i], 0))
    grid_spec = pltpu.PrefetchScalarGridSpec(
        num_scalar_prefetch=4,
        grid=(nb,),
        in_specs=[xspec, wspec, bspec, wspec, bspec, wspec, bspec],
        out_specs=xspec,
        scratch_shapes=[pltpu.VMEM((D, D_FF), BF16), pltpu.VMEM((D, D_FF), BF16),
                        pltpu.VMEM((D_FF, D), BF16)],
    )
    b3 = lambda b: b.reshape(N_EXPERTS, 1, -1)
    return pl.pallas_call(
        functools.partial(_expert_kernel, blk=blk),
        grid_spec=grid_spec,
        out_shape=jax.ShapeDtypeStruct(x_slots.shape, F32),
        compiler_params=_params(("arbitrary",)),
        name="experts",
    )(block_expert, block_src, block_active, block_first, x_slots,
      wg, b3(bg), wu, b3(bu), wd, b3(bd))


def _combine_kernel(dest_ref, dest_next_ref, ys_ref, w_ref, x2_ref, g_ref, o_ref, buf, sems, *, tt):
    i = pl.program_id(0)
    n = pl.num_programs(0)
    cur = i % 2

    def row_copy(idx_ref, slot, t, k):
        return pltpu.make_async_copy(_row_tile(ys_ref, idx_ref[t * TOP_K + k]),
                                     _row_tile(buf, (slot * TOP_K + k) * tt + t), sems.at[slot])

    def gather(idx_ref, slot):
        _for_each_token_k(tt, lambda t, k: row_copy(idx_ref, slot, t, k).start(priority=k % 2))

    @pl.when(i == 0)
    def _():
        gather(dest_ref, 0)

    @pl.when(i + 1 < n)
    def _():
        gather(dest_next_ref, 1 - cur)

    _for_each_token_k(tt, lambda t, k: row_copy(dest_ref, cur, t, k).wait())
    w = w_ref[...]
    y = x2_ref[...]
    for k in range(TOP_K):
        y = y + w[:, k:k + 1] * _load_token_major(buf, (cur * TOP_K + k) * tt, tt)
    inv = lax.rsqrt(jnp.mean(y * y, axis=-1, keepdims=True) + EPS)
    o_ref[...] = y * inv * g_ref[...]


def _combine(dest_flat, y_slots, top_w, x2, g_final):
    T, D = x2.shape
    tt = min(256, T)
    nt = T // tt
    kern = functools.partial(_combine_kernel, tt=tt)
    return pl.pallas_call(
        kern,
        grid=(nt,),
        in_specs=[pl.BlockSpec((tt * TOP_K,), lambda i: (i,), memory_space=pltpu.SMEM),
                  pl.BlockSpec((tt * TOP_K,), lambda i: (jnp.minimum(i + 1, nt - 1),),
                               memory_space=pltpu.SMEM),
                  pl.BlockSpec(memory_space=pl.ANY),
                  pl.BlockSpec((tt, LANES), lambda i: (i, 0)),
                  pl.BlockSpec((tt, D), lambda i: (i, 0)),
                  pl.BlockSpec((1, D), lambda i: (0, 0))],
        out_specs=pl.BlockSpec((tt, D), lambda i: (i, 0)),
        out_shape=jax.ShapeDtypeStruct((T, D), F32),
        scratch_shapes=[pltpu.VMEM((2 * TOP_K * tt * ROW_TILES, LANES), F32),
                        pltpu.SemaphoreType.DMA((2,))],
        compiler_params=_params(("arbitrary",)),
        name="combine",
    )(dest_flat, dest_flat, y_slots, top_w, x2, g_final.reshape(1, D))


def _moe_block_size(T):
    return min(512, max(SUBLANES, T * TOP_K // N_EXPERTS))


def _block_tables(counts, blk, nb):
    cnt = counts.astype(I32)
    padded = (cnt + blk - 1) // blk * blk
    ends = jnp.cumsum(padded)
    used = ends[-1] // blk
    starts = jnp.arange(nb, dtype=I32) * blk
    src = jnp.minimum(jnp.arange(nb, dtype=I32), jnp.maximum(used - 1, 0))
    expert = jnp.minimum(jnp.sum((starts[:, None] >= ends[None, :]).astype(I32), axis=1), N_EXPERTS - 1)
    expert = expert[src]
    active = (jnp.arange(nb, dtype=I32) < used).astype(I32)
    prev = jnp.concatenate([jnp.full((1,), -1, I32), expert[:-1]])
    first = active * (expert != prev).astype(I32)
    return ends.astype(I32), expert, src, active, first


def kernel(x, norm_mix_g, w_in, conv_w, conv_b, w_rg_a, b_rg_a, w_rg_x, b_rg_x, lru_lambda, diff_lambda_q1, diff_lambda_k1, diff_lambda_q2, diff_lambda_k2, subln_g, rel_bias_table, w_proj_rnn, w_proj_att, w_out, norm_ffn_g, w_router, b_router, w_gate_e, b_gate_e, w_up_e, b_up_e, w_down_e, b_down_e, norm_final_g):
    B, S, D = x.shape
    T = B * S
    assert norm_mix_g.shape[0] == 1, "single-layer block: the final norm is fused into the MoE combine"
    l = 0
    xt = x.reshape(T, D)
    lam_init = 0.8 - 0.6 * math.exp(-0.3 * l)
    proj = _inproj(xt, norm_mix_g[l], w_in[l].astype(BF16))
    y_rnn = _rglru(proj, B, S, conv_w[l], conv_b[l], w_rg_a[l], b_rg_a[l], w_rg_x[l], b_rg_x[l],
                   lru_lambda[l])
    y_att = _attention(proj, B, S, diff_lambda_q1[l], diff_lambda_k1[l], diff_lambda_q2[l],
                       diff_lambda_k2[l], subln_g[l], rel_bias_table, lam_init)
    x2, hf, top_idx, top_w = _merge_router(xt, y_rnn, y_att, proj, w_proj_rnn[l], w_proj_att[l],
                                           w_out[l], norm_ffn_g[l], w_router[l], b_router[l])
    blk = _moe_block_size(T)
    nb = T * TOP_K // blk + N_EXPERTS
    dest, counts = _route(top_idx, blk)
    dest_flat = dest[:, :TOP_K].reshape(-1)
    ends, expert, src, active, first = _block_tables(counts[0, :N_EXPERTS], blk, nb)
    x_slots = _dispatch(ends, dest_flat, hf, nb * blk, blk)
    y_slots = _experts(x_slots, blk, expert, src, active, first, w_gate_e[l], b_gate_e[l], w_up_e[l],
                       b_up_e[l], w_down_e[l], b_down_e[l])
    out = _combine(dest_flat, y_slots, top_w, x2, norm_final_g)
    return out.reshape(B, S, D)
```

```python
import functools
import math

import jax
import jax.numpy as jnp
from jax import lax
from jax.experimental import pallas as pl
from jax.experimental.pallas import tpu as pltpu

F32 = jnp.float32
BF16 = jnp.bfloat16
I32 = jnp.int32

D_MODEL = 1024
D_RNN = 1024
RNN_BLOCKS = 16
RNN_BLOCK = D_RNN // RNN_BLOCKS
CONV_W = 4
RGLRU_C = 8.0
N_HEADS = 8
HEAD_DIM = 64
V_DIM = 2 * HEAD_DIM
ATT_QK = N_HEADS * 2 * HEAD_DIM
ATT_V = N_HEADS * V_DIM
D_IN = 2 * D_RNN + 2 * ATT_QK + ATT_V + 2 * D_MODEL
NUM_BUCKETS = 32
MAX_EXACT = NUM_BUCKETS // 2
MAX_DISTANCE = 128
N_EXPERTS = 32
TOP_K = 4
D_FF = D_MODEL
SWIGLU_LIMIT = 7.0
SWIGLU_ALPHA = 1.702
EPS = 1e-6
NEG_INF = -1e30

LANES = 128
SUBLANES = 8
MXU_DIM = 256
VMEM_LIMIT = 56 * 1024 * 1024

COL_XR, COL_GR, COL_Q, COL_K, COL_V, COL_G0, COL_G1 = range(7)


def _params(sem, vmem=VMEM_LIMIT):
    return pltpu.CompilerParams(dimension_semantics=sem, vmem_limit_bytes=vmem)


ROW_TILES = D_MODEL // LANES
assert ROW_TILES == SUBLANES


def _store_token_major(ref, val, start=0):
    n = val.shape[0]
    for c in range(ROW_TILES):
        ref[pl.ds(start * ROW_TILES + c, n, stride=ROW_TILES), :] = val[:, c * LANES:(c + 1) * LANES]


def _load_token_major(ref, start, n):
    return jnp.concatenate(
        [ref[pl.ds(start * ROW_TILES + c, n, stride=ROW_TILES), :] for c in range(ROW_TILES)], axis=-1)


def _inproj_kernel(x_ref, g_ref, w_ref, o_ref, h_scr):
    @pl.when(pl.program_id(1) == 0)
    def _():
        x = x_ref[...]
        inv = lax.rsqrt(jnp.mean(x * x, axis=-1, keepdims=True) + EPS)
        h_scr[...] = (x * inv * g_ref[...]).astype(BF16)

    o_ref[...] = jnp.dot(h_scr[...], w_ref[...], preferred_element_type=F32).astype(o_ref.dtype)


def _inproj(x2d, g, w_bf16):
    T, D = x2d.shape
    N = w_bf16.shape[1]
    tm = min(1024, T)
    tn = 3584 if N % 3584 == 0 else 1024
    return pl.pallas_call(
        _inproj_kernel,
        grid=(T // tm, N // tn),
        in_specs=[
            pl.BlockSpec((tm, D), lambda i, j: (i, 0)),
            pl.BlockSpec((1, D), lambda i, j: (0, 0)),
            pl.BlockSpec((D, tn), lambda i, j: (0, j)),
        ],
        out_specs=pl.BlockSpec((tm, tn), lambda i, j: (i, j)),
        out_shape=jax.ShapeDtypeStruct((T, N), BF16),
        scratch_shapes=[pltpu.VMEM((tm, D), BF16)],
        compiler_params=_params(("parallel", "arbitrary")),
        name="inproj",
    )(x2d, g.reshape(1, D), w_bf16)


def _rglru_kernel(xr_ref, gr_ref, cw_ref, cb_ref, wa_ref, ba_ref, wx_ref, bx_ref, lam_ref,
                  o_ref, xbuf, tail_scr, a_scr, u_scr, h_scr):
    nb, ts, _ = xr_ref.shape
    pad = SUBLANES
    nslab = D_RNN // LANES

    @pl.when(pl.program_id(0) == 0)
    def _():
        tail_scr[...] = jnp.zeros_like(tail_scr)
        h_scr[...] = jnp.zeros_like(h_scr)

    cw = cw_ref[...]
    z = -lam_ref[...]
    softplus = jnp.maximum(z, 0.0) + jnp.log1p(jnp.exp(-jnp.abs(z)))
    sigmoid = lambda v: 0.5 * jnp.tanh(0.5 * v) + 0.5
    nchunk = D_RNN // MXU_DIM
    for b in range(nb):
        x = xr_ref[b].astype(F32)
        xbuf[0:pad, :] = tail_scr[b]
        xbuf[pad:pad + ts, :] = x
        xc = (cw[3:4, :] * x
              + cw[2:3, :] * xbuf[pad - 1:pad - 1 + ts, :]
              + cw[1:2, :] * xbuf[pad - 2:pad - 2 + ts, :]
              + cw[0:1, :] * xbuf[pad - 3:pad - 3 + ts, :]) + cb_ref[...]
        tail_scr[b] = x[ts - pad:ts, :]
        xcb = xc.astype(BF16)
        r_pre = jnp.concatenate(
            [jnp.dot(xcb[:, c * MXU_DIM:(c + 1) * MXU_DIM], wa_ref[c], preferred_element_type=F32)
             for c in range(nchunk)], axis=-1)
        i_pre = jnp.concatenate(
            [jnp.dot(xcb[:, c * MXU_DIM:(c + 1) * MXU_DIM], wx_ref[c], preferred_element_type=F32)
             for c in range(nchunk)], axis=-1)
        r = sigmoid(r_pre + ba_ref[...])
        ig = sigmoid(i_pre + bx_ref[...])
        a = jnp.exp((-RGLRU_C) * r * softplus)
        u = jnp.sqrt(1.0 - a * a) * (ig * xc)
        for c in range(nslab):
            a_scr[c, pl.ds(b, ts, stride=nb), :] = a[:, c * LANES:(c + 1) * LANES]
            u_scr[c, pl.ds(b, ts, stride=nb), :] = u[:, c * LANES:(c + 1) * LANES]

    def body(g, hs):
        for j in range(SCAN_UNROLL):
            off = pl.multiple_of((g * SCAN_UNROLL + j) * nb, nb)
            new = []
            for c in range(nslab):
                h = a_scr[c, pl.ds(off, nb), :] * hs[c] + u_scr[c, pl.ds(off, nb), :]
                u_scr[c, pl.ds(off, nb), :] = h
                new.append(h)
            hs = tuple(new)
        return hs

    hs = lax.fori_loop(0, ts // SCAN_UNROLL, body, tuple(h_scr[c] for c in range(nslab)))
    for c in range(nslab):
        h_scr[c] = hs[c]
    for b in range(nb):
        h = jnp.concatenate([u_scr[c, pl.ds(b, ts, stride=nb), :] for c in range(nslab)], axis=-1)
        gate = jax.nn.gelu(gr_ref[b].astype(F32), approximate=True)
        o_ref[b] = (h * gate).astype(o_ref.dtype)


SCAN_UNROLL = 8


def _block_diag_chunks(w):
    per = MXU_DIM // RNN_BLOCK
    w = w.reshape(D_RNN // MXU_DIM, per, RNN_BLOCK, RNN_BLOCK)
    eye = jnp.eye(per, dtype=w.dtype)
    out = jnp.einsum('gpcd,pq->gpcqd', w, eye)
    return out.reshape(D_RNN // MXU_DIM, MXU_DIM, MXU_DIM)


def _rglru(proj, B, S, conv_w, conv_b, w_a, b_a, w_x, b_x, lru_lambda):
    assert B <= SUBLANES, "all batch rows of a time step share one vreg in the scan"
    ts = min(256, S)
    ns = S // ts
    wa = _block_diag_chunks(w_a).astype(BF16)
    wx = _block_diag_chunks(w_x).astype(BF16)
    nchunk = D_RNN // MXU_DIM
    nslab = D_RNN // LANES
    row = lambda v: v.reshape(1, D_RNN)
    const2 = lambda s: (0, 0)
    proj3 = proj.reshape(B, S, proj.shape[-1])
    out = pl.pallas_call(
        _rglru_kernel,
        grid=(ns,),
        in_specs=[
            pl.BlockSpec((B, ts, D_RNN), lambda s: (0, s, COL_XR)),
            pl.BlockSpec((B, ts, D_RNN), lambda s: (0, s, COL_GR)),
            pl.BlockSpec((CONV_W, D_RNN), const2),
            pl.BlockSpec((1, D_RNN), const2),
            pl.BlockSpec((nchunk, MXU_DIM, MXU_DIM), lambda s: (0, 0, 0)),
            pl.BlockSpec((1, D_RNN), const2),
            pl.BlockSpec((nchunk, MXU_DIM, MXU_DIM), lambda s: (0, 0, 0)),
            pl.BlockSpec((1, D_RNN), const2),
            pl.BlockSpec((1, D_RNN), const2),
        ],
        out_specs=pl.BlockSpec((B, ts, D_RNN), lambda s: (0, s, 0)),
        out_shape=jax.ShapeDtypeStruct((B, S, D_RNN), BF16),
        scratch_shapes=[
            pltpu.VMEM((ts + SUBLANES, D_RNN), F32),
            pltpu.VMEM((B, SUBLANES, D_RNN), F32),
            pltpu.VMEM((nslab, ts * B, LANES), F32),
            pltpu.VMEM((nslab, ts * B, LANES), F32),
            pltpu.VMEM((nslab, B, LANES), F32),
        ],
        compiler_params=_params(("arbitrary",)),
        name="rglru",
    )(proj3, proj3, conv_w, row(conv_b), wa, row(b_a), wx, row(b_x), row(lru_lambda))
    return out.reshape(B * S, D_RNN)


def _attn_kernel(q_ref, k_ref, v_ref, bias_ref, lq1_ref, lk1_ref, lq2_ref, lk2_ref, sg_ref,
                 o_ref, qs_scr, vx_scr, s_scr, m_scr, acc_scr, *, tq, lam_init):
    S = q_ref.shape[0]
    nq = S // tq
    scale = HEAD_DIM ** -0.5 * LOG2E
    lane = lax.broadcasted_iota(I32, (tq, V_DIM), 1)
    lam = (jnp.exp(jnp.sum(lq1_ref[...] * lk1_ref[...], keepdims=True))
           - jnp.exp(jnp.sum(lq2_ref[...] * lk2_ref[...], keepdims=True)) + lam_init)
    vx_scr[:, 0:V_DIM] = v_ref[...]
    vx_scr[:, V_DIM:2 * V_DIM] = jnp.ones((S, V_DIM), BF16)

    Z0 = 2

    def prep_q(qi, buf):
        q = (q_ref[pl.ds(pl.multiple_of(qi * tq, tq), tq), :].astype(F32) * scale).astype(BF16)
        zero = jnp.zeros_like(q)
        qs_scr[buf, 0:tq, :] = jnp.where(lane < HEAD_DIM, q, zero)
        qs_scr[buf, tq:2 * tq, :] = jnp.where(lane >= HEAD_DIM, q, zero)

    def scores(buf, j, dst):
        k = k_ref[pl.ds(pl.multiple_of(j * tq, tq), tq), :]
        s_scr[dst] = lax.dot_general(qs_scr[buf], k, (((1,), (1,)), ((), ())),
                                     preferred_element_type=F32)

    def step(j, src, bias_idx, prefetch):
        scores(*prefetch)
        s = s_scr[src]
        if bias_idx is not None:
            b = bias_ref[bias_idx]
            s = s + jnp.concatenate([b, b], axis=0)
        vx = vx_scr[pl.ds(pl.multiple_of(j * tq, tq), tq), :]
        m_prev = m_scr[...]
        m_new = jnp.maximum(m_prev, jnp.max(s, axis=1)[:, None])
        p = jnp.exp2(s - jnp.tile(m_new, (1, tq // LANES)))
        alpha = jnp.exp2(m_prev - m_new)
        acc_scr[...] = (jnp.tile(alpha, (1, 2)) * acc_scr[...]
                        + jnp.dot(p.astype(BF16), vx, preferred_element_type=F32))
        m_scr[...] = m_new

    def begin_block():
        m_scr[...] = jnp.full(m_scr.shape, NEG_INF, F32)
        acc_scr[...] = jnp.zeros(acc_scr.shape, F32)

    def diag_step(qi, par, src):
        nxt = jnp.minimum(qi + 1, nq - 1)
        prep_q(nxt, 1 - par)
        step(qi, src, 0, (1 - par, 0, Z0 + 1 - par))

    def end_block(qi):
        acc = acc_scr[...]
        o_all = acc[:, 0:V_DIM] / acc[:, V_DIM:2 * V_DIM]
        o = o_all[0:tq, :] - lam * o_all[tq:2 * tq, :]
        inv = lax.rsqrt(jnp.mean(o * o, axis=-1, keepdims=True) + EPS)
        y = (o * inv * sg_ref[...]) * (1.0 - lam_init)
        o_ref[pl.ds(pl.multiple_of(qi * tq, tq), tq), :] = y.astype(o_ref.dtype)

    def far_pairs(par, npairs):
        def body(m, c):
            step(2 * m + 1, 0, None, (par, 2 * m + 2, 1))
            step(2 * m + 2, 1, None, (par, 2 * m + 3, 0))
            return c

        lax.fori_loop(0, npairs, body, 0)

    prep_q(0, 0)
    scores(0, 0, Z0)
    begin_block()
    diag_step(0, 0, Z0)
    end_block(0)
    begin_block()
    step(0, Z0 + 1, 1, (1, 1, 0))
    diag_step(1, 1, 0)
    end_block(1)

    def block_pair(i, carry):
        qi = 2 * i
        begin_block()
        step(0, Z0, None, (0, 1, 0))
        far_pairs(0, i - 1)
        step(qi - 1, 0, 1, (0, qi, 1))
        diag_step(qi, 0, 1)
        end_block(qi)
        qi = 2 * i + 1
        begin_block()
        step(0, Z0 + 1, None, (1, 1, 0))
        far_pairs(1, i - 1)
        step(qi - 2, 0, None, (1, qi - 1, 1))
        step(qi - 1, 1, 1, (1, qi, 0))
        diag_step(qi, 1, 0)
        end_block(qi)
        return carry

    lax.fori_loop(1, nq // 2, block_pair, 0)


def _rel_bucket(n):
    n = jnp.maximum(n, 0)
    nf = jnp.maximum(n, MAX_EXACT).astype(F32)
    large = MAX_EXACT + (jnp.log(nf / MAX_EXACT) / math.log(MAX_DISTANCE / MAX_EXACT)
                         * (NUM_BUCKETS - MAX_EXACT)).astype(I32)
    large = jnp.minimum(large, NUM_BUCKETS - 1)
    return jnp.where(n < MAX_EXACT, n, large)


def _bias_tiles(rel_table, tq):
    i = jnp.arange(tq, dtype=I32)[:, None]
    j = jnp.arange(tq, dtype=I32)[None, :]
    table = rel_table.astype(F32) - rel_table[NUM_BUCKETS - 1].astype(F32)[None, :]
    tiles = []
    for delta in (0, tq):
        n = i - j + delta
        onehot = (_rel_bucket(n)[:, :, None] == jnp.arange(NUM_BUCKETS, dtype=I32)).astype(F32)
        b = jnp.einsum('ijb,bh->hij', onehot, table, precision=lax.Precision.HIGHEST) * LOG2E
        tiles.append(jnp.where((n >= 0)[None], b, NEG_INF))
    return jnp.stack(tiles, axis=1)


ATT_TQ = 512
LOG2E = math.log2(math.e)


def _attention(proj, B, S, lq1, lk1, lq2, lk2, subln_g, rel_table, lam_init):
    T = B * S
    tq = min(ATT_TQ, S // 2)
    assert V_DIM == LANES and MAX_DISTANCE <= tq and S % (2 * tq) == 0
    bias = _bias_tiles(rel_table, tq)
    vec = lambda v: v.reshape(1, -1).astype(F32)
    const2 = lambda b, h: (0, 0)
    kern = functools.partial(_attn_kernel, tq=tq, lam_init=lam_init)
    return pl.pallas_call(
        kern,
        grid=(B, N_HEADS),
        in_specs=[
            pl.BlockSpec((S, V_DIM), lambda b, h: (b, COL_Q * N_HEADS + h)),
            pl.BlockSpec((S, V_DIM), lambda b, h: (b, COL_K * N_HEADS + h)),
            pl.BlockSpec((S, V_DIM), lambda b, h: (b, COL_V * N_HEADS + h)),
            pl.BlockSpec((None, 2, tq, tq), lambda b, h: (h, 0, 0, 0)),
            pl.BlockSpec((1, HEAD_DIM), const2),
            pl.BlockSpec((1, HEAD_DIM), const2),
            pl.BlockSpec((1, HEAD_DIM), const2),
            pl.BlockSpec((1, HEAD_DIM), const2),
            pl.BlockSpec((1, V_DIM), const2),
        ],
        out_specs=pl.BlockSpec((S, V_DIM), lambda b, h: (b, h)),
        out_shape=jax.ShapeDtypeStruct((T, ATT_V), BF16),
        scratch_shapes=[
            pltpu.VMEM((2, 2 * tq, V_DIM), BF16),
            pltpu.VMEM((S, 2 * V_DIM), BF16),
            pltpu.VMEM((4, 2 * tq, tq), F32),
            pltpu.VMEM((2 * tq, LANES), F32),
            pltpu.VMEM((2 * tq, 2 * V_DIM), F32),
        ],
        compiler_params=_params(("parallel", "arbitrary")),
        name="diff_attn",
    )(proj, proj, proj, bias, vec(lq1), vec(lk1), vec(lq2), vec(lk2), vec(subln_g))


def _merge_kernel(x_ref, yr_ref, ya_ref, g0_ref, g1_ref, wr_ref, wa_ref, wo_ref, gf_ref, wrt_ref, brt_ref,
                  x2_ref, hf_ref, idx_ref, wgt_ref):
    pr = jnp.dot(yr_ref[...], wr_ref[...], preferred_element_type=F32)
    pa = jnp.dot(ya_ref[...], wa_ref[...], preferred_element_type=F32)
    merged = (jax.nn.sigmoid(g0_ref[...].astype(F32)) * pr
              + jax.nn.sigmoid(g1_ref[...].astype(F32)) * pa)
    x2 = x_ref[...] + jnp.dot(merged.astype(BF16), wo_ref[...], preferred_element_type=F32)
    x2_ref[...] = x2
    inv = lax.rsqrt(jnp.mean(x2 * x2, axis=-1, keepdims=True) + EPS)
    hf = x2 * inv * gf_ref[...]
    _store_token_major(hf_ref, hf)
    nt = (((1,), (1,)), ((), ()))
    hi = hf.astype(BF16)
    lo = (hf - hi.astype(F32)).astype(BF16)
    hw = lax.dot_general(wrt_ref[...], hi, nt, preferred_element_type=F32)
    lw = lax.dot_general(wrt_ref[0:LANES, :], lo, nt, preferred_element_type=F32)
    logits = (hw[0:N_EXPERTS, :] + hw[LANES:LANES + N_EXPERTS, :] + lw[0:N_EXPERTS, :]
              + brt_ref[...])
    tm = logits.shape[1]
    row = lax.broadcasted_iota(I32, (N_EXPERTS, tm), 0)
    work = logits
    idx_rows, val_rows = [], []
    for k in range(TOP_K):
        mx = jnp.max(work, axis=0, keepdims=True)
        sel = jnp.min(jnp.where(work == mx, row, N_EXPERTS), axis=0, keepdims=True)
        idx_rows.append(sel)
        val_rows.append(mx)
        work = jnp.where(row == sel, -jnp.inf, work)
    es = [jnp.exp(v - val_rows[0]) for v in val_rows]
    inv_sum = 1.0 / functools.reduce(lambda a, b: a + b, es)
    pad_i = [jnp.zeros((1, tm), I32)] * (SUBLANES - TOP_K)
    pad_f = [jnp.zeros((1, tm), F32)] * (SUBLANES - TOP_K)
    idx_ref[...] = jnp.concatenate(idx_rows + pad_i, axis=0)
    wgt_ref[...] = jnp.concatenate([e * inv_sum for e in es] + pad_f, axis=0)


def _merge_router(x2d, y_rnn, y_att, proj, w_pr, w_pa, w_o, g_ffn, w_router, b_router):
    T, D = x2d.shape
    tm = min(512, T)
    w_hi = w_router.astype(BF16)
    w_lo = (w_router - w_hi.astype(F32)).astype(BF16)
    wrt = (jnp.zeros((2 * LANES, D), BF16).at[:N_EXPERTS, :].set(w_hi.T)
           .at[LANES:LANES + N_EXPERTS, :].set(w_lo.T))
    brt = b_router.reshape(N_EXPERTS, 1).astype(F32)
    rowblk = lambda c: pl.BlockSpec((tm, D), lambda i, c=c: (i, c))
    full = lambda a: pl.BlockSpec(a.shape, lambda i: (0,) * a.ndim)
    wr, wa, wo = w_pr.astype(BF16), w_pa.astype(BF16), w_o.astype(BF16)
    gf = g_ffn.reshape(1, D)
    return pl.pallas_call(
        _merge_kernel,
        grid=(T // tm,),
        in_specs=[rowblk(0), rowblk(0), rowblk(0), rowblk(COL_G0), rowblk(COL_G1),
                  full(wr), full(wa), full(wo), full(gf), full(wrt), full(brt)],
        out_specs=[rowblk(0), pl.BlockSpec((tm * ROW_TILES, LANES), lambda i: (i, 0)),
                   pl.BlockSpec((SUBLANES, tm), lambda i: (0, i)),
                   pl.BlockSpec((SUBLANES, tm), lambda i: (0, i))],
        out_shape=[jax.ShapeDtypeStruct((T, D), F32), jax.ShapeDtypeStruct((T * ROW_TILES, LANES), F32),
                   jax.ShapeDtypeStruct((SUBLANES, T), I32), jax.ShapeDtypeStruct((SUBLANES, T), F32)],
        compiler_params=_params(("parallel",)),
        name="merge_router",
    )(x2d, y_rnn, y_att, proj, proj, wr, wa, wo, gf, wrt, brt)


def _route_kernel(idx_ref, dest_ref, cnt_ref, cnt_scr, run_scr, start_scr, *, blk):
    ph = pl.program_id(0)
    i = pl.program_id(1)
    tt = idx_ref.shape[1]
    idx = idx_ref[...]
    row = lax.broadcasted_iota(I32, (N_EXPERTS, tt), 0)
    onehot = jnp.zeros((N_EXPERTS, tt), F32)
    for k in range(TOP_K):
        onehot = onehot + (idx[k:k + 1, :] == row).astype(F32)
    tile_cnt = jnp.sum(onehot, axis=1, keepdims=True)

    @pl.when((ph == 0) & (i == 0))
    def _():
        cnt_scr[...] = jnp.zeros_like(cnt_scr)

    @pl.when(ph == 0)
    def _():
        cnt_scr[...] += tile_cnt

    @pl.when((ph == 1) & (i == 0))
    def _():
        padded = jnp.floor((cnt_scr[...] + (blk - 1)) / blk) * blk
        r = lax.broadcasted_iota(I32, (N_EXPERTS, N_EXPERTS), 0)
        c = lax.broadcasted_iota(I32, (N_EXPERTS, N_EXPERTS), 1)
        start_scr[...] = jnp.dot((c < r).astype(F32), padded, preferred_element_type=F32,
                                 precision=lax.Precision.HIGHEST)
        run_scr[...] = jnp.zeros_like(run_scr)

    @pl.when(ph == 1)
    def _():
        r = lax.broadcasted_iota(I32, (tt, tt), 0)
        c = lax.broadcasted_iota(I32, (tt, tt), 1)
        earlier = (r < c).astype(BF16)
        before = jnp.dot(onehot.astype(BF16), earlier, preferred_element_type=F32)
        base = before + run_scr[:, 0:1] + start_scr[:, 0:1]
        rows = [jnp.sum(jnp.where(idx[k:k + 1, :] == row, base, 0.0), axis=0, keepdims=True)
                for k in range(TOP_K)]
        rows += [jnp.zeros((1, tt), F32)] * (SUBLANES - TOP_K)
        dest_ref[...] = jnp.concatenate(rows, axis=0).astype(I32)
        run_scr[...] += tile_cnt
        cnt_ref[...] = cnt_scr[...]


def _route(top_idx, blk):
    T = top_idx.shape[1]
    tt = min(512, T)
    kern = functools.partial(_route_kernel, blk=blk)
    return pl.pallas_call(
        kern,
        grid=(2, T // tt),
        in_specs=[pl.BlockSpec((SUBLANES, tt), lambda p, i: (0, i))],
        out_specs=[pl.BlockSpec((SUBLANES, tt), lambda p, i: (0, p * i)),
                   pl.BlockSpec((N_EXPERTS, LANES), lambda p, i: (0, 0))],
        out_shape=[jax.ShapeDtypeStruct((SUBLANES, T), I32),
                   jax.ShapeDtypeStruct((N_EXPERTS, LANES), F32)],
        scratch_shapes=[pltpu.VMEM((N_EXPERTS, LANES), F32), pltpu.VMEM((N_EXPERTS, LANES), F32),
                        pltpu.VMEM((N_EXPERTS, LANES), F32)],
        compiler_params=_params(("arbitrary", "arbitrary")),
        name="route",
    )(top_idx)


TOKEN_UNROLL = 2


def _for_each_token_k(n_tok, fn):
    def body(g, c):
        for u in range(TOKEN_UNROLL):
            for k in range(TOP_K):
                fn(g * TOKEN_UNROLL + u, k)
        return c

    lax.fori_loop(0, n_tok // TOKEN_UNROLL, body, 0)


def _row_tile(ref, row):
    return ref.at[pl.ds(pl.multiple_of(row * ROW_TILES, ROW_TILES), ROW_TILES), :]


def _dispatch_kernel(ends_ref, dest_ref, hf_ref, xs_ref, zero_buf, sem, zsem, *, tt, blk):
    @pl.when(pl.program_id(0) == 0)
    def _():
        zero_buf[...] = jnp.zeros_like(zero_buf)

        def tail_fill(e):
            prev_end = ends_ref[e - 1] if e > 0 else 0
            start = pl.multiple_of((ends_ref[e] - blk) * ROW_TILES, ROW_TILES)
            return ends_ref[e] > prev_end, pltpu.make_async_copy(
                zero_buf, xs_ref.at[pl.ds(start, blk * ROW_TILES), :], zsem)

        for e in range(N_EXPERTS):
            nonempty, cp = tail_fill(e)
            pl.when(nonempty)(cp.start)
        for e in range(N_EXPERTS):
            nonempty, cp = tail_fill(e)
            pl.when(nonempty)(cp.wait)

    def row_copy(t, k):
        return pltpu.make_async_copy(_row_tile(hf_ref, t),
                                     _row_tile(xs_ref, dest_ref[k, t]), sem)

    _for_each_token_k(tt, lambda t, k: row_copy(t, k).start(priority=k % 2))
    _for_each_token_k(tt, lambda t, k: row_copy(t, k).wait())


def _dispatch(ends, dest, hf_tm, n_slots, blk):
    T = hf_tm.shape[0] // ROW_TILES
    tt = min(512, T)
    kern = functools.partial(_dispatch_kernel, tt=tt, blk=blk)
    grid_spec = pltpu.PrefetchScalarGridSpec(
        num_scalar_prefetch=1,
        grid=(T // tt,),
        in_specs=[pl.BlockSpec((SUBLANES, tt), lambda i, ends: (0, i), memory_space=pltpu.SMEM),
                  pl.BlockSpec((tt * ROW_TILES, LANES), lambda i, ends: (i, 0))],
        out_specs=pl.BlockSpec(memory_space=pl.ANY),
        scratch_shapes=[pltpu.VMEM((blk * ROW_TILES, LANES), F32),
                        pltpu.SemaphoreType.DMA(()), pltpu.SemaphoreType.DMA(())],
    )
    return pl.pallas_call(
        kern,
        grid_spec=grid_spec,
        out_shape=jax.ShapeDtypeStruct((n_slots * ROW_TILES, LANES), F32),
        compiler_params=_params(("arbitrary",)),
        name="dispatch",
    )(ends, dest, hf_tm)


def _expert_kernel(be_ref, bsrc_ref, act_ref, first_ref, x_ref, wg_ref, bg_ref, wu_ref, bu_ref, wd_ref, bd_ref,
                   o_ref, wg_scr, wu_scr, wd_scr, *, blk):
    del be_ref, bsrc_ref
    i = pl.program_id(0)

    @pl.when(first_ref[i] == 1)
    def _():
        wg_scr[...] = wg_ref[0].astype(BF16)
        wu_scr[...] = wu_ref[0].astype(BF16)
        wd_scr[...] = wd_ref[0].astype(BF16)

    @pl.when(act_ref[i] == 1)
    def _():
        x = _load_token_major(x_ref, 0, blk).astype(BF16)
        g = jnp.dot(x, wg_scr[...], preferred_element_type=F32) + bg_ref[0]
        u = jnp.dot(x, wu_scr[...], preferred_element_type=F32) + bu_ref[0]
        g = jnp.minimum(g, SWIGLU_LIMIT)
        u = jnp.clip(u, -SWIGLU_LIMIT, SWIGLU_LIMIT)
        act = (u + 1.0) * (g * jax.nn.sigmoid(SWIGLU_ALPHA * g))
        y = jnp.dot(act.astype(BF16), wd_scr[...], preferred_element_type=F32) + bd_ref[0]
        _store_token_major(o_ref, y)


def _experts(x_slots, blk, block_expert, block_src, block_active, block_first, wg, bg, wu, bu, wd, bd):
    D = D_MODEL
    nb = x_slots.shape[0] // (blk * ROW_TILES)
    wspec = pl.BlockSpec((1, D, D_FF), lambda i, be, bs, ac, fi: (be[i], 0, 0))
    bspec = pl.BlockSpec((1, 1, D_FF), lambda i, be, bs, ac, fi: (be[i], 0, 0))
    xspec = pl.BlockSpec((blk * ROW_TILES, LANES), lambda i, be, bs, ac, fi: (bs[i], 0))
    grid_spec = pltpu.PrefetchScalarGridSpec(
        num_scalar_prefetch=4,
        grid=(nb,),
        in_specs=[xspec, wspec, bspec, wspec, bspec, wspec, bspec],
        out_specs=xspec,
        scratch_shapes=[pltpu.VMEM((D, D_FF), BF16), pltpu.VMEM((D, D_FF), BF16),
                        pltpu.VMEM((D_FF, D), BF16)],
    )
    b3 = lambda b: b.reshape(N_EXPERTS, 1, -1)
    return pl.pallas_call(
        functools.partial(_expert_kernel, blk=blk),
        grid_spec=grid_spec,
        out_shape=jax.ShapeDtypeStruct(x_slots.shape, F32),
        compiler_params=_params(("arbitrary",)),
        name="experts",
    )(block_expert, block_src, block_active, block_first, x_slots,
      wg, b3(bg), wu, b3(bu), wd, b3(bd))


def _combine_kernel(dest_ref, dest_next_ref, ys_ref, w_ref, x2_ref, g_ref, o_ref, buf, sems, *, tt):
    i = pl.program_id(0)
    n = pl.num_programs(0)
    cur = i % 2

    def row_copy(idx_ref, slot, t, k):
        return pltpu.make_async_copy(_row_tile(ys_ref, idx_ref[k, t]),
                                     _row_tile(buf, (slot * TOP_K + k) * tt + t), sems.at[slot])

    def gather(idx_ref, slot):
        _for_each_token_k(tt, lambda t, k: row_copy(idx_ref, slot, t, k).start(priority=k % 2))

    @pl.when(i == 0)
    def _():
        gather(dest_ref, 0)

    @pl.when(i + 1 < n)
    def _():
        gather(dest_next_ref, 1 - cur)

    _for_each_token_k(tt, lambda t, k: row_copy(dest_ref, cur, t, k).wait())
    eye = (lax.broadcasted_iota(I32, (tt, tt), 0) == lax.broadcasted_iota(I32, (tt, tt), 1)).astype(F32)
    w = lax.dot_general(eye, w_ref[...], (((1,), (1,)), ((), ())), preferred_element_type=F32,
                        precision=lax.Precision.HIGHEST)
    y = x2_ref[...]
    for k in range(TOP_K):
        y = y + w[:, k:k + 1] * _load_token_major(buf, (cur * TOP_K + k) * tt, tt)
    inv = lax.rsqrt(jnp.mean(y * y, axis=-1, keepdims=True) + EPS)
    o_ref[...] = y * inv * g_ref[...]


def _combine(dest, y_slots, top_w, x2, g_final):
    T, D = x2.shape
    tt = min(256, T)
    nt = T // tt
    kern = functools.partial(_combine_kernel, tt=tt)
    return pl.pallas_call(
        kern,
        grid=(nt,),
        in_specs=[pl.BlockSpec((SUBLANES, tt), lambda i: (0, i), memory_space=pltpu.SMEM),
                  pl.BlockSpec((SUBLANES, tt), lambda i: (0, jnp.minimum(i + 1, nt - 1)),
                               memory_space=pltpu.SMEM),
                  pl.BlockSpec(memory_space=pl.ANY),
                  pl.BlockSpec((SUBLANES, tt), lambda i: (0, i)),
                  pl.BlockSpec((tt, D), lambda i: (i, 0)),
                  pl.BlockSpec((1, D), lambda i: (0, 0))],
        out_specs=pl.BlockSpec((tt, D), lambda i: (i, 0)),
        out_shape=jax.ShapeDtypeStruct((T, D), F32),
        scratch_shapes=[pltpu.VMEM((2 * TOP_K * tt * ROW_TILES, LANES), F32),
                        pltpu.SemaphoreType.DMA((2,))],
        compiler_params=_params(("arbitrary",)),
        name="combine",
    )(dest, dest, y_slots, top_w, x2, g_final.reshape(1, D))


def _moe_block_size(T):
    return min(512, max(SUBLANES, T * TOP_K // N_EXPERTS))


def _block_tables(counts, blk, nb):
    cnt = counts.astype(I32)
    padded = (cnt + blk - 1) // blk * blk
    ends = jnp.cumsum(padded)
    used = ends[-1] // blk
    starts = jnp.arange(nb, dtype=I32) * blk
    src = jnp.minimum(jnp.arange(nb, dtype=I32), jnp.maximum(used - 1, 0))
    expert = jnp.minimum(jnp.sum((starts[:, None] >= ends[None, :]).astype(I32), axis=1), N_EXPERTS - 1)
    expert = expert[src]
    active = (jnp.arange(nb, dtype=I32) < used).astype(I32)
    prev = jnp.concatenate([jnp.full((1,), -1, I32), expert[:-1]])
    first = active * (expert != prev).astype(I32)
    return ends.astype(I32), expert, src, active, first


def kernel(x, norm_mix_g, w_in, conv_w, conv_b, w_rg_a, b_rg_a, w_rg_x, b_rg_x, lru_lambda, diff_lambda_q1, diff_lambda_k1, diff_lambda_q2, diff_lambda_k2, subln_g, rel_bias_table, w_proj_rnn, w_proj_att, w_out, norm_ffn_g, w_router, b_router, w_gate_e, b_gate_e, w_up_e, b_up_e, w_down_e, b_down_e, norm_final_g):
    B, S, D = x.shape
    T = B * S
    assert norm_mix_g.shape[0] == 1, "single-layer block: the final norm is fused into the MoE combine"
    l = 0
    xt = x.reshape(T, D)
    lam_init = 0.8 - 0.6 * math.exp(-0.3 * l)
    proj = _inproj(xt, norm_mix_g[l], w_in[l].astype(BF16))
    y_rnn = _rglru(proj, B, S, conv_w[l], conv_b[l], w_rg_a[l], b_rg_a[l], w_rg_x[l], b_rg_x[l],
                   lru_lambda[l])
    y_att = _attention(proj, B, S, diff_lambda_q1[l], diff_lambda_k1[l], diff_lambda_q2[l],
                       diff_lambda_k2[l], subln_g[l], rel_bias_table, lam_init)
    x2, hf, top_idx, top_w = _merge_router(xt, y_rnn, y_att, proj, w_proj_rnn[l], w_proj_att[l],
                                           w_out[l], norm_ffn_g[l], w_router[l], b_router[l])
    blk = _moe_block_size(T)
    nb = T * TOP_K // blk + N_EXPERTS
    dest, counts = _route(top_idx, blk)
    ends, expert, src, active, first = _block_tables(counts[:, 0], blk, nb)
    x_slots = _dispatch(ends, dest, hf, nb * blk, blk)
    y_slots = _experts(x_slots, blk, expert, src, active, first, w_gate_e[l], b_gate_e[l], w_up_e[l],
                       b_up_e[l], w_down_e[l], b_down_e[l])
    out = _combine(dest, y_slots, top_w, x2, norm_final_g)
    return out.reshape(B, S, D)
```

```python
import functools
import math

import jax
import jax.numpy as jnp
from jax import lax
from jax.experimental import pallas as pl
from jax.experimental.pallas import tpu as pltpu

F32 = jnp.float32
BF16 = jnp.bfloat16
I32 = jnp.int32

D_MODEL = 1024
D_RNN = 1024
RNN_BLOCKS = 16
RNN_BLOCK = D_RNN // RNN_BLOCKS
CONV_W = 4
RGLRU_C = 8.0
N_HEADS = 8
HEAD_DIM = 64
V_DIM = 2 * HEAD_DIM
ATT_QK = N_HEADS * 2 * HEAD_DIM
ATT_V = N_HEADS * V_DIM
D_IN = 2 * D_RNN + 2 * ATT_QK + ATT_V + 2 * D_MODEL
NUM_BUCKETS = 32
MAX_EXACT = NUM_BUCKETS // 2
MAX_DISTANCE = 128
N_EXPERTS = 32
TOP_K = 4
D_FF = D_MODEL
SWIGLU_LIMIT = 7.0
SWIGLU_ALPHA = 1.702
EPS = 1e-6
NEG_INF = -1e30

LANES = 128
SUBLANES = 8
MXU_DIM = 256
VMEM_LIMIT = 56 * 1024 * 1024

COL_XR, COL_GR, COL_Q, COL_K, COL_V, COL_G0, COL_G1 = range(7)


def _params(sem, vmem=VMEM_LIMIT):
    return pltpu.CompilerParams(dimension_semantics=sem, vmem_limit_bytes=vmem)


ROW_TILES = D_MODEL // LANES
assert ROW_TILES == SUBLANES


def _store_token_major(ref, val, start=0):
    n = val.shape[0]
    for c in range(ROW_TILES):
        ref[pl.ds(start * ROW_TILES + c, n, stride=ROW_TILES), :] = val[:, c * LANES:(c + 1) * LANES]


def _load_token_major(ref, start, n):
    return jnp.concatenate(
        [ref[pl.ds(start * ROW_TILES + c, n, stride=ROW_TILES), :] for c in range(ROW_TILES)], axis=-1)


def _inproj_kernel(x_ref, g_ref, w_ref, o_ref, h_scr):
    @pl.when(pl.program_id(1) == 0)
    def _():
        x = x_ref[...]
        inv = lax.rsqrt(jnp.mean(x * x, axis=-1, keepdims=True) + EPS)
        h_scr[...] = (x * inv * g_ref[...]).astype(BF16)

    o_ref[...] = jnp.dot(h_scr[...], w_ref[...], preferred_element_type=F32).astype(o_ref.dtype)


def _inproj(x2d, g, w_bf16):
    T, D = x2d.shape
    N = w_bf16.shape[1]
    tm = min(1024, T)
    tn = 3584 if N % 3584 == 0 else 1024
    return pl.pallas_call(
        _inproj_kernel,
        grid=(T // tm, N // tn),
        in_specs=[
            pl.BlockSpec((tm, D), lambda i, j: (i, 0)),
            pl.BlockSpec((1, D), lambda i, j: (0, 0)),
            pl.BlockSpec((D, tn), lambda i, j: (0, j)),
        ],
        out_specs=pl.BlockSpec((tm, tn), lambda i, j: (i, j)),
        out_shape=jax.ShapeDtypeStruct((T, N), BF16),
        scratch_shapes=[pltpu.VMEM((tm, D), BF16)],
        compiler_params=_params(("parallel", "arbitrary")),
        name="inproj",
    )(x2d, g.reshape(1, D), w_bf16)


def _rglru_kernel(xr_ref, gr_ref, cw_ref, cb_ref, wa_ref, ba_ref, wx_ref, bx_ref, lam_ref,
                  o_ref, xbuf, tail_scr, a_scr, u_scr, h_scr):
    nb, ts, _ = xr_ref.shape
    pad = SUBLANES
    nslab = D_RNN // LANES

    @pl.when(pl.program_id(0) == 0)
    def _():
        tail_scr[...] = jnp.zeros_like(tail_scr)
        h_scr[...] = jnp.zeros_like(h_scr)

    cw = cw_ref[...]
    z = -lam_ref[...]
    softplus = jnp.maximum(z, 0.0) + jnp.log1p(jnp.exp(-jnp.abs(z)))
    sigmoid = lambda v: 0.5 * jnp.tanh(0.5 * v) + 0.5
    nchunk = D_RNN // MXU_DIM
    for b in range(nb):
        x = xr_ref[b].astype(F32)
        xbuf[0:pad, :] = tail_scr[b]
        xbuf[pad:pad + ts, :] = x
        xc = (cw[3:4, :] * x
              + cw[2:3, :] * xbuf[pad - 1:pad - 1 + ts, :]
              + cw[1:2, :] * xbuf[pad - 2:pad - 2 + ts, :]
              + cw[0:1, :] * xbuf[pad - 3:pad - 3 + ts, :]) + cb_ref[...]
        tail_scr[b] = x[ts - pad:ts, :]
        xcb = xc.astype(BF16)
        r_pre = jnp.concatenate(
            [jnp.dot(xcb[:, c * MXU_DIM:(c + 1) * MXU_DIM], wa_ref[c], preferred_element_type=F32)
             for c in range(nchunk)], axis=-1)
        i_pre = jnp.concatenate(
            [jnp.dot(xcb[:, c * MXU_DIM:(c + 1) * MXU_DIM], wx_ref[c], preferred_element_type=F32)
             for c in range(nchunk)], axis=-1)
        r = sigmoid(r_pre + ba_ref[...])
        ig = sigmoid(i_pre + bx_ref[...])
        a = jnp.exp((-RGLRU_C) * r * softplus)
        u = jnp.sqrt(1.0 - a * a) * (ig * xc)
        for c in range(nslab):
            a_scr[c, pl.ds(b, ts, stride=nb), :] = a[:, c * LANES:(c + 1) * LANES]
            u_scr[c, pl.ds(b, ts, stride=nb), :] = u[:, c * LANES:(c + 1) * LANES]

    def body(g, hs):
        for j in range(SCAN_UNROLL):
            off = pl.multiple_of((g * SCAN_UNROLL + j) * nb, nb)
            new = []
            for c in range(nslab):
                h = a_scr[c, pl.ds(off, nb), :] * hs[c] + u_scr[c, pl.ds(off, nb), :]
                u_scr[c, pl.ds(off, nb), :] = h
                new.append(h)
            hs = tuple(new)
        return hs

    hs = lax.fori_loop(0, ts // SCAN_UNROLL, body, tuple(h_scr[c] for c in range(nslab)))
    for c in range(nslab):
        h_scr[c] = hs[c]
    for b in range(nb):
        h = jnp.concatenate([u_scr[c, pl.ds(b, ts, stride=nb), :] for c in range(nslab)], axis=-1)
        gate = jax.nn.gelu(gr_ref[b].astype(F32), approximate=True)
        o_ref[b] = (h * gate).astype(o_ref.dtype)


SCAN_UNROLL = 8


def _block_diag_chunks(w):
    per = MXU_DIM // RNN_BLOCK
    w = w.reshape(D_RNN // MXU_DIM, per, RNN_BLOCK, RNN_BLOCK)
    eye = jnp.eye(per, dtype=w.dtype)
    out = jnp.einsum('gpcd,pq->gpcqd', w, eye)
    return out.reshape(D_RNN // MXU_DIM, MXU_DIM, MXU_DIM)


def _rglru(proj, B, S, conv_w, conv_b, w_a, b_a, w_x, b_x, lru_lambda):
    assert B <= SUBLANES, "all batch rows of a time step share one vreg in the scan"
    ts = min(256, S)
    ns = S // ts
    wa = _block_diag_chunks(w_a).astype(BF16)
    wx = _block_diag_chunks(w_x).astype(BF16)
    nchunk = D_RNN // MXU_DIM
    nslab = D_RNN // LANES
    row = lambda v: v.reshape(1, D_RNN)
    const2 = lambda s: (0, 0)
    proj3 = proj.reshape(B, S, proj.shape[-1])
    out = pl.pallas_call(
        _rglru_kernel,
        grid=(ns,),
        in_specs=[
            pl.BlockSpec((B, ts, D_RNN), lambda s: (0, s, COL_XR)),
            pl.BlockSpec((B, ts, D_RNN), lambda s: (0, s, COL_GR)),
            pl.BlockSpec((CONV_W, D_RNN), const2),
            pl.BlockSpec((1, D_RNN), const2),
            pl.BlockSpec((nchunk, MXU_DIM, MXU_DIM), lambda s: (0, 0, 0)),
            pl.BlockSpec((1, D_RNN), const2),
            pl.BlockSpec((nchunk, MXU_DIM, MXU_DIM), lambda s: (0, 0, 0)),
            pl.BlockSpec((1, D_RNN), const2),
            pl.BlockSpec((1, D_RNN), const2),
        ],
        out_specs=pl.BlockSpec((B, ts, D_RNN), lambda s: (0, s, 0)),
        out_shape=jax.ShapeDtypeStruct((B, S, D_RNN), BF16),
        scratch_shapes=[
            pltpu.VMEM((ts + SUBLANES, D_RNN), F32),
            pltpu.VMEM((B, SUBLANES, D_RNN), F32),
            pltpu.VMEM((nslab, ts * B, LANES), F32),
            pltpu.VMEM((nslab, ts * B, LANES), F32),
            pltpu.VMEM((nslab, B, LANES), F32),
        ],
        compiler_params=_params(("arbitrary",)),
        name="rglru",
    )(proj3, proj3, conv_w, row(conv_b), wa, row(b_a), wx, row(b_x), row(lru_lambda))
    return out.reshape(B * S, D_RNN)


def _attn_kernel(q_ref, k_ref, v_ref, bias_ref, lq1_ref, lk1_ref, lq2_ref, lk2_ref, sg_ref,
                 o_ref, qs_scr, vx_scr, s_scr, m_scr, acc_scr, *, tq, lam_init):
    S = q_ref.shape[0]
    nq = S // tq
    scale = HEAD_DIM ** -0.5 * LOG2E
    lane = lax.broadcasted_iota(I32, (tq, V_DIM), 1)
    lam = (jnp.exp(jnp.sum(lq1_ref[...] * lk1_ref[...], keepdims=True))
           - jnp.exp(jnp.sum(lq2_ref[...] * lk2_ref[...], keepdims=True)) + lam_init)
    vx_scr[:, 0:V_DIM] = v_ref[...]
    vx_scr[:, V_DIM:2 * V_DIM] = jnp.ones((S, V_DIM), BF16)

    Z0 = 2

    def prep_q(qi, buf):
        q = (q_ref[pl.ds(pl.multiple_of(qi * tq, tq), tq), :].astype(F32) * scale).astype(BF16)
        zero = jnp.zeros_like(q)
        qs_scr[buf, 0:tq, :] = jnp.where(lane < HEAD_DIM, q, zero)
        qs_scr[buf, tq:2 * tq, :] = jnp.where(lane >= HEAD_DIM, q, zero)

    def scores(buf, j, dst):
        k = k_ref[pl.ds(pl.multiple_of(j * tq, tq), tq), :]
        s_scr[dst] = lax.dot_general(qs_scr[buf], k, (((1,), (1,)), ((), ())),
                                     preferred_element_type=F32)

    def step(j, src, bias_idx, prefetch):
        scores(*prefetch)
        s = s_scr[src]
        if bias_idx is not None:
            b = bias_ref[bias_idx]
            s = s + jnp.concatenate([b, b], axis=0)
        vx = vx_scr[pl.ds(pl.multiple_of(j * tq, tq), tq), :]
        m_prev = m_scr[...]
        m_new = jnp.maximum(m_prev, jnp.max(s, axis=1)[:, None])
        p = jnp.exp2(s - jnp.tile(m_new, (1, tq // LANES)))
        alpha = jnp.exp2(m_prev - m_new)
        acc_scr[...] = (jnp.tile(alpha, (1, 2)) * acc_scr[...]
                        + jnp.dot(p.astype(BF16), vx, preferred_element_type=F32))
        m_scr[...] = m_new

    def begin_block():
        m_scr[...] = jnp.full(m_scr.shape, NEG_INF, F32)
        acc_scr[...] = jnp.zeros(acc_scr.shape, F32)

    def diag_step(qi, par, src):
        nxt = jnp.minimum(qi + 1, nq - 1)
        prep_q(nxt, 1 - par)
        step(qi, src, 0, (1 - par, 0, Z0 + 1 - par))

    def end_block(qi):
        acc = acc_scr[...]
        o_all = acc[:, 0:V_DIM] / acc[:, V_DIM:2 * V_DIM]
        o = o_all[0:tq, :] - lam * o_all[tq:2 * tq, :]
        inv = lax.rsqrt(jnp.mean(o * o, axis=-1, keepdims=True) + EPS)
        y = (o * inv * sg_ref[...]) * (1.0 - lam_init)
        o_ref[pl.ds(pl.multiple_of(qi * tq, tq), tq), :] = y.astype(o_ref.dtype)

    def far_pairs(par, npairs):
        def body(m, c):
            step(2 * m + 1, 0, None, (par, 2 * m + 2, 1))
            step(2 * m + 2, 1, None, (par, 2 * m + 3, 0))
            return c

        lax.fori_loop(0, npairs, body, 0)

    prep_q(0, 0)
    scores(0, 0, Z0)
    begin_block()
    diag_step(0, 0, Z0)
    end_block(0)
    begin_block()
    step(0, Z0 + 1, 1, (1, 1, 0))
    diag_step(1, 1, 0)
    end_block(1)

    def block_pair(i, carry):
        qi = 2 * i
        begin_block()
        step(0, Z0, None, (0, 1, 0))
        far_pairs(0, i - 1)
        step(qi - 1, 0, 1, (0, qi, 1))
        diag_step(qi, 0, 1)
        end_block(qi)
        qi = 2 * i + 1
        begin_block()
        step(0, Z0 + 1, None, (1, 1, 0))
        far_pairs(1, i - 1)
        step(qi - 2, 0, None, (1, qi - 1, 1))
        step(qi - 1, 1, 1, (1, qi, 0))
        diag_step(qi, 1, 0)
        end_block(qi)
        return carry

    lax.fori_loop(1, nq // 2, block_pair, 0)


def _rel_bucket(n):
    n = jnp.maximum(n, 0)
    nf = jnp.maximum(n, MAX_EXACT).astype(F32)
    large = MAX_EXACT + (jnp.log(nf / MAX_EXACT) / math.log(MAX_DISTANCE / MAX_EXACT)
                         * (NUM_BUCKETS - MAX_EXACT)).astype(I32)
    large = jnp.minimum(large, NUM_BUCKETS - 1)
    return jnp.where(n < MAX_EXACT, n, large)


def _bias_tiles(rel_table, tq):
    i = jnp.arange(tq, dtype=I32)[:, None]
    j = jnp.arange(tq, dtype=I32)[None, :]
    table = rel_table.astype(F32) - rel_table[NUM_BUCKETS - 1].astype(F32)[None, :]
    tiles = []
    for delta in (0, tq):
        n = i - j + delta
        onehot = (_rel_bucket(n)[:, :, None] == jnp.arange(NUM_BUCKETS, dtype=I32)).astype(F32)
        b = jnp.einsum('ijb,bh->hij', onehot, table, precision=lax.Precision.HIGHEST) * LOG2E
        tiles.append(jnp.where((n >= 0)[None], b, NEG_INF))
    return jnp.stack(tiles, axis=1)


ATT_TQ = 512
LOG2E = math.log2(math.e)


def _attention(proj, B, S, lq1, lk1, lq2, lk2, subln_g, rel_table, lam_init):
    T = B * S
    tq = min(ATT_TQ, S // 2)
    assert V_DIM == LANES and MAX_DISTANCE <= tq and S % (2 * tq) == 0
    bias = _bias_tiles(rel_table, tq)
    vec = lambda v: v.reshape(1, -1).astype(F32)
    const2 = lambda b, h: (0, 0)
    kern = functools.partial(_attn_kernel, tq=tq, lam_init=lam_init)
    return pl.pallas_call(
        kern,
        grid=(B, N_HEADS),
        in_specs=[
            pl.BlockSpec((S, V_DIM), lambda b, h: (b, COL_Q * N_HEADS + h)),
            pl.BlockSpec((S, V_DIM), lambda b, h: (b, COL_K * N_HEADS + h)),
            pl.BlockSpec((S, V_DIM), lambda b, h: (b, COL_V * N_HEADS + h)),
            pl.BlockSpec((None, 2, tq, tq), lambda b, h: (h, 0, 0, 0)),
            pl.BlockSpec((1, HEAD_DIM), const2),
            pl.BlockSpec((1, HEAD_DIM), const2),
            pl.BlockSpec((1, HEAD_DIM), const2),
            pl.BlockSpec((1, HEAD_DIM), const2),
            pl.BlockSpec((1, V_DIM), const2),
        ],
        out_specs=pl.BlockSpec((S, V_DIM), lambda b, h: (b, h)),
        out_shape=jax.ShapeDtypeStruct((T, ATT_V), BF16),
        scratch_shapes=[
            pltpu.VMEM((2, 2 * tq, V_DIM), BF16),
            pltpu.VMEM((S, 2 * V_DIM), BF16),
            pltpu.VMEM((4, 2 * tq, tq), F32),
            pltpu.VMEM((2 * tq, LANES), F32),
            pltpu.VMEM((2 * tq, 2 * V_DIM), F32),
        ],
        compiler_params=_params(("parallel", "arbitrary")),
        name="diff_attn",
    )(proj, proj, proj, bias, vec(lq1), vec(lk1), vec(lq2), vec(lk2), vec(subln_g))


def _merge_kernel(x_ref, yr_ref, ya_ref, g0_ref, g1_ref, wr_ref, wa_ref, wo_ref, gf_ref, wrt_ref, brt_ref,
                  x2_ref, hf_ref, idx_ref, wgt_ref):
    pr = jnp.dot(yr_ref[...], wr_ref[...], preferred_element_type=F32)
    pa = jnp.dot(ya_ref[...], wa_ref[...], preferred_element_type=F32)
    merged = (jax.nn.sigmoid(g0_ref[...].astype(F32)) * pr
              + jax.nn.sigmoid(g1_ref[...].astype(F32)) * pa)
    x2 = x_ref[...] + jnp.dot(merged.astype(BF16), wo_ref[...], preferred_element_type=F32)
    x2_ref[...] = x2
    inv = lax.rsqrt(jnp.mean(x2 * x2, axis=-1, keepdims=True) + EPS)
    hf = x2 * inv * gf_ref[...]
    _store_token_major(hf_ref, hf)
    nt = (((1,), (1,)), ((), ()))
    hi = hf.astype(BF16)
    lo = (hf - hi.astype(F32)).astype(BF16)
    hw = lax.dot_general(wrt_ref[...], hi, nt, preferred_element_type=F32)
    lw = lax.dot_general(wrt_ref[0:LANES, :], lo, nt, preferred_element_type=F32)
    logits = (hw[0:N_EXPERTS, :] + hw[LANES:LANES + N_EXPERTS, :] + lw[0:N_EXPERTS, :]
              + brt_ref[...])
    tm = logits.shape[1]
    row = lax.broadcasted_iota(I32, (N_EXPERTS, tm), 0)
    work = logits
    idx_rows, val_rows = [], []
    for k in range(TOP_K):
        mx = jnp.max(work, axis=0, keepdims=True)
        sel = jnp.min(jnp.where(work == mx, row, N_EXPERTS), axis=0, keepdims=True)
        idx_rows.append(sel)
        val_rows.append(mx)
        work = jnp.where(row == sel, -jnp.inf, work)
    es = [jnp.exp(v - val_rows[0]) for v in val_rows]
    inv_sum = 1.0 / functools.reduce(lambda a, b: a + b, es)
    pad_i = [jnp.zeros((1, tm), I32)] * (SUBLANES - TOP_K)
    pad_f = [jnp.zeros((1, tm), F32)] * (SUBLANES - TOP_K)
    idx_ref[...] = jnp.concatenate(idx_rows + pad_i, axis=0)
    wgt_ref[...] = jnp.concatenate([e * inv_sum for e in es] + pad_f, axis=0)


def _merge_router(x2d, y_rnn, y_att, proj, w_pr, w_pa, w_o, g_ffn, w_router, b_router):
    T, D = x2d.shape
    tm = min(512, T)
    w_hi = w_router.astype(BF16)
    w_lo = (w_router - w_hi.astype(F32)).astype(BF16)
    wrt = (jnp.zeros((2 * LANES, D), BF16).at[:N_EXPERTS, :].set(w_hi.T)
           .at[LANES:LANES + N_EXPERTS, :].set(w_lo.T))
    brt = b_router.reshape(N_EXPERTS, 1).astype(F32)
    rowblk = lambda c: pl.BlockSpec((tm, D), lambda i, c=c: (i, c))
    full = lambda a: pl.BlockSpec(a.shape, lambda i: (0,) * a.ndim)
    wr, wa, wo = w_pr.astype(BF16), w_pa.astype(BF16), w_o.astype(BF16)
    gf = g_ffn.reshape(1, D)
    return pl.pallas_call(
        _merge_kernel,
        grid=(T // tm,),
        in_specs=[rowblk(0), rowblk(0), rowblk(0), rowblk(COL_G0), rowblk(COL_G1),
                  full(wr), full(wa), full(wo), full(gf), full(wrt), full(brt)],
        out_specs=[rowblk(0), pl.BlockSpec((tm * ROW_TILES, LANES), lambda i: (i, 0)),
                   pl.BlockSpec((SUBLANES, tm), lambda i: (0, i)),
                   pl.BlockSpec((SUBLANES, tm), lambda i: (0, i))],
        out_shape=[jax.ShapeDtypeStruct((T, D), F32), jax.ShapeDtypeStruct((T * ROW_TILES, LANES), F32),
                   jax.ShapeDtypeStruct((SUBLANES, T), I32), jax.ShapeDtypeStruct((SUBLANES, T), F32)],
        compiler_params=_params(("parallel",)),
        name="merge_router",
    )(x2d, y_rnn, y_att, proj, proj, wr, wa, wo, gf, wrt, brt)


def _route_kernel(idx_ref, dest_ref, cnt_ref, cnt_scr, run_scr, start_scr, *, blk):
    ph = pl.program_id(0)
    i = pl.program_id(1)
    tt = idx_ref.shape[1]
    idx = idx_ref[...]
    row = lax.broadcasted_iota(I32, (N_EXPERTS, tt), 0)
    onehot = jnp.zeros((N_EXPERTS, tt), F32)
    for k in range(TOP_K):
        onehot = onehot + (idx[k:k + 1, :] == row).astype(F32)
    tile_cnt = jnp.sum(onehot, axis=1, keepdims=True)

    @pl.when((ph == 0) & (i == 0))
    def _():
        cnt_scr[...] = jnp.zeros_like(cnt_scr)

    @pl.when(ph == 0)
    def _():
        cnt_scr[...] += tile_cnt

    @pl.when((ph == 1) & (i == 0))
    def _():
        padded = jnp.floor((cnt_scr[...] + (blk - 1)) / blk) * blk
        r = lax.broadcasted_iota(I32, (N_EXPERTS, N_EXPERTS), 0)
        c = lax.broadcasted_iota(I32, (N_EXPERTS, N_EXPERTS), 1)
        start_scr[...] = jnp.dot((c < r).astype(F32), padded, preferred_element_type=F32,
                                 precision=lax.Precision.HIGHEST)
        run_scr[...] = jnp.zeros_like(run_scr)

    @pl.when(ph == 1)
    def _():
        r = lax.broadcasted_iota(I32, (tt, tt), 0)
        c = lax.broadcasted_iota(I32, (tt, tt), 1)
        earlier = (r < c).astype(BF16)
        before = jnp.dot(onehot.astype(BF16), earlier, preferred_element_type=F32)
        base = before + run_scr[:, 0:1] + start_scr[:, 0:1]
        rows = [jnp.sum(jnp.where(idx[k:k + 1, :] == row, base, 0.0), axis=0, keepdims=True)
                for k in range(TOP_K)]
        rows += [jnp.zeros((1, tt), F32)] * (SUBLANES - TOP_K)
        dest_ref[...] = jnp.concatenate(rows, axis=0).astype(I32)
        run_scr[...] += tile_cnt
        cnt_ref[...] = cnt_scr[...]


def _route(top_idx, blk):
    T = top_idx.shape[1]
    tt = min(512, T)
    kern = functools.partial(_route_kernel, blk=blk)
    return pl.pallas_call(
        kern,
        grid=(2, T // tt),
        in_specs=[pl.BlockSpec((SUBLANES, tt), lambda p, i: (0, i))],
        out_specs=[pl.BlockSpec((SUBLANES, tt), lambda p, i: (0, p * i)),
                   pl.BlockSpec((N_EXPERTS, LANES), lambda p, i: (0, 0))],
        out_shape=[jax.ShapeDtypeStruct((SUBLANES, T), I32),
                   jax.ShapeDtypeStruct((N_EXPERTS, LANES), F32)],
        scratch_shapes=[pltpu.VMEM((N_EXPERTS, LANES), F32), pltpu.VMEM((N_EXPERTS, LANES), F32),
                        pltpu.VMEM((N_EXPERTS, LANES), F32)],
        compiler_params=_params(("arbitrary", "arbitrary")),
        name="route",
    )(top_idx)


TOKEN_UNROLL = 2


def _for_each_token_k(n_tok, fn):
    def body(g, c):
        for u in range(TOKEN_UNROLL):
            for k in range(TOP_K):
                fn(g * TOKEN_UNROLL + u, k)
        return c

    lax.fori_loop(0, n_tok // TOKEN_UNROLL, body, 0)


def _tile_major_ranks(dest, tt):
    T = dest.shape[1]
    return dest[:TOP_K].reshape(TOP_K, T // tt, tt).transpose(1, 0, 2).reshape(-1)


def _row_tile(ref, row):
    return ref.at[pl.ds(pl.multiple_of(row * ROW_TILES, ROW_TILES), ROW_TILES), :]


def _dispatch_kernel(ends_ref, dest_ref, hf_ref, xs_ref, zero_buf, sem, zsem, *, tt, blk):
    @pl.when(pl.program_id(0) == 0)
    def _():
        zero_buf[...] = jnp.zeros_like(zero_buf)

        def tail_fill(e):
            prev_end = ends_ref[e - 1] if e > 0 else 0
            start = pl.multiple_of((ends_ref[e] - blk) * ROW_TILES, ROW_TILES)
            return ends_ref[e] > prev_end, pltpu.make_async_copy(
                zero_buf, xs_ref.at[pl.ds(start, blk * ROW_TILES), :], zsem)

        for e in range(N_EXPERTS):
            nonempty, cp = tail_fill(e)
            pl.when(nonempty)(cp.start)
        for e in range(N_EXPERTS):
            nonempty, cp = tail_fill(e)
            pl.when(nonempty)(cp.wait)

    def row_copy(t, k):
        return pltpu.make_async_copy(_row_tile(hf_ref, t),
                                     _row_tile(xs_ref, dest_ref[k * tt + t]), sem)

    _for_each_token_k(tt, lambda t, k: row_copy(t, k).start(priority=k % 2))
    _for_each_token_k(tt, lambda t, k: row_copy(t, k).wait())


def _dispatch(ends, dest, hf_tm, n_slots, blk):
    T = hf_tm.shape[0] // ROW_TILES
    tt = min(512, T)
    kern = functools.partial(_dispatch_kernel, tt=tt, blk=blk)
    grid_spec = pltpu.PrefetchScalarGridSpec(
        num_scalar_prefetch=1,
        grid=(T // tt,),
        in_specs=[pl.BlockSpec((TOP_K * tt,), lambda i, ends: (i,), memory_space=pltpu.SMEM),
                  pl.BlockSpec((tt * ROW_TILES, LANES), lambda i, ends: (i, 0))],
        out_specs=pl.BlockSpec(memory_space=pl.ANY),
        scratch_shapes=[pltpu.VMEM((blk * ROW_TILES, LANES), F32),
                        pltpu.SemaphoreType.DMA(()), pltpu.SemaphoreType.DMA(())],
    )
    return pl.pallas_call(
        kern,
        grid_spec=grid_spec,
        out_shape=jax.ShapeDtypeStruct((n_slots * ROW_TILES, LANES), F32),
        compiler_params=_params(("arbitrary",)),
        name="dispatch",
    )(ends, _tile_major_ranks(dest, tt), hf_tm)


def _expert_kernel(be_ref, bsrc_ref, act_ref, first_ref, x_ref, wg_ref, bg_ref, wu_ref, bu_ref, wd_ref, bd_ref,
                   o_ref, wg_scr, wu_scr, wd_scr, *, blk):
    del be_ref, bsrc_ref
    i = pl.program_id(0)

    @pl.when(first_ref[i] == 1)
    def _():
        wg_scr[...] = wg_ref[0].astype(BF16)
        wu_scr[...] = wu_ref[0].astype(BF16)
        wd_scr[...] = wd_ref[0].astype(BF16)

    @pl.when(act_ref[i] == 1)
    def _():
        x = _load_token_major(x_ref, 0, blk).astype(BF16)
        g = jnp.dot(x, wg_scr[...], preferred_element_type=F32) + bg_ref[0]
        u = jnp.dot(x, wu_scr[...], preferred_element_type=F32) + bu_ref[0]
        g = jnp.minimum(g, SWIGLU_LIMIT)
        u = jnp.clip(u, -SWIGLU_LIMIT, SWIGLU_LIMIT)
        act = (u + 1.0) * (g * jax.nn.sigmoid(SWIGLU_ALPHA * g))
        y = jnp.dot(act.astype(BF16), wd_scr[...], preferred_element_type=F32) + bd_ref[0]
        _store_token_major(o_ref, y)


def _experts(x_slots, blk, block_expert, block_src, block_active, block_first, wg, bg, wu, bu, wd, bd):
    D = D_MODEL
    nb = x_slots.shape[0] // (blk * ROW_TILES)
    wspec = pl.BlockSpec((1, D, D_FF), lambda i, be, bs, ac, fi: (be[i], 0, 0))
    bspec = pl.BlockSpec((1, 1, D_FF), lambda i, be, bs, ac, fi: (be[i], 0, 0))
    xspec = pl.BlockSpec((blk * ROW_TILES, LANES), lambda i, be, bs, ac, fi: (bs[i], 0))
    grid_spec = pltpu.PrefetchScalarGridSpec(
        num_scalar_prefetch=4,
        grid=(nb,),
        in_specs=[xspec, wspec, bspec, wspec, bspec, wspec, bspec],
        out_specs=xspec,
        scratch_shapes=[pltpu.VMEM((D, D_FF), BF16), pltpu.VMEM((D, D_FF), BF16),
                        pltpu.VMEM((D_FF, D), BF16)],
    )
    b3 = lambda b: b.reshape(N_EXPERTS, 1, -1)
    return pl.pallas_call(
        functools.partial(_expert_kernel, blk=blk),
        grid_spec=grid_spec,
        out_shape=jax.ShapeDtypeStruct(x_slots.shape, F32),
        compiler_params=_params(("arbitrary",)),
        name="experts",
    )(block_expert, block_src, block_active, block_first, x_slots,
      wg, b3(bg), wu, b3(bu), wd, b3(bd))


def _combine_kernel(dest_ref, dest_next_ref, ys_ref, w_ref, x2_ref, g_ref, o_ref, buf, sems, *, tt):
    i = pl.program_id(0)
    n = pl.num_programs(0)
    cur = i % 2

    def row_copy(idx_ref, slot, t, k):
        return pltpu.make_async_copy(_row_tile(ys_ref, idx_ref[k * tt + t]),
                                     _row_tile(buf, (slot * TOP_K + k) * tt + t), sems.at[slot])

    def gather(idx_ref, slot):
        _for_each_token_k(tt, lambda t, k: row_copy(idx_ref, slot, t, k).start(priority=k % 2))

    @pl.when(i == 0)
    def _():
        gather(dest_ref, 0)

    @pl.when(i + 1 < n)
    def _():
        gather(dest_next_ref, 1 - cur)

    _for_each_token_k(tt, lambda t, k: row_copy(dest_ref, cur, t, k).wait())
    w = w_ref[...]
    y = x2_ref[...]
    for k in range(TOP_K):
        y = y + w[:, k:k + 1] * _load_token_major(buf, (cur * TOP_K + k) * tt, tt)
    inv = lax.rsqrt(jnp.mean(y * y, axis=-1, keepdims=True) + EPS)
    o_ref[...] = y * inv * g_ref[...]


def _combine(dest, y_slots, top_w, x2, g_final):
    T, D = x2.shape
    tt = min(256, T)
    nt = T // tt
    kern = functools.partial(_combine_kernel, tt=tt)
    return pl.pallas_call(
        kern,
        grid=(nt,),
        in_specs=[pl.BlockSpec((TOP_K * tt,), lambda i: (i,), memory_space=pltpu.SMEM),
                  pl.BlockSpec((TOP_K * tt,), lambda i: (jnp.minimum(i + 1, nt - 1),),
                               memory_space=pltpu.SMEM),
                  pl.BlockSpec(memory_space=pl.ANY),
                  pl.BlockSpec((tt, SUBLANES), lambda i: (i, 0)),
                  pl.BlockSpec((tt, D), lambda i: (i, 0)),
                  pl.BlockSpec((1, D), lambda i: (0, 0))],
        out_specs=pl.BlockSpec((tt, D), lambda i: (i, 0)),
        out_shape=jax.ShapeDtypeStruct((T, D), F32),
        scratch_shapes=[pltpu.VMEM((2 * TOP_K * tt * ROW_TILES, LANES), F32),
                        pltpu.SemaphoreType.DMA((2,))],
        compiler_params=_params(("arbitrary",)),
        name="combine",
    )(_tile_major_ranks(dest, tt), _tile_major_ranks(dest, tt), y_slots, top_w.T, x2,
      g_final.reshape(1, D))


def _moe_block_size(T):
    return min(512, max(SUBLANES, T * TOP_K // N_EXPERTS))


def _block_tables(counts, blk, nb):
    cnt = counts.astype(I32)
    padded = (cnt + blk - 1) // blk * blk
    ends = jnp.cumsum(padded)
    used = ends[-1] // blk
    starts = jnp.arange(nb, dtype=I32) * blk
    src = jnp.minimum(jnp.arange(nb, dtype=I32), jnp.maximum(used - 1, 0))
    expert = jnp.minimum(jnp.sum((starts[:, None] >= ends[None, :]).astype(I32), axis=1), N_EXPERTS - 1)
    expert = expert[src]
    active = (jnp.arange(nb, dtype=I32) < used).astype(I32)
    prev = jnp.concatenate([jnp.full((1,), -1, I32), expert[:-1]])
    first = active * (expert != prev).astype(I32)
    return ends.astype(I32), expert, src, active, first


def kernel(x, norm_mix_g, w_in, conv_w, conv_b, w_rg_a, b_rg_a, w_rg_x, b_rg_x, lru_lambda, diff_lambda_q1, diff_lambda_k1, diff_lambda_q2, diff_lambda_k2, subln_g, rel_bias_table, w_proj_rnn, w_proj_att, w_out, norm_ffn_g, w_router, b_router, w_gate_e, b_gate_e, w_up_e, b_up_e, w_down_e, b_down_e, norm_final_g):
    B, S, D = x.shape
    T = B * S
    assert norm_mix_g.shape[0] == 1, "single-layer block: the final norm is fused into the MoE combine"
    l = 0
    xt = x.reshape(T, D)
    lam_init = 0.8 - 0.6 * math.exp(-0.3 * l)
    proj = _inproj(xt, norm_mix_g[l], w_in[l].astype(BF16))
    y_rnn = _rglru(proj, B, S, conv_w[l], conv_b[l], w_rg_a[l], b_rg_a[l], w_rg_x[l], b_rg_x[l],
                   lru_lambda[l])
    y_att = _attention(proj, B, S, diff_lambda_q1[l], diff_lambda_k1[l], diff_lambda_q2[l],
                       diff_lambda_k2[l], subln_g[l], rel_bias_table, lam_init)
    x2, hf, top_idx, top_w = _merge_router(xt, y_rnn, y_att, proj, w_proj_rnn[l], w_proj_att[l],
                                           w_out[l], norm_ffn_g[l], w_router[l], b_router[l])
    blk = _moe_block_size(T)
    nb = T * TOP_K // blk + N_EXPERTS
    dest, counts = _route(top_idx, blk)
    ends, expert, src, active, first = _block_tables(counts[:, 0], blk, nb)
    x_slots = _dispatch(ends, dest, hf, nb * blk, blk)
    y_slots = _experts(x_slots, blk, expert, src, active, first, w_gate_e[l], b_gate_e[l], w_up_e[l],
                       b_up_e[l], w_down_e[l], b_down_e[l])
    out = _combine(dest, y_slots, top_w, x2, norm_final_g)
    return out.reshape(B, S, D)
```

```python
import functools
import math

import jax
import jax.numpy as jnp
from jax import lax
from jax.experimental import pallas as pl
from jax.experimental.pallas import tpu as pltpu

F32 = jnp.float32
BF16 = jnp.bfloat16
I32 = jnp.int32

D_MODEL = 1024
D_RNN = 1024
RNN_BLOCKS = 16
RNN_BLOCK = D_RNN // RNN_BLOCKS
CONV_W = 4
RGLRU_C = 8.0
N_HEADS = 8
HEAD_DIM = 64
V_DIM = 2 * HEAD_DIM
ATT_QK = N_HEADS * 2 * HEAD_DIM
ATT_V = N_HEADS * V_DIM
D_IN = 2 * D_RNN + 2 * ATT_QK + ATT_V + 2 * D_MODEL
NUM_BUCKETS = 32
MAX_EXACT = NUM_BUCKETS // 2
MAX_DISTANCE = 128
N_EXPERTS = 32
TOP_K = 4
D_FF = D_MODEL
SWIGLU_LIMIT = 7.0
SWIGLU_ALPHA = 1.702
EPS = 1e-6
NEG_INF = -1e30

LANES = 128
SUBLANES = 8
MXU_DIM = 256
VMEM_LIMIT = 56 * 1024 * 1024

COL_XR, COL_GR, COL_Q, COL_K, COL_V, COL_G0, COL_G1 = range(7)


def _params(sem, vmem=VMEM_LIMIT):
    return pltpu.CompilerParams(dimension_semantics=sem, vmem_limit_bytes=vmem)


ROW_TILES = D_MODEL // LANES
assert ROW_TILES == SUBLANES


def _store_token_major(ref, val, start=0):
    n = val.shape[0]
    for c in range(ROW_TILES):
        ref[pl.ds(start * ROW_TILES + c, n, stride=ROW_TILES), :] = val[:, c * LANES:(c + 1) * LANES]


def _load_token_major(ref, start, n):
    return jnp.concatenate(
        [ref[pl.ds(start * ROW_TILES + c, n, stride=ROW_TILES), :] for c in range(ROW_TILES)], axis=-1)


def _inproj_kernel(x_ref, g_ref, w_ref, o_ref, h_scr):
    @pl.when(pl.program_id(1) == 0)
    def _():
        x = x_ref[...]
        inv = lax.rsqrt(jnp.mean(x * x, axis=-1, keepdims=True) + EPS)
        h_scr[...] = (x * inv * g_ref[...]).astype(BF16)

    o_ref[...] = jnp.dot(h_scr[...], w_ref[...], preferred_element_type=F32).astype(o_ref.dtype)


def _inproj(x2d, g, w_bf16):
    T, D = x2d.shape
    N = w_bf16.shape[1]
    tm = min(1024, T)
    tn = 3584 if N % 3584 == 0 else 1024
    return pl.pallas_call(
        _inproj_kernel,
        grid=(T // tm, N // tn),
        in_specs=[
            pl.BlockSpec((tm, D), lambda i, j: (i, 0)),
            pl.BlockSpec((1, D), lambda i, j: (0, 0)),
            pl.BlockSpec((D, tn), lambda i, j: (0, j)),
        ],
        out_specs=pl.BlockSpec((tm, tn), lambda i, j: (i, j)),
        out_shape=jax.ShapeDtypeStruct((T, N), BF16),
        scratch_shapes=[pltpu.VMEM((tm, D), BF16)],
        compiler_params=_params(("parallel", "arbitrary")),
        name="inproj",
    )(x2d, g.reshape(1, D), w_bf16)


def _rglru_kernel(xr_ref, gr_ref, cw_ref, cb_ref, wa_ref, ba_ref, wx_ref, bx_ref, lam_ref,
                  o_ref, xbuf, tail_scr, a_scr, u_scr, h_scr):
    nb, ts, _ = xr_ref.shape
    pad = SUBLANES
    nslab = D_RNN // LANES

    @pl.when(pl.program_id(0) == 0)
    def _():
        tail_scr[...] = jnp.zeros_like(tail_scr)
        h_scr[...] = jnp.zeros_like(h_scr)

    cw = cw_ref[...]
    z = -lam_ref[...]
    softplus = jnp.maximum(z, 0.0) + jnp.log1p(jnp.exp(-jnp.abs(z)))
    sigmoid = lambda v: 0.5 * jnp.tanh(0.5 * v) + 0.5
    nchunk = D_RNN // MXU_DIM
    for b in range(nb):
        x = xr_ref[b].astype(F32)
        xbuf[0:pad, :] = tail_scr[b]
        xbuf[pad:pad + ts, :] = x
        xc = (cw[3:4, :] * x
              + cw[2:3, :] * xbuf[pad - 1:pad - 1 + ts, :]
              + cw[1:2, :] * xbuf[pad - 2:pad - 2 + ts, :]
              + cw[0:1, :] * xbuf[pad - 3:pad - 3 + ts, :]) + cb_ref[...]
        tail_scr[b] = x[ts - pad:ts, :]
        xcb = xc.astype(BF16)
        r_pre = jnp.concatenate(
            [jnp.dot(xcb[:, c * MXU_DIM:(c + 1) * MXU_DIM], wa_ref[c], preferred_element_type=F32)
             for c in range(nchunk)], axis=-1)
        i_pre = jnp.concatenate(
            [jnp.dot(xcb[:, c * MXU_DIM:(c + 1) * MXU_DIM], wx_ref[c], preferred_element_type=F32)
             for c in range(nchunk)], axis=-1)
        r = sigmoid(r_pre + ba_ref[...])
        ig = sigmoid(i_pre + bx_ref[...])
        a = jnp.exp((-RGLRU_C) * r * softplus)
        u = jnp.sqrt(1.0 - a * a) * (ig * xc)
        for c in range(nslab):
            a_scr[c, pl.ds(b, ts, stride=nb), :] = a[:, c * LANES:(c + 1) * LANES]
            u_scr[c, pl.ds(b, ts, stride=nb), :] = u[:, c * LANES:(c + 1) * LANES]

    def body(g, hs):
        for j in range(SCAN_UNROLL):
            off = pl.multiple_of((g * SCAN_UNROLL + j) * nb, nb)
            new = []
            for c in range(nslab):
                h = a_scr[c, pl.ds(off, nb), :] * hs[c] + u_scr[c, pl.ds(off, nb), :]
                u_scr[c, pl.ds(off, nb), :] = h
                new.append(h)
            hs = tuple(new)
        return hs

    hs = lax.fori_loop(0, ts // SCAN_UNROLL, body, tuple(h_scr[c] for c in range(nslab)))
    for c in range(nslab):
        h_scr[c] = hs[c]
    for b in range(nb):
        h = jnp.concatenate([u_scr[c, pl.ds(b, ts, stride=nb), :] for c in range(nslab)], axis=-1)
        gate = jax.nn.gelu(gr_ref[b].astype(F32), approximate=True)
        o_ref[b] = (h * gate).astype(o_ref.dtype)


SCAN_UNROLL = 8


def _block_diag_chunks(w):
    per = MXU_DIM // RNN_BLOCK
    w = w.reshape(D_RNN // MXU_DIM, per, RNN_BLOCK, RNN_BLOCK)
    eye = jnp.eye(per, dtype=w.dtype)
    out = jnp.einsum('gpcd,pq->gpcqd', w, eye)
    return out.reshape(D_RNN // MXU_DIM, MXU_DIM, MXU_DIM)


def _rglru(proj, B, S, conv_w, conv_b, w_a, b_a, w_x, b_x, lru_lambda):
    assert B <= SUBLANES, "all batch rows of a time step share one vreg in the scan"
    ts = min(256, S)
    ns = S // ts
    wa = _block_diag_chunks(w_a).astype(BF16)
    wx = _block_diag_chunks(w_x).astype(BF16)
    nchunk = D_RNN // MXU_DIM
    nslab = D_RNN // LANES
    row = lambda v: v.reshape(1, D_RNN)
    const2 = lambda s: (0, 0)
    proj3 = proj.reshape(B, S, proj.shape[-1])
    out = pl.pallas_call(
        _rglru_kernel,
        grid=(ns,),
        in_specs=[
            pl.BlockSpec((B, ts, D_RNN), lambda s: (0, s, COL_XR)),
            pl.BlockSpec((B, ts, D_RNN), lambda s: (0, s, COL_GR)),
            pl.BlockSpec((CONV_W, D_RNN), const2),
            pl.BlockSpec((1, D_RNN), const2),
            pl.BlockSpec((nchunk, MXU_DIM, MXU_DIM), lambda s: (0, 0, 0)),
            pl.BlockSpec((1, D_RNN), const2),
            pl.BlockSpec((nchunk, MXU_DIM, MXU_DIM), lambda s: (0, 0, 0)),
            pl.BlockSpec((1, D_RNN), const2),
            pl.BlockSpec((1, D_RNN), const2),
        ],
        out_specs=pl.BlockSpec((B, ts, D_RNN), lambda s: (0, s, 0)),
        out_shape=jax.ShapeDtypeStruct((B, S, D_RNN), BF16),
        scratch_shapes=[
            pltpu.VMEM((ts + SUBLANES, D_RNN), F32),
            pltpu.VMEM((B, SUBLANES, D_RNN), F32),
            pltpu.VMEM((nslab, ts * B, LANES), F32),
            pltpu.VMEM((nslab, ts * B, LANES), F32),
            pltpu.VMEM((nslab, B, LANES), F32),
        ],
        compiler_params=_params(("arbitrary",)),
        name="rglru",
    )(proj3, proj3, conv_w, row(conv_b), wa, row(b_a), wx, row(b_x), row(lru_lambda))
    return out.reshape(B * S, D_RNN)


def _attn_kernel(q_ref, k_ref, v_ref, bias_ref, lq1_ref, lk1_ref, lq2_ref, lk2_ref, sg_ref,
                 o_ref, qs_scr, vx_scr, s_scr, m_scr, acc_scr, *, tq, lam_init):
    S = q_ref.shape[0]
    nq = S // tq
    scale = HEAD_DIM ** -0.5 * LOG2E
    lane = lax.broadcasted_iota(I32, (tq, V_DIM), 1)
    lam = (jnp.exp(jnp.sum(lq1_ref[...] * lk1_ref[...], keepdims=True))
           - jnp.exp(jnp.sum(lq2_ref[...] * lk2_ref[...], keepdims=True)) + lam_init)
    vx_scr[:, 0:V_DIM] = v_ref[...]
    vx_scr[:, V_DIM:2 * V_DIM] = jnp.ones((S, V_DIM), BF16)

    Z0 = 2

    def prep_q(qi, buf):
        q = (q_ref[pl.ds(pl.multiple_of(qi * tq, tq), tq), :].astype(F32) * scale).astype(BF16)
        zero = jnp.zeros_like(q)
        qs_scr[buf, 0:tq, :] = jnp.where(lane < HEAD_DIM, q, zero)
        qs_scr[buf, tq:2 * tq, :] = jnp.where(lane >= HEAD_DIM, q, zero)

    def scores(buf, j, dst):
        k = k_ref[pl.ds(pl.multiple_of(j * tq, tq), tq), :]
        s_scr[dst] = lax.dot_general(qs_scr[buf], k, (((1,), (1,)), ((), ())),
                                     preferred_element_type=F32)

    def step(j, src, bias_idx, prefetch):
        scores(*prefetch)
        s = s_scr[src]
        if bias_idx is not None:
            b = bias_ref[bias_idx]
            s = s + jnp.concatenate([b, b], axis=0)
        vx = vx_scr[pl.ds(pl.multiple_of(j * tq, tq), tq), :]
        m_prev = m_scr[...]
        m_new = jnp.maximum(m_prev, jnp.max(s, axis=1)[:, None])
        p = jnp.exp2(s - jnp.tile(m_new, (1, tq // LANES)))
        alpha = jnp.exp2(m_prev - m_new)
        acc_scr[...] = (jnp.tile(alpha, (1, 2)) * acc_scr[...]
                        + jnp.dot(p.astype(BF16), vx, preferred_element_type=F32))
        m_scr[...] = m_new

    def begin_block():
        m_scr[...] = jnp.full(m_scr.shape, NEG_INF, F32)
        acc_scr[...] = jnp.zeros(acc_scr.shape, F32)

    def diag_step(qi, par, src):
        nxt = jnp.minimum(qi + 1, nq - 1)
        prep_q(nxt, 1 - par)
        step(qi, src, 0, (1 - par, 0, Z0 + 1 - par))

    def end_block(qi):
        acc = acc_scr[...]
        o_all = acc[:, 0:V_DIM] / acc[:, V_DIM:2 * V_DIM]
        o = o_all[0:tq, :] - lam * o_all[tq:2 * tq, :]
        inv = lax.rsqrt(jnp.mean(o * o, axis=-1, keepdims=True) + EPS)
        y = (o * inv * sg_ref[...]) * (1.0 - lam_init)
        o_ref[pl.ds(pl.multiple_of(qi * tq, tq), tq), :] = y.astype(o_ref.dtype)

    def far_pairs(par, npairs):
        def body(m, c):
            step(2 * m + 1, 0, None, (par, 2 * m + 2, 1))
            step(2 * m + 2, 1, None, (par, 2 * m + 3, 0))
            return c

        lax.fori_loop(0, npairs, body, 0)

    prep_q(0, 0)
    scores(0, 0, Z0)
    begin_block()
    diag_step(0, 0, Z0)
    end_block(0)
    begin_block()
    step(0, Z0 + 1, 1, (1, 1, 0))
    diag_step(1, 1, 0)
    end_block(1)

    def block_pair(i, carry):
        qi = 2 * i
        begin_block()
        step(0, Z0, None, (0, 1, 0))
        far_pairs(0, i - 1)
        step(qi - 1, 0, 1, (0, qi, 1))
        diag_step(qi, 0, 1)
        end_block(qi)
        qi = 2 * i + 1
        begin_block()
        step(0, Z0 + 1, None, (1, 1, 0))
        far_pairs(1, i - 1)
        step(qi - 2, 0, None, (1, qi - 1, 1))
        step(qi - 1, 1, 1, (1, qi, 0))
        diag_step(qi, 1, 0)
        end_block(qi)
        return carry

    lax.fori_loop(1, nq // 2, block_pair, 0)


def _rel_bucket(n):
    n = jnp.maximum(n, 0)
    nf = jnp.maximum(n, MAX_EXACT).astype(F32)
    large = MAX_EXACT + (jnp.log(nf / MAX_EXACT) / math.log(MAX_DISTANCE / MAX_EXACT)
                         * (NUM_BUCKETS - MAX_EXACT)).astype(I32)
    large = jnp.minimum(large, NUM_BUCKETS - 1)
    return jnp.where(n < MAX_EXACT, n, large)


def _bias_tiles(rel_table, tq):
    i = jnp.arange(tq, dtype=I32)[:, None]
    j = jnp.arange(tq, dtype=I32)[None, :]
    table = rel_table.astype(F32) - rel_table[NUM_BUCKETS - 1].astype(F32)[None, :]
    tiles = []
    for delta in (0, tq):
        n = i - j + delta
        onehot = (_rel_bucket(n)[:, :, None] == jnp.arange(NUM_BUCKETS, dtype=I32)).astype(F32)
        b = jnp.einsum('ijb,bh->hij', onehot, table, precision=lax.Precision.HIGHEST) * LOG2E
        tiles.append(jnp.where((n >= 0)[None], b, NEG_INF))
    return jnp.stack(tiles, axis=1)


ATT_TQ = 512
LOG2E = math.log2(math.e)


def _attention(proj, B, S, lq1, lk1, lq2, lk2, subln_g, rel_table, lam_init):
    T = B * S
    tq = min(ATT_TQ, S // 2)
    assert V_DIM == LANES and MAX_DISTANCE <= tq and S % (2 * tq) == 0
    bias = _bias_tiles(rel_table, tq)
    vec = lambda v: v.reshape(1, -1).astype(F32)
    const2 = lambda b, h: (0, 0)
    kern = functools.partial(_attn_kernel, tq=tq, lam_init=lam_init)
    return pl.pallas_call(
        kern,
        grid=(B, N_HEADS),
        in_specs=[
            pl.BlockSpec((S, V_DIM), lambda b, h: (b, COL_Q * N_HEADS + h)),
            pl.BlockSpec((S, V_DIM), lambda b, h: (b, COL_K * N_HEADS + h)),
            pl.BlockSpec((S, V_DIM), lambda b, h: (b, COL_V * N_HEADS + h)),
            pl.BlockSpec((None, 2, tq, tq), lambda b, h: (h, 0, 0, 0)),
            pl.BlockSpec((1, HEAD_DIM), const2),
            pl.BlockSpec((1, HEAD_DIM), const2),
            pl.BlockSpec((1, HEAD_DIM), const2),
            pl.BlockSpec((1, HEAD_DIM), const2),
            pl.BlockSpec((1, V_DIM), const2),
        ],
        out_specs=pl.BlockSpec((S, V_DIM), lambda b, h: (b, h)),
        out_shape=jax.ShapeDtypeStruct((T, ATT_V), BF16),
        scratch_shapes=[
            pltpu.VMEM((2, 2 * tq, V_DIM), BF16),
            pltpu.VMEM((S, 2 * V_DIM), BF16),
            pltpu.VMEM((4, 2 * tq, tq), F32),
            pltpu.VMEM((2 * tq, LANES), F32),
            pltpu.VMEM((2 * tq, 2 * V_DIM), F32),
        ],
        compiler_params=_params(("parallel", "arbitrary")),
        name="diff_attn",
    )(proj, proj, proj, bias, vec(lq1), vec(lk1), vec(lq2), vec(lk2), vec(subln_g))


def _merge_kernel(x_ref, yr_ref, ya_ref, g0_ref, g1_ref, wr_ref, wa_ref, wo_ref, gf_ref, wrt_ref, brt_ref,
                  x2_ref, hf_ref, idx_ref, wgt_ref):
    pr = jnp.dot(yr_ref[...], wr_ref[...], preferred_element_type=F32)
    pa = jnp.dot(ya_ref[...], wa_ref[...], preferred_element_type=F32)
    merged = (jax.nn.sigmoid(g0_ref[...].astype(F32)) * pr
              + jax.nn.sigmoid(g1_ref[...].astype(F32)) * pa)
    x2 = x_ref[...] + jnp.dot(merged.astype(BF16), wo_ref[...], preferred_element_type=F32)
    x2_ref[...] = x2
    inv = lax.rsqrt(jnp.mean(x2 * x2, axis=-1, keepdims=True) + EPS)
    hf = x2 * inv * gf_ref[...]
    _store_token_major(hf_ref, hf)
    nt = (((1,), (1,)), ((), ()))
    hi = hf.astype(BF16)
    lo = (hf - hi.astype(F32)).astype(BF16)
    hw = lax.dot_general(wrt_ref[...], hi, nt, preferred_element_type=F32)
    lw = lax.dot_general(wrt_ref[0:LANES, :], lo, nt, preferred_element_type=F32)
    logits = (hw[0:N_EXPERTS, :] + hw[LANES:LANES + N_EXPERTS, :] + lw[0:N_EXPERTS, :]
              + brt_ref[...])
    tm = logits.shape[1]
    row = lax.broadcasted_iota(I32, (N_EXPERTS, tm), 0)
    work = logits
    idx_rows, val_rows = [], []
    for k in range(TOP_K):
        mx = jnp.max(work, axis=0, keepdims=True)
        sel = jnp.min(jnp.where(work == mx, row, N_EXPERTS), axis=0, keepdims=True)
        idx_rows.append(sel)
        val_rows.append(mx)
        work = jnp.where(row == sel, -jnp.inf, work)
    es = [jnp.exp(v - val_rows[0]) for v in val_rows]
    inv_sum = 1.0 / functools.reduce(lambda a, b: a + b, es)
    pad_i = [jnp.zeros((1, tm), I32)] * (SUBLANES - TOP_K)
    pad_f = [jnp.zeros((1, tm), F32)] * (SUBLANES - TOP_K)
    idx_ref[...] = jnp.concatenate(idx_rows + pad_i, axis=0)
    wgt_ref[...] = jnp.concatenate([e * inv_sum for e in es] + pad_f, axis=0)


def _merge_router(x2d, y_rnn, y_att, proj, w_pr, w_pa, w_o, g_ffn, w_router, b_router):
    T, D = x2d.shape
    tm = min(512, T)
    w_hi = w_router.astype(BF16)
    w_lo = (w_router - w_hi.astype(F32)).astype(BF16)
    wrt = (jnp.zeros((2 * LANES, D), BF16).at[:N_EXPERTS, :].set(w_hi.T)
           .at[LANES:LANES + N_EXPERTS, :].set(w_lo.T))
    brt = b_router.reshape(N_EXPERTS, 1).astype(F32)
    rowblk = lambda c: pl.BlockSpec((tm, D), lambda i, c=c: (i, c))
    full = lambda a: pl.BlockSpec(a.shape, lambda i: (0,) * a.ndim)
    wr, wa, wo = w_pr.astype(BF16), w_pa.astype(BF16), w_o.astype(BF16)
    gf = g_ffn.reshape(1, D)
    return pl.pallas_call(
        _merge_kernel,
        grid=(T // tm,),
        in_specs=[rowblk(0), rowblk(0), rowblk(0), rowblk(COL_G0), rowblk(COL_G1),
                  full(wr), full(wa), full(wo), full(gf), full(wrt), full(brt)],
        out_specs=[rowblk(0), pl.BlockSpec((tm * ROW_TILES, LANES), lambda i: (i, 0)),
                   pl.BlockSpec((SUBLANES, tm), lambda i: (0, i)),
                   pl.BlockSpec((SUBLANES, tm), lambda i: (0, i))],
        out_shape=[jax.ShapeDtypeStruct((T, D), F32), jax.ShapeDtypeStruct((T * ROW_TILES, LANES), F32),
                   jax.ShapeDtypeStruct((SUBLANES, T), I32), jax.ShapeDtypeStruct((SUBLANES, T), F32)],
        compiler_params=_params(("parallel",)),
        name="merge_router",
    )(x2d, y_rnn, y_att, proj, proj, wr, wa, wo, gf, wrt, brt)


def _route_kernel(idx_ref, dest_ref, cnt_ref, cnt_scr, run_scr, start_scr, *, blk):
    ph = pl.program_id(0)
    i = pl.program_id(1)
    tt = idx_ref.shape[1]
    idx = idx_ref[...]
    row = lax.broadcasted_iota(I32, (N_EXPERTS, tt), 0)
    onehot = jnp.zeros((N_EXPERTS, tt), F32)
    for k in range(TOP_K):
        onehot = onehot + (idx[k:k + 1, :] == row).astype(F32)
    tile_cnt = jnp.sum(onehot, axis=1, keepdims=True)

    @pl.when((ph == 0) & (i == 0))
    def _():
        cnt_scr[...] = jnp.zeros_like(cnt_scr)

    @pl.when(ph == 0)
    def _():
        cnt_scr[...] += tile_cnt

    @pl.when((ph == 1) & (i == 0))
    def _():
        padded = jnp.floor((cnt_scr[...] + (blk - 1)) / blk) * blk
        r = lax.broadcasted_iota(I32, (N_EXPERTS, N_EXPERTS), 0)
        c = lax.broadcasted_iota(I32, (N_EXPERTS, N_EXPERTS), 1)
        start_scr[...] = jnp.dot((c < r).astype(F32), padded, preferred_element_type=F32,
                                 precision=lax.Precision.HIGHEST)
        run_scr[...] = jnp.zeros_like(run_scr)

    @pl.when(ph == 1)
    def _():
        r = lax.broadcasted_iota(I32, (tt, tt), 0)
        c = lax.broadcasted_iota(I32, (tt, tt), 1)
        earlier = (r < c).astype(BF16)
        before = jnp.dot(onehot.astype(BF16), earlier, preferred_element_type=F32)
        base = before + run_scr[:, 0:1] + start_scr[:, 0:1]
        rows = [jnp.sum(jnp.where(idx[k:k + 1, :] == row, base, 0.0), axis=0, keepdims=True)
                for k in range(TOP_K)]
        rows += [jnp.zeros((1, tt), F32)] * (SUBLANES - TOP_K)
        dest_ref[...] = jnp.concatenate(rows, axis=0).astype(I32)
        run_scr[...] += tile_cnt
        cnt_ref[...] = cnt_scr[...]


def _route(top_idx, blk):
    T = top_idx.shape[1]
    tt = min(512, T)
    kern = functools.partial(_route_kernel, blk=blk)
    return pl.pallas_call(
        kern,
        grid=(2, T // tt),
        in_specs=[pl.BlockSpec((SUBLANES, tt), lambda p, i: (0, i))],
        out_specs=[pl.BlockSpec((SUBLANES, tt), lambda p, i: (0, p * i)),
                   pl.BlockSpec((N_EXPERTS, LANES), lambda p, i: (0, 0))],
        out_shape=[jax.ShapeDtypeStruct((SUBLANES, T), I32),
                   jax.ShapeDtypeStruct((N_EXPERTS, LANES), F32)],
        scratch_shapes=[pltpu.VMEM((N_EXPERTS, LANES), F32), pltpu.VMEM((N_EXPERTS, LANES), F32),
                        pltpu.VMEM((N_EXPERTS, LANES), F32)],
        compiler_params=_params(("arbitrary", "arbitrary")),
        name="route",
    )(top_idx)


TOKEN_UNROLL = 2


def _for_each_token_k(n_tok, fn):
    def body(g, c):
        for u in range(TOKEN_UNROLL):
            for k in range(TOP_K):
                fn(g * TOKEN_UNROLL + u, k)
        return c

    lax.fori_loop(0, n_tok // TOKEN_UNROLL, body, 0)


def _tile_major_ranks(dest, tt):
    T = dest.shape[1]
    return dest[:TOP_K].reshape(TOP_K, T // tt, tt).transpose(1, 0, 2).reshape(-1)


def _row_tile(ref, row):
    return ref.at[pl.ds(pl.multiple_of(row * ROW_TILES, ROW_TILES), ROW_TILES), :]


def _dispatch_kernel(ends_ref, dest_ref, hf_ref, xs_ref, zero_buf, sem, zsem, *, tt, blk):
    @pl.when(pl.program_id(0) == 0)
    def _():
        zero_buf[...] = jnp.zeros_like(zero_buf)

        def tail_fill(e):
            prev_end = ends_ref[e - 1] if e > 0 else 0
            start = pl.multiple_of((ends_ref[e] - blk) * ROW_TILES, ROW_TILES)
            return ends_ref[e] > prev_end, pltpu.make_async_copy(
                zero_buf, xs_ref.at[pl.ds(start, blk * ROW_TILES), :], zsem)

        for e in range(N_EXPERTS):
            nonempty, cp = tail_fill(e)
            pl.when(nonempty)(cp.start)
        for e in range(N_EXPERTS):
            nonempty, cp = tail_fill(e)
            pl.when(nonempty)(cp.wait)

    def row_copy(t, k):
        return pltpu.make_async_copy(_row_tile(hf_ref, t),
                                     _row_tile(xs_ref, dest_ref[k * tt + t]), sem)

    _for_each_token_k(tt, lambda t, k: row_copy(t, k).start(priority=k % 2))
    _for_each_token_k(tt, lambda t, k: row_copy(t, k).wait())


def _dispatch(ends, dest, hf_tm, n_slots, blk):
    T = hf_tm.shape[0] // ROW_TILES
    tt = min(512, T)
    kern = functools.partial(_dispatch_kernel, tt=tt, blk=blk)
    grid_spec = pltpu.PrefetchScalarGridSpec(
        num_scalar_prefetch=1,
        grid=(T // tt,),
        in_specs=[pl.BlockSpec((TOP_K * tt,), lambda i, ends: (i,), memory_space=pltpu.SMEM),
                  pl.BlockSpec((tt * ROW_TILES, LANES), lambda i, ends: (i, 0))],
        out_specs=pl.BlockSpec(memory_space=pl.ANY),
        scratch_shapes=[pltpu.VMEM((blk * ROW_TILES, LANES), F32),
                        pltpu.SemaphoreType.DMA(()), pltpu.SemaphoreType.DMA(())],
    )
    return pl.pallas_call(
        kern,
        grid_spec=grid_spec,
        out_shape=jax.ShapeDtypeStruct((n_slots * ROW_TILES, LANES), F32),
        compiler_params=_params(("arbitrary",)),
        name="dispatch",
    )(ends, _tile_major_ranks(dest, tt), hf_tm)


def _expert_kernel(first_ref, count_ref, x_ref, wg_ref, bg_ref, wu_ref, bu_ref, wd_ref, bd_ref,
                   o_ref, wg_scr, wu_scr, wd_scr, xbuf, ybuf, xsem, ysem, *, blk):
    e = pl.program_id(0)
    nblk = count_ref[e]
    first = first_ref[e]
    rows = blk * ROW_TILES

    def hbm_block(ref, j):
        return ref.at[pl.ds(pl.multiple_of((first + j) * rows, rows), rows), :]

    def x_copy(j, slot):
        return pltpu.make_async_copy(hbm_block(x_ref, j), xbuf.at[slot], xsem.at[slot])

    def y_copy(j, slot):
        return pltpu.make_async_copy(ybuf.at[slot], hbm_block(o_ref, j), ysem.at[slot])

    @pl.when(nblk > 0)
    def _():
        x_copy(0, 0).start()
        wg_scr[...] = wg_ref[0].astype(BF16)
        wu_scr[...] = wu_ref[0].astype(BF16)
        wd_scr[...] = wd_ref[0].astype(BF16)

    def block(j, carry):
        slot = j % 2
        x_copy(j, slot).wait()

        @pl.when(j + 1 < nblk)
        def _():
            x_copy(j + 1, 1 - slot).start()

        @pl.when(j >= 2)
        def _():
            y_copy(j - 2, slot).wait()

        x = _load_token_major(xbuf.at[slot], 0, blk).astype(BF16)
        g = jnp.dot(x, wg_scr[...], preferred_element_type=F32) + bg_ref[0]
        u = jnp.dot(x, wu_scr[...], preferred_element_type=F32) + bu_ref[0]
        g = jnp.minimum(g, SWIGLU_LIMIT)
        u = jnp.clip(u, -SWIGLU_LIMIT, SWIGLU_LIMIT)
        act = (u + 1.0) * (g * jax.nn.sigmoid(SWIGLU_ALPHA * g))
        y = jnp.dot(act.astype(BF16), wd_scr[...], preferred_element_type=F32) + bd_ref[0]
        _store_token_major(ybuf.at[slot], y)
        y_copy(j, slot).start()
        return carry

    lax.fori_loop(0, nblk, block, 0)

    @pl.when(nblk >= 2)
    def _():
        y_copy(nblk - 2, nblk % 2).wait()

    @pl.when(nblk >= 1)
    def _():
        y_copy(nblk - 1, (nblk - 1) % 2).wait()


def _experts(x_slots, blk, first_block, block_count, wg, bg, wu, bu, wd, bd):
    D = D_MODEL
    rows = blk * ROW_TILES
    wspec = pl.BlockSpec((1, D, D_FF), lambda e, fb, bc: (e, 0, 0))
    bspec = pl.BlockSpec((1, 1, D_FF), lambda e, fb, bc: (e, 0, 0))
    anyspec = pl.BlockSpec(memory_space=pl.ANY)
    grid_spec = pltpu.PrefetchScalarGridSpec(
        num_scalar_prefetch=2,
        grid=(N_EXPERTS,),
        in_specs=[anyspec, wspec, bspec, wspec, bspec, wspec, bspec],
        out_specs=anyspec,
        scratch_shapes=[pltpu.VMEM((D, D_FF), BF16), pltpu.VMEM((D, D_FF), BF16),
                        pltpu.VMEM((D_FF, D), BF16),
                        pltpu.VMEM((2, rows, LANES), F32), pltpu.VMEM((2, rows, LANES), F32),
                        pltpu.SemaphoreType.DMA((2,)), pltpu.SemaphoreType.DMA((2,))],
    )
    b3 = lambda b: b.reshape(N_EXPERTS, 1, -1)
    return pl.pallas_call(
        functools.partial(_expert_kernel, blk=blk),
        grid_spec=grid_spec,
        out_shape=jax.ShapeDtypeStruct(x_slots.shape, F32),
        compiler_params=_params(("arbitrary",)),
        name="experts",
    )(first_block, block_count, x_slots, wg, b3(bg), wu, b3(bu), wd, b3(bd))


def _combine_kernel(dest_ref, dest_next_ref, ys_ref, w_ref, x2_ref, g_ref, o_ref, buf, sems, *, tt):
    i = pl.program_id(0)
    n = pl.num_programs(0)
    cur = i % 2

    def row_copy(idx_ref, slot, t, k):
        return pltpu.make_async_copy(_row_tile(ys_ref, idx_ref[k * tt + t]),
                                     _row_tile(buf, (slot * TOP_K + k) * tt + t), sems.at[slot])

    def gather(idx_ref, slot):
        _for_each_token_k(tt, lambda t, k: row_copy(idx_ref, slot, t, k).start(priority=k % 2))

    @pl.when(i == 0)
    def _():
        gather(dest_ref, 0)

    @pl.when(i + 1 < n)
    def _():
        gather(dest_next_ref, 1 - cur)

    _for_each_token_k(tt, lambda t, k: row_copy(dest_ref, cur, t, k).wait())
    w = w_ref[...]
    y = x2_ref[...]
    for k in range(TOP_K):
        y = y + w[:, k:k + 1] * _load_token_major(buf, (cur * TOP_K + k) * tt, tt)
    inv = lax.rsqrt(jnp.mean(y * y, axis=-1, keepdims=True) + EPS)
    o_ref[...] = y * inv * g_ref[...]


def _combine(dest, y_slots, top_w, x2, g_final):
    T, D = x2.shape
    tt = min(256, T)
    nt = T // tt
    kern = functools.partial(_combine_kernel, tt=tt)
    return pl.pallas_call(
        kern,
        grid=(nt,),
        in_specs=[pl.BlockSpec((TOP_K * tt,), lambda i: (i,), memory_space=pltpu.SMEM),
                  pl.BlockSpec((TOP_K * tt,), lambda i: (jnp.minimum(i + 1, nt - 1),),
                               memory_space=pltpu.SMEM),
                  pl.BlockSpec(memory_space=pl.ANY),
                  pl.BlockSpec((tt, SUBLANES), lambda i: (i, 0)),
                  pl.BlockSpec((tt, D), lambda i: (i, 0)),
                  pl.BlockSpec((1, D), lambda i: (0, 0))],
        out_specs=pl.BlockSpec((tt, D), lambda i: (i, 0)),
        out_shape=jax.ShapeDtypeStruct((T, D), F32),
        scratch_shapes=[pltpu.VMEM((2 * TOP_K * tt * ROW_TILES, LANES), F32),
                        pltpu.SemaphoreType.DMA((2,))],
        compiler_params=_params(("arbitrary",)),
        name="combine",
    )(_tile_major_ranks(dest, tt), _tile_major_ranks(dest, tt), y_slots, top_w.T, x2,
      g_final.reshape(1, D))


def _moe_block_size(T):
    return min(512, max(SUBLANES, T * TOP_K // N_EXPERTS))


def _block_tables(counts, blk):
    cnt = counts.astype(I32)
    padded = (cnt + blk - 1) // blk * blk
    ends = jnp.cumsum(padded).astype(I32)
    return ends, (ends - padded) // blk, padded // blk


def kernel(x, norm_mix_g, w_in, conv_w, conv_b, w_rg_a, b_rg_a, w_rg_x, b_rg_x, lru_lambda, diff_lambda_q1, diff_lambda_k1, diff_lambda_q2, diff_lambda_k2, subln_g, rel_bias_table, w_proj_rnn, w_proj_att, w_out, norm_ffn_g, w_router, b_router, w_gate_e, b_gate_e, w_up_e, b_up_e, w_down_e, b_down_e, norm_final_g):
    B, S, D = x.shape
    T = B * S
    assert norm_mix_g.shape[0] == 1, "single-layer block: the final norm is fused into the MoE combine"
    l = 0
    xt = x.reshape(T, D)
    lam_init = 0.8 - 0.6 * math.exp(-0.3 * l)
    proj = _inproj(xt, norm_mix_g[l], w_in[l].astype(BF16))
    y_rnn = _rglru(proj, B, S, conv_w[l], conv_b[l], w_rg_a[l], b_rg_a[l], w_rg_x[l], b_rg_x[l],
                   lru_lambda[l])
    y_att = _attention(proj, B, S, diff_lambda_q1[l], diff_lambda_k1[l], diff_lambda_q2[l],
                       diff_lambda_k2[l], subln_g[l], rel_bias_table, lam_init)
    x2, hf, top_idx, top_w = _merge_router(xt, y_rnn, y_att, proj, w_proj_rnn[l], w_proj_att[l],
                                           w_out[l], norm_ffn_g[l], w_router[l], b_router[l])
    blk = _moe_block_size(T)
    nb = T * TOP_K // blk + N_EXPERTS
    dest, counts = _route(top_idx, blk)
    ends, first_block, block_count = _block_tables(counts[:, 0], blk)
    x_slots = _dispatch(ends, dest, hf, nb * blk, blk)
    y_slots = _experts(x_slots, blk, first_block, block_count, w_gate_e[l], b_gate_e[l], w_up_e[l],
                       b_up_e[l], w_down_e[l], b_down_e[l])
    out = _combine(dest, y_slots, top_w, x2, norm_final_g)
    return out.reshape(B, S, D)
```

```python
import functools
import math

import jax
import jax.numpy as jnp
from jax import lax
from jax.experimental import pallas as pl
from jax.experimental.pallas import tpu as pltpu

F32 = jnp.float32
BF16 = jnp.bfloat16
I32 = jnp.int32

D_MODEL = 1024
D_RNN = 1024
RNN_BLOCKS = 16
RNN_BLOCK = D_RNN // RNN_BLOCKS
CONV_W = 4
RGLRU_C = 8.0
N_HEADS = 8
HEAD_DIM = 64
V_DIM = 2 * HEAD_DIM
ATT_QK = N_HEADS * 2 * HEAD_DIM
ATT_V = N_HEADS * V_DIM
D_IN = 2 * D_RNN + 2 * ATT_QK + ATT_V + 2 * D_MODEL
NUM_BUCKETS = 32
MAX_EXACT = NUM_BUCKETS // 2
MAX_DISTANCE = 128
N_EXPERTS = 32
TOP_K = 4
D_FF = D_MODEL
SWIGLU_LIMIT = 7.0
SWIGLU_ALPHA = 1.702
EPS = 1e-6
NEG_INF = -1e30

LANES = 128
SUBLANES = 8
MXU_DIM = 256
VMEM_LIMIT = 56 * 1024 * 1024

COL_XR, COL_GR, COL_Q, COL_K, COL_V, COL_G0, COL_G1 = range(7)


def _params(sem, vmem=VMEM_LIMIT):
    return pltpu.CompilerParams(dimension_semantics=sem, vmem_limit_bytes=vmem)


ROW_TILES = D_MODEL // LANES
assert ROW_TILES == SUBLANES


def _store_token_major(ref, val, start=0):
    n = val.shape[0]
    for c in range(ROW_TILES):
        ref[pl.ds(start * ROW_TILES + c, n, stride=ROW_TILES), :] = val[:, c * LANES:(c + 1) * LANES]


def _load_token_major(ref, start, n):
    return jnp.concatenate(
        [ref[pl.ds(start * ROW_TILES + c, n, stride=ROW_TILES), :] for c in range(ROW_TILES)], axis=-1)


def _inproj_kernel(x_ref, g_ref, w_ref, o_ref, h_scr):
    @pl.when(pl.program_id(1) == 0)
    def _():
        x = x_ref[...]
        inv = lax.rsqrt(jnp.mean(x * x, axis=-1, keepdims=True) + EPS)
        h_scr[...] = (x * inv * g_ref[...]).astype(BF16)

    o_ref[...] = jnp.dot(h_scr[...], w_ref[...], preferred_element_type=F32).astype(o_ref.dtype)


def _inproj(x2d, g, w_bf16):
    T, D = x2d.shape
    N = w_bf16.shape[1]
    tm = min(1024, T)
    tn = 3584 if N % 3584 == 0 else 1024
    return pl.pallas_call(
        _inproj_kernel,
        grid=(T // tm, N // tn),
        in_specs=[
            pl.BlockSpec((tm, D), lambda i, j: (i, 0)),
            pl.BlockSpec((1, D), lambda i, j: (0, 0)),
            pl.BlockSpec((D, tn), lambda i, j: (0, j)),
        ],
        out_specs=pl.BlockSpec((tm, tn), lambda i, j: (i, j)),
        out_shape=jax.ShapeDtypeStruct((T, N), BF16),
        scratch_shapes=[pltpu.VMEM((tm, D), BF16)],
        compiler_params=_params(("parallel", "arbitrary")),
        name="inproj",
    )(x2d, g.reshape(1, D), w_bf16)


def _rglru_kernel(xr_ref, gr_ref, cw_ref, cb_ref, wa_ref, ba_ref, wx_ref, bx_ref, lam_ref,
                  o_ref, xbuf, tail_scr, a_scr, u_scr, h_scr):
    nb, ts, _ = xr_ref.shape
    pad = SUBLANES
    nslab = D_RNN // LANES

    @pl.when(pl.program_id(0) == 0)
    def _():
        tail_scr[...] = jnp.zeros_like(tail_scr)
        h_scr[...] = jnp.zeros_like(h_scr)

    cw = cw_ref[...]
    z = -lam_ref[...]
    softplus = jnp.maximum(z, 0.0) + jnp.log1p(jnp.exp(-jnp.abs(z)))
    sigmoid = lambda v: 0.5 * jnp.tanh(0.5 * v) + 0.5
    nchunk = D_RNN // MXU_DIM
    for b in range(nb):
        x = xr_ref[b].astype(F32)
        xbuf[0:pad, :] = tail_scr[b]
        xbuf[pad:pad + ts, :] = x
        xc = (cw[3:4, :] * x
              + cw[2:3, :] * xbuf[pad - 1:pad - 1 + ts, :]
              + cw[1:2, :] * xbuf[pad - 2:pad - 2 + ts, :]
              + cw[0:1, :] * xbuf[pad - 3:pad - 3 + ts, :]) + cb_ref[...]
        tail_scr[b] = x[ts - pad:ts, :]
        xcb = xc.astype(BF16)
        r_pre = jnp.concatenate(
            [jnp.dot(xcb[:, c * MXU_DIM:(c + 1) * MXU_DIM], wa_ref[c], preferred_element_type=F32)
             for c in range(nchunk)], axis=-1)
        i_pre = jnp.concatenate(
            [jnp.dot(xcb[:, c * MXU_DIM:(c + 1) * MXU_DIM], wx_ref[c], preferred_element_type=F32)
             for c in range(nchunk)], axis=-1)
        r = sigmoid(r_pre + ba_ref[...])
        ig = sigmoid(i_pre + bx_ref[...])
        a = jnp.exp((-RGLRU_C) * r * softplus)
        u = jnp.sqrt(1.0 - a * a) * (ig * xc)
        for c in range(nslab):
            a_scr[c, pl.ds(b, ts, stride=nb), :] = a[:, c * LANES:(c + 1) * LANES]
            u_scr[c, pl.ds(b, ts, stride=nb), :] = u[:, c * LANES:(c + 1) * LANES]

    def body(g, hs):
        for j in range(SCAN_UNROLL):
            off = pl.multiple_of((g * SCAN_UNROLL + j) * nb, nb)
            new = []
            for c in range(nslab):
                h = a_scr[c, pl.ds(off, nb), :] * hs[c] + u_scr[c, pl.ds(off, nb), :]
                u_scr[c, pl.ds(off, nb), :] = h
                new.append(h)
            hs = tuple(new)
        return hs

    hs = lax.fori_loop(0, ts // SCAN_UNROLL, body, tuple(h_scr[c] for c in range(nslab)))
    for c in range(nslab):
        h_scr[c] = hs[c]
    for b in range(nb):
        h = jnp.concatenate([u_scr[c, pl.ds(b, ts, stride=nb), :] for c in range(nslab)], axis=-1)
        gate = jax.nn.gelu(gr_ref[b].astype(F32), approximate=True)
        o_ref[b] = (h * gate).astype(o_ref.dtype)


SCAN_UNROLL = 8


def _block_diag_chunks(w):
    per = MXU_DIM // RNN_BLOCK
    w = w.reshape(D_RNN // MXU_DIM, per, RNN_BLOCK, RNN_BLOCK)
    eye = jnp.eye(per, dtype=w.dtype)
    out = jnp.einsum('gpcd,pq->gpcqd', w, eye)
    return out.reshape(D_RNN // MXU_DIM, MXU_DIM, MXU_DIM)


def _rglru(proj, B, S, conv_w, conv_b, w_a, b_a, w_x, b_x, lru_lambda):
    assert B <= SUBLANES, "all batch rows of a time step share one vreg in the scan"
    ts = min(256, S)
    ns = S // ts
    wa = _block_diag_chunks(w_a).astype(BF16)
    wx = _block_diag_chunks(w_x).astype(BF16)
    nchunk = D_RNN // MXU_DIM
    nslab = D_RNN // LANES
    row = lambda v: v.reshape(1, D_RNN)
    const2 = lambda s: (0, 0)
    proj3 = proj.reshape(B, S, proj.shape[-1])
    out = pl.pallas_call(
        _rglru_kernel,
        grid=(ns,),
        in_specs=[
            pl.BlockSpec((B, ts, D_RNN), lambda s: (0, s, COL_XR)),
            pl.BlockSpec((B, ts, D_RNN), lambda s: (0, s, COL_GR)),
            pl.BlockSpec((CONV_W, D_RNN), const2),
            pl.BlockSpec((1, D_RNN), const2),
            pl.BlockSpec((nchunk, MXU_DIM, MXU_DIM), lambda s: (0, 0, 0)),
            pl.BlockSpec((1, D_RNN), const2),
            pl.BlockSpec((nchunk, MXU_DIM, MXU_DIM), lambda s: (0, 0, 0)),
            pl.BlockSpec((1, D_RNN), const2),
            pl.BlockSpec((1, D_RNN), const2),
        ],
        out_specs=pl.BlockSpec((B, ts, D_RNN), lambda s: (0, s, 0)),
        out_shape=jax.ShapeDtypeStruct((B, S, D_RNN), BF16),
        scratch_shapes=[
            pltpu.VMEM((ts + SUBLANES, D_RNN), F32),
            pltpu.VMEM((B, SUBLANES, D_RNN), F32),
            pltpu.VMEM((nslab, ts * B, LANES), F32),
            pltpu.VMEM((nslab, ts * B, LANES), F32),
            pltpu.VMEM((nslab, B, LANES), F32),
        ],
        compiler_params=_params(("arbitrary",)),
        name="rglru",
    )(proj3, proj3, conv_w, row(conv_b), wa, row(b_a), wx, row(b_x), row(lru_lambda))
    return out.reshape(B * S, D_RNN)


def _attn_kernel(q_ref, k_ref, v_ref, bias_ref, lq1_ref, lk1_ref, lq2_ref, lk2_ref, sg_ref,
                 o_ref, qs_scr, vx_scr, s_scr, m_scr, acc_scr, *, tq, lam_init):
    S = q_ref.shape[0]
    nq = S // tq
    scale = HEAD_DIM ** -0.5 * LOG2E
    lane = lax.broadcasted_iota(I32, (tq, V_DIM), 1)
    lam = (jnp.exp(jnp.sum(lq1_ref[...] * lk1_ref[...], keepdims=True))
           - jnp.exp(jnp.sum(lq2_ref[...] * lk2_ref[...], keepdims=True)) + lam_init)
    vx_scr[:, 0:V_DIM] = v_ref[...]
    vx_scr[:, V_DIM:2 * V_DIM] = jnp.ones((S, V_DIM), BF16)

    Z0 = 2

    def prep_q(qi, buf):
        q = (q_ref[pl.ds(pl.multiple_of(qi * tq, tq), tq), :].astype(F32) * scale).astype(BF16)
        zero = jnp.zeros_like(q)
        qs_scr[buf, 0:tq, :] = jnp.where(lane < HEAD_DIM, q, zero)
        qs_scr[buf, tq:2 * tq, :] = jnp.where(lane >= HEAD_DIM, q, zero)

    def scores(buf, j, dst):
        k = k_ref[pl.ds(pl.multiple_of(j * tq, tq), tq), :]
        s_scr[dst] = lax.dot_general(qs_scr[buf], k, (((1,), (1,)), ((), ())),
                                     preferred_element_type=F32)

    def step(j, src, bias_idx, prefetch):
        scores(*prefetch)
        s = s_scr[src]
        if bias_idx is not None:
            b = bias_ref[bias_idx]
            s = s + jnp.concatenate([b, b], axis=0)
        vx = vx_scr[pl.ds(pl.multiple_of(j * tq, tq), tq), :]
        m_prev = m_scr[...]
        m_new = jnp.maximum(m_prev, jnp.max(s, axis=1)[:, None])
        p = jnp.exp2(s - jnp.tile(m_new, (1, tq // LANES)))
        alpha = jnp.exp2(m_prev - m_new)
        acc_scr[...] = (jnp.tile(alpha, (1, 2)) * acc_scr[...]
                        + jnp.dot(p.astype(BF16), vx, preferred_element_type=F32))
        m_scr[...] = m_new

    def begin_block():
        m_scr[...] = jnp.full(m_scr.shape, NEG_INF, F32)
        acc_scr[...] = jnp.zeros(acc_scr.shape, F32)

    def diag_step(qi, par, src):
        nxt = jnp.minimum(qi + 1, nq - 1)
        prep_q(nxt, 1 - par)
        step(qi, src, 0, (1 - par, 0, Z0 + 1 - par))

    def end_block(qi):
        acc = acc_scr[...]
        o_all = acc[:, 0:V_DIM] / acc[:, V_DIM:2 * V_DIM]
        o = o_all[0:tq, :] - lam * o_all[tq:2 * tq, :]
        inv = lax.rsqrt(jnp.mean(o * o, axis=-1, keepdims=True) + EPS)
        y = (o * inv * sg_ref[...]) * (1.0 - lam_init)
        o_ref[pl.ds(pl.multiple_of(qi * tq, tq), tq), :] = y.astype(o_ref.dtype)

    def far_pairs(par, npairs):
        def body(m, c):
            step(2 * m + 1, 0, None, (par, 2 * m + 2, 1))
            step(2 * m + 2, 1, None, (par, 2 * m + 3, 0))
            return c

        lax.fori_loop(0, npairs, body, 0)

    prep_q(0, 0)
    scores(0, 0, Z0)
    begin_block()
    diag_step(0, 0, Z0)
    end_block(0)
    begin_block()
    step(0, Z0 + 1, 1, (1, 1, 0))
    diag_step(1, 1, 0)
    end_block(1)

    def block_pair(i, carry):
        qi = 2 * i
        begin_block()
        step(0, Z0, None, (0, 1, 0))
        far_pairs(0, i - 1)
        step(qi - 1, 0, 1, (0, qi, 1))
        diag_step(qi, 0, 1)
        end_block(qi)
        qi = 2 * i + 1
        begin_block()
        step(0, Z0 + 1, None, (1, 1, 0))
        far_pairs(1, i - 1)
        step(qi - 2, 0, None, (1, qi - 1, 1))
        step(qi - 1, 1, 1, (1, qi, 0))
        diag_step(qi, 1, 0)
        end_block(qi)
        return carry

    lax.fori_loop(1, nq // 2, block_pair, 0)


def _rel_bucket(n):
    n = jnp.maximum(n, 0)
    nf = jnp.maximum(n, MAX_EXACT).astype(F32)
    large = MAX_EXACT + (jnp.log(nf / MAX_EXACT) / math.log(MAX_DISTANCE / MAX_EXACT)
                         * (NUM_BUCKETS - MAX_EXACT)).astype(I32)
    large = jnp.minimum(large, NUM_BUCKETS - 1)
    return jnp.where(n < MAX_EXACT, n, large)


def _bias_tiles(rel_table, tq):
    i = jnp.arange(tq, dtype=I32)[:, None]
    j = jnp.arange(tq, dtype=I32)[None, :]
    table = rel_table.astype(F32) - rel_table[NUM_BUCKETS - 1].astype(F32)[None, :]
    tiles = []
    for delta in (0, tq):
        n = i - j + delta
        onehot = (_rel_bucket(n)[:, :, None] == jnp.arange(NUM_BUCKETS, dtype=I32)).astype(F32)
        b = jnp.einsum('ijb,bh->hij', onehot, table, precision=lax.Precision.HIGHEST) * LOG2E
        tiles.append(jnp.where((n >= 0)[None], b, NEG_INF))
    return jnp.stack(tiles, axis=1)


ATT_TQ = 512
LOG2E = math.log2(math.e)


def _attention(proj, B, S, lq1, lk1, lq2, lk2, subln_g, rel_table, lam_init):
    T = B * S
    tq = min(ATT_TQ, S // 2)
    assert V_DIM == LANES and MAX_DISTANCE <= tq and S % (2 * tq) == 0
    bias = _bias_tiles(rel_table, tq)
    vec = lambda v: v.reshape(1, -1).astype(F32)
    const2 = lambda b, h: (0, 0)
    kern = functools.partial(_attn_kernel, tq=tq, lam_init=lam_init)
    return pl.pallas_call(
        kern,
        grid=(B, N_HEADS),
        in_specs=[
            pl.BlockSpec((S, V_DIM), lambda b, h: (b, COL_Q * N_HEADS + h)),
            pl.BlockSpec((S, V_DIM), lambda b, h: (b, COL_K * N_HEADS + h)),
            pl.BlockSpec((S, V_DIM), lambda b, h: (b, COL_V * N_HEADS + h)),
            pl.BlockSpec((None, 2, tq, tq), lambda b, h: (h, 0, 0, 0)),
            pl.BlockSpec((1, HEAD_DIM), const2),
            pl.BlockSpec((1, HEAD_DIM), const2),
            pl.BlockSpec((1, HEAD_DIM), const2),
            pl.BlockSpec((1, HEAD_DIM), const2),
            pl.BlockSpec((1, V_DIM), const2),
        ],
        out_specs=pl.BlockSpec((S, V_DIM), lambda b, h: (b, h)),
        out_shape=jax.ShapeDtypeStruct((T, ATT_V), BF16),
        scratch_shapes=[
            pltpu.VMEM((2, 2 * tq, V_DIM), BF16),
            pltpu.VMEM((S, 2 * V_DIM), BF16),
            pltpu.VMEM((4, 2 * tq, tq), F32),
            pltpu.VMEM((2 * tq, LANES), F32),
            pltpu.VMEM((2 * tq, 2 * V_DIM), F32),
        ],
        compiler_params=_params(("parallel", "arbitrary")),
        name="diff_attn",
    )(proj, proj, proj, bias, vec(lq1), vec(lk1), vec(lq2), vec(lk2), vec(subln_g))


def _merge_kernel(x_ref, yr_ref, ya_ref, g0_ref, g1_ref, wr_ref, wa_ref, wo_ref, gf_ref, wrt_ref, brt_ref,
                  x2_ref, hf_ref, idx_ref, wgt_ref):
    pr = jnp.dot(yr_ref[...], wr_ref[...], preferred_element_type=F32)
    pa = jnp.dot(ya_ref[...], wa_ref[...], preferred_element_type=F32)
    merged = (jax.nn.sigmoid(g0_ref[...].astype(F32)) * pr
              + jax.nn.sigmoid(g1_ref[...].astype(F32)) * pa)
    x2 = x_ref[...] + jnp.dot(merged.astype(BF16), wo_ref[...], preferred_element_type=F32)
    x2_ref[...] = x2
    inv = lax.rsqrt(jnp.mean(x2 * x2, axis=-1, keepdims=True) + EPS)
    hf = x2 * inv * gf_ref[...]
    _store_token_major(hf_ref, hf)
    nt = (((1,), (1,)), ((), ()))
    hi = hf.astype(BF16)
    lo = (hf - hi.astype(F32)).astype(BF16)
    hw = lax.dot_general(wrt_ref[...], hi, nt, preferred_element_type=F32)
    lw = lax.dot_general(wrt_ref[0:LANES, :], lo, nt, preferred_element_type=F32)
    logits = (hw[0:N_EXPERTS, :] + hw[LANES:LANES + N_EXPERTS, :] + lw[0:N_EXPERTS, :]
              + brt_ref[...])
    tm = logits.shape[1]
    row = lax.broadcasted_iota(I32, (N_EXPERTS, tm), 0)
    work = logits
    idx_rows, val_rows = [], []
    for k in range(TOP_K):
        mx = jnp.max(work, axis=0, keepdims=True)
        sel = jnp.min(jnp.where(work == mx, row, N_EXPERTS), axis=0, keepdims=True)
        idx_rows.append(sel)
        val_rows.append(mx)
        work = jnp.where(row == sel, -jnp.inf, work)
    es = [jnp.exp(v - val_rows[0]) for v in val_rows]
    inv_sum = 1.0 / functools.reduce(lambda a, b: a + b, es)
    pad_i = [jnp.zeros((1, tm), I32)] * (SUBLANES - TOP_K)
    pad_f = [jnp.zeros((1, tm), F32)] * (SUBLANES - TOP_K)
    idx_ref[...] = jnp.concatenate(idx_rows + pad_i, axis=0)
    wgt_ref[...] = jnp.concatenate([e * inv_sum for e in es] + pad_f, axis=0)


def _merge_router(x2d, y_rnn, y_att, proj, w_pr, w_pa, w_o, g_ffn, w_router, b_router):
    T, D = x2d.shape
    tm = min(512, T)
    w_hi = w_router.astype(BF16)
    w_lo = (w_router - w_hi.astype(F32)).astype(BF16)
    wrt = (jnp.zeros((2 * LANES, D), BF16).at[:N_EXPERTS, :].set(w_hi.T)
           .at[LANES:LANES + N_EXPERTS, :].set(w_lo.T))
    brt = b_router.reshape(N_EXPERTS, 1).astype(F32)
    rowblk = lambda c: pl.BlockSpec((tm, D), lambda i, c=c: (i, c))
    full = lambda a: pl.BlockSpec(a.shape, lambda i: (0,) * a.ndim)
    wr, wa, wo = w_pr.astype(BF16), w_pa.astype(BF16), w_o.astype(BF16)
    gf = g_ffn.reshape(1, D)
    return pl.pallas_call(
        _merge_kernel,
        grid=(T // tm,),
        in_specs=[rowblk(0), rowblk(0), rowblk(0), rowblk(COL_G0), rowblk(COL_G1),
                  full(wr), full(wa), full(wo), full(gf), full(wrt), full(brt)],
        out_specs=[rowblk(0), pl.BlockSpec((tm * ROW_TILES, LANES), lambda i: (i, 0)),
                   pl.BlockSpec((SUBLANES, tm), lambda i: (0, i)),
                   pl.BlockSpec((SUBLANES, tm), lambda i: (0, i))],
        out_shape=[jax.ShapeDtypeStruct((T, D), F32), jax.ShapeDtypeStruct((T * ROW_TILES, LANES), F32),
                   jax.ShapeDtypeStruct((SUBLANES, T), I32), jax.ShapeDtypeStruct((SUBLANES, T), F32)],
        compiler_params=_params(("parallel",)),
        name="merge_router",
    )(x2d, y_rnn, y_att, proj, proj, wr, wa, wo, gf, wrt, brt)


def _route_kernel(idx_ref, dest_ref, cnt_ref, cnt_scr, run_scr, start_scr, *, blk):
    ph = pl.program_id(0)
    i = pl.program_id(1)
    tt = idx_ref.shape[1]
    idx = idx_ref[...]
    row = lax.broadcasted_iota(I32, (N_EXPERTS, tt), 0)
    onehot = jnp.zeros((N_EXPERTS, tt), F32)
    for k in range(TOP_K):
        onehot = onehot + (idx[k:k + 1, :] == row).astype(F32)
    tile_cnt = jnp.sum(onehot, axis=1, keepdims=True)

    @pl.when((ph == 0) & (i == 0))
    def _():
        cnt_scr[...] = jnp.zeros_like(cnt_scr)

    @pl.when(ph == 0)
    def _():
        cnt_scr[...] += tile_cnt

    @pl.when((ph == 1) & (i == 0))
    def _():
        padded = jnp.floor((cnt_scr[...] + (blk - 1)) / blk) * blk
        r = lax.broadcasted_iota(I32, (N_EXPERTS, N_EXPERTS), 0)
        c = lax.broadcasted_iota(I32, (N_EXPERTS, N_EXPERTS), 1)
        start_scr[...] = jnp.dot((c < r).astype(F32), padded, preferred_element_type=F32,
                                 precision=lax.Precision.HIGHEST)
        run_scr[...] = jnp.zeros_like(run_scr)

    @pl.when(ph == 1)
    def _():
        r = lax.broadcasted_iota(I32, (tt, tt), 0)
        c = lax.broadcasted_iota(I32, (tt, tt), 1)
        earlier = (r < c).astype(BF16)
        before = jnp.dot(onehot.astype(BF16), earlier, preferred_element_type=F32)
        base = before + run_scr[:, 0:1] + start_scr[:, 0:1]
        rows = [jnp.sum(jnp.where(idx[k:k + 1, :] == row, base, 0.0), axis=0, keepdims=True)
                for k in range(TOP_K)]
        rows += [jnp.zeros((1, tt), F32)] * (SUBLANES - TOP_K)
        dest_ref[...] = jnp.concatenate(rows, axis=0).astype(I32)
        run_scr[...] += tile_cnt
        cnt_ref[...] = cnt_scr[...]


def _route(top_idx, blk):
    T = top_idx.shape[1]
    tt = min(512, T)
    kern = functools.partial(_route_kernel, blk=blk)
    return pl.pallas_call(
        kern,
        grid=(2, T // tt),
        in_specs=[pl.BlockSpec((SUBLANES, tt), lambda p, i: (0, i))],
        out_specs=[pl.BlockSpec((SUBLANES, tt), lambda p, i: (0, p * i)),
                   pl.BlockSpec((N_EXPERTS, LANES), lambda p, i: (0, 0))],
        out_shape=[jax.ShapeDtypeStruct((SUBLANES, T), I32),
                   jax.ShapeDtypeStruct((N_EXPERTS, LANES), F32)],
        scratch_shapes=[pltpu.VMEM((N_EXPERTS, LANES), F32), pltpu.VMEM((N_EXPERTS, LANES), F32),
                        pltpu.VMEM((N_EXPERTS, LANES), F32)],
        compiler_params=_params(("arbitrary", "arbitrary")),
        name="route",
    )(top_idx)


TOKEN_UNROLL = 2


def _for_each_token_k(n_tok, fn):
    def body(g, c):
        for u in range(TOKEN_UNROLL):
            for k in range(TOP_K):
                fn(g * TOKEN_UNROLL + u, k)
        return c

    lax.fori_loop(0, n_tok // TOKEN_UNROLL, body, 0)


def _tile_major_ranks(dest, tt):
    T = dest.shape[1]
    return dest[:TOP_K].reshape(TOP_K, T // tt, tt).transpose(1, 0, 2).reshape(-1)


def _row_tile(ref, row):
    return ref.at[pl.ds(pl.multiple_of(row * ROW_TILES, ROW_TILES), ROW_TILES), :]


def _dispatch_kernel(ends_ref, dest_ref, hf_ref, xs_ref, zero_buf, sem, zsem, *, tt, blk):
    @pl.when(pl.program_id(0) == 0)
    def _():
        zero_buf[...] = jnp.zeros_like(zero_buf)

        def tail_fill(e):
            prev_end = ends_ref[e - 1] if e > 0 else 0
            start = pl.multiple_of((ends_ref[e] - blk) * ROW_TILES, ROW_TILES)
            return ends_ref[e] > prev_end, pltpu.make_async_copy(
                zero_buf, xs_ref.at[pl.ds(start, blk * ROW_TILES), :], zsem)

        for e in range(N_EXPERTS):
            nonempty, cp = tail_fill(e)
            pl.when(nonempty)(cp.start)
        for e in range(N_EXPERTS):
            nonempty, cp = tail_fill(e)
            pl.when(nonempty)(cp.wait)

    def row_copy(t, k):
        return pltpu.make_async_copy(_row_tile(hf_ref, t),
                                     _row_tile(xs_ref, dest_ref[k * tt + t]), sem)

    _for_each_token_k(tt, lambda t, k: row_copy(t, k).start(priority=k % 2))
    _for_each_token_k(tt, lambda t, k: row_copy(t, k).wait())


def _dispatch(ends, dest, hf_tm, n_slots, blk):
    T = hf_tm.shape[0] // ROW_TILES
    tt = min(512, T)
    kern = functools.partial(_dispatch_kernel, tt=tt, blk=blk)
    grid_spec = pltpu.PrefetchScalarGridSpec(
        num_scalar_prefetch=1,
        grid=(T // tt,),
        in_specs=[pl.BlockSpec((TOP_K * tt,), lambda i, ends: (i,), memory_space=pltpu.SMEM),
                  pl.BlockSpec((tt * ROW_TILES, LANES), lambda i, ends: (i, 0))],
        out_specs=pl.BlockSpec(memory_space=pl.ANY),
        scratch_shapes=[pltpu.VMEM((blk * ROW_TILES, LANES), F32),
                        pltpu.SemaphoreType.DMA(()), pltpu.SemaphoreType.DMA(())],
    )
    return pl.pallas_call(
        kern,
        grid_spec=grid_spec,
        out_shape=jax.ShapeDtypeStruct((n_slots * ROW_TILES, LANES), F32),
        compiler_params=_params(("arbitrary",)),
        name="dispatch",
    )(ends, _tile_major_ranks(dest, tt), hf_tm)


def _expert_kernel(first_ref, count_ref, x_ref, wg_ref, bg_ref, wu_ref, bu_ref, wd_ref, bd_ref,
                   o_ref, wg_scr, wu_scr, wd_scr, xbuf, ybuf, xsem, ysem, *, blk):
    e = pl.program_id(0)
    nblk = count_ref[e]
    first = first_ref[e]
    rows = blk * ROW_TILES

    def hbm_block(ref, j):
        return ref.at[pl.ds(pl.multiple_of((first + j) * rows, rows), rows), :]

    def x_copy(j, slot):
        return pltpu.make_async_copy(hbm_block(x_ref, j), xbuf.at[slot], xsem.at[slot])

    def y_copy(j, slot):
        return pltpu.make_async_copy(ybuf.at[slot], hbm_block(o_ref, j), ysem.at[slot])

    @pl.when(nblk > 0)
    def _():
        x_copy(0, 0).start()
        wg_scr[...] = wg_ref[0].astype(BF16)
        wu_scr[...] = wu_ref[0].astype(BF16)
        wd_scr[...] = wd_ref[0].astype(BF16)

    def block(j, slot):
        x_copy(j, slot).wait()

        @pl.when(j + 1 < nblk)
        def _():
            x_copy(j + 1, 1 - slot).start()

        @pl.when(j >= 2)
        def _():
            y_copy(j - 2, slot).wait()

        x = _load_token_major(xbuf.at[slot], 0, blk).astype(BF16)
        g = jnp.dot(x, wg_scr[...], preferred_element_type=F32) + bg_ref[0]
        u = jnp.dot(x, wu_scr[...], preferred_element_type=F32) + bu_ref[0]
        g = jnp.minimum(g, SWIGLU_LIMIT)
        u = jnp.clip(u, -SWIGLU_LIMIT, SWIGLU_LIMIT)
        act = (u + 1.0) * (g * jax.nn.sigmoid(SWIGLU_ALPHA * g))
        y = jnp.dot(act.astype(BF16), wd_scr[...], preferred_element_type=F32) + bd_ref[0]
        _store_token_major(ybuf.at[slot], y)
        y_copy(j, slot).start()

    def block_pair(p, carry):
        block(2 * p, 0)
        block(2 * p + 1, 1)
        return carry

    lax.fori_loop(0, nblk // 2, block_pair, 0)

    @pl.when(nblk % 2 == 1)
    def _():
        block(nblk - 1, 0)

    @pl.when(nblk >= 2)
    def _():
        y_copy(nblk - 2, nblk % 2).wait()

    @pl.when(nblk >= 1)
    def _():
        y_copy(nblk - 1, (nblk - 1) % 2).wait()


def _experts(x_slots, blk, first_block, block_count, wg, bg, wu, bu, wd, bd):
    D = D_MODEL
    rows = blk * ROW_TILES
    wspec = pl.BlockSpec((1, D, D_FF), lambda e, fb, bc: (e, 0, 0))
    bspec = pl.BlockSpec((1, 1, D_FF), lambda e, fb, bc: (e, 0, 0))
    anyspec = pl.BlockSpec(memory_space=pl.ANY)
    grid_spec = pltpu.PrefetchScalarGridSpec(
        num_scalar_prefetch=2,
        grid=(N_EXPERTS,),
        in_specs=[anyspec, wspec, bspec, wspec, bspec, wspec, bspec],
        out_specs=anyspec,
        scratch_shapes=[pltpu.VMEM((D, D_FF), BF16), pltpu.VMEM((D, D_FF), BF16),
                        pltpu.VMEM((D_FF, D), BF16),
                        pltpu.VMEM((2, rows, LANES), F32), pltpu.VMEM((2, rows, LANES), F32),
                        pltpu.SemaphoreType.DMA((2,)), pltpu.SemaphoreType.DMA((2,))],
    )
    b3 = lambda b: b.reshape(N_EXPERTS, 1, -1)
    return pl.pallas_call(
        functools.partial(_expert_kernel, blk=blk),
        grid_spec=grid_spec,
        out_shape=jax.ShapeDtypeStruct(x_slots.shape, F32),
        compiler_params=_params(("arbitrary",)),
        name="experts",
    )(first_block, block_count, x_slots, wg, b3(bg), wu, b3(bu), wd, b3(bd))


def _combine_kernel(dest_ref, dest_next_ref, ys_ref, w_ref, x2_ref, g_ref, o_ref, buf, sems, *, tt):
    i = pl.program_id(0)
    n = pl.num_programs(0)
    cur = i % 2

    def row_copy(idx_ref, slot, t, k):
        return pltpu.make_async_copy(_row_tile(ys_ref, idx_ref[k * tt + t]),
                                     _row_tile(buf, (slot * TOP_K + k) * tt + t), sems.at[slot])

    def gather(idx_ref, slot):
        _for_each_token_k(tt, lambda t, k: row_copy(idx_ref, slot, t, k).start(priority=k % 2))

    @pl.when(i == 0)
    def _():
        gather(dest_ref, 0)

    @pl.when(i + 1 < n)
    def _():
        gather(dest_next_ref, 1 - cur)

    _for_each_token_k(tt, lambda t, k: row_copy(dest_ref, cur, t, k).wait())
    w = w_ref[...]
    y = x2_ref[...]
    for k in range(TOP_K):
        y = y + w[:, k:k + 1] * _load_token_major(buf, (cur * TOP_K + k) * tt, tt)
    inv = lax.rsqrt(jnp.mean(y * y, axis=-1, keepdims=True) + EPS)
    o_ref[...] = y * inv * g_ref[...]


def _combine(dest, y_slots, top_w, x2, g_final):
    T, D = x2.shape
    tt = min(256, T)
    nt = T // tt
    kern = functools.partial(_combine_kernel, tt=tt)
    return pl.pallas_call(
        kern,
        grid=(nt,),
        in_specs=[pl.BlockSpec((TOP_K * tt,), lambda i: (i,), memory_space=pltpu.SMEM),
                  pl.BlockSpec((TOP_K * tt,), lambda i: (jnp.minimum(i + 1, nt - 1),),
                               memory_space=pltpu.SMEM),
                  pl.BlockSpec(memory_space=pl.ANY),
                  pl.BlockSpec((tt, SUBLANES), lambda i: (i, 0)),
                  pl.BlockSpec((tt, D), lambda i: (i, 0)),
                  pl.BlockSpec((1, D), lambda i: (0, 0))],
        out_specs=pl.BlockSpec((tt, D), lambda i: (i, 0)),
        out_shape=jax.ShapeDtypeStruct((T, D), F32),
        scratch_shapes=[pltpu.VMEM((2 * TOP_K * tt * ROW_TILES, LANES), F32),
                        pltpu.SemaphoreType.DMA((2,))],
        compiler_params=_params(("arbitrary",)),
        name="combine",
    )(_tile_major_ranks(dest, tt), _tile_major_ranks(dest, tt), y_slots, top_w.T, x2,
      g_final.reshape(1, D))


def _moe_block_size(T):
    return min(512, max(SUBLANES, T * TOP_K // N_EXPERTS))


def _block_tables(counts, blk):
    cnt = counts.astype(I32)
    padded = (cnt + blk - 1) // blk * blk
    ends = jnp.cumsum(padded).astype(I32)
    return ends, (ends - padded) // blk, padded // blk


def kernel(x, norm_mix_g, w_in, conv_w, conv_b, w_rg_a, b_rg_a, w_rg_x, b_rg_x, lru_lambda, diff_lambda_q1, diff_lambda_k1, diff_lambda_q2, diff_lambda_k2, subln_g, rel_bias_table, w_proj_rnn, w_proj_att, w_out, norm_ffn_g, w_router, b_router, w_gate_e, b_gate_e, w_up_e, b_up_e, w_down_e, b_down_e, norm_final_g):
    B, S, D = x.shape
    T = B * S
    assert norm_mix_g.shape[0] == 1, "single-layer block: the final norm is fused into the MoE combine"
    l = 0
    xt = x.reshape(T, D)
    lam_init = 0.8 - 0.6 * math.exp(-0.3 * l)
    proj = _inproj(xt, norm_mix_g[l], w_in[l].astype(BF16))
    y_rnn = _rglru(proj, B, S, conv_w[l], conv_b[l], w_rg_a[l], b_rg_a[l], w_rg_x[l], b_rg_x[l],
                   lru_lambda[l])
    y_att = _attention(proj, B, S, diff_lambda_q1[l], diff_lambda_k1[l], diff_lambda_q2[l],
                       diff_lambda_k2[l], subln_g[l], rel_bias_table, lam_init)
    x2, hf, top_idx, top_w = _merge_router(xt, y_rnn, y_att, proj, w_proj_rnn[l], w_proj_att[l],
                                           w_out[l], norm_ffn_g[l], w_router[l], b_router[l])
    blk = _moe_block_size(T)
    nb = T * TOP_K // blk + N_EXPERTS
    dest, counts = _route(top_idx, blk)
    ends, first_block, block_count = _block_tables(counts[:, 0], blk)
    x_slots = _dispatch(ends, dest, hf, nb * blk, blk)
    y_slots = _experts(x_slots, blk, first_block, block_count, w_gate_e[l], b_gate_e[l], w_up_e[l],
                       b_up_e[l], w_down_e[l], b_down_e[l])
    out = _combine(dest, y_slots, top_w, x2, norm_final_g)
    return out.reshape(B, S, D)
```

```python
import functools
import math

import jax
import jax.numpy as jnp
from jax import lax
from jax.experimental import pallas as pl
from jax.experimental.pallas import tpu as pltpu

F32 = jnp.float32
BF16 = jnp.bfloat16
I32 = jnp.int32

D_MODEL = 1024
D_RNN = 1024
RNN_BLOCKS = 16
RNN_BLOCK = D_RNN // RNN_BLOCKS
CONV_W = 4
RGLRU_C = 8.0
N_HEADS = 8
HEAD_DIM = 64
V_DIM = 2 * HEAD_DIM
ATT_QK = N_HEADS * 2 * HEAD_DIM
ATT_V = N_HEADS * V_DIM
D_IN = 2 * D_RNN + 2 * ATT_QK + ATT_V + 2 * D_MODEL
NUM_BUCKETS = 32
MAX_EXACT = NUM_BUCKETS // 2
MAX_DISTANCE = 128
N_EXPERTS = 32
TOP_K = 4
D_FF = D_MODEL
SWIGLU_LIMIT = 7.0
SWIGLU_ALPHA = 1.702
EPS = 1e-6
NEG_INF = -1e30

LANES = 128
SUBLANES = 8
MXU_DIM = 256
VMEM_LIMIT = 56 * 1024 * 1024

COL_XR, COL_GR, COL_Q, COL_K, COL_V, COL_G0, COL_G1 = range(7)


def _params(sem, vmem=VMEM_LIMIT):
    return pltpu.CompilerParams(dimension_semantics=sem, vmem_limit_bytes=vmem)


ROW_TILES = D_MODEL // LANES
assert ROW_TILES == SUBLANES


def _store_token_major(ref, val, start=0):
    n = val.shape[0]
    for c in range(ROW_TILES):
        ref[pl.ds(start * ROW_TILES + c, n, stride=ROW_TILES), :] = val[:, c * LANES:(c + 1) * LANES]


def _load_token_major(ref, start, n):
    return jnp.concatenate(
        [ref[pl.ds(start * ROW_TILES + c, n, stride=ROW_TILES), :] for c in range(ROW_TILES)], axis=-1)


def _inproj_kernel(x_ref, g_ref, w_ref, o_ref, h_scr):
    @pl.when(pl.program_id(1) == 0)
    def _():
        x = x_ref[...]
        inv = lax.rsqrt(jnp.mean(x * x, axis=-1, keepdims=True) + EPS)
        h_scr[...] = (x * inv * g_ref[...]).astype(BF16)

    o_ref[...] = jnp.dot(h_scr[...], w_ref[...], preferred_element_type=F32).astype(o_ref.dtype)


def _inproj(x2d, g, w_bf16):
    T, D = x2d.shape
    N = w_bf16.shape[1]
    tm = min(1024, T)
    tn = 3584 if N % 3584 == 0 else 1024
    return pl.pallas_call(
        _inproj_kernel,
        grid=(T // tm, N // tn),
        in_specs=[
            pl.BlockSpec((tm, D), lambda i, j: (i, 0)),
            pl.BlockSpec((1, D), lambda i, j: (0, 0)),
            pl.BlockSpec((D, tn), lambda i, j: (0, j)),
        ],
        out_specs=pl.BlockSpec((tm, tn), lambda i, j: (i, j)),
        out_shape=jax.ShapeDtypeStruct((T, N), BF16),
        scratch_shapes=[pltpu.VMEM((tm, D), BF16)],
        compiler_params=_params(("parallel", "arbitrary")),
        name="inproj",
    )(x2d, g.reshape(1, D), w_bf16)


def _rglru_kernel(xr_ref, gr_ref, cw_ref, cb_ref, wa_ref, ba_ref, wx_ref, bx_ref, lam_ref,
                  o_ref, xbuf, tail_scr, a_scr, u_scr, h_scr):
    nb, ts, _ = xr_ref.shape
    pad = SUBLANES
    nslab = D_RNN // LANES

    @pl.when(pl.program_id(0) == 0)
    def _():
        tail_scr[...] = jnp.zeros_like(tail_scr)
        h_scr[...] = jnp.zeros_like(h_scr)

    cw = cw_ref[...]
    z = -lam_ref[...]
    softplus = jnp.maximum(z, 0.0) + jnp.log1p(jnp.exp(-jnp.abs(z)))
    sigmoid = lambda v: 0.5 * jnp.tanh(0.5 * v) + 0.5
    nchunk = D_RNN // MXU_DIM
    for b in range(nb):
        x = xr_ref[b].astype(F32)
        xbuf[0:pad, :] = tail_scr[b]
        xbuf[pad:pad + ts, :] = x
        xc = (cw[3:4, :] * x
              + cw[2:3, :] * xbuf[pad - 1:pad - 1 + ts, :]
              + cw[1:2, :] * xbuf[pad - 2:pad - 2 + ts, :]
              + cw[0:1, :] * xbuf[pad - 3:pad - 3 + ts, :]) + cb_ref[...]
        tail_scr[b] = x[ts - pad:ts, :]
        xcb = xc.astype(BF16)
        r_pre = jnp.concatenate(
            [jnp.dot(xcb[:, c * MXU_DIM:(c + 1) * MXU_DIM], wa_ref[c], preferred_element_type=F32)
             for c in range(nchunk)], axis=-1)
        i_pre = jnp.concatenate(
            [jnp.dot(xcb[:, c * MXU_DIM:(c + 1) * MXU_DIM], wx_ref[c], preferred_element_type=F32)
             for c in range(nchunk)], axis=-1)
        r = sigmoid(r_pre + ba_ref[...])
        ig = sigmoid(i_pre + bx_ref[...])
        a = jnp.exp((-RGLRU_C) * r * softplus)
        u = jnp.sqrt(1.0 - a * a) * (ig * xc)
        for c in range(nslab):
            a_scr[c, pl.ds(b, ts, stride=nb), :] = a[:, c * LANES:(c + 1) * LANES]
            u_scr[c, pl.ds(b, ts, stride=nb), :] = u[:, c * LANES:(c + 1) * LANES]

    def body(g, hs):
        for j in range(SCAN_UNROLL):
            off = pl.multiple_of((g * SCAN_UNROLL + j) * nb, nb)
            new = []
            for c in range(nslab):
                h = a_scr[c, pl.ds(off, nb), :] * hs[c] + u_scr[c, pl.ds(off, nb), :]
                u_scr[c, pl.ds(off, nb), :] = h
                new.append(h)
            hs = tuple(new)
        return hs

    hs = lax.fori_loop(0, ts // SCAN_UNROLL, body, tuple(h_scr[c] for c in range(nslab)))
    for c in range(nslab):
        h_scr[c] = hs[c]
    for b in range(nb):
        h = jnp.concatenate([u_scr[c, pl.ds(b, ts, stride=nb), :] for c in range(nslab)], axis=-1)
        gate = jax.nn.gelu(gr_ref[b].astype(F32), approximate=True)
        o_ref[b] = (h * gate).astype(o_ref.dtype)


SCAN_UNROLL = 8


def _block_diag_chunks(w):
    per = MXU_DIM // RNN_BLOCK
    w = w.reshape(D_RNN // MXU_DIM, per, RNN_BLOCK, RNN_BLOCK)
    eye = jnp.eye(per, dtype=w.dtype)
    out = jnp.einsum('gpcd,pq->gpcqd', w, eye)
    return out.reshape(D_RNN // MXU_DIM, MXU_DIM, MXU_DIM)


def _rglru(proj, B, S, conv_w, conv_b, w_a, b_a, w_x, b_x, lru_lambda):
    assert B <= SUBLANES, "all batch rows of a time step share one vreg in the scan"
    ts = min(256, S)
    ns = S // ts
    wa = _block_diag_chunks(w_a).astype(BF16)
    wx = _block_diag_chunks(w_x).astype(BF16)
    nchunk = D_RNN // MXU_DIM
    nslab = D_RNN // LANES
    row = lambda v: v.reshape(1, D_RNN)
    const2 = lambda s: (0, 0)
    proj3 = proj.reshape(B, S, proj.shape[-1])
    out = pl.pallas_call(
        _rglru_kernel,
        grid=(ns,),
        in_specs=[
            pl.BlockSpec((B, ts, D_RNN), lambda s: (0, s, COL_XR)),
            pl.BlockSpec((B, ts, D_RNN), lambda s: (0, s, COL_GR)),
            pl.BlockSpec((CONV_W, D_RNN), const2),
            pl.BlockSpec((1, D_RNN), const2),
            pl.BlockSpec((nchunk, MXU_DIM, MXU_DIM), lambda s: (0, 0, 0)),
            pl.BlockSpec((1, D_RNN), const2),
            pl.BlockSpec((nchunk, MXU_DIM, MXU_DIM), lambda s: (0, 0, 0)),
            pl.BlockSpec((1, D_RNN), const2),
            pl.BlockSpec((1, D_RNN), const2),
        ],
        out_specs=pl.BlockSpec((B, ts, D_RNN), lambda s: (0, s, 0)),
        out_shape=jax.ShapeDtypeStruct((B, S, D_RNN), BF16),
        scratch_shapes=[
            pltpu.VMEM((ts + SUBLANES, D_RNN), F32),
            pltpu.VMEM((B, SUBLANES, D_RNN), F32),
            pltpu.VMEM((nslab, ts * B, LANES), F32),
            pltpu.VMEM((nslab, ts * B, LANES), F32),
            pltpu.VMEM((nslab, B, LANES), F32),
        ],
        compiler_params=_params(("arbitrary",)),
        name="rglru",
    )(proj3, proj3, conv_w, row(conv_b), wa, row(b_a), wx, row(b_x), row(lru_lambda))
    return out.reshape(B * S, D_RNN)


def _attn_kernel(q_ref, k_ref, v_ref, bias_ref, lq1_ref, lk1_ref, lq2_ref, lk2_ref, sg_ref,
                 o_ref, qs_scr, vx_scr, s_scr, m_scr, acc_scr, *, tq, lam_init):
    S = q_ref.shape[0]
    nq = S // tq
    scale = HEAD_DIM ** -0.5 * LOG2E
    lane = lax.broadcasted_iota(I32, (tq, V_DIM), 1)
    lam = (jnp.exp(jnp.sum(lq1_ref[...] * lk1_ref[...], keepdims=True))
           - jnp.exp(jnp.sum(lq2_ref[...] * lk2_ref[...], keepdims=True)) + lam_init)
    vx_scr[:, 0:V_DIM] = v_ref[...]
    vx_scr[:, V_DIM:2 * V_DIM] = jnp.ones((S, V_DIM), BF16)

    Z0 = 2

    def prep_q(qi, buf):
        q = (q_ref[pl.ds(pl.multiple_of(qi * tq, tq), tq), :].astype(F32) * scale).astype(BF16)
        zero = jnp.zeros_like(q)
        qs_scr[buf, 0:tq, :] = jnp.where(lane < HEAD_DIM, q, zero)
        qs_scr[buf, tq:2 * tq, :] = jnp.where(lane >= HEAD_DIM, q, zero)

    def scores(buf, j, dst):
        k = k_ref[pl.ds(pl.multiple_of(j * tq, tq), tq), :]
        s_scr[dst] = lax.dot_general(qs_scr[buf], k, (((1,), (1,)), ((), ())),
                                     preferred_element_type=F32)

    def step(j, src, bias_idx, prefetch):
        scores(*prefetch)
        s = s_scr[src]
        if bias_idx is not None:
            b = bias_ref[bias_idx]
            s = s + jnp.concatenate([b, b], axis=0)
        vx = vx_scr[pl.ds(pl.multiple_of(j * tq, tq), tq), :]
        m_prev = m_scr[...]
        m_new = jnp.maximum(m_prev, jnp.max(s, axis=1)[:, None])
        p = jnp.exp2(s - jnp.tile(m_new, (1, tq // LANES)))
        alpha = jnp.exp2(m_prev - m_new)
        acc_scr[...] = (jnp.tile(alpha, (1, 2)) * acc_scr[...]
                        + jnp.dot(p.astype(BF16), vx, preferred_element_type=F32))
        m_scr[...] = m_new

    def begin_block():
        m_scr[...] = jnp.full(m_scr.shape, NEG_INF, F32)
        acc_scr[...] = jnp.zeros(acc_scr.shape, F32)

    def diag_step(qi, par, src):
        nxt = jnp.minimum(qi + 1, nq - 1)
        prep_q(nxt, 1 - par)
        step(qi, src, 0, (1 - par, 0, Z0 + 1 - par))

    def end_block(qi):
        acc = acc_scr[...]
        o_all = acc[:, 0:V_DIM] / acc[:, V_DIM:2 * V_DIM]
        o = o_all[0:tq, :] - lam * o_all[tq:2 * tq, :]
        inv = lax.rsqrt(jnp.mean(o * o, axis=-1, keepdims=True) + EPS)
        y = (o * inv * sg_ref[...]) * (1.0 - lam_init)
        o_ref[pl.ds(pl.multiple_of(qi * tq, tq), tq), :] = y.astype(o_ref.dtype)

    def far_pairs(par, npairs):
        def body(m, c):
            step(2 * m + 1, 0, None, (par, 2 * m + 2, 1))
            step(2 * m + 2, 1, None, (par, 2 * m + 3, 0))
            return c

        lax.fori_loop(0, npairs, body, 0)

    prep_q(0, 0)
    scores(0, 0, Z0)
    begin_block()
    diag_step(0, 0, Z0)
    end_block(0)
    begin_block()
    step(0, Z0 + 1, 1, (1, 1, 0))
    diag_step(1, 1, 0)
    end_block(1)

    def block_pair(i, carry):
        qi = 2 * i
        begin_block()
        step(0, Z0, None, (0, 1, 0))
        far_pairs(0, i - 1)
        step(qi - 1, 0, 1, (0, qi, 1))
        diag_step(qi, 0, 1)
        end_block(qi)
        qi = 2 * i + 1
        begin_block()
        step(0, Z0 + 1, None, (1, 1, 0))
        far_pairs(1, i - 1)
        step(qi - 2, 0, None, (1, qi - 1, 1))
        step(qi - 1, 1, 1, (1, qi, 0))
        diag_step(qi, 1, 0)
        end_block(qi)
        return carry

    lax.fori_loop(1, nq // 2, block_pair, 0)


def _rel_bucket(n):
    n = jnp.maximum(n, 0)
    nf = jnp.maximum(n, MAX_EXACT).astype(F32)
    large = MAX_EXACT + (jnp.log(nf / MAX_EXACT) / math.log(MAX_DISTANCE / MAX_EXACT)
                         * (NUM_BUCKETS - MAX_EXACT)).astype(I32)
    large = jnp.minimum(large, NUM_BUCKETS - 1)
    return jnp.where(n < MAX_EXACT, n, large)


def _bias_tiles(rel_table, tq):
    i = jnp.arange(tq, dtype=I32)[:, None]
    j = jnp.arange(tq, dtype=I32)[None, :]
    table = rel_table.astype(F32) - rel_table[NUM_BUCKETS - 1].astype(F32)[None, :]
    tiles = []
    for delta in (0, tq):
        n = i - j + delta
        onehot = (_rel_bucket(n)[:, :, None] == jnp.arange(NUM_BUCKETS, dtype=I32)).astype(F32)
        b = jnp.einsum('ijb,bh->hij', onehot, table, precision=lax.Precision.HIGHEST) * LOG2E
        tiles.append(jnp.where((n >= 0)[None], b, NEG_INF))
    return jnp.stack(tiles, axis=1)


ATT_TQ = 512
LOG2E = math.log2(math.e)


def _attention(proj, B, S, lq1, lk1, lq2, lk2, subln_g, rel_table, lam_init):
    T = B * S
    tq = min(ATT_TQ, S // 2)
    assert V_DIM == LANES and MAX_DISTANCE <= tq and S % (2 * tq) == 0
    bias = _bias_tiles(rel_table, tq)
    vec = lambda v: v.reshape(1, -1).astype(F32)
    const2 = lambda b, h: (0, 0)
    kern = functools.partial(_attn_kernel, tq=tq, lam_init=lam_init)
    return pl.pallas_call(
        kern,
        grid=(B, N_HEADS),
        in_specs=[
            pl.BlockSpec((S, V_DIM), lambda b, h: (b, COL_Q * N_HEADS + h)),
            pl.BlockSpec((S, V_DIM), lambda b, h: (b, COL_K * N_HEADS + h)),
            pl.BlockSpec((S, V_DIM), lambda b, h: (b, COL_V * N_HEADS + h)),
            pl.BlockSpec((None, 2, tq, tq), lambda b, h: (h, 0, 0, 0)),
            pl.BlockSpec((1, HEAD_DIM), const2),
            pl.BlockSpec((1, HEAD_DIM), const2),
            pl.BlockSpec((1, HEAD_DIM), const2),
            pl.BlockSpec((1, HEAD_DIM), const2),
            pl.BlockSpec((1, V_DIM), const2),
        ],
        out_specs=pl.BlockSpec((S, V_DIM), lambda b, h: (b, h)),
        out_shape=jax.ShapeDtypeStruct((T, ATT_V), BF16),
        scratch_shapes=[
            pltpu.VMEM((2, 2 * tq, V_DIM), BF16),
            pltpu.VMEM((S, 2 * V_DIM), BF16),
            pltpu.VMEM((4, 2 * tq, tq), F32),
            pltpu.VMEM((2 * tq, LANES), F32),
            pltpu.VMEM((2 * tq, 2 * V_DIM), F32),
        ],
        compiler_params=_params(("parallel", "arbitrary")),
        name="diff_attn",
    )(proj, proj, proj, bias, vec(lq1), vec(lk1), vec(lq2), vec(lk2), vec(subln_g))


def _merge_kernel(x_ref, yr_ref, ya_ref, g0_ref, g1_ref, wr_ref, wa_ref, wo_ref, gf_ref, wrt_ref, brt_ref,
                  x2_ref, hf_ref, idx_ref, wgt_ref):
    pr = jnp.dot(yr_ref[...], wr_ref[...], preferred_element_type=F32)
    pa = jnp.dot(ya_ref[...], wa_ref[...], preferred_element_type=F32)
    merged = (jax.nn.sigmoid(g0_ref[...].astype(F32)) * pr
              + jax.nn.sigmoid(g1_ref[...].astype(F32)) * pa)
    x2 = x_ref[...] + jnp.dot(merged.astype(BF16), wo_ref[...], preferred_element_type=F32)
    x2_ref[...] = x2
    inv = lax.rsqrt(jnp.mean(x2 * x2, axis=-1, keepdims=True) + EPS)
    hf = x2 * inv * gf_ref[...]
    _store_token_major(hf_ref, hf)
    nt = (((1,), (1,)), ((), ()))
    hi = hf.astype(BF16)
    lo = (hf - hi.astype(F32)).astype(BF16)
    hw = lax.dot_general(wrt_ref[...], hi, nt, preferred_element_type=F32)
    lw = lax.dot_general(wrt_ref[0:LANES, :], lo, nt, preferred_element_type=F32)
    logits = (hw[0:N_EXPERTS, :] + hw[LANES:LANES + N_EXPERTS, :] + lw[0:N_EXPERTS, :]
              + brt_ref[...])
    tm = logits.shape[1]
    row = lax.broadcasted_iota(I32, (N_EXPERTS, tm), 0)
    work = logits
    idx_rows, val_rows = [], []
    for k in range(TOP_K):
        mx = jnp.max(work, axis=0, keepdims=True)
        sel = jnp.min(jnp.where(work == mx, row, N_EXPERTS), axis=0, keepdims=True)
        idx_rows.append(sel)
        val_rows.append(mx)
        work = jnp.where(row == sel, -jnp.inf, work)
    es = [jnp.exp(v - val_rows[0]) for v in val_rows]
    inv_sum = 1.0 / functools.reduce(lambda a, b: a + b, es)
    pad_i = [jnp.zeros((1, tm), I32)] * (SUBLANES - TOP_K)
    pad_f = [jnp.zeros((1, tm), F32)] * (SUBLANES - TOP_K)
    idx_ref[...] = jnp.concatenate(idx_rows + pad_i, axis=0)
    wgt_ref[...] = jnp.concatenate([e * inv_sum for e in es] + pad_f, axis=0)


def _merge_router(x2d, y_rnn, y_att, proj, w_pr, w_pa, w_o, g_ffn, w_router, b_router):
    T, D = x2d.shape
    tm = min(512, T)
    w_hi = w_router.astype(BF16)
    w_lo = (w_router - w_hi.astype(F32)).astype(BF16)
    wrt = (jnp.zeros((2 * LANES, D), BF16).at[:N_EXPERTS, :].set(w_hi.T)
           .at[LANES:LANES + N_EXPERTS, :].set(w_lo.T))
    brt = b_router.reshape(N_EXPERTS, 1).astype(F32)
    rowblk = lambda c: pl.BlockSpec((tm, D), lambda i, c=c: (i, c))
    full = lambda a: pl.BlockSpec(a.shape, lambda i: (0,) * a.ndim)
    wr, wa, wo = w_pr.astype(BF16), w_pa.astype(BF16), w_o.astype(BF16)
    gf = g_ffn.reshape(1, D)
    return pl.pallas_call(
        _merge_kernel,
        grid=(T // tm,),
        in_specs=[rowblk(0), rowblk(0), rowblk(0), rowblk(COL_G0), rowblk(COL_G1),
                  full(wr), full(wa), full(wo), full(gf), full(wrt), full(brt)],
        out_specs=[rowblk(0), pl.BlockSpec((tm * ROW_TILES, LANES), lambda i: (i, 0)),
                   pl.BlockSpec((SUBLANES, tm), lambda i: (0, i)),
                   pl.BlockSpec((SUBLANES, tm), lambda i: (0, i))],
        out_shape=[jax.ShapeDtypeStruct((T, D), F32), jax.ShapeDtypeStruct((T * ROW_TILES, LANES), F32),
                   jax.ShapeDtypeStruct((SUBLANES, T), I32), jax.ShapeDtypeStruct((SUBLANES, T), F32)],
        compiler_params=_params(("parallel",)),
        name="merge_router",
    )(x2d, y_rnn, y_att, proj, proj, wr, wa, wo, gf, wrt, brt)


def _route_kernel(idx_ref, dest_ref, cnt_ref, cnt_scr, run_scr, start_scr, *, blk):
    ph = pl.program_id(0)
    i = pl.program_id(1)
    tt = idx_ref.shape[1]
    idx = idx_ref[...]
    row = lax.broadcasted_iota(I32, (N_EXPERTS, tt), 0)
    onehot = jnp.zeros((N_EXPERTS, tt), F32)
    for k in range(TOP_K):
        onehot = onehot + (idx[k:k + 1, :] == row).astype(F32)
    tile_cnt = jnp.sum(onehot, axis=1, keepdims=True)

    @pl.when((ph == 0) & (i == 0))
    def _():
        cnt_scr[...] = jnp.zeros_like(cnt_scr)

    @pl.when(ph == 0)
    def _():
        cnt_scr[...] += tile_cnt

    @pl.when((ph == 1) & (i == 0))
    def _():
        padded = jnp.floor((cnt_scr[...] + (blk - 1)) / blk) * blk
        r = lax.broadcasted_iota(I32, (N_EXPERTS, N_EXPERTS), 0)
        c = lax.broadcasted_iota(I32, (N_EXPERTS, N_EXPERTS), 1)
        start_scr[...] = jnp.dot((c < r).astype(F32), padded, preferred_element_type=F32,
                                 precision=lax.Precision.HIGHEST)
        run_scr[...] = jnp.zeros_like(run_scr)

    @pl.when(ph == 1)
    def _():
        r = lax.broadcasted_iota(I32, (tt, tt), 0)
        c = lax.broadcasted_iota(I32, (tt, tt), 1)
        earlier = (r < c).astype(BF16)
        before = jnp.dot(onehot.astype(BF16), earlier, preferred_element_type=F32)
        base = before + run_scr[:, 0:1] + start_scr[:, 0:1]
        rows = [jnp.sum(jnp.where(idx[k:k + 1, :] == row, base, 0.0), axis=0, keepdims=True)
                for k in range(TOP_K)]
        rows += [jnp.zeros((1, tt), F32)] * (SUBLANES - TOP_K)
        dest_ref[...] = jnp.concatenate(rows, axis=0).astype(I32)
        run_scr[...] += tile_cnt
        cnt_ref[...] = cnt_scr[...]


def _route(top_idx, blk):
    T = top_idx.shape[1]
    tt = min(512, T)
    kern = functools.partial(_route_kernel, blk=blk)
    return pl.pallas_call(
        kern,
        grid=(2, T // tt),
        in_specs=[pl.BlockSpec((SUBLANES, tt), lambda p, i: (0, i))],
        out_specs=[pl.BlockSpec((SUBLANES, tt), lambda p, i: (0, p * i)),
                   pl.BlockSpec((N_EXPERTS, LANES), lambda p, i: (0, 0))],
        out_shape=[jax.ShapeDtypeStruct((SUBLANES, T), I32),
                   jax.ShapeDtypeStruct((N_EXPERTS, LANES), F32)],
        scratch_shapes=[pltpu.VMEM((N_EXPERTS, LANES), F32), pltpu.VMEM((N_EXPERTS, LANES), F32),
                        pltpu.VMEM((N_EXPERTS, LANES), F32)],
        compiler_params=_params(("arbitrary", "arbitrary")),
        name="route",
    )(top_idx)


TOKEN_UNROLL = 2


def _for_each_token_k(n_tok, fn):
    def body(g, c):
        for u in range(TOKEN_UNROLL):
            for k in range(TOP_K):
                fn(g * TOKEN_UNROLL + u, k)
        return c

    lax.fori_loop(0, n_tok // TOKEN_UNROLL, body, 0)


def _tile_major_ranks(dest, tt):
    T = dest.shape[1]
    return dest[:TOP_K].reshape(TOP_K, T // tt, tt).transpose(1, 0, 2).reshape(-1)


def _row_tile(ref, row):
    return ref.at[pl.ds(pl.multiple_of(row * ROW_TILES, ROW_TILES), ROW_TILES), :]


def _dispatch_kernel(ends_ref, dest_ref, hf_ref, xs_ref, zero_buf, sem, zsem, *, tt, blk):
    @pl.when(pl.program_id(0) == 0)
    def _():
        zero_buf[...] = jnp.zeros_like(zero_buf)

        def tail_fill(e):
            prev_end = ends_ref[e - 1] if e > 0 else 0
            start = pl.multiple_of((ends_ref[e] - blk) * ROW_TILES, ROW_TILES)
            return ends_ref[e] > prev_end, pltpu.make_async_copy(
                zero_buf, xs_ref.at[pl.ds(start, blk * ROW_TILES), :], zsem)

        for e in range(N_EXPERTS):
            nonempty, cp = tail_fill(e)
            pl.when(nonempty)(cp.start)
        for e in range(N_EXPERTS):
            nonempty, cp = tail_fill(e)
            pl.when(nonempty)(cp.wait)

    def row_copy(t, k):
        return pltpu.make_async_copy(_row_tile(hf_ref, t),
                                     _row_tile(xs_ref, dest_ref[k * tt + t]), sem)

    _for_each_token_k(tt, lambda t, k: row_copy(t, k).start(priority=k % 2))
    _for_each_token_k(tt, lambda t, k: row_copy(t, k).wait())


def _dispatch(ends, dest, hf_tm, n_slots, blk):
    T = hf_tm.shape[0] // ROW_TILES
    tt = min(512, T)
    kern = functools.partial(_dispatch_kernel, tt=tt, blk=blk)
    grid_spec = pltpu.PrefetchScalarGridSpec(
        num_scalar_prefetch=1,
        grid=(T // tt,),
        in_specs=[pl.BlockSpec((TOP_K * tt,), lambda i, ends: (i,), memory_space=pltpu.SMEM),
                  pl.BlockSpec((tt * ROW_TILES, LANES), lambda i, ends: (i, 0))],
        out_specs=pl.BlockSpec(memory_space=pl.ANY),
        scratch_shapes=[pltpu.VMEM((blk * ROW_TILES, LANES), F32),
                        pltpu.SemaphoreType.DMA(()), pltpu.SemaphoreType.DMA(())],
    )
    return pl.pallas_call(
        kern,
        grid_spec=grid_spec,
        out_shape=jax.ShapeDtypeStruct((n_slots * ROW_TILES, LANES), F32),
        compiler_params=_params(("arbitrary",)),
        name="dispatch",
    )(ends, _tile_major_ranks(dest, tt), hf_tm)


def _expert_kernel(first_ref, count_ref, x_ref, wg_ref, bg_ref, wu_ref, bu_ref, wd_ref, bd_ref,
                   o_ref, wg_scr, wu_scr, wd_scr, xbuf, ybuf, xsem, ysem, *, blk):
    e = pl.program_id(0)
    nblk = count_ref[e]
    first = first_ref[e]
    rows = blk * ROW_TILES

    def hbm_block(ref, j):
        return ref.at[pl.ds(pl.multiple_of((first + j) * rows, rows), rows), :]

    def x_copy(j, slot):
        return pltpu.make_async_copy(hbm_block(x_ref, j), xbuf.at[slot], xsem.at[slot])

    def y_copy(j, slot):
        return pltpu.make_async_copy(ybuf.at[slot], hbm_block(o_ref, j), ysem.at[slot])

    @pl.when(nblk > 0)
    def _():
        x_copy(0, 0).start(priority=1)
        wg_scr[...] = wg_ref[0].astype(BF16)
        wu_scr[...] = wu_ref[0].astype(BF16)
        wd_scr[...] = wd_ref[0].astype(BF16)

    def block(j, slot):
        x_copy(j, slot).wait()

        @pl.when(j + 1 < nblk)
        def _():
            x_copy(j + 1, 1 - slot).start(priority=1)

        @pl.when(j >= 2)
        def _():
            y_copy(j - 2, slot).wait()

        x = _load_token_major(xbuf.at[slot], 0, blk).astype(BF16)
        g = jnp.dot(x, wg_scr[...], preferred_element_type=F32) + bg_ref[0]
        u = jnp.dot(x, wu_scr[...], preferred_element_type=F32) + bu_ref[0]
        g = jnp.minimum(g, SWIGLU_LIMIT)
        u = jnp.clip(u, -SWIGLU_LIMIT, SWIGLU_LIMIT)
        act = (u + 1.0) * (g * jax.nn.sigmoid(SWIGLU_ALPHA * g))
        y = jnp.dot(act.astype(BF16), wd_scr[...], preferred_element_type=F32) + bd_ref[0]
        _store_token_major(ybuf.at[slot], y)
        y_copy(j, slot).start(priority=1)

    def block_pair(p, carry):
        block(2 * p, 0)
        block(2 * p + 1, 1)
        return carry

    lax.fori_loop(0, nblk // 2, block_pair, 0)

    @pl.when(nblk % 2 == 1)
    def _():
        block(nblk - 1, 0)

    @pl.when(nblk >= 2)
    def _():
        y_copy(nblk - 2, nblk % 2).wait()

    @pl.when(nblk >= 1)
    def _():
        y_copy(nblk - 1, (nblk - 1) % 2).wait()


def _experts(x_slots, blk, first_block, block_count, wg, bg, wu, bu, wd, bd):
    D = D_MODEL
    rows = blk * ROW_TILES
    wspec = pl.BlockSpec((1, D, D_FF), lambda e, fb, bc: (e, 0, 0))
    bspec = pl.BlockSpec((1, 1, D_FF), lambda e, fb, bc: (e, 0, 0))
    anyspec = pl.BlockSpec(memory_space=pl.ANY)
    grid_spec = pltpu.PrefetchScalarGridSpec(
        num_scalar_prefetch=2,
        grid=(N_EXPERTS,),
        in_specs=[anyspec, wspec, bspec, wspec, bspec, wspec, bspec],
        out_specs=anyspec,
        scratch_shapes=[pltpu.VMEM((D, D_FF), BF16), pltpu.VMEM((D, D_FF), BF16),
                        pltpu.VMEM((D_FF, D), BF16),
                        pltpu.VMEM((2, rows, LANES), F32), pltpu.VMEM((2, rows, LANES), F32),
                        pltpu.SemaphoreType.DMA((2,)), pltpu.SemaphoreType.DMA((2,))],
    )
    b3 = lambda b: b.reshape(N_EXPERTS, 1, -1)
    return pl.pallas_call(
        functools.partial(_expert_kernel, blk=blk),
        grid_spec=grid_spec,
        out_shape=jax.ShapeDtypeStruct(x_slots.shape, F32),
        compiler_params=_params(("arbitrary",)),
        name="experts",
    )(first_block, block_count, x_slots, wg, b3(bg), wu, b3(bu), wd, b3(bd))


def _combine_kernel(dest_ref, dest_next_ref, ys_ref, w_ref, x2_ref, g_ref, o_ref, buf, sems, *, tt):
    i = pl.program_id(0)
    n = pl.num_programs(0)
    cur = i % 2

    def row_copy(idx_ref, slot, t, k):
        return pltpu.make_async_copy(_row_tile(ys_ref, idx_ref[k * tt + t]),
                                     _row_tile(buf, (slot * TOP_K + k) * tt + t), sems.at[slot])

    def gather(idx_ref, slot):
        _for_each_token_k(tt, lambda t, k: row_copy(idx_ref, slot, t, k).start(priority=k % 2))

    @pl.when(i == 0)
    def _():
        gather(dest_ref, 0)

    @pl.when(i + 1 < n)
    def _():
        gather(dest_next_ref, 1 - cur)

    _for_each_token_k(tt, lambda t, k: row_copy(dest_ref, cur, t, k).wait())
    w = w_ref[...]
    y = x2_ref[...]
    for k in range(TOP_K):
        y = y + w[:, k:k + 1] * _load_token_major(buf, (cur * TOP_K + k) * tt, tt)
    inv = lax.rsqrt(jnp.mean(y * y, axis=-1, keepdims=True) + EPS)
    o_ref[...] = y * inv * g_ref[...]


def _combine(dest, y_slots, top_w, x2, g_final):
    T, D = x2.shape
    tt = min(256, T)
    nt = T // tt
    kern = functools.partial(_combine_kernel, tt=tt)
    return pl.pallas_call(
        kern,
        grid=(nt,),
        in_specs=[pl.BlockSpec((TOP_K * tt,), lambda i: (i,), memory_space=pltpu.SMEM),
                  pl.BlockSpec((TOP_K * tt,), lambda i: (jnp.minimum(i + 1, nt - 1),),
                               memory_space=pltpu.SMEM),
                  pl.BlockSpec(memory_space=pl.ANY),
                  pl.BlockSpec((tt, SUBLANES), lambda i: (i, 0)),
                  pl.BlockSpec((tt, D), lambda i: (i, 0)),
                  pl.BlockSpec((1, D), lambda i: (0, 0))],
        out_specs=pl.BlockSpec((tt, D), lambda i: (i, 0)),
        out_shape=jax.ShapeDtypeStruct((T, D), F32),
        scratch_shapes=[pltpu.VMEM((2 * TOP_K * tt * ROW_TILES, LANES), F32),
                        pltpu.SemaphoreType.DMA((2,))],
        compiler_params=_params(("arbitrary",)),
        name="combine",
    )(_tile_major_ranks(dest, tt), _tile_major_ranks(dest, tt), y_slots, top_w.T, x2,
      g_final.reshape(1, D))


def _moe_block_size(T):
    return min(512, max(SUBLANES, T * TOP_K // N_EXPERTS))


def _block_tables(counts, blk):
    cnt = counts.astype(I32)
    padded = (cnt + blk - 1) // blk * blk
    ends = jnp.cumsum(padded).astype(I32)
    return ends, (ends - padded) // blk, padded // blk


def kernel(x, norm_mix_g, w_in, conv_w, conv_b, w_rg_a, b_rg_a, w_rg_x, b_rg_x, lru_lambda, diff_lambda_q1, diff_lambda_k1, diff_lambda_q2, diff_lambda_k2, subln_g, rel_bias_table, w_proj_rnn, w_proj_att, w_out, norm_ffn_g, w_router, b_router, w_gate_e, b_gate_e, w_up_e, b_up_e, w_down_e, b_down_e, norm_final_g):
    B, S, D = x.shape
    T = B * S
    assert norm_mix_g.shape[0] == 1, "single-layer block: the final norm is fused into the MoE combine"
    l = 0
    xt = x.reshape(T, D)
    lam_init = 0.8 - 0.6 * math.exp(-0.3 * l)
    proj = _inproj(xt, norm_mix_g[l], w_in[l].astype(BF16))
    y_rnn = _rglru(proj, B, S, conv_w[l], conv_b[l], w_rg_a[l], b_rg_a[l], w_rg_x[l], b_rg_x[l],
                   lru_lambda[l])
    y_att = _attention(proj, B, S, diff_lambda_q1[l], diff_lambda_k1[l], diff_lambda_q2[l],
                       diff_lambda_k2[l], subln_g[l], rel_bias_table, lam_init)
    x2, hf, top_idx, top_w = _merge_router(xt, y_rnn, y_att, proj, w_proj_rnn[l], w_proj_att[l],
                                           w_out[l], norm_ffn_g[l], w_router[l], b_router[l])
    blk = _moe_block_size(T)
    nb = T * TOP_K // blk + N_EXPERTS
    dest, counts = _route(top_idx, blk)
    ends, first_block, block_count = _block_tables(counts[:, 0], blk)
    x_slots = _dispatch(ends, dest, hf, nb * blk, blk)
    y_slots = _experts(x_slots, blk, first_block, block_count, w_gate_e[l], b_gate_e[l], w_up_e[l],
                       b_up_e[l], w_down_e[l], b_down_e[l])
    out = _combine(dest, y_slots, top_w, x2, norm_final_g)
    return out.reshape(B, S, D)
```

```python
import functools
import math

import jax
import jax.numpy as jnp
from jax import lax
from jax.experimental import pallas as pl
from jax.experimental.pallas import tpu as pltpu

F32 = jnp.float32
BF16 = jnp.bfloat16
I32 = jnp.int32

D_MODEL = 1024
D_RNN = 1024
RNN_BLOCKS = 16
RNN_BLOCK = D_RNN // RNN_BLOCKS
CONV_W = 4
RGLRU_C = 8.0
N_HEADS = 8
HEAD_DIM = 64
V_DIM = 2 * HEAD_DIM
ATT_QK = N_HEADS * 2 * HEAD_DIM
ATT_V = N_HEADS * V_DIM
D_IN = 2 * D_RNN + 2 * ATT_QK + ATT_V + 2 * D_MODEL
NUM_BUCKETS = 32
MAX_EXACT = NUM_BUCKETS // 2
MAX_DISTANCE = 128
N_EXPERTS = 32
TOP_K = 4
D_FF = D_MODEL
SWIGLU_LIMIT = 7.0
SWIGLU_ALPHA = 1.702
EPS = 1e-6
NEG_INF = -1e30

LANES = 128
SUBLANES = 8
MXU_DIM = 256
VMEM_LIMIT = 56 * 1024 * 1024

TM_INPROJ = 1024
TN_INPROJ = 3584
TS_RGLRU = 256
ATT_TQ = 512
TM_MERGE = 512
TT_ROUTE = 1024
TT_DISPATCH = 512
TT_COMBINE = 512
MOE_BLOCK = 512
TOKEN_UNROLL = 4

COL_XR, COL_GR, COL_Q, COL_K, COL_V, COL_G0, COL_G1 = range(7)


def _params(sem, vmem=VMEM_LIMIT):
    return pltpu.CompilerParams(dimension_semantics=sem, vmem_limit_bytes=vmem)


ROW_TILES = D_MODEL // LANES
assert ROW_TILES == SUBLANES


def _store_token_major(ref, val, start=0):
    n = val.shape[0]
    for c in range(ROW_TILES):
        ref[pl.ds(start * ROW_TILES + c, n, stride=ROW_TILES), :] = val[:, c * LANES:(c + 1) * LANES]


def _load_token_major(ref, start, n):
    return jnp.concatenate(
        [ref[pl.ds(start * ROW_TILES + c, n, stride=ROW_TILES), :] for c in range(ROW_TILES)], axis=-1)


def _inproj_kernel(x_ref, g_ref, w_ref, o_ref, h_scr):
    @pl.when(pl.program_id(1) == 0)
    def _():
        x = x_ref[...]
        inv = lax.rsqrt(jnp.mean(x * x, axis=-1, keepdims=True) + EPS)
        h_scr[...] = (x * inv * g_ref[...]).astype(BF16)

    o_ref[...] = jnp.dot(h_scr[...], w_ref[...], preferred_element_type=F32).astype(o_ref.dtype)


def _inproj(x2d, g, w_bf16):
    T, D = x2d.shape
    N = w_bf16.shape[1]
    tm = min(TM_INPROJ, T)
    tn = TN_INPROJ if N % TN_INPROJ == 0 else D_MODEL
    return pl.pallas_call(
        _inproj_kernel,
        grid=(T // tm, N // tn),
        in_specs=[
            pl.BlockSpec((tm, D), lambda i, j: (i, 0)),
            pl.BlockSpec((1, D), lambda i, j: (0, 0)),
            pl.BlockSpec((D, tn), lambda i, j: (0, j)),
        ],
        out_specs=pl.BlockSpec((tm, tn), lambda i, j: (i, j)),
        out_shape=jax.ShapeDtypeStruct((T, N), BF16),
        scratch_shapes=[pltpu.VMEM((tm, D), BF16)],
        compiler_params=_params(("parallel", "arbitrary")),
        name="inproj",
    )(x2d, g.reshape(1, D), w_bf16)


def _rglru_kernel(xr_ref, gr_ref, cw_ref, cb_ref, wa_ref, ba_ref, wx_ref, bx_ref, lam_ref,
                  o_ref, xbuf, tail_scr, a_scr, u_scr, h_scr):
    nb, ts, _ = xr_ref.shape
    pad = SUBLANES
    nslab = D_RNN // LANES

    @pl.when(pl.program_id(0) == 0)
    def _():
        tail_scr[...] = jnp.zeros_like(tail_scr)
        h_scr[...] = jnp.zeros_like(h_scr)

    cw = cw_ref[...]
    z = -lam_ref[...]
    softplus = jnp.maximum(z, 0.0) + jnp.log1p(jnp.exp(-jnp.abs(z)))
    sigmoid = lambda v: 0.5 * jnp.tanh(0.5 * v) + 0.5
    nchunk = D_RNN // MXU_DIM
    for b in range(nb):
        x = xr_ref[b].astype(F32)
        xbuf[0:pad, :] = tail_scr[b]
        xbuf[pad:pad + ts, :] = x
        xc = (cw[3:4, :] * x
              + cw[2:3, :] * xbuf[pad - 1:pad - 1 + ts, :]
              + cw[1:2, :] * xbuf[pad - 2:pad - 2 + ts, :]
              + cw[0:1, :] * xbuf[pad - 3:pad - 3 + ts, :]) + cb_ref[...]
        tail_scr[b] = x[ts - pad:ts, :]
        xcb = xc.astype(BF16)
        r_pre = jnp.concatenate(
            [jnp.dot(xcb[:, c * MXU_DIM:(c + 1) * MXU_DIM], wa_ref[c], preferred_element_type=F32)
             for c in range(nchunk)], axis=-1)
        i_pre = jnp.concatenate(
            [jnp.dot(xcb[:, c * MXU_DIM:(c + 1) * MXU_DIM], wx_ref[c], preferred_element_type=F32)
             for c in range(nchunk)], axis=-1)
        r = sigmoid(r_pre + ba_ref[...])
        ig = sigmoid(i_pre + bx_ref[...])
        a = jnp.exp((-RGLRU_C) * r * softplus)
        u = jnp.sqrt(1.0 - a * a) * (ig * xc)
        for c in range(nslab):
            a_scr[c, pl.ds(b, ts, stride=nb), :] = a[:, c * LANES:(c + 1) * LANES]
            u_scr[c, pl.ds(b, ts, stride=nb), :] = u[:, c * LANES:(c + 1) * LANES]

    def body(g, hs):
        for j in range(SCAN_UNROLL):
            off = pl.multiple_of((g * SCAN_UNROLL + j) * nb, nb)
            new = []
            for c in range(nslab):
                h = a_scr[c, pl.ds(off, nb), :] * hs[c] + u_scr[c, pl.ds(off, nb), :]
                u_scr[c, pl.ds(off, nb), :] = h
                new.append(h)
            hs = tuple(new)
        return hs

    hs = lax.fori_loop(0, ts // SCAN_UNROLL, body, tuple(h_scr[c] for c in range(nslab)))
    for c in range(nslab):
        h_scr[c] = hs[c]
    for b in range(nb):
        h = jnp.concatenate([u_scr[c, pl.ds(b, ts, stride=nb), :] for c in range(nslab)], axis=-1)
        gate = jax.nn.gelu(gr_ref[b].astype(F32), approximate=True)
        o_ref[b] = (h * gate).astype(o_ref.dtype)


SCAN_UNROLL = 8


def _block_diag_chunks(w):
    per = MXU_DIM // RNN_BLOCK
    w = w.reshape(D_RNN // MXU_DIM, per, RNN_BLOCK, RNN_BLOCK)
    eye = jnp.eye(per, dtype=w.dtype)
    out = jnp.einsum('gpcd,pq->gpcqd', w, eye)
    return out.reshape(D_RNN // MXU_DIM, MXU_DIM, MXU_DIM)


def _rglru(proj, B, S, conv_w, conv_b, w_a, b_a, w_x, b_x, lru_lambda):
    assert B <= SUBLANES, "all batch rows of a time step share one vreg in the scan"
    ts = min(TS_RGLRU, S)
    ns = S // ts
    wa = _block_diag_chunks(w_a).astype(BF16)
    wx = _block_diag_chunks(w_x).astype(BF16)
    nchunk = D_RNN // MXU_DIM
    nslab = D_RNN // LANES
    row = lambda v: v.reshape(1, D_RNN)
    const2 = lambda s: (0, 0)
    proj3 = proj.reshape(B, S, proj.shape[-1])
    out = pl.pallas_call(
        _rglru_kernel,
        grid=(ns,),
        in_specs=[
            pl.BlockSpec((B, ts, D_RNN), lambda s: (0, s, COL_XR)),
            pl.BlockSpec((B, ts, D_RNN), lambda s: (0, s, COL_GR)),
            pl.BlockSpec((CONV_W, D_RNN), const2),
            pl.BlockSpec((1, D_RNN), const2),
            pl.BlockSpec((nchunk, MXU_DIM, MXU_DIM), lambda s: (0, 0, 0)),
            pl.BlockSpec((1, D_RNN), const2),
            pl.BlockSpec((nchunk, MXU_DIM, MXU_DIM), lambda s: (0, 0, 0)),
            pl.BlockSpec((1, D_RNN), const2),
            pl.BlockSpec((1, D_RNN), const2),
        ],
        out_specs=pl.BlockSpec((B, ts, D_RNN), lambda s: (0, s, 0)),
        out_shape=jax.ShapeDtypeStruct((B, S, D_RNN), BF16),
        scratch_shapes=[
            pltpu.VMEM((ts + SUBLANES, D_RNN), F32),
            pltpu.VMEM((B, SUBLANES, D_RNN), F32),
            pltpu.VMEM((nslab, ts * B, LANES), F32),
            pltpu.VMEM((nslab, ts * B, LANES), F32),
            pltpu.VMEM((nslab, B, LANES), F32),
        ],
        compiler_params=_params(("arbitrary",)),
        name="rglru",
    )(proj3, proj3, conv_w, row(conv_b), wa, row(b_a), wx, row(b_x), row(lru_lambda))
    return out.reshape(B * S, D_RNN)


def _attn_kernel(q_ref, k_ref, v_ref, bias_ref, lq1_ref, lk1_ref, lq2_ref, lk2_ref, sg_ref,
                 o_ref, qs_scr, vx_scr, s_scr, m_scr, acc_scr, *, tq, lam_init):
    S = q_ref.shape[0]
    nq = S // tq
    scale = HEAD_DIM ** -0.5 * LOG2E
    lane = lax.broadcasted_iota(I32, (tq, V_DIM), 1)
    lam = (jnp.exp(jnp.sum(lq1_ref[...] * lk1_ref[...], keepdims=True))
           - jnp.exp(jnp.sum(lq2_ref[...] * lk2_ref[...], keepdims=True)) + lam_init)
    vx_scr[:, 0:V_DIM] = v_ref[...]
    vx_scr[:, V_DIM:2 * V_DIM] = jnp.ones((S, V_DIM), BF16)

    Z0 = 2

    def prep_q(qi, buf):
        q = (q_ref[pl.ds(pl.multiple_of(qi * tq, tq), tq), :].astype(F32) * scale).astype(BF16)
        zero = jnp.zeros_like(q)
        qs_scr[buf, 0:tq, :] = jnp.where(lane < HEAD_DIM, q, zero)
        qs_scr[buf, tq:2 * tq, :] = jnp.where(lane >= HEAD_DIM, q, zero)

    def scores(buf, j, dst):
        k = k_ref[pl.ds(pl.multiple_of(j * tq, tq), tq), :]
        s_scr[dst] = lax.dot_general(qs_scr[buf], k, (((1,), (1,)), ((), ())),
                                     preferred_element_type=F32)

    def step(j, src, bias_idx, prefetch):
        scores(*prefetch)
        s = s_scr[src]
        if bias_idx is not None:
            b = bias_ref[bias_idx]
            s = s + jnp.concatenate([b, b], axis=0)
        vx = vx_scr[pl.ds(pl.multiple_of(j * tq, tq), tq), :]
        m_prev = m_scr[...]
        m_new = jnp.maximum(m_prev, jnp.max(s, axis=1)[:, None])
        p = jnp.exp2(s - jnp.tile(m_new, (1, tq // LANES)))
        alpha = jnp.exp2(m_prev - m_new)
        acc_scr[...] = (jnp.tile(alpha, (1, 2)) * acc_scr[...]
                        + jnp.dot(p.astype(BF16), vx, preferred_element_type=F32))
        m_scr[...] = m_new

    def begin_block():
        m_scr[...] = jnp.full(m_scr.shape, NEG_INF, F32)
        acc_scr[...] = jnp.zeros(acc_scr.shape, F32)

    def diag_step(qi, par, src):
        nxt = jnp.minimum(qi + 1, nq - 1)
        prep_q(nxt, 1 - par)
        step(qi, src, 0, (1 - par, 0, Z0 + 1 - par))

    def end_block(qi):
        acc = acc_scr[...]
        o_all = acc[:, 0:V_DIM] / acc[:, V_DIM:2 * V_DIM]
        o = o_all[0:tq, :] - lam * o_all[tq:2 * tq, :]
        inv = lax.rsqrt(jnp.mean(o * o, axis=-1, keepdims=True) + EPS)
        y = (o * inv * sg_ref[...]) * (1.0 - lam_init)
        o_ref[pl.ds(pl.multiple_of(qi * tq, tq), tq), :] = y.astype(o_ref.dtype)

    def far_pairs(par, npairs):
        def body(m, c):
            step(2 * m + 1, 0, None, (par, 2 * m + 2, 1))
            step(2 * m + 2, 1, None, (par, 2 * m + 3, 0))
            return c

        lax.fori_loop(0, npairs, body, 0)

    prep_q(0, 0)
    scores(0, 0, Z0)
    begin_block()
    diag_step(0, 0, Z0)
    end_block(0)
    begin_block()
    step(0, Z0 + 1, 1, (1, 1, 0))
    diag_step(1, 1, 0)
    end_block(1)

    def block_pair(i, carry):
        qi = 2 * i
        begin_block()
        step(0, Z0, None, (0, 1, 0))
        far_pairs(0, i - 1)
        step(qi - 1, 0, 1, (0, qi, 1))
        diag_step(qi, 0, 1)
        end_block(qi)
        qi = 2 * i + 1
        begin_block()
        step(0, Z0 + 1, None, (1, 1, 0))
        far_pairs(1, i - 1)
        step(qi - 2, 0, None, (1, qi - 1, 1))
        step(qi - 1, 1, 1, (1, qi, 0))
        diag_step(qi, 1, 0)
        end_block(qi)
        return carry

    lax.fori_loop(1, nq // 2, block_pair, 0)


def _rel_bucket(n):
    n = jnp.maximum(n, 0)
    nf = jnp.maximum(n, MAX_EXACT).astype(F32)
    large = MAX_EXACT + (jnp.log(nf / MAX_EXACT) / math.log(MAX_DISTANCE / MAX_EXACT)
                         * (NUM_BUCKETS - MAX_EXACT)).astype(I32)
    large = jnp.minimum(large, NUM_BUCKETS - 1)
    return jnp.where(n < MAX_EXACT, n, large)


def _bias_tiles(rel_table, tq):
    i = jnp.arange(tq, dtype=I32)[:, None]
    j = jnp.arange(tq, dtype=I32)[None, :]
    table = rel_table.astype(F32) - rel_table[NUM_BUCKETS - 1].astype(F32)[None, :]
    tiles = []
    for delta in (0, tq):
        n = i - j + delta
        onehot = (_rel_bucket(n)[:, :, None] == jnp.arange(NUM_BUCKETS, dtype=I32)).astype(F32)
        b = jnp.einsum('ijb,bh->hij', onehot, table, precision=lax.Precision.HIGHEST) * LOG2E
        tiles.append(jnp.where((n >= 0)[None], b, NEG_INF))
    return jnp.stack(tiles, axis=1)


LOG2E = math.log2(math.e)


def _attention(proj, B, S, lq1, lk1, lq2, lk2, subln_g, rel_table, lam_init):
    T = B * S
    tq = min(ATT_TQ, S // 2)
    assert V_DIM == LANES and MAX_DISTANCE <= tq and S % (2 * tq) == 0
    bias = _bias_tiles(rel_table, tq)
    vec = lambda v: v.reshape(1, -1).astype(F32)
    const2 = lambda b, h: (0, 0)
    kern = functools.partial(_attn_kernel, tq=tq, lam_init=lam_init)
    return pl.pallas_call(
        kern,
        grid=(B, N_HEADS),
        in_specs=[
            pl.BlockSpec((S, V_DIM), lambda b, h: (b, COL_Q * N_HEADS + h)),
            pl.BlockSpec((S, V_DIM), lambda b, h: (b, COL_K * N_HEADS + h)),
            pl.BlockSpec((S, V_DIM), lambda b, h: (b, COL_V * N_HEADS + h)),
            pl.BlockSpec((None, 2, tq, tq), lambda b, h: (h, 0, 0, 0)),
            pl.BlockSpec((1, HEAD_DIM), const2),
            pl.BlockSpec((1, HEAD_DIM), const2),
            pl.BlockSpec((1, HEAD_DIM), const2),
            pl.BlockSpec((1, HEAD_DIM), const2),
            pl.BlockSpec((1, V_DIM), const2),
        ],
        out_specs=pl.BlockSpec((S, V_DIM), lambda b, h: (b, h)),
        out_shape=jax.ShapeDtypeStruct((T, ATT_V), BF16),
        scratch_shapes=[
            pltpu.VMEM((2, 2 * tq, V_DIM), BF16),
            pltpu.VMEM((S, 2 * V_DIM), BF16),
            pltpu.VMEM((4, 2 * tq, tq), F32),
            pltpu.VMEM((2 * tq, LANES), F32),
            pltpu.VMEM((2 * tq, 2 * V_DIM), F32),
        ],
        compiler_params=_params(("parallel", "arbitrary")),
        name="diff_attn",
    )(proj, proj, proj, bias, vec(lq1), vec(lk1), vec(lq2), vec(lk2), vec(subln_g))


def _merge_kernel(x_ref, yr_ref, ya_ref, g0_ref, g1_ref, wr_ref, wa_ref, wo_ref, gf_ref, wrt_ref, brt_ref,
                  x2_ref, hf_ref, idx_ref, wgt_ref):
    pr = jnp.dot(yr_ref[...], wr_ref[...], preferred_element_type=F32)
    pa = jnp.dot(ya_ref[...], wa_ref[...], preferred_element_type=F32)
    merged = (jax.nn.sigmoid(g0_ref[...].astype(F32)) * pr
              + jax.nn.sigmoid(g1_ref[...].astype(F32)) * pa)
    x2 = x_ref[...] + jnp.dot(merged.astype(BF16), wo_ref[...], preferred_element_type=F32)
    x2_ref[...] = x2
    inv = lax.rsqrt(jnp.mean(x2 * x2, axis=-1, keepdims=True) + EPS)
    hf = x2 * inv * gf_ref[...]
    _store_token_major(hf_ref, hf)
    nt = (((1,), (1,)), ((), ()))
    hi = hf.astype(BF16)
    lo = (hf - hi.astype(F32)).astype(BF16)
    hw = lax.dot_general(wrt_ref[...], hi, nt, preferred_element_type=F32)
    lw = lax.dot_general(wrt_ref[0:LANES, :], lo, nt, preferred_element_type=F32)
    logits = (hw[0:N_EXPERTS, :] + hw[LANES:LANES + N_EXPERTS, :] + lw[0:N_EXPERTS, :]
              + brt_ref[...])
    tm = logits.shape[1]
    row = lax.broadcasted_iota(I32, (N_EXPERTS, tm), 0)
    work = logits
    idx_rows, val_rows = [], []
    for k in range(TOP_K):
        mx = jnp.max(work, axis=0, keepdims=True)
        sel = jnp.min(jnp.where(work == mx, row, N_EXPERTS), axis=0, keepdims=True)
        idx_rows.append(sel)
        val_rows.append(mx)
        work = jnp.where(row == sel, -jnp.inf, work)
    es = [jnp.exp(v - val_rows[0]) for v in val_rows]
    inv_sum = 1.0 / functools.reduce(lambda a, b: a + b, es)
    pad_i = [jnp.zeros((1, tm), I32)] * (SUBLANES - TOP_K)
    pad_f = [jnp.zeros((1, tm), F32)] * (SUBLANES - TOP_K)
    idx_ref[...] = jnp.concatenate(idx_rows + pad_i, axis=0)
    wgt_ref[...] = jnp.concatenate([e * inv_sum for e in es] + pad_f, axis=0)


def _merge_router(x2d, y_rnn, y_att, proj, w_pr, w_pa, w_o, g_ffn, w_router, b_router):
    T, D = x2d.shape
    tm = min(TM_MERGE, T)
    w_hi = w_router.astype(BF16)
    w_lo = (w_router - w_hi.astype(F32)).astype(BF16)
    wrt = (jnp.zeros((2 * LANES, D), BF16).at[:N_EXPERTS, :].set(w_hi.T)
           .at[LANES:LANES + N_EXPERTS, :].set(w_lo.T))
    brt = b_router.reshape(N_EXPERTS, 1).astype(F32)
    rowblk = lambda c: pl.BlockSpec((tm, D), lambda i, c=c: (i, c))
    full = lambda a: pl.BlockSpec(a.shape, lambda i: (0,) * a.ndim)
    wr, wa, wo = w_pr.astype(BF16), w_pa.astype(BF16), w_o.astype(BF16)
    gf = g_ffn.reshape(1, D)
    return pl.pallas_call(
        _merge_kernel,
        grid=(T // tm,),
        in_specs=[rowblk(0), rowblk(0), rowblk(0), rowblk(COL_G0), rowblk(COL_G1),
                  full(wr), full(wa), full(wo), full(gf), full(wrt), full(brt)],
        out_specs=[rowblk(0), pl.BlockSpec((tm * ROW_TILES, LANES), lambda i: (i, 0)),
                   pl.BlockSpec((SUBLANES, tm), lambda i: (0, i)),
                   pl.BlockSpec((SUBLANES, tm), lambda i: (0, i))],
        out_shape=[jax.ShapeDtypeStruct((T, D), F32), jax.ShapeDtypeStruct((T * ROW_TILES, LANES), F32),
                   jax.ShapeDtypeStruct((SUBLANES, T), I32), jax.ShapeDtypeStruct((SUBLANES, T), F32)],
        compiler_params=_params(("parallel",)),
        name="merge_router",
    )(x2d, y_rnn, y_att, proj, proj, wr, wa, wo, gf, wrt, brt)


def _route_kernel(idx_ref, dest_ref, cnt_ref, cnt_scr, run_scr, start_scr, *, blk):
    ph = pl.program_id(0)
    i = pl.program_id(1)
    tt = idx_ref.shape[1]
    idx = idx_ref[...]
    row = lax.broadcasted_iota(I32, (N_EXPERTS, tt), 0)
    onehot = jnp.zeros((N_EXPERTS, tt), F32)
    for k in range(TOP_K):
        onehot = onehot + (idx[k:k + 1, :] == row).astype(F32)
    tile_cnt = jnp.sum(onehot, axis=1, keepdims=True)

    @pl.when((ph == 0) & (i == 0))
    def _():
        cnt_scr[...] = jnp.zeros_like(cnt_scr)

    @pl.when(ph == 0)
    def _():
        cnt_scr[...] += tile_cnt

    @pl.when((ph == 1) & (i == 0))
    def _():
        padded = jnp.floor((cnt_scr[...] + (blk - 1)) / blk) * blk
        r = lax.broadcasted_iota(I32, (N_EXPERTS, N_EXPERTS), 0)
        c = lax.broadcasted_iota(I32, (N_EXPERTS, N_EXPERTS), 1)
        start_scr[...] = jnp.dot((c < r).astype(F32), padded, preferred_element_type=F32,
                                 precision=lax.Precision.HIGHEST)
        run_scr[...] = jnp.zeros_like(run_scr)

    @pl.when(ph == 1)
    def _():
        r = lax.broadcasted_iota(I32, (tt, tt), 0)
        c = lax.broadcasted_iota(I32, (tt, tt), 1)
        earlier = (r < c).astype(BF16)
        before = jnp.dot(onehot.astype(BF16), earlier, preferred_element_type=F32)
        base = before + run_scr[:, 0:1] + start_scr[:, 0:1]
        rows = [jnp.sum(jnp.where(idx[k:k + 1, :] == row, base, 0.0), axis=0, keepdims=True)
                for k in range(TOP_K)]
        rows += [jnp.zeros((1, tt), F32)] * (SUBLANES - TOP_K)
        dest_ref[...] = jnp.concatenate(rows, axis=0).astype(I32)
        run_scr[...] += tile_cnt
        cnt_ref[...] = cnt_scr[...]


def _route(top_idx, blk):
    T = top_idx.shape[1]
    tt = min(TT_ROUTE, T)
    kern = functools.partial(_route_kernel, blk=blk)
    return pl.pallas_call(
        kern,
        grid=(2, T // tt),
        in_specs=[pl.BlockSpec((SUBLANES, tt), lambda p, i: (0, i))],
        out_specs=[pl.BlockSpec((SUBLANES, tt), lambda p, i: (0, p * i)),
                   pl.BlockSpec((N_EXPERTS, LANES), lambda p, i: (0, 0))],
        out_shape=[jax.ShapeDtypeStruct((SUBLANES, T), I32),
                   jax.ShapeDtypeStruct((N_EXPERTS, LANES), F32)],
        scratch_shapes=[pltpu.VMEM((N_EXPERTS, LANES), F32), pltpu.VMEM((N_EXPERTS, LANES), F32),
                        pltpu.VMEM((N_EXPERTS, LANES), F32)],
        compiler_params=_params(("arbitrary", "arbitrary")),
        name="route",
    )(top_idx)


def _for_each_token_k(n_tok, fn):
    def body(g, c):
        for u in range(TOKEN_UNROLL):
            for k in range(TOP_K):
                fn(g * TOKEN_UNROLL + u, k)
        return c

    lax.fori_loop(0, n_tok // TOKEN_UNROLL, body, 0)


def _tile_major_ranks(dest, tt):
    T = dest.shape[1]
    return dest[:TOP_K].reshape(TOP_K, T // tt, tt).transpose(1, 0, 2).reshape(-1)


def _row_tile(ref, row):
    return ref.at[pl.ds(pl.multiple_of(row * ROW_TILES, ROW_TILES), ROW_TILES), :]


def _dispatch_kernel(ends_ref, dest_ref, hf_ref, xs_ref, zero_buf, sem, zsem, *, tt, blk):
    @pl.when(pl.program_id(0) == 0)
    def _():
        zero_buf[...] = jnp.zeros_like(zero_buf)

        def tail_fill(e):
            prev_end = ends_ref[e - 1] if e > 0 else 0
            start = pl.multiple_of((ends_ref[e] - blk) * ROW_TILES, ROW_TILES)
            return ends_ref[e] > prev_end, pltpu.make_async_copy(
                zero_buf, xs_ref.at[pl.ds(start, blk * ROW_TILES), :], zsem)

        for e in range(N_EXPERTS):
            nonempty, cp = tail_fill(e)
            pl.when(nonempty)(cp.start)
        for e in range(N_EXPERTS):
            nonempty, cp = tail_fill(e)
            pl.when(nonempty)(cp.wait)

    def row_copy(t, k):
        return pltpu.make_async_copy(_row_tile(hf_ref, t),
                                     _row_tile(xs_ref, dest_ref[k * tt + t]), sem)

    _for_each_token_k(tt, lambda t, k: row_copy(t, k).start(priority=k % 2))
    _for_each_token_k(tt, lambda t, k: row_copy(t, k).wait())


def _dispatch(ends, dest, hf_tm, n_slots, blk):
    T = hf_tm.shape[0] // ROW_TILES
    tt = min(TT_DISPATCH, T)
    kern = functools.partial(_dispatch_kernel, tt=tt, blk=blk)
    grid_spec = pltpu.PrefetchScalarGridSpec(
        num_scalar_prefetch=1,
        grid=(T // tt,),
        in_specs=[pl.BlockSpec((TOP_K * tt,), lambda i, ends: (i,), memory_space=pltpu.SMEM),
                  pl.BlockSpec((tt * ROW_TILES, LANES), lambda i, ends: (i, 0))],
        out_specs=pl.BlockSpec(memory_space=pl.ANY),
        scratch_shapes=[pltpu.VMEM((blk * ROW_TILES, LANES), F32),
                        pltpu.SemaphoreType.DMA(()), pltpu.SemaphoreType.DMA(())],
    )
    return pl.pallas_call(
        kern,
        grid_spec=grid_spec,
        out_shape=jax.ShapeDtypeStruct((n_slots * ROW_TILES, LANES), F32),
        compiler_params=_params(("arbitrary",)),
        name="dispatch",
    )(ends, _tile_major_ranks(dest, tt), hf_tm)


def _expert_kernel(be_ref, bsrc_ref, act_ref, first_ref, x_ref, wg_ref, bg_ref, wu_ref, bu_ref, wd_ref, bd_ref,
                   o_ref, wg_scr, wu_scr, wd_scr, *, blk):
    del be_ref, bsrc_ref
    i = pl.program_id(0)

    @pl.when(first_ref[i] == 1)
    def _():
        wg_scr[...] = wg_ref[0].astype(BF16)
        wu_scr[...] = wu_ref[0].astype(BF16)
        wd_scr[...] = wd_ref[0].astype(BF16)

    @pl.when(act_ref[i] == 1)
    def _():
        x = _load_token_major(x_ref, 0, blk).astype(BF16)
        g = jnp.dot(x, wg_scr[...], preferred_element_type=F32) + bg_ref[0]
        u = jnp.dot(x, wu_scr[...], preferred_element_type=F32) + bu_ref[0]
        g = jnp.minimum(g, SWIGLU_LIMIT)
        u = jnp.clip(u, -SWIGLU_LIMIT, SWIGLU_LIMIT)
        act = (u + 1.0) * (g * jax.nn.sigmoid(SWIGLU_ALPHA * g))
        y = jnp.dot(act.astype(BF16), wd_scr[...], preferred_element_type=F32) + bd_ref[0]
        _store_token_major(o_ref, y)


def _experts(x_slots, blk, block_expert, block_src, block_active, block_first, wg, bg, wu, bu, wd, bd):
    D = D_MODEL
    nb = x_slots.shape[0] // (blk * ROW_TILES)
    wspec = pl.BlockSpec((1, D, D_FF), lambda i, be, bs, ac, fi: (be[i], 0, 0))
    bspec = pl.BlockSpec((1, 1, D_FF), lambda i, be, bs, ac, fi: (be[i], 0, 0))
    xspec = pl.BlockSpec((blk * ROW_TILES, LANES), lambda i, be, bs, ac, fi: (bs[i], 0))
    grid_spec = pltpu.PrefetchScalarGridSpec(
        num_scalar_prefetch=4,
        grid=(nb,),
        in_specs=[xspec, wspec, bspec, wspec, bspec, wspec, bspec],
        out_specs=xspec,
        scratch_shapes=[pltpu.VMEM((D, D_FF), BF16), pltpu.VMEM((D, D_FF), BF16),
                        pltpu.VMEM((D_FF, D), BF16)],
    )
    b3 = lambda b: b.reshape(N_EXPERTS, 1, -1)
    return pl.pallas_call(
        functools.partial(_expert_kernel, blk=blk),
        grid_spec=grid_spec,
        out_shape=jax.ShapeDtypeStruct(x_slots.shape, F32),
        compiler_params=_params(("arbitrary",)),
        name="experts",
    )(block_expert, block_src, block_active, block_first, x_slots,
      wg, b3(bg), wu, b3(bu), wd, b3(bd))


def _combine_kernel(dest_ref, dest_next_ref, ys_ref, w_ref, x2_ref, g_ref, o_ref, buf, sems, *, tt):
    i = pl.program_id(0)
    n = pl.num_programs(0)
    cur = i % 2

    def row_copy(idx_ref, slot, t, k):
        return pltpu.make_async_copy(_row_tile(ys_ref, idx_ref[k * tt + t]),
                                     _row_tile(buf, (slot * TOP_K + k) * tt + t), sems.at[slot])

    def gather(idx_ref, slot):
        _for_each_token_k(tt, lambda t, k: row_copy(idx_ref, slot, t, k).start(priority=k % 2))

    @pl.when(i == 0)
    def _():
        gather(dest_ref, 0)

    @pl.when(i + 1 < n)
    def _():
        gather(dest_next_ref, 1 - cur)

    _for_each_token_k(tt, lambda t, k: row_copy(dest_ref, cur, t, k).wait())
    w = w_ref[...]
    y = x2_ref[...]
    for k in range(TOP_K):
        y = y + w[:, k:k + 1] * _load_token_major(buf, (cur * TOP_K + k) * tt, tt)
    inv = lax.rsqrt(jnp.mean(y * y, axis=-1, keepdims=True) + EPS)
    o_ref[...] = y * inv * g_ref[...]


def _combine(dest, y_slots, top_w, x2, g_final):
    T, D = x2.shape
    tt = min(TT_COMBINE, T)
    nt = T // tt
    kern = functools.partial(_combine_kernel, tt=tt)
    return pl.pallas_call(
        kern,
        grid=(nt,),
        in_specs=[pl.BlockSpec((TOP_K * tt,), lambda i: (i,), memory_space=pltpu.SMEM),
                  pl.BlockSpec((TOP_K * tt,), lambda i: (jnp.minimum(i + 1, nt - 1),),
                               memory_space=pltpu.SMEM),
                  pl.BlockSpec(memory_space=pl.ANY),
                  pl.BlockSpec((tt, SUBLANES), lambda i: (i, 0)),
                  pl.BlockSpec((tt, D), lambda i: (i, 0)),
                  pl.BlockSpec((1, D), lambda i: (0, 0))],
        out_specs=pl.BlockSpec((tt, D), lambda i: (i, 0)),
        out_shape=jax.ShapeDtypeStruct((T, D), F32),
        scratch_shapes=[pltpu.VMEM((2 * TOP_K * tt * ROW_TILES, LANES), F32),
                        pltpu.SemaphoreType.DMA((2,))],
        compiler_params=_params(("arbitrary",)),
        name="combine",
    )(_tile_major_ranks(dest, tt), _tile_major_ranks(dest, tt), y_slots, top_w.T, x2,
      g_final.reshape(1, D))


def _moe_block_size(T):
    return min(MOE_BLOCK, max(SUBLANES, T * TOP_K // N_EXPERTS))


def _block_tables(counts, blk, nb):
    cnt = counts.astype(I32)
    padded = (cnt + blk - 1) // blk * blk
    ends = jnp.cumsum(padded)
    used = ends[-1] // blk
    starts = jnp.arange(nb, dtype=I32) * blk
    src = jnp.minimum(jnp.arange(nb, dtype=I32), jnp.maximum(used - 1, 0))
    expert = jnp.minimum(jnp.sum((starts[:, None] >= ends[None, :]).astype(I32), axis=1), N_EXPERTS - 1)
    expert = expert[src]
    active = (jnp.arange(nb, dtype=I32) < used).astype(I32)
    prev = jnp.concatenate([jnp.full((1,), -1, I32), expert[:-1]])
    first = active * (expert != prev).astype(I32)
    return ends.astype(I32), expert, src, active, first


def kernel(x, norm_mix_g, w_in, conv_w, conv_b, w_rg_a, b_rg_a, w_rg_x, b_rg_x, lru_lambda, diff_lambda_q1, diff_lambda_k1, diff_lambda_q2, diff_lambda_k2, subln_g, rel_bias_table, w_proj_rnn, w_proj_att, w_out, norm_ffn_g, w_router, b_router, w_gate_e, b_gate_e, w_up_e, b_up_e, w_down_e, b_down_e, norm_final_g):
    B, S, D = x.shape
    T = B * S
    assert norm_mix_g.shape[0] == 1, "single-layer block: the final norm is fused into the MoE combine"
    l = 0
    xt = x.reshape(T, D)
    lam_init = 0.8 - 0.6 * math.exp(-0.3 * l)
    proj = _inproj(xt, norm_mix_g[l], w_in[l].astype(BF16))
    y_rnn = _rglru(proj, B, S, conv_w[l], conv_b[l], w_rg_a[l], b_rg_a[l], w_rg_x[l], b_rg_x[l],
                   lru_lambda[l])
    y_att = _attention(proj, B, S, diff_lambda_q1[l], diff_lambda_k1[l], diff_lambda_q2[l],
                       diff_lambda_k2[l], subln_g[l], rel_bias_table, lam_init)
    x2, hf, top_idx, top_w = _merge_router(xt, y_rnn, y_att, proj, w_proj_rnn[l], w_proj_att[l],
                                           w_out[l], norm_ffn_g[l], w_router[l], b_router[l])
    blk = _moe_block_size(T)
    nb = T * TOP_K // blk + N_EXPERTS
    dest, counts = _route(top_idx, blk)
    ends, expert, src, active, first = _block_tables(counts[:, 0], blk, nb)
    x_slots = _dispatch(ends, dest, hf, nb * blk, blk)
    y_slots = _experts(x_slots, blk, expert, src, active, first, w_gate_e[l], b_gate_e[l], w_up_e[l],
                       b_up_e[l], w_down_e[l], b_down_e[l])
    out = _combine(dest, y_slots, top_w, x2, norm_final_g)
    return out.reshape(B, S, D)
```

```python
import functools
import math

import jax
import jax.numpy as jnp
from jax import lax
from jax.experimental import pallas as pl
from jax.experimental.pallas import tpu as pltpu

F32 = jnp.float32
BF16 = jnp.bfloat16
I32 = jnp.int32

D_MODEL = 1024
D_RNN = 1024
RNN_BLOCKS = 16
RNN_BLOCK = D_RNN // RNN_BLOCKS
CONV_W = 4
RGLRU_C = 8.0
N_HEADS = 8
HEAD_DIM = 64
V_DIM = 2 * HEAD_DIM
ATT_QK = N_HEADS * 2 * HEAD_DIM
ATT_V = N_HEADS * V_DIM
D_IN = 2 * D_RNN + 2 * ATT_QK + ATT_V + 2 * D_MODEL
NUM_BUCKETS = 32
MAX_EXACT = NUM_BUCKETS // 2
MAX_DISTANCE = 128
N_EXPERTS = 32
TOP_K = 4
D_FF = D_MODEL
SWIGLU_LIMIT = 7.0
SWIGLU_ALPHA = 1.702
EPS = 1e-6
NEG_INF = -1e30

LANES = 128
SUBLANES = 8
MXU_DIM = 256
VMEM_LIMIT = 56 * 1024 * 1024

TM_INPROJ = 1024
TN_INPROJ = 3584
TS_RGLRU = 256
ATT_TQ = 512
TM_MERGE = 512
TT_ROUTE = 1024
TT_DISPATCH = 512
TT_COMBINE = 512
MOE_BLOCK = 512
TOKEN_UNROLL = 4

COL_XR, COL_GR, COL_Q, COL_K, COL_V, COL_G0, COL_G1 = range(7)


def _params(sem, vmem=VMEM_LIMIT):
    return pltpu.CompilerParams(dimension_semantics=sem, vmem_limit_bytes=vmem)


ROW_TILES = D_MODEL // LANES
assert ROW_TILES == SUBLANES


def _store_token_major(ref, val, start=0):
    n = val.shape[0]
    for c in range(ROW_TILES):
        ref[pl.ds(start * ROW_TILES + c, n, stride=ROW_TILES), :] = val[:, c * LANES:(c + 1) * LANES]


def _load_token_major(ref, start, n):
    return jnp.concatenate(
        [ref[pl.ds(start * ROW_TILES + c, n, stride=ROW_TILES), :] for c in range(ROW_TILES)], axis=-1)


def _inproj_kernel(x_ref, g_ref, w_ref, o_ref, h_scr):
    @pl.when(pl.program_id(1) == 0)
    def _():
        x = x_ref[...]
        inv = lax.rsqrt(jnp.mean(x * x, axis=-1, keepdims=True) + EPS)
        h_scr[...] = (x * inv * g_ref[...]).astype(BF16)

    o_ref[...] = jnp.dot(h_scr[...], w_ref[...], preferred_element_type=F32).astype(o_ref.dtype)


def _inproj(x2d, g, w_bf16):
    T, D = x2d.shape
    N = w_bf16.shape[1]
    tm = min(TM_INPROJ, T)
    tn = TN_INPROJ if N % TN_INPROJ == 0 else D_MODEL
    return pl.pallas_call(
        _inproj_kernel,
        grid=(T // tm, N // tn),
        in_specs=[
            pl.BlockSpec((tm, D), lambda i, j: (i, 0)),
            pl.BlockSpec((1, D), lambda i, j: (0, 0)),
            pl.BlockSpec((D, tn), lambda i, j: (0, j)),
        ],
        out_specs=pl.BlockSpec((tm, tn), lambda i, j: (i, j)),
        out_shape=jax.ShapeDtypeStruct((T, N), BF16),
        scratch_shapes=[pltpu.VMEM((tm, D), BF16)],
        compiler_params=_params(("parallel", "arbitrary")),
        name="inproj",
    )(x2d, g.reshape(1, D), w_bf16)


def _rglru_kernel(xr_ref, gr_ref, cw_ref, cb_ref, wa_ref, ba_ref, wx_ref, bx_ref, lam_ref,
                  o_ref, tail_scr, a_scr, u_scr, h_scr):
    nb, ts, _ = xr_ref.shape
    pad = SUBLANES
    nslab = D_RNN // LANES

    @pl.when(pl.program_id(0) == 0)
    def _():
        tail_scr[...] = jnp.zeros_like(tail_scr)
        h_scr[...] = jnp.zeros_like(h_scr)

    cw = cw_ref[...]
    z = -lam_ref[...]
    softplus = jnp.maximum(z, 0.0) + jnp.log1p(jnp.exp(-jnp.abs(z)))
    sigmoid = lambda v: 0.5 * jnp.tanh(0.5 * v) + 0.5
    nchunk = D_RNN // MXU_DIM
    nshift = CONV_W - 1
    sr = lax.broadcasted_iota(I32, (nshift * ts, ts), 0)
    sc = lax.broadcasted_iota(I32, (nshift * ts, ts), 1)
    shift_mat = jnp.zeros((nshift * ts, ts), F32)
    for d in range(1, CONV_W):
        hit = (sr >= (d - 1) * ts) & (sr < d * ts) & (sr - (d - 1) * ts - d == sc)
        shift_mat = jnp.where(hit, 1.0, shift_mat)
    shift_mat = shift_mat.astype(BF16)
    row8 = lax.broadcasted_iota(I32, (pad, D_RNN), 0)
    for b in range(nb):
        xb = xr_ref[b]
        x = xb.astype(F32)
        shifted = jnp.dot(shift_mat, xb, preferred_element_type=F32)
        xc = cw[CONV_W - 1:CONV_W, :] * x + cb_ref[...]
        tail = tail_scr[b]
        head_fix = jnp.zeros((pad, D_RNN), F32)
        for d in range(1, CONV_W):
            wd = cw[CONV_W - 1 - d:CONV_W - d, :]
            xc = xc + wd * shifted[(d - 1) * ts:d * ts, :]
            head_fix = head_fix + wd * jnp.where(row8 < d, pltpu.roll(tail, d, 0), 0.0)
        xc = jnp.concatenate([xc[0:pad, :] + head_fix, xc[pad:ts, :]], axis=0)
        tail_scr[b] = x[ts - pad:ts, :]
        xcb = xc.astype(BF16)
        r_pre = jnp.concatenate(
            [jnp.dot(xcb[:, c * MXU_DIM:(c + 1) * MXU_DIM], wa_ref[c], preferred_element_type=F32)
             for c in range(nchunk)], axis=-1)
        i_pre = jnp.concatenate(
            [jnp.dot(xcb[:, c * MXU_DIM:(c + 1) * MXU_DIM], wx_ref[c], preferred_element_type=F32)
             for c in range(nchunk)], axis=-1)
        r = sigmoid(r_pre + ba_ref[...])
        ig = sigmoid(i_pre + bx_ref[...])
        a = jnp.exp((-RGLRU_C) * r * softplus)
        m2 = 1.0 - a * a
        mult = jnp.where(m2 > 0.0, m2 * lax.rsqrt(m2), 0.0)
        u = mult * (ig * xc)
        for c in range(nslab):
            a_scr[c, pl.ds(b, ts, stride=nb), :] = a[:, c * LANES:(c + 1) * LANES]
            u_scr[c, pl.ds(b, ts, stride=nb), :] = u[:, c * LANES:(c + 1) * LANES]

    def body(g, hs):
        for j in range(SCAN_UNROLL):
            off = pl.multiple_of((g * SCAN_UNROLL + j) * nb, nb)
            new = []
            for c in range(nslab):
                h = a_scr[c, pl.ds(off, nb), :] * hs[c] + u_scr[c, pl.ds(off, nb), :]
                u_scr[c, pl.ds(off, nb), :] = h
                new.append(h)
            hs = tuple(new)
        return hs

    hs = lax.fori_loop(0, ts // SCAN_UNROLL, body, tuple(h_scr[c] for c in range(nslab)))
    for c in range(nslab):
        h_scr[c] = hs[c]
    for b in range(nb):
        h = jnp.concatenate([u_scr[c, pl.ds(b, ts, stride=nb), :] for c in range(nslab)], axis=-1)
        gate = jax.nn.gelu(gr_ref[b].astype(F32), approximate=True)
        o_ref[b] = (h * gate).astype(o_ref.dtype)


SCAN_UNROLL = 8


def _block_diag_chunks(w):
    per = MXU_DIM // RNN_BLOCK
    w = w.reshape(D_RNN // MXU_DIM, per, RNN_BLOCK, RNN_BLOCK)
    eye = jnp.eye(per, dtype=w.dtype)
    out = jnp.einsum('gpcd,pq->gpcqd', w, eye)
    return out.reshape(D_RNN // MXU_DIM, MXU_DIM, MXU_DIM)


def _rglru(proj, B, S, conv_w, conv_b, w_a, b_a, w_x, b_x, lru_lambda):
    assert B <= SUBLANES, "all batch rows of a time step share one vreg in the scan"
    ts = min(TS_RGLRU, S)
    ns = S // ts
    wa = _block_diag_chunks(w_a).astype(BF16)
    wx = _block_diag_chunks(w_x).astype(BF16)
    nchunk = D_RNN // MXU_DIM
    nslab = D_RNN // LANES
    row = lambda v: v.reshape(1, D_RNN)
    const2 = lambda s: (0, 0)
    proj3 = proj.reshape(B, S, proj.shape[-1])
    out = pl.pallas_call(
        _rglru_kernel,
        grid=(ns,),
        in_specs=[
            pl.BlockSpec((B, ts, D_RNN), lambda s: (0, s, COL_XR)),
            pl.BlockSpec((B, ts, D_RNN), lambda s: (0, s, COL_GR)),
            pl.BlockSpec((CONV_W, D_RNN), const2),
            pl.BlockSpec((1, D_RNN), const2),
            pl.BlockSpec((nchunk, MXU_DIM, MXU_DIM), lambda s: (0, 0, 0)),
            pl.BlockSpec((1, D_RNN), const2),
            pl.BlockSpec((nchunk, MXU_DIM, MXU_DIM), lambda s: (0, 0, 0)),
            pl.BlockSpec((1, D_RNN), const2),
            pl.BlockSpec((1, D_RNN), const2),
        ],
        out_specs=pl.BlockSpec((B, ts, D_RNN), lambda s: (0, s, 0)),
        out_shape=jax.ShapeDtypeStruct((B, S, D_RNN), BF16),
        scratch_shapes=[
            pltpu.VMEM((B, SUBLANES, D_RNN), F32),
            pltpu.VMEM((nslab, ts * B, LANES), F32),
            pltpu.VMEM((nslab, ts * B, LANES), F32),
            pltpu.VMEM((nslab, B, LANES), F32),
        ],
        compiler_params=_params(("arbitrary",)),
        name="rglru",
    )(proj3, proj3, conv_w, row(conv_b), wa, row(b_a), wx, row(b_x), row(lru_lambda))
    return out.reshape(B * S, D_RNN)


def _attn_kernel(q_ref, k_ref, v_ref, bias_ref, lq1_ref, lk1_ref, lq2_ref, lk2_ref, sg_ref,
                 o_ref, qs_scr, vx_scr, s_scr, m_scr, acc_scr, *, tq, lam_init):
    S = q_ref.shape[0]
    nq = S // tq
    scale = HEAD_DIM ** -0.5 * LOG2E
    lane = lax.broadcasted_iota(I32, (tq, V_DIM), 1)
    lam = (jnp.exp(jnp.sum(lq1_ref[...] * lk1_ref[...], keepdims=True))
           - jnp.exp(jnp.sum(lq2_ref[...] * lk2_ref[...], keepdims=True)) + lam_init)
    vx_scr[:, 0:V_DIM] = v_ref[...]
    vx_scr[:, V_DIM:2 * V_DIM] = jnp.ones((S, V_DIM), BF16)

    Z0 = 2

    def prep_q(qi, buf):
        q = (q_ref[pl.ds(pl.multiple_of(qi * tq, tq), tq), :].astype(F32) * scale).astype(BF16)
        zero = jnp.zeros_like(q)
        qs_scr[buf, 0:tq, :] = jnp.where(lane < HEAD_DIM, q, zero)
        qs_scr[buf, tq:2 * tq, :] = jnp.where(lane >= HEAD_DIM, q, zero)

    def scores(buf, j, dst):
        k = k_ref[pl.ds(pl.multiple_of(j * tq, tq), tq), :]
        s_scr[dst] = lax.dot_general(qs_scr[buf], k, (((1,), (1,)), ((), ())),
                                     preferred_element_type=F32)

    def step(j, src, bias_idx, prefetch):
        scores(*prefetch)
        s = s_scr[src]
        if bias_idx is not None:
            b = bias_ref[bias_idx]
            s = s + jnp.concatenate([b, b], axis=0)
        vx = vx_scr[pl.ds(pl.multiple_of(j * tq, tq), tq), :]
        m_prev = m_scr[...]
        m_new = jnp.maximum(m_prev, jnp.max(s, axis=1)[:, None])
        p = jnp.exp2(s - jnp.tile(m_new, (1, tq // LANES)))
        alpha = jnp.exp2(m_prev - m_new)
        acc_scr[...] = (jnp.tile(alpha, (1, 2)) * acc_scr[...]
                        + jnp.dot(p.astype(BF16), vx, preferred_element_type=F32))
        m_scr[...] = m_new

    def begin_block():
        m_scr[...] = jnp.full(m_scr.shape, NEG_INF, F32)
        acc_scr[...] = jnp.zeros(acc_scr.shape, F32)

    def diag_step(qi, par, src):
        nxt = jnp.minimum(qi + 1, nq - 1)
        prep_q(nxt, 1 - par)
        step(qi, src, 0, (1 - par, 0, Z0 + 1 - par))

    def end_block(qi):
        acc = acc_scr[...]
        o_all = acc[:, 0:V_DIM] / acc[:, V_DIM:2 * V_DIM]
        o = o_all[0:tq, :] - lam * o_all[tq:2 * tq, :]
        inv = lax.rsqrt(jnp.mean(o * o, axis=-1, keepdims=True) + EPS)
        y = (o * inv * sg_ref[...]) * (1.0 - lam_init)
        o_ref[pl.ds(pl.multiple_of(qi * tq, tq), tq), :] = y.astype(o_ref.dtype)

    def far_pairs(par, npairs):
        def body(m, c):
            step(2 * m + 1, 0, None, (par, 2 * m + 2, 1))
            step(2 * m + 2, 1, None, (par, 2 * m + 3, 0))
            return c

        lax.fori_loop(0, npairs, body, 0)

    prep_q(0, 0)
    scores(0, 0, Z0)
    begin_block()
    diag_step(0, 0, Z0)
    end_block(0)
    begin_block()
    step(0, Z0 + 1, 1, (1, 1, 0))
    diag_step(1, 1, 0)
    end_block(1)

    def block_pair(i, carry):
        qi = 2 * i
        begin_block()
        step(0, Z0, None, (0, 1, 0))
        far_pairs(0, i - 1)
        step(qi - 1, 0, 1, (0, qi, 1))
        diag_step(qi, 0, 1)
        end_block(qi)
        qi = 2 * i + 1
        begin_block()
        step(0, Z0 + 1, None, (1, 1, 0))
        far_pairs(1, i - 1)
        step(qi - 2, 0, None, (1, qi - 1, 1))
        step(qi - 1, 1, 1, (1, qi, 0))
        diag_step(qi, 1, 0)
        end_block(qi)
        return carry

    lax.fori_loop(1, nq // 2, block_pair, 0)


def _rel_bucket(n):
    n = jnp.maximum(n, 0)
    nf = jnp.maximum(n, MAX_EXACT).astype(F32)
    large = MAX_EXACT + (jnp.log(nf / MAX_EXACT) / math.log(MAX_DISTANCE / MAX_EXACT)
                         * (NUM_BUCKETS - MAX_EXACT)).astype(I32)
    large = jnp.minimum(large, NUM_BUCKETS - 1)
    return jnp.where(n < MAX_EXACT, n, large)


def _bias_tiles(rel_table, tq):
    i = jnp.arange(tq, dtype=I32)[:, None]
    j = jnp.arange(tq, dtype=I32)[None, :]
    table = rel_table.astype(F32) - rel_table[NUM_BUCKETS - 1].astype(F32)[None, :]
    tiles = []
    for delta in (0, tq):
        n = i - j + delta
        onehot = (_rel_bucket(n)[:, :, None] == jnp.arange(NUM_BUCKETS, dtype=I32)).astype(F32)
        b = jnp.einsum('ijb,bh->hij', onehot, table, precision=lax.Precision.HIGHEST) * LOG2E
        tiles.append(jnp.where((n >= 0)[None], b, NEG_INF))
    return jnp.stack(tiles, axis=1)


LOG2E = math.log2(math.e)


def _attention(proj, B, S, lq1, lk1, lq2, lk2, subln_g, rel_table, lam_init):
    T = B * S
    tq = min(ATT_TQ, S // 2)
    assert V_DIM == LANES and MAX_DISTANCE <= tq and S % (2 * tq) == 0
    bias = _bias_tiles(rel_table, tq)
    vec = lambda v: v.reshape(1, -1).astype(F32)
    const2 = lambda b, h: (0, 0)
    kern = functools.partial(_attn_kernel, tq=tq, lam_init=lam_init)
    return pl.pallas_call(
        kern,
        grid=(B, N_HEADS),
        in_specs=[
            pl.BlockSpec((S, V_DIM), lambda b, h: (b, COL_Q * N_HEADS + h)),
            pl.BlockSpec((S, V_DIM), lambda b, h: (b, COL_K * N_HEADS + h)),
            pl.BlockSpec((S, V_DIM), lambda b, h: (b, COL_V * N_HEADS + h)),
            pl.BlockSpec((None, 2, tq, tq), lambda b, h: (h, 0, 0, 0)),
            pl.BlockSpec((1, HEAD_DIM), const2),
            pl.BlockSpec((1, HEAD_DIM), const2),
            pl.BlockSpec((1, HEAD_DIM), const2),
            pl.BlockSpec((1, HEAD_DIM), const2),
            pl.BlockSpec((1, V_DIM), const2),
        ],
        out_specs=pl.BlockSpec((S, V_DIM), lambda b, h: (b, h)),
        out_shape=jax.ShapeDtypeStruct((T, ATT_V), BF16),
        scratch_shapes=[
            pltpu.VMEM((2, 2 * tq, V_DIM), BF16),
            pltpu.VMEM((S, 2 * V_DIM), BF16),
            pltpu.VMEM((4, 2 * tq, tq), F32),
            pltpu.VMEM((2 * tq, LANES), F32),
            pltpu.VMEM((2 * tq, 2 * V_DIM), F32),
        ],
        compiler_params=_params(("parallel", "arbitrary")),
        name="diff_attn",
    )(proj, proj, proj, bias, vec(lq1), vec(lk1), vec(lq2), vec(lk2), vec(subln_g))


def _merge_kernel(x_ref, yr_ref, ya_ref, g0_ref, g1_ref, wr_ref, wa_ref, wo_ref, gf_ref, wrt_ref, brt_ref,
                  x2_ref, hf_ref, idx_ref, wgt_ref):
    pr = jnp.dot(yr_ref[...], wr_ref[...], preferred_element_type=F32)
    pa = jnp.dot(ya_ref[...], wa_ref[...], preferred_element_type=F32)
    merged = (jax.nn.sigmoid(g0_ref[...].astype(F32)) * pr
              + jax.nn.sigmoid(g1_ref[...].astype(F32)) * pa)
    x2 = x_ref[...] + jnp.dot(merged.astype(BF16), wo_ref[...], preferred_element_type=F32)
    x2_ref[...] = x2
    inv = lax.rsqrt(jnp.mean(x2 * x2, axis=-1, keepdims=True) + EPS)
    hf = x2 * inv * gf_ref[...]
    _store_token_major(hf_ref, hf)
    nt = (((1,), (1,)), ((), ()))
    hi = hf.astype(BF16)
    lo = (hf - hi.astype(F32)).astype(BF16)
    hw = lax.dot_general(wrt_ref[...], hi, nt, preferred_element_type=F32)
    lw = lax.dot_general(wrt_ref[0:LANES, :], lo, nt, preferred_element_type=F32)
    logits = (hw[0:N_EXPERTS, :] + hw[LANES:LANES + N_EXPERTS, :] + lw[0:N_EXPERTS, :]
              + brt_ref[...])
    tm = logits.shape[1]
    row = lax.broadcasted_iota(I32, (N_EXPERTS, tm), 0)
    work = logits
    idx_rows, val_rows = [], []
    for k in range(TOP_K):
        mx = jnp.max(work, axis=0, keepdims=True)
        sel = jnp.min(jnp.where(work == mx, row, N_EXPERTS), axis=0, keepdims=True)
        idx_rows.append(sel)
        val_rows.append(mx)
        work = jnp.where(row == sel, -jnp.inf, work)
    es = [jnp.exp(v - val_rows[0]) for v in val_rows]
    inv_sum = 1.0 / functools.reduce(lambda a, b: a + b, es)
    pad_i = [jnp.zeros((1, tm), I32)] * (SUBLANES - TOP_K)
    pad_f = [jnp.zeros((1, tm), F32)] * (SUBLANES - TOP_K)
    idx_ref[...] = jnp.concatenate(idx_rows + pad_i, axis=0)
    wgt_ref[...] = jnp.concatenate([e * inv_sum for e in es] + pad_f, axis=0)


def _merge_router(x2d, y_rnn, y_att, proj, w_pr, w_pa, w_o, g_ffn, w_router, b_router):
    T, D = x2d.shape
    tm = min(TM_MERGE, T)
    w_hi = w_router.astype(BF16)
    w_lo = (w_router - w_hi.astype(F32)).astype(BF16)
    wrt = (jnp.zeros((2 * LANES, D), BF16).at[:N_EXPERTS, :].set(w_hi.T)
           .at[LANES:LANES + N_EXPERTS, :].set(w_lo.T))
    brt = b_router.reshape(N_EXPERTS, 1).astype(F32)
    rowblk = lambda c: pl.BlockSpec((tm, D), lambda i, c=c: (i, c))
    full = lambda a: pl.BlockSpec(a.shape, lambda i: (0,) * a.ndim)
    wr, wa, wo = w_pr.astype(BF16), w_pa.astype(BF16), w_o.astype(BF16)
    gf = g_ffn.reshape(1, D)
    return pl.pallas_call(
        _merge_kernel,
        grid=(T // tm,),
        in_specs=[rowblk(0), rowblk(0), rowblk(0), rowblk(COL_G0), rowblk(COL_G1),
                  full(wr), full(wa), full(wo), full(gf), full(wrt), full(brt)],
        out_specs=[rowblk(0), pl.BlockSpec((tm * ROW_TILES, LANES), lambda i: (i, 0)),
                   pl.BlockSpec((SUBLANES, tm), lambda i: (0, i)),
                   pl.BlockSpec((SUBLANES, tm), lambda i: (0, i))],
        out_shape=[jax.ShapeDtypeStruct((T, D), F32), jax.ShapeDtypeStruct((T * ROW_TILES, LANES), F32),
                   jax.ShapeDtypeStruct((SUBLANES, T), I32), jax.ShapeDtypeStruct((SUBLANES, T), F32)],
        compiler_params=_params(("parallel",)),
        name="merge_router",
    )(x2d, y_rnn, y_att, proj, proj, wr, wa, wo, gf, wrt, brt)


def _route_kernel(idx_ref, dest_ref, cnt_ref, cnt_scr, run_scr, start_scr, *, blk):
    ph = pl.program_id(0)
    i = pl.program_id(1)
    tt = idx_ref.shape[1]
    idx = idx_ref[...]
    row = lax.broadcasted_iota(I32, (N_EXPERTS, tt), 0)
    onehot = jnp.zeros((N_EXPERTS, tt), F32)
    for k in range(TOP_K):
        onehot = onehot + (idx[k:k + 1, :] == row).astype(F32)
    tile_cnt = jnp.sum(onehot, axis=1, keepdims=True)

    @pl.when((ph == 0) & (i == 0))
    def _():
        cnt_scr[...] = jnp.zeros_like(cnt_scr)

    @pl.when(ph == 0)
    def _():
        cnt_scr[...] += tile_cnt

    @pl.when((ph == 1) & (i == 0))
    def _():
        padded = jnp.floor((cnt_scr[...] + (blk - 1)) / blk) * blk
        r = lax.broadcasted_iota(I32, (N_EXPERTS, N_EXPERTS), 0)
        c = lax.broadcasted_iota(I32, (N_EXPERTS, N_EXPERTS), 1)
        start_scr[...] = jnp.dot((c < r).astype(F32), padded, preferred_element_type=F32,
                                 precision=lax.Precision.HIGHEST)
        run_scr[...] = jnp.zeros_like(run_scr)

    @pl.when(ph == 1)
    def _():
        r = lax.broadcasted_iota(I32, (tt, tt), 0)
        c = lax.broadcasted_iota(I32, (tt, tt), 1)
        earlier = (r < c).astype(BF16)
        before = jnp.dot(onehot.astype(BF16), earlier, preferred_element_type=F32)
        base = before + run_scr[:, 0:1] + start_scr[:, 0:1]
        rows = [jnp.sum(jnp.where(idx[k:k + 1, :] == row, base, 0.0), axis=0, keepdims=True)
                for k in range(TOP_K)]
        rows += [jnp.zeros((1, tt), F32)] * (SUBLANES - TOP_K)
        dest_ref[...] = jnp.concatenate(rows, axis=0).astype(I32)
        run_scr[...] += tile_cnt
        cnt_ref[...] = cnt_scr[...]


def _route(top_idx, blk):
    T = top_idx.shape[1]
    tt = min(TT_ROUTE, T)
    kern = functools.partial(_route_kernel, blk=blk)
    return pl.pallas_call(
        kern,
        grid=(2, T // tt),
        in_specs=[pl.BlockSpec((SUBLANES, tt), lambda p, i: (0, i))],
        out_specs=[pl.BlockSpec((SUBLANES, tt), lambda p, i: (0, p * i)),
                   pl.BlockSpec((N_EXPERTS, LANES), lambda p, i: (0, 0))],
        out_shape=[jax.ShapeDtypeStruct((SUBLANES, T), I32),
                   jax.ShapeDtypeStruct((N_EXPERTS, LANES), F32)],
        scratch_shapes=[pltpu.VMEM((N_EXPERTS, LANES), F32), pltpu.VMEM((N_EXPERTS, LANES), F32),
                        pltpu.VMEM((N_EXPERTS, LANES), F32)],
        compiler_params=_params(("arbitrary", "arbitrary")),
        name="route",
    )(top_idx)


def _for_each_token_k(n_tok, fn):
    def body(g, c):
        for u in range(TOKEN_UNROLL):
            for k in range(TOP_K):
                fn(g * TOKEN_UNROLL + u, k)
        return c

    lax.fori_loop(0, n_tok // TOKEN_UNROLL, body, 0)


def _tile_major_ranks(dest, tt):
    T = dest.shape[1]
    return dest[:TOP_K].reshape(TOP_K, T // tt, tt).transpose(1, 0, 2).reshape(-1)


def _row_tile(ref, row):
    return ref.at[pl.ds(pl.multiple_of(row * ROW_TILES, ROW_TILES), ROW_TILES), :]


def _dispatch_kernel(ends_ref, dest_ref, hf_ref, xs_ref, zero_buf, sem, zsem, *, tt, blk):
    @pl.when(pl.program_id(0) == 0)
    def _():
        zero_buf[...] = jnp.zeros_like(zero_buf)

        def tail_fill(e):
            prev_end = ends_ref[e - 1] if e > 0 else 0
            start = pl.multiple_of((ends_ref[e] - blk) * ROW_TILES, ROW_TILES)
            return ends_ref[e] > prev_end, pltpu.make_async_copy(
                zero_buf, xs_ref.at[pl.ds(start, blk * ROW_TILES), :], zsem)

        for e in range(N_EXPERTS):
            nonempty, cp = tail_fill(e)
            pl.when(nonempty)(cp.start)
        for e in range(N_EXPERTS):
            nonempty, cp = tail_fill(e)
            pl.when(nonempty)(cp.wait)

    def row_copy(t, k):
        return pltpu.make_async_copy(_row_tile(hf_ref, t),
                                     _row_tile(xs_ref, dest_ref[k * tt + t]), sem)

    _for_each_token_k(tt, lambda t, k: row_copy(t, k).start(priority=k % 2))
    _for_each_token_k(tt, lambda t, k: row_copy(t, k).wait())


def _dispatch(ends, dest, hf_tm, n_slots, blk):
    T = hf_tm.shape[0] // ROW_TILES
    tt = min(TT_DISPATCH, T)
    kern = functools.partial(_dispatch_kernel, tt=tt, blk=blk)
    grid_spec = pltpu.PrefetchScalarGridSpec(
        num_scalar_prefetch=1,
        grid=(T // tt,),
        in_specs=[pl.BlockSpec((TOP_K * tt,), lambda i, ends: (i,), memory_space=pltpu.SMEM),
                  pl.BlockSpec((tt * ROW_TILES, LANES), lambda i, ends: (i, 0))],
        out_specs=pl.BlockSpec(memory_space=pl.ANY),
        scratch_shapes=[pltpu.VMEM((blk * ROW_TILES, LANES), F32),
                        pltpu.SemaphoreType.DMA(()), pltpu.SemaphoreType.DMA(())],
    )
    return pl.pallas_call(
        kern,
        grid_spec=grid_spec,
        out_shape=jax.ShapeDtypeStruct((n_slots * ROW_TILES, LANES), F32),
        compiler_params=_params(("arbitrary",)),
        name="dispatch",
    )(ends, _tile_major_ranks(dest, tt), hf_tm)


def _expert_kernel(be_ref, bsrc_ref, act_ref, first_ref, x_ref, wg_ref, bg_ref, wu_ref, bu_ref, wd_ref, bd_ref,
                   o_ref, wg_scr, wu_scr, wd_scr, *, blk):
    del be_ref, bsrc_ref
    i = pl.program_id(0)

    @pl.when(first_ref[i] == 1)
    def _():
        wg_scr[...] = wg_ref[0].astype(BF16)
        wu_scr[...] = wu_ref[0].astype(BF16)
        wd_scr[...] = wd_ref[0].astype(BF16)

    @pl.when(act_ref[i] == 1)
    def _():
        x = _load_token_major(x_ref, 0, blk).astype(BF16)
        g = jnp.dot(x, wg_scr[...], preferred_element_type=F32) + bg_ref[0]
        u = jnp.dot(x, wu_scr[...], preferred_element_type=F32) + bu_ref[0]
        g = jnp.minimum(g, SWIGLU_LIMIT)
        u = jnp.clip(u, -SWIGLU_LIMIT, SWIGLU_LIMIT)
        act = (u + 1.0) * (g * jax.nn.sigmoid(SWIGLU_ALPHA * g))
        y = jnp.dot(act.astype(BF16), wd_scr[...], preferred_element_type=F32) + bd_ref[0]
        _store_token_major(o_ref, y)


def _experts(x_slots, blk, block_expert, block_src, block_active, block_first, wg, bg, wu, bu, wd, bd):
    D = D_MODEL
    nb = x_slots.shape[0] // (blk * ROW_TILES)
    wspec = pl.BlockSpec((1, D, D_FF), lambda i, be, bs, ac, fi: (be[i], 0, 0))
    bspec = pl.BlockSpec((1, 1, D_FF), lambda i, be, bs, ac, fi: (be[i], 0, 0))
    xspec = pl.BlockSpec((blk * ROW_TILES, LANES), lambda i, be, bs, ac, fi: (bs[i], 0))
    grid_spec = pltpu.PrefetchScalarGridSpec(
        num_scalar_prefetch=4,
        grid=(nb,),
        in_specs=[xspec, wspec, bspec, wspec, bspec, wspec, bspec],
        out_specs=xspec,
        scratch_shapes=[pltpu.VMEM((D, D_FF), BF16), pltpu.VMEM((D, D_FF), BF16),
                        pltpu.VMEM((D_FF, D), BF16)],
    )
    b3 = lambda b: b.reshape(N_EXPERTS, 1, -1)
    return pl.pallas_call(
        functools.partial(_expert_kernel, blk=blk),
        grid_spec=grid_spec,
        out_shape=jax.ShapeDtypeStruct(x_slots.shape, F32),
        compiler_params=_params(("arbitrary",)),
        name="experts",
    )(block_expert, block_src, block_active, block_first, x_slots,
      wg, b3(bg), wu, b3(bu), wd, b3(bd))


def _combine_kernel(dest_ref, dest_next_ref, ys_ref, w_ref, x2_ref, g_ref, o_ref, buf, sems, *, tt):
    i = pl.program_id(0)
    n = pl.num_programs(0)
    cur = i % 2

    def row_copy(idx_ref, slot, t, k):
        return pltpu.make_async_copy(_row_tile(ys_ref, idx_ref[k * tt + t]),
                                     _row_tile(buf, (slot * TOP_K + k) * tt + t), sems.at[slot])

    def gather(idx_ref, slot):
        _for_each_token_k(tt, lambda t, k: row_copy(idx_ref, slot, t, k).start(priority=k % 2))

    @pl.when(i == 0)
    def _():
        gather(dest_ref, 0)

    @pl.when(i + 1 < n)
    def _():
        gather(dest_next_ref, 1 - cur)

    _for_each_token_k(tt, lambda t, k: row_copy(dest_ref, cur, t, k).wait())
    w = w_ref[...]
    y = x2_ref[...]
    for k in range(TOP_K):
        y = y + w[:, k:k + 1] * _load_token_major(buf, (cur * TOP_K + k) * tt, tt)
    inv = lax.rsqrt(jnp.mean(y * y, axis=-1, keepdims=True) + EPS)
    o_ref[...] = y * inv * g_ref[...]


def _combine(dest, y_slots, top_w, x2, g_final):
    T, D = x2.shape
    tt = min(TT_COMBINE, T)
    nt = T // tt
    kern = functools.partial(_combine_kernel, tt=tt)
    return pl.pallas_call(
        kern,
        grid=(nt,),
        in_specs=[pl.BlockSpec((TOP_K * tt,), lambda i: (i,), memory_space=pltpu.SMEM),
                  pl.BlockSpec((TOP_K * tt,), lambda i: (jnp.minimum(i + 1, nt - 1),),
                               memory_space=pltpu.SMEM),
                  pl.BlockSpec(memory_space=pl.ANY),
                  pl.BlockSpec((tt, SUBLANES), lambda i: (i, 0)),
                  pl.BlockSpec((tt, D), lambda i: (i, 0)),
                  pl.BlockSpec((1, D), lambda i: (0, 0))],
        out_specs=pl.BlockSpec((tt, D), lambda i: (i, 0)),
        out_shape=jax.ShapeDtypeStruct((T, D), F32),
        scratch_shapes=[pltpu.VMEM((2 * TOP_K * tt * ROW_TILES, LANES), F32),
                        pltpu.SemaphoreType.DMA((2,))],
        compiler_params=_params(("arbitrary",)),
        name="combine",
    )(_tile_major_ranks(dest, tt), _tile_major_ranks(dest, tt), y_slots, top_w.T, x2,
      g_final.reshape(1, D))


def _moe_block_size(T):
    return min(MOE_BLOCK, max(SUBLANES, T * TOP_K // N_EXPERTS))


def _block_tables(counts, blk, nb):
    cnt = counts.astype(I32)
    padded = (cnt + blk - 1) // blk * blk
    ends = jnp.cumsum(padded)
    used = ends[-1] // blk
    starts = jnp.arange(nb, dtype=I32) * blk
    src = jnp.minimum(jnp.arange(nb, dtype=I32), jnp.maximum(used - 1, 0))
    expert = jnp.minimum(jnp.sum((starts[:, None] >= ends[None, :]).astype(I32), axis=1), N_EXPERTS - 1)
    expert = expert[src]
    active = (jnp.arange(nb, dtype=I32) < used).astype(I32)
    prev = jnp.concatenate([jnp.full((1,), -1, I32), expert[:-1]])
    first = active * (expert != prev).astype(I32)
    return ends.astype(I32), expert, src, active, first


def kernel(x, norm_mix_g, w_in, conv_w, conv_b, w_rg_a, b_rg_a, w_rg_x, b_rg_x, lru_lambda, diff_lambda_q1, diff_lambda_k1, diff_lambda_q2, diff_lambda_k2, subln_g, rel_bias_table, w_proj_rnn, w_proj_att, w_out, norm_ffn_g, w_router, b_router, w_gate_e, b_gate_e, w_up_e, b_up_e, w_down_e, b_down_e, norm_final_g):
    B, S, D = x.shape
    T = B * S
    assert norm_mix_g.shape[0] == 1, "single-layer block: the final norm is fused into the MoE combine"
    l = 0
    xt = x.reshape(T, D)
    lam_init = 0.8 - 0.6 * math.exp(-0.3 * l)
    proj = _inproj(xt, norm_mix_g[l], w_in[l].astype(BF16))
    y_rnn = _rglru(proj, B, S, conv_w[l], conv_b[l], w_rg_a[l], b_rg_a[l], w_rg_x[l], b_rg_x[l],
                   lru_lambda[l])
    y_att = _attention(proj, B, S, diff_lambda_q1[l], diff_lambda_k1[l], diff_lambda_q2[l],
                       diff_lambda_k2[l], subln_g[l], rel_bias_table, lam_init)
    x2, hf, top_idx, top_w = _merge_router(xt, y_rnn, y_att, proj, w_proj_rnn[l], w_proj_att[l],
                                           w_out[l], norm_ffn_g[l], w_router[l], b_router[l])
    blk = _moe_block_size(T)
    nb = T * TOP_K // blk + N_EXPERTS
    dest, counts = _route(top_idx, blk)
    ends, expert, src, active, first = _block_tables(counts[:, 0], blk, nb)
    x_slots = _dispatch(ends, dest, hf, nb * blk, blk)
    y_slots = _experts(x_slots, blk, expert, src, active, first, w_gate_e[l], b_gate_e[l], w_up_e[l],
                       b_up_e[l], w_down_e[l], b_down_e[l])
    out = _combine(dest, y_slots, top_w, x2, norm_final_g)
    return out.reshape(B, S, D)
```

```python
import functools
import math

import jax
import jax.numpy as jnp
from jax import lax
from jax.experimental import pallas as pl
from jax.experimental.pallas import tpu as pltpu

F32 = jnp.float32
BF16 = jnp.bfloat16
I32 = jnp.int32

D_MODEL = 1024
D_RNN = 1024
RNN_BLOCKS = 16
RNN_BLOCK = D_RNN // RNN_BLOCKS
CONV_W = 4
RGLRU_C = 8.0
N_HEADS = 8
HEAD_DIM = 64
V_DIM = 2 * HEAD_DIM
ATT_QK = N_HEADS * 2 * HEAD_DIM
ATT_V = N_HEADS * V_DIM
D_IN = 2 * D_RNN + 2 * ATT_QK + ATT_V + 2 * D_MODEL
NUM_BUCKETS = 32
MAX_EXACT = NUM_BUCKETS // 2
MAX_DISTANCE = 128
N_EXPERTS = 32
TOP_K = 4
D_FF = D_MODEL
SWIGLU_LIMIT = 7.0
SWIGLU_ALPHA = 1.702
EPS = 1e-6
NEG_INF = -1e30

LANES = 128
SUBLANES = 8
MXU_DIM = 256
VMEM_LIMIT = 56 * 1024 * 1024

TM_INPROJ = 1024
TN_INPROJ = 3584
TS_RGLRU = 256
ATT_TQ = 512
TM_MERGE = 512
TT_ROUTE = 1024
TT_DISPATCH = 1024
TT_COMBINE = 512
MOE_BLOCK = 512
TOKEN_UNROLL = 2

COL_XR, COL_GR, COL_Q, COL_K, COL_V, COL_G0, COL_G1 = range(7)


def _params(sem, vmem=VMEM_LIMIT):
    return pltpu.CompilerParams(dimension_semantics=sem, vmem_limit_bytes=vmem)


ROW_TILES = D_MODEL // LANES
assert ROW_TILES == SUBLANES


def _store_token_major(ref, val, start=0):
    n = val.shape[0]
    for c in range(ROW_TILES):
        ref[pl.ds(start * ROW_TILES + c, n, stride=ROW_TILES), :] = val[:, c * LANES:(c + 1) * LANES]


def _load_token_major(ref, start, n):
    return jnp.concatenate(
        [ref[pl.ds(start * ROW_TILES + c, n, stride=ROW_TILES), :] for c in range(ROW_TILES)], axis=-1)


def _inproj_kernel(x_ref, g_ref, w_ref, o_ref, h_scr):
    @pl.when(pl.program_id(1) == 0)
    def _():
        x = x_ref[...]
        inv = lax.rsqrt(jnp.mean(x * x, axis=-1, keepdims=True) + EPS)
        h_scr[...] = (x * inv * g_ref[...]).astype(BF16)

    o_ref[...] = jnp.dot(h_scr[...], w_ref[...], preferred_element_type=F32).astype(o_ref.dtype)


def _inproj(x2d, g, w_bf16):
    T, D = x2d.shape
    N = w_bf16.shape[1]
    tm = min(TM_INPROJ, T)
    tn = TN_INPROJ if N % TN_INPROJ == 0 else D_MODEL
    return pl.pallas_call(
        _inproj_kernel,
        grid=(T // tm, N // tn),
        in_specs=[
            pl.BlockSpec((tm, D), lambda i, j: (i, 0)),
            pl.BlockSpec((1, D), lambda i, j: (0, 0)),
            pl.BlockSpec((D, tn), lambda i, j: (0, j)),
        ],
        out_specs=pl.BlockSpec((tm, tn), lambda i, j: (i, j)),
        out_shape=jax.ShapeDtypeStruct((T, N), BF16),
        scratch_shapes=[pltpu.VMEM((tm, D), BF16)],
        compiler_params=_params(("parallel", "arbitrary")),
        name="inproj",
    )(x2d, g.reshape(1, D), w_bf16)


def _rglru_kernel(xr_ref, gr_ref, cw_ref, cb_ref, wa_ref, ba_ref, wx_ref, bx_ref, lam_ref,
                  o_ref, tail_scr, a_scr, u_scr, h_scr):
    nb, ts, _ = xr_ref.shape
    pad = SUBLANES
    nslab = D_RNN // LANES

    @pl.when(pl.program_id(0) == 0)
    def _():
        tail_scr[...] = jnp.zeros_like(tail_scr)
        h_scr[...] = jnp.zeros_like(h_scr)

    cw = cw_ref[...]
    z = -lam_ref[...]
    softplus = jnp.maximum(z, 0.0) + jnp.log1p(jnp.exp(-jnp.abs(z)))
    sigmoid = lambda v: 0.5 * jnp.tanh(0.5 * v) + 0.5
    nchunk = D_RNN // MXU_DIM
    nshift = CONV_W - 1
    sr = lax.broadcasted_iota(I32, (nshift * ts, ts), 0)
    sc = lax.broadcasted_iota(I32, (nshift * ts, ts), 1)
    shift_mat = jnp.zeros((nshift * ts, ts), F32)
    for d in range(1, CONV_W):
        hit = (sr >= (d - 1) * ts) & (sr < d * ts) & (sr - (d - 1) * ts - d == sc)
        shift_mat = jnp.where(hit, 1.0, shift_mat)
    shift_mat = shift_mat.astype(BF16)
    row8 = lax.broadcasted_iota(I32, (pad, D_RNN), 0)
    for b in range(nb):
        xb = xr_ref[b]
        x = xb.astype(F32)
        shifted = jnp.dot(shift_mat, xb, preferred_element_type=F32)
        xc = cw[CONV_W - 1:CONV_W, :] * x + cb_ref[...]
        tail = tail_scr[b]
        head_fix = jnp.zeros((pad, D_RNN), F32)
        for d in range(1, CONV_W):
            wd = cw[CONV_W - 1 - d:CONV_W - d, :]
            xc = xc + wd * shifted[(d - 1) * ts:d * ts, :]
            head_fix = head_fix + wd * jnp.where(row8 < d, pltpu.roll(tail, d, 0), 0.0)
        xc = jnp.concatenate([xc[0:pad, :] + head_fix, xc[pad:ts, :]], axis=0)
        tail_scr[b] = x[ts - pad:ts, :]
        xcb = xc.astype(BF16)
        r_pre = jnp.concatenate(
            [jnp.dot(xcb[:, c * MXU_DIM:(c + 1) * MXU_DIM], wa_ref[c], preferred_element_type=F32)
             for c in range(nchunk)], axis=-1)
        i_pre = jnp.concatenate(
            [jnp.dot(xcb[:, c * MXU_DIM:(c + 1) * MXU_DIM], wx_ref[c], preferred_element_type=F32)
             for c in range(nchunk)], axis=-1)
        r = sigmoid(r_pre + ba_ref[...])
        ig = sigmoid(i_pre + bx_ref[...])
        a = jnp.exp((-RGLRU_C) * r * softplus)
        m2 = 1.0 - a * a
        mult = jnp.where(m2 > 0.0, m2 * lax.rsqrt(m2), 0.0)
        u = mult * (ig * xc)
        for c in range(nslab):
            a_scr[c, pl.ds(b, ts, stride=nb), :] = a[:, c * LANES:(c + 1) * LANES]
            u_scr[c, pl.ds(b, ts, stride=nb), :] = u[:, c * LANES:(c + 1) * LANES]

    def body(g, hs):
        for j in range(SCAN_UNROLL):
            off = pl.multiple_of((g * SCAN_UNROLL + j) * nb, nb)
            new = []
            for c in range(nslab):
                h = a_scr[c, pl.ds(off, nb), :] * hs[c] + u_scr[c, pl.ds(off, nb), :]
                u_scr[c, pl.ds(off, nb), :] = h
                new.append(h)
            hs = tuple(new)
        return hs

    hs = lax.fori_loop(0, ts // SCAN_UNROLL, body, tuple(h_scr[c] for c in range(nslab)))
    for c in range(nslab):
        h_scr[c] = hs[c]
    for b in range(nb):
        h = jnp.concatenate([u_scr[c, pl.ds(b, ts, stride=nb), :] for c in range(nslab)], axis=-1)
        gate = jax.nn.gelu(gr_ref[b].astype(F32), approximate=True)
        o_ref[b] = (h * gate).astype(o_ref.dtype)


SCAN_UNROLL = 8


def _block_diag_chunks(w):
    per = MXU_DIM // RNN_BLOCK
    w = w.reshape(D_RNN // MXU_DIM, per, RNN_BLOCK, RNN_BLOCK)
    eye = jnp.eye(per, dtype=w.dtype)
    out = jnp.einsum('gpcd,pq->gpcqd', w, eye)
    return out.reshape(D_RNN // MXU_DIM, MXU_DIM, MXU_DIM)


def _rglru(proj, B, S, conv_w, conv_b, w_a, b_a, w_x, b_x, lru_lambda):
    assert B <= SUBLANES, "all batch rows of a time step share one vreg in the scan"
    ts = min(TS_RGLRU, S)
    ns = S // ts
    wa = _block_diag_chunks(w_a).astype(BF16)
    wx = _block_diag_chunks(w_x).astype(BF16)
    nchunk = D_RNN // MXU_DIM
    nslab = D_RNN // LANES
    row = lambda v: v.reshape(1, D_RNN)
    const2 = lambda s: (0, 0)
    proj3 = proj.reshape(B, S, proj.shape[-1])
    out = pl.pallas_call(
        _rglru_kernel,
        grid=(ns,),
        in_specs=[
            pl.BlockSpec((B, ts, D_RNN), lambda s: (0, s, COL_XR)),
            pl.BlockSpec((B, ts, D_RNN), lambda s: (0, s, COL_GR)),
            pl.BlockSpec((CONV_W, D_RNN), const2),
            pl.BlockSpec((1, D_RNN), const2),
            pl.BlockSpec((nchunk, MXU_DIM, MXU_DIM), lambda s: (0, 0, 0)),
            pl.BlockSpec((1, D_RNN), const2),
            pl.BlockSpec((nchunk, MXU_DIM, MXU_DIM), lambda s: (0, 0, 0)),
            pl.BlockSpec((1, D_RNN), const2),
            pl.BlockSpec((1, D_RNN), const2),
        ],
        out_specs=pl.BlockSpec((B, ts, D_RNN), lambda s: (0, s, 0)),
        out_shape=jax.ShapeDtypeStruct((B, S, D_RNN), BF16),
        scratch_shapes=[
            pltpu.VMEM((B, SUBLANES, D_RNN), F32),
            pltpu.VMEM((nslab, ts * B, LANES), F32),
            pltpu.VMEM((nslab, ts * B, LANES), F32),
            pltpu.VMEM((nslab, B, LANES), F32),
        ],
        compiler_params=_params(("arbitrary",)),
        name="rglru",
    )(proj3, proj3, conv_w, row(conv_b), wa, row(b_a), wx, row(b_x), row(lru_lambda))
    return out.reshape(B * S, D_RNN)


def _attn_kernel(q_ref, k_ref, v_ref, bias_ref, lq1_ref, lk1_ref, lq2_ref, lk2_ref, sg_ref,
                 o_ref, qs_scr, vx_scr, s_scr, m_scr, acc_scr, *, tq, lam_init):
    S = q_ref.shape[0]
    nq = S // tq
    scale = HEAD_DIM ** -0.5 * LOG2E
    lane = lax.broadcasted_iota(I32, (tq, V_DIM), 1)
    lam = (jnp.exp(jnp.sum(lq1_ref[...] * lk1_ref[...], keepdims=True))
           - jnp.exp(jnp.sum(lq2_ref[...] * lk2_ref[...], keepdims=True)) + lam_init)
    vx_scr[:, 0:V_DIM] = v_ref[...]
    vx_scr[:, V_DIM:2 * V_DIM] = jnp.ones((S, V_DIM), BF16)

    Z0 = 2

    def prep_q(qi, buf):
        q = (q_ref[pl.ds(pl.multiple_of(qi * tq, tq), tq), :].astype(F32) * scale).astype(BF16)
        zero = jnp.zeros_like(q)
        qs_scr[buf, 0:tq, :] = jnp.where(lane < HEAD_DIM, q, zero)
        qs_scr[buf, tq:2 * tq, :] = jnp.where(lane >= HEAD_DIM, q, zero)

    def scores(buf, j, dst):
        k = k_ref[pl.ds(pl.multiple_of(j * tq, tq), tq), :]
        s_scr[dst] = lax.dot_general(qs_scr[buf], k, (((1,), (1,)), ((), ())),
                                     preferred_element_type=F32)

    def step(j, src, bias_idx, prefetch):
        scores(*prefetch)
        s = s_scr[src]
        if bias_idx is not None:
            b = bias_ref[bias_idx]
            s = s + jnp.concatenate([b, b], axis=0)
        vx = vx_scr[pl.ds(pl.multiple_of(j * tq, tq), tq), :]
        m_prev = m_scr[...]
        m_new = jnp.maximum(m_prev, jnp.max(s, axis=1)[:, None])
        p = jnp.exp2(s - jnp.tile(m_new, (1, tq // LANES)))
        alpha = jnp.exp2(m_prev - m_new)
        acc_scr[...] = (jnp.tile(alpha, (1, 2)) * acc_scr[...]
                        + jnp.dot(p.astype(BF16), vx, preferred_element_type=F32))
        m_scr[...] = m_new

    def begin_block():
        m_scr[...] = jnp.full(m_scr.shape, NEG_INF, F32)
        acc_scr[...] = jnp.zeros(acc_scr.shape, F32)

    def diag_step(qi, par, src):
        nxt = jnp.minimum(qi + 1, nq - 1)
        prep_q(nxt, 1 - par)
        step(qi, src, 0, (1 - par, 0, Z0 + 1 - par))

    def end_block(qi):
        acc = acc_scr[...]
        o_all = acc[:, 0:V_DIM] / acc[:, V_DIM:2 * V_DIM]
        o = o_all[0:tq, :] - lam * o_all[tq:2 * tq, :]
        inv = lax.rsqrt(jnp.mean(o * o, axis=-1, keepdims=True) + EPS)
        y = (o * inv * sg_ref[...]) * (1.0 - lam_init)
        o_ref[pl.ds(pl.multiple_of(qi * tq, tq), tq), :] = y.astype(o_ref.dtype)

    def far_pairs(par, npairs):
        def body(m, c):
            step(2 * m + 1, 0, None, (par, 2 * m + 2, 1))
            step(2 * m + 2, 1, None, (par, 2 * m + 3, 0))
            return c

        lax.fori_loop(0, npairs, body, 0)

    prep_q(0, 0)
    scores(0, 0, Z0)
    begin_block()
    diag_step(0, 0, Z0)
    end_block(0)
    begin_block()
    step(0, Z0 + 1, 1, (1, 1, 0))
    diag_step(1, 1, 0)
    end_block(1)

    def block_pair(i, carry):
        qi = 2 * i
        begin_block()
        step(0, Z0, None, (0, 1, 0))
        far_pairs(0, i - 1)
        step(qi - 1, 0, 1, (0, qi, 1))
        diag_step(qi, 0, 1)
        end_block(qi)
        qi = 2 * i + 1
        begin_block()
        step(0, Z0 + 1, None, (1, 1, 0))
        far_pairs(1, i - 1)
        step(qi - 2, 0, None, (1, qi - 1, 1))
        step(qi - 1, 1, 1, (1, qi, 0))
        diag_step(qi, 1, 0)
        end_block(qi)
        return carry

    lax.fori_loop(1, nq // 2, block_pair, 0)


def _rel_bucket(n):
    n = jnp.maximum(n, 0)
    nf = jnp.maximum(n, MAX_EXACT).astype(F32)
    large = MAX_EXACT + (jnp.log(nf / MAX_EXACT) / math.log(MAX_DISTANCE / MAX_EXACT)
                         * (NUM_BUCKETS - MAX_EXACT)).astype(I32)
    large = jnp.minimum(large, NUM_BUCKETS - 1)
    return jnp.where(n < MAX_EXACT, n, large)


def _bias_tiles(rel_table, tq):
    i = jnp.arange(tq, dtype=I32)[:, None]
    j = jnp.arange(tq, dtype=I32)[None, :]
    table = rel_table.astype(F32) - rel_table[NUM_BUCKETS - 1].astype(F32)[None, :]
    tiles = []
    for delta in (0, tq):
        n = i - j + delta
        onehot = (_rel_bucket(n)[:, :, None] == jnp.arange(NUM_BUCKETS, dtype=I32)).astype(F32)
        b = jnp.einsum('ijb,bh->hij', onehot, table, precision=lax.Precision.HIGHEST) * LOG2E
        tiles.append(jnp.where((n >= 0)[None], b, NEG_INF))
    return jnp.stack(tiles, axis=1)


LOG2E = math.log2(math.e)


def _attention(proj, B, S, lq1, lk1, lq2, lk2, subln_g, rel_table, lam_init):
    T = B * S
    tq = min(ATT_TQ, S // 2)
    assert V_DIM == LANES and MAX_DISTANCE <= tq and S % (2 * tq) == 0
    bias = _bias_tiles(rel_table, tq)
    vec = lambda v: v.reshape(1, -1).astype(F32)
    const2 = lambda b, h: (0, 0)
    kern = functools.partial(_attn_kernel, tq=tq, lam_init=lam_init)
    return pl.pallas_call(
        kern,
        grid=(B, N_HEADS),
        in_specs=[
            pl.BlockSpec((S, V_DIM), lambda b, h: (b, COL_Q * N_HEADS + h)),
            pl.BlockSpec((S, V_DIM), lambda b, h: (b, COL_K * N_HEADS + h)),
            pl.BlockSpec((S, V_DIM), lambda b, h: (b, COL_V * N_HEADS + h)),
            pl.BlockSpec((None, 2, tq, tq), lambda b, h: (h, 0, 0, 0)),
            pl.BlockSpec((1, HEAD_DIM), const2),
            pl.BlockSpec((1, HEAD_DIM), const2),
            pl.BlockSpec((1, HEAD_DIM), const2),
            pl.BlockSpec((1, HEAD_DIM), const2),
            pl.BlockSpec((1, V_DIM), const2),
        ],
        out_specs=pl.BlockSpec((S, V_DIM), lambda b, h: (b, h)),
        out_shape=jax.ShapeDtypeStruct((T, ATT_V), BF16),
        scratch_shapes=[
            pltpu.VMEM((2, 2 * tq, V_DIM), BF16),
            pltpu.VMEM((S, 2 * V_DIM), BF16),
            pltpu.VMEM((4, 2 * tq, tq), F32),
            pltpu.VMEM((2 * tq, LANES), F32),
            pltpu.VMEM((2 * tq, 2 * V_DIM), F32),
        ],
        compiler_params=_params(("parallel", "arbitrary")),
        name="diff_attn",
    )(proj, proj, proj, bias, vec(lq1), vec(lk1), vec(lq2), vec(lk2), vec(subln_g))


def _merge_kernel(x_ref, yr_ref, ya_ref, g0_ref, g1_ref, wr_ref, wa_ref, wo_ref, gf_ref, wrt_ref, brt_ref,
                  x2_ref, hf_ref, idx_ref, wgt_ref):
    pr = jnp.dot(yr_ref[...], wr_ref[...], preferred_element_type=F32)
    pa = jnp.dot(ya_ref[...], wa_ref[...], preferred_element_type=F32)
    merged = (jax.nn.sigmoid(g0_ref[...].astype(F32)) * pr
              + jax.nn.sigmoid(g1_ref[...].astype(F32)) * pa)
    x2 = x_ref[...] + jnp.dot(merged.astype(BF16), wo_ref[...], preferred_element_type=F32)
    x2_ref[...] = x2
    inv = lax.rsqrt(jnp.mean(x2 * x2, axis=-1, keepdims=True) + EPS)
    hf = x2 * inv * gf_ref[...]
    _store_token_major(hf_ref, hf)
    nt = (((1,), (1,)), ((), ()))
    hi = hf.astype(BF16)
    lo = (hf - hi.astype(F32)).astype(BF16)
    hw = lax.dot_general(wrt_ref[...], hi, nt, preferred_element_type=F32)
    lw = lax.dot_general(wrt_ref[0:LANES, :], lo, nt, preferred_element_type=F32)
    logits = (hw[0:N_EXPERTS, :] + hw[LANES:LANES + N_EXPERTS, :] + lw[0:N_EXPERTS, :]
              + brt_ref[...])
    tm = logits.shape[1]
    row = lax.broadcasted_iota(I32, (N_EXPERTS, tm), 0)
    work = logits
    idx_rows, val_rows = [], []
    for k in range(TOP_K):
        mx = jnp.max(work, axis=0, keepdims=True)
        sel = jnp.min(jnp.where(work == mx, row, N_EXPERTS), axis=0, keepdims=True)
        idx_rows.append(sel)
        val_rows.append(mx)
        work = jnp.where(row == sel, -jnp.inf, work)
    es = [jnp.exp(v - val_rows[0]) for v in val_rows]
    inv_sum = 1.0 / functools.reduce(lambda a, b: a + b, es)
    pad_i = [jnp.zeros((1, tm), I32)] * (SUBLANES - TOP_K)
    pad_f = [jnp.zeros((1, tm), F32)] * (SUBLANES - TOP_K)
    idx_ref[...] = jnp.concatenate(idx_rows + pad_i, axis=0)
    wgt_ref[...] = jnp.concatenate([e * inv_sum for e in es] + pad_f, axis=0)


def _merge_router(x2d, y_rnn, y_att, proj, w_pr, w_pa, w_o, g_ffn, w_router, b_router):
    T, D = x2d.shape
    tm = min(TM_MERGE, T)
    w_hi = w_router.astype(BF16)
    w_lo = (w_router - w_hi.astype(F32)).astype(BF16)
    wrt = (jnp.zeros((2 * LANES, D), BF16).at[:N_EXPERTS, :].set(w_hi.T)
           .at[LANES:LANES + N_EXPERTS, :].set(w_lo.T))
    brt = b_router.reshape(N_EXPERTS, 1).astype(F32)
    rowblk = lambda c: pl.BlockSpec((tm, D), lambda i, c=c: (i, c))
    full = lambda a: pl.BlockSpec(a.shape, lambda i: (0,) * a.ndim)
    wr, wa, wo = w_pr.astype(BF16), w_pa.astype(BF16), w_o.astype(BF16)
    gf = g_ffn.reshape(1, D)
    return pl.pallas_call(
        _merge_kernel,
        grid=(T // tm,),
        in_specs=[rowblk(0), rowblk(0), rowblk(0), rowblk(COL_G0), rowblk(COL_G1),
                  full(wr), full(wa), full(wo), full(gf), full(wrt), full(brt)],
        out_specs=[rowblk(0), pl.BlockSpec((tm * ROW_TILES, LANES), lambda i: (i, 0)),
                   pl.BlockSpec((SUBLANES, tm), lambda i: (0, i)),
                   pl.BlockSpec((SUBLANES, tm), lambda i: (0, i))],
        out_shape=[jax.ShapeDtypeStruct((T, D), F32), jax.ShapeDtypeStruct((T * ROW_TILES, LANES), F32),
                   jax.ShapeDtypeStruct((SUBLANES, T), I32), jax.ShapeDtypeStruct((SUBLANES, T), F32)],
        compiler_params=_params(("parallel",)),
        name="merge_router",
    )(x2d, y_rnn, y_att, proj, proj, wr, wa, wo, gf, wrt, brt)


def _route_kernel(idx_ref, dest_ref, cnt_ref, cnt_scr, run_scr, start_scr, *, blk):
    ph = pl.program_id(0)
    i = pl.program_id(1)
    tt = idx_ref.shape[1]
    idx = idx_ref[...]
    row = lax.broadcasted_iota(I32, (N_EXPERTS, tt), 0)
    onehot = jnp.zeros((N_EXPERTS, tt), F32)
    for k in range(TOP_K):
        onehot = onehot + (idx[k:k + 1, :] == row).astype(F32)
    tile_cnt = jnp.sum(onehot, axis=1, keepdims=True)

    @pl.when((ph == 0) & (i == 0))
    def _():
        cnt_scr[...] = jnp.zeros_like(cnt_scr)

    @pl.when(ph == 0)
    def _():
        cnt_scr[...] += tile_cnt

    @pl.when((ph == 1) & (i == 0))
    def _():
        padded = jnp.floor((cnt_scr[...] + (blk - 1)) / blk) * blk
        r = lax.broadcasted_iota(I32, (N_EXPERTS, N_EXPERTS), 0)
        c = lax.broadcasted_iota(I32, (N_EXPERTS, N_EXPERTS), 1)
        start_scr[...] = jnp.dot((c < r).astype(F32), padded, preferred_element_type=F32,
                                 precision=lax.Precision.HIGHEST)
        run_scr[...] = jnp.zeros_like(run_scr)

    @pl.when(ph == 1)
    def _():
        r = lax.broadcasted_iota(I32, (tt, tt), 0)
        c = lax.broadcasted_iota(I32, (tt, tt), 1)
        earlier = (r < c).astype(BF16)
        before = jnp.dot(onehot.astype(BF16), earlier, preferred_element_type=F32)
        base = before + run_scr[:, 0:1] + start_scr[:, 0:1]
        rows = [jnp.sum(jnp.where(idx[k:k + 1, :] == row, base, 0.0), axis=0, keepdims=True)
                for k in range(TOP_K)]
        rows += [jnp.zeros((1, tt), F32)] * (SUBLANES - TOP_K)
        dest_ref[...] = jnp.concatenate(rows, axis=0).astype(I32)
        run_scr[...] += tile_cnt
        cnt_ref[...] = cnt_scr[...]


def _route(top_idx, blk):
    T = top_idx.shape[1]
    tt = min(TT_ROUTE, T)
    kern = functools.partial(_route_kernel, blk=blk)
    return pl.pallas_call(
        kern,
        grid=(2, T // tt),
        in_specs=[pl.BlockSpec((SUBLANES, tt), lambda p, i: (0, i))],
        out_specs=[pl.BlockSpec((SUBLANES, tt), lambda p, i: (0, p * i)),
                   pl.BlockSpec((N_EXPERTS, LANES), lambda p, i: (0, 0))],
        out_shape=[jax.ShapeDtypeStruct((SUBLANES, T), I32),
                   jax.ShapeDtypeStruct((N_EXPERTS, LANES), F32)],
        scratch_shapes=[pltpu.VMEM((N_EXPERTS, LANES), F32), pltpu.VMEM((N_EXPERTS, LANES), F32),
                        pltpu.VMEM((N_EXPERTS, LANES), F32)],
        compiler_params=_params(("arbitrary", "arbitrary")),
        name="route",
    )(top_idx)


def _for_each_token_k(n_tok, fn):
    def body(g, c):
        for u in range(TOKEN_UNROLL):
            for k in range(TOP_K):
                fn(g * TOKEN_UNROLL + u, k)
        return c

    lax.fori_loop(0, n_tok // TOKEN_UNROLL, body, 0)


def _tile_major_ranks(dest, tt):
    T = dest.shape[1]
    return dest[:TOP_K].reshape(TOP_K, T // tt, tt).transpose(1, 0, 2).reshape(-1)


def _row_tile(ref, row):
    return ref.at[pl.ds(pl.multiple_of(row * ROW_TILES, ROW_TILES), ROW_TILES), :]


def _dispatch_kernel(ends_ref, dest_ref, hf_ref, xs_ref, zero_buf, sem, zsem, *, tt, blk):
    @pl.when(pl.program_id(0) == 0)
    def _():
        zero_buf[...] = jnp.zeros_like(zero_buf)

        def tail_fill(e):
            prev_end = ends_ref[e - 1] if e > 0 else 0
            start = pl.multiple_of((ends_ref[e] - blk) * ROW_TILES, ROW_TILES)
            return ends_ref[e] > prev_end, pltpu.make_async_copy(
                zero_buf, xs_ref.at[pl.ds(start, blk * ROW_TILES), :], zsem)

        for e in range(N_EXPERTS):
            nonempty, cp = tail_fill(e)
            pl.when(nonempty)(cp.start)
        for e in range(N_EXPERTS):
            nonempty, cp = tail_fill(e)
            pl.when(nonempty)(cp.wait)

    def row_copy(t, k):
        return pltpu.make_async_copy(_row_tile(hf_ref, t),
                                     _row_tile(xs_ref, dest_ref[k * tt + t]), sem)

    _for_each_token_k(tt, lambda t, k: row_copy(t, k).start(priority=k % 2))
    _for_each_token_k(tt, lambda t, k: row_copy(t, k).wait())


def _dispatch(ends, dest, hf_tm, n_slots, blk):
    T = hf_tm.shape[0] // ROW_TILES
    tt = min(TT_DISPATCH, T)
    kern = functools.partial(_dispatch_kernel, tt=tt, blk=blk)
    grid_spec = pltpu.PrefetchScalarGridSpec(
        num_scalar_prefetch=1,
        grid=(T // tt,),
        in_specs=[pl.BlockSpec((TOP_K * tt,), lambda i, ends: (i,), memory_space=pltpu.SMEM),
                  pl.BlockSpec((tt * ROW_TILES, LANES), lambda i, ends: (i, 0))],
        out_specs=pl.BlockSpec(memory_space=pl.ANY),
        scratch_shapes=[pltpu.VMEM((blk * ROW_TILES, LANES), F32),
                        pltpu.SemaphoreType.DMA(()), pltpu.SemaphoreType.DMA(())],
    )
    return pl.pallas_call(
        kern,
        grid_spec=grid_spec,
        out_shape=jax.ShapeDtypeStruct((n_slots * ROW_TILES, LANES), F32),
        compiler_params=_params(("arbitrary",)),
        name="dispatch",
    )(ends, _tile_major_ranks(dest, tt), hf_tm)


def _expert_kernel(be_ref, bsrc_ref, act_ref, first_ref, x_ref, wg_ref, bg_ref, wu_ref, bu_ref, wd_ref, bd_ref,
                   o_ref, wg_scr, wu_scr, wd_scr, *, blk):
    del be_ref, bsrc_ref
    i = pl.program_id(0)

    @pl.when(first_ref[i] == 1)
    def _():
        wg_scr[...] = wg_ref[0].astype(BF16)
        wu_scr[...] = wu_ref[0].astype(BF16)
        wd_scr[...] = wd_ref[0].astype(BF16)

    @pl.when(act_ref[i] == 1)
    def _():
        x = _load_token_major(x_ref, 0, blk).astype(BF16)
        g = jnp.dot(x, wg_scr[...], preferred_element_type=F32) + bg_ref[0]
        u = jnp.dot(x, wu_scr[...], preferred_element_type=F32) + bu_ref[0]
        g = jnp.minimum(g, SWIGLU_LIMIT)
        u = jnp.clip(u, -SWIGLU_LIMIT, SWIGLU_LIMIT)
        act = (u + 1.0) * (g * jax.nn.sigmoid(SWIGLU_ALPHA * g))
        y = jnp.dot(act.astype(BF16), wd_scr[...], preferred_element_type=F32) + bd_ref[0]
        _store_token_major(o_ref, y)


def _experts(x_slots, blk, block_expert, block_src, block_active, block_first, wg, bg, wu, bu, wd, bd):
    D = D_MODEL
    nb = x_slots.shape[0] // (blk * ROW_TILES)
    wspec = pl.BlockSpec((1, D, D_FF), lambda i, be, bs, ac, fi: (be[i], 0, 0))
    bspec = pl.BlockSpec((1, 1, D_FF), lambda i, be, bs, ac, fi: (be[i], 0, 0))
    xspec = pl.BlockSpec((blk * ROW_TILES, LANES), lambda i, be, bs, ac, fi: (bs[i], 0))
    grid_spec = pltpu.PrefetchScalarGridSpec(
        num_scalar_prefetch=4,
        grid=(nb,),
        in_specs=[xspec, wspec, bspec, wspec, bspec, wspec, bspec],
        out_specs=xspec,
        scratch_shapes=[pltpu.VMEM((D, D_FF), BF16), pltpu.VMEM((D, D_FF), BF16),
                        pltpu.VMEM((D_FF, D), BF16)],
    )
    b3 = lambda b: b.reshape(N_EXPERTS, 1, -1)
    return pl.pallas_call(
        functools.partial(_expert_kernel, blk=blk),
        grid_spec=grid_spec,
        out_shape=jax.ShapeDtypeStruct(x_slots.shape, F32),
        compiler_params=_params(("arbitrary",)),
        name="experts",
    )(block_expert, block_src, block_active, block_first, x_slots,
      wg, b3(bg), wu, b3(bu), wd, b3(bd))


def _combine_kernel(dest_ref, dest_next_ref, ys_ref, w_ref, x2_ref, g_ref, o_ref, buf, sems, *, tt):
    i = pl.program_id(0)
    n = pl.num_programs(0)
    cur = i % 2

    def row_copy(idx_ref, slot, t, k):
        return pltpu.make_async_copy(_row_tile(ys_ref, idx_ref[k * tt + t]),
                                     _row_tile(buf, (slot * TOP_K + k) * tt + t), sems.at[slot])

    def gather(idx_ref, slot):
        _for_each_token_k(tt, lambda t, k: row_copy(idx_ref, slot, t, k).start(priority=k % 2))

    @pl.when(i == 0)
    def _():
        gather(dest_ref, 0)

    @pl.when(i + 1 < n)
    def _():
        gather(dest_next_ref, 1 - cur)

    _for_each_token_k(tt, lambda t, k: row_copy(dest_ref, cur, t, k).wait())
    w = w_ref[...]
    y = x2_ref[...]
    for k in range(TOP_K):
        y = y + w[:, k:k + 1] * _load_token_major(buf, (cur * TOP_K + k) * tt, tt)
    inv = lax.rsqrt(jnp.mean(y * y, axis=-1, keepdims=True) + EPS)
    o_ref[...] = y * inv * g_ref[...]


def _combine(dest, y_slots, top_w, x2, g_final):
    T, D = x2.shape
    tt = min(TT_COMBINE, T)
    nt = T // tt
    kern = functools.partial(_combine_kernel, tt=tt)
    return pl.pallas_call(
        kern,
        grid=(nt,),
        in_specs=[pl.BlockSpec((TOP_K * tt,), lambda i: (i,), memory_space=pltpu.SMEM),
                  pl.BlockSpec((TOP_K * tt,), lambda i: (jnp.minimum(i + 1, nt - 1),),
                               memory_space=pltpu.SMEM),
                  pl.BlockSpec(memory_space=pl.ANY),
                  pl.BlockSpec((tt, SUBLANES), lambda i: (i, 0)),
                  pl.BlockSpec((tt, D), lambda i: (i, 0)),
                  pl.BlockSpec((1, D), lambda i: (0, 0))],
        out_specs=pl.BlockSpec((tt, D), lambda i: (i, 0)),
        out_shape=jax.ShapeDtypeStruct((T, D), F32),
        scratch_shapes=[pltpu.VMEM((2 * TOP_K * tt * ROW_TILES, LANES), F32),
                        pltpu.SemaphoreType.DMA((2,))],
        compiler_params=_params(("arbitrary",)),
        name="combine",
    )(_tile_major_ranks(dest, tt), _tile_major_ranks(dest, tt), y_slots, top_w.T, x2,
      g_final.reshape(1, D))


def _moe_block_size(T):
    return min(MOE_BLOCK, max(SUBLANES, T * TOP_K // N_EXPERTS))


def _block_tables(counts, blk, nb):
    cnt = counts.astype(I32)
    padded = (cnt + blk - 1) // blk * blk
    ends = jnp.cumsum(padded)
    used = ends[-1] // blk
    starts = jnp.arange(nb, dtype=I32) * blk
    src = jnp.minimum(jnp.arange(nb, dtype=I32), jnp.maximum(used - 1, 0))
    expert = jnp.minimum(jnp.sum((starts[:, None] >= ends[None, :]).astype(I32), axis=1), N_EXPERTS - 1)
    expert = expert[src]
    active = (jnp.arange(nb, dtype=I32) < used).astype(I32)
    prev = jnp.concatenate([jnp.full((1,), -1, I32), expert[:-1]])
    first = active * (expert != prev).astype(I32)
    return ends.astype(I32), expert, src, active, first


def kernel(x, norm_mix_g, w_in, conv_w, conv_b, w_rg_a, b_rg_a, w_rg_x, b_rg_x, lru_lambda, diff_lambda_q1, diff_lambda_k1, diff_lambda_q2, diff_lambda_k2, subln_g, rel_bias_table, w_proj_rnn, w_proj_att, w_out, norm_ffn_g, w_router, b_router, w_gate_e, b_gate_e, w_up_e, b_up_e, w_down_e, b_down_e, norm_final_g):
    B, S, D = x.shape
    T = B * S
    assert norm_mix_g.shape[0] == 1, "single-layer block: the final norm is fused into the MoE combine"
    l = 0
    xt = x.reshape(T, D)
    lam_init = 0.8 - 0.6 * math.exp(-0.3 * l)
    proj = _inproj(xt, norm_mix_g[l], w_in[l].astype(BF16))
    y_rnn = _rglru(proj, B, S, conv_w[l], conv_b[l], w_rg_a[l], b_rg_a[l], w_rg_x[l], b_rg_x[l],
                   lru_lambda[l])
    y_att = _attention(proj, B, S, diff_lambda_q1[l], diff_lambda_k1[l], diff_lambda_q2[l],
                       diff_lambda_k2[l], subln_g[l], rel_bias_table, lam_init)
    x2, hf, top_idx, top_w = _merge_router(xt, y_rnn, y_att, proj, w_proj_rnn[l], w_proj_att[l],
                                           w_out[l], norm_ffn_g[l], w_router[l], b_router[l])
    blk = _moe_block_size(T)
    nb = T * TOP_K // blk + N_EXPERTS
    dest, counts = _route(top_idx, blk)
    ends, expert, src, active, first = _block_tables(counts[:, 0], blk, nb)
    x_slots = _dispatch(ends, dest, hf, nb * blk, blk)
    y_slots = _experts(x_slots, blk, expert, src, active, first, w_gate_e[l], b_gate_e[l], w_up_e[l],
                       b_up_e[l], w_down_e[l], b_down_e[l])
    out = _combine(dest, y_slots, top_w, x2, norm_final_g)
    return out.reshape(B, S, D)
```

```python
import functools
import math

import jax
import jax.numpy as jnp
from jax import lax
from jax.experimental import pallas as pl
from jax.experimental.pallas import tpu as pltpu

F32 = jnp.float32
BF16 = jnp.bfloat16
I32 = jnp.int32

D_MODEL = 1024
D_RNN = 1024
RNN_BLOCKS = 16
RNN_BLOCK = D_RNN // RNN_BLOCKS
CONV_W = 4
RGLRU_C = 8.0
N_HEADS = 8
HEAD_DIM = 64
V_DIM = 2 * HEAD_DIM
ATT_QK = N_HEADS * 2 * HEAD_DIM
ATT_V = N_HEADS * V_DIM
D_IN = 2 * D_RNN + 2 * ATT_QK + ATT_V + 2 * D_MODEL
NUM_BUCKETS = 32
MAX_EXACT = NUM_BUCKETS // 2
MAX_DISTANCE = 128
N_EXPERTS = 32
TOP_K = 4
D_FF = D_MODEL
SWIGLU_LIMIT = 7.0
SWIGLU_ALPHA = 1.702
EPS = 1e-6
NEG_INF = -1e30

LANES = 128
SUBLANES = 8
MXU_DIM = 256
VMEM_LIMIT = 56 * 1024 * 1024

TM_INPROJ = 1024
TN_INPROJ = 3584
TS_RGLRU = 256
ATT_TQ = 512
TM_MERGE = 512
TT_ROUTE = 1024
TT_DISPATCH = 1024
TT_COMBINE = 512
MOE_BLOCK = 512
TOKEN_UNROLL = 8

COL_XR, COL_GR, COL_Q, COL_K, COL_V, COL_G0, COL_G1 = range(7)


def _params(sem, vmem=VMEM_LIMIT):
    return pltpu.CompilerParams(dimension_semantics=sem, vmem_limit_bytes=vmem)


ROW_TILES = D_MODEL // LANES
assert ROW_TILES == SUBLANES


def _store_token_major(ref, val, start=0):
    n = val.shape[0]
    for c in range(ROW_TILES):
        ref[pl.ds(start * ROW_TILES + c, n, stride=ROW_TILES), :] = val[:, c * LANES:(c + 1) * LANES]


def _load_token_major(ref, start, n):
    return jnp.concatenate(
        [ref[pl.ds(start * ROW_TILES + c, n, stride=ROW_TILES), :] for c in range(ROW_TILES)], axis=-1)


def _inproj_kernel(x_ref, g_ref, w_ref, o_ref, h_scr):
    @pl.when(pl.program_id(1) == 0)
    def _():
        x = x_ref[...]
        inv = lax.rsqrt(jnp.mean(x * x, axis=-1, keepdims=True) + EPS)
        h_scr[...] = (x * inv * g_ref[...]).astype(BF16)

    o_ref[...] = jnp.dot(h_scr[...], w_ref[...], preferred_element_type=F32).astype(o_ref.dtype)


def _inproj(x2d, g, w_bf16):
    T, D = x2d.shape
    N = w_bf16.shape[1]
    tm = min(TM_INPROJ, T)
    tn = TN_INPROJ if N % TN_INPROJ == 0 else D_MODEL
    return pl.pallas_call(
        _inproj_kernel,
        grid=(T // tm, N // tn),
        in_specs=[
            pl.BlockSpec((tm, D), lambda i, j: (i, 0)),
            pl.BlockSpec((1, D), lambda i, j: (0, 0)),
            pl.BlockSpec((D, tn), lambda i, j: (0, j)),
        ],
        out_specs=pl.BlockSpec((tm, tn), lambda i, j: (i, j)),
        out_shape=jax.ShapeDtypeStruct((T, N), BF16),
        scratch_shapes=[pltpu.VMEM((tm, D), BF16)],
        compiler_params=_params(("parallel", "arbitrary")),
        name="inproj",
    )(x2d, g.reshape(1, D), w_bf16)


def _rglru_kernel(xr_ref, gr_ref, cw_ref, cb_ref, wa_ref, ba_ref, wx_ref, bx_ref, lam_ref,
                  o_ref, tail_scr, a_scr, u_scr, h_scr):
    nb, ts, _ = xr_ref.shape
    pad = SUBLANES
    nslab = D_RNN // LANES

    @pl.when(pl.program_id(0) == 0)
    def _():
        tail_scr[...] = jnp.zeros_like(tail_scr)
        h_scr[...] = jnp.zeros_like(h_scr)

    cw = cw_ref[...]
    z = -lam_ref[...]
    softplus = jnp.maximum(z, 0.0) + jnp.log1p(jnp.exp(-jnp.abs(z)))
    sigmoid = lambda v: 0.5 * jnp.tanh(0.5 * v) + 0.5
    nchunk = D_RNN // MXU_DIM
    nshift = CONV_W - 1
    sr = lax.broadcasted_iota(I32, (nshift * ts, ts), 0)
    sc = lax.broadcasted_iota(I32, (nshift * ts, ts), 1)
    shift_mat = jnp.zeros((nshift * ts, ts), F32)
    for d in range(1, CONV_W):
        hit = (sr >= (d - 1) * ts) & (sr < d * ts) & (sr - (d - 1) * ts - d == sc)
        shift_mat = jnp.where(hit, 1.0, shift_mat)
    shift_mat = shift_mat.astype(BF16)
    row8 = lax.broadcasted_iota(I32, (pad, D_RNN), 0)
    for b in range(nb):
        xb = xr_ref[b]
        x = xb.astype(F32)
        shifted = jnp.dot(shift_mat, xb, preferred_element_type=F32)
        xc = cw[CONV_W - 1:CONV_W, :] * x + cb_ref[...]
        tail = tail_scr[b]
        head_fix = jnp.zeros((pad, D_RNN), F32)
        for d in range(1, CONV_W):
            wd = cw[CONV_W - 1 - d:CONV_W - d, :]
            xc = xc + wd * shifted[(d - 1) * ts:d * ts, :]
            head_fix = head_fix + wd * jnp.where(row8 < d, pltpu.roll(tail, d, 0), 0.0)
        xc = jnp.concatenate([xc[0:pad, :] + head_fix, xc[pad:ts, :]], axis=0)
        tail_scr[b] = x[ts - pad:ts, :]
        xcb = xc.astype(BF16)
        r_pre = jnp.concatenate(
            [jnp.dot(xcb[:, c * MXU_DIM:(c + 1) * MXU_DIM], wa_ref[c], preferred_element_type=F32)
             for c in range(nchunk)], axis=-1)
        i_pre = jnp.concatenate(
            [jnp.dot(xcb[:, c * MXU_DIM:(c + 1) * MXU_DIM], wx_ref[c], preferred_element_type=F32)
             for c in range(nchunk)], axis=-1)
        r = sigmoid(r_pre + ba_ref[...])
        ig = sigmoid(i_pre + bx_ref[...])
        a = jnp.exp((-RGLRU_C) * r * softplus)
        m2 = 1.0 - a * a
        mult = jnp.where(m2 > 0.0, m2 * lax.rsqrt(m2), 0.0)
        u = mult * (ig * xc)
        for c in range(nslab):
            a_scr[c, pl.ds(b, ts, stride=nb), :] = a[:, c * LANES:(c + 1) * LANES]
            u_scr[c, pl.ds(b, ts, stride=nb), :] = u[:, c * LANES:(c + 1) * LANES]

    def body(g, hs):
        for j in range(SCAN_UNROLL):
            off = pl.multiple_of((g * SCAN_UNROLL + j) * nb, nb)
            new = []
            for c in range(nslab):
                h = a_scr[c, pl.ds(off, nb), :] * hs[c] + u_scr[c, pl.ds(off, nb), :]
                u_scr[c, pl.ds(off, nb), :] = h
                new.append(h)
            hs = tuple(new)
        return hs

    hs = lax.fori_loop(0, ts // SCAN_UNROLL, body, tuple(h_scr[c] for c in range(nslab)))
    for c in range(nslab):
        h_scr[c] = hs[c]
    for b in range(nb):
        h = jnp.concatenate([u_scr[c, pl.ds(b, ts, stride=nb), :] for c in range(nslab)], axis=-1)
        gate = jax.nn.gelu(gr_ref[b].astype(F32), approximate=True)
        o_ref[b] = (h * gate).astype(o_ref.dtype)


SCAN_UNROLL = 8


def _block_diag_chunks(w):
    per = MXU_DIM // RNN_BLOCK
    w = w.reshape(D_RNN // MXU_DIM, per, RNN_BLOCK, RNN_BLOCK)
    eye = jnp.eye(per, dtype=w.dtype)
    out = jnp.einsum('gpcd,pq->gpcqd', w, eye)
    return out.reshape(D_RNN // MXU_DIM, MXU_DIM, MXU_DIM)


def _rglru(proj, B, S, conv_w, conv_b, w_a, b_a, w_x, b_x, lru_lambda):
    assert B <= SUBLANES, "all batch rows of a time step share one vreg in the scan"
    ts = min(TS_RGLRU, S)
    ns = S // ts
    wa = _block_diag_chunks(w_a).astype(BF16)
    wx = _block_diag_chunks(w_x).astype(BF16)
    nchunk = D_RNN // MXU_DIM
    nslab = D_RNN // LANES
    row = lambda v: v.reshape(1, D_RNN)
    const2 = lambda s: (0, 0)
    proj3 = proj.reshape(B, S, proj.shape[-1])
    out = pl.pallas_call(
        _rglru_kernel,
        grid=(ns,),
        in_specs=[
            pl.BlockSpec((B, ts, D_RNN), lambda s: (0, s, COL_XR)),
            pl.BlockSpec((B, ts, D_RNN), lambda s: (0, s, COL_GR)),
            pl.BlockSpec((CONV_W, D_RNN), const2),
            pl.BlockSpec((1, D_RNN), const2),
            pl.BlockSpec((nchunk, MXU_DIM, MXU_DIM), lambda s: (0, 0, 0)),
            pl.BlockSpec((1, D_RNN), const2),
            pl.BlockSpec((nchunk, MXU_DIM, MXU_DIM), lambda s: (0, 0, 0)),
            pl.BlockSpec((1, D_RNN), const2),
            pl.BlockSpec((1, D_RNN), const2),
        ],
        out_specs=pl.BlockSpec((B, ts, D_RNN), lambda s: (0, s, 0)),
        out_shape=jax.ShapeDtypeStruct((B, S, D_RNN), BF16),
        scratch_shapes=[
            pltpu.VMEM((B, SUBLANES, D_RNN), F32),
            pltpu.VMEM((nslab, ts * B, LANES), F32),
            pltpu.VMEM((nslab, ts * B, LANES), F32),
            pltpu.VMEM((nslab, B, LANES), F32),
        ],
        compiler_params=_params(("arbitrary",)),
        name="rglru",
    )(proj3, proj3, conv_w, row(conv_b), wa, row(b_a), wx, row(b_x), row(lru_lambda))
    return out.reshape(B * S, D_RNN)


def _attn_kernel(q_ref, k_ref, v_ref, bias_ref, lq1_ref, lk1_ref, lq2_ref, lk2_ref, sg_ref,
                 o_ref, qs_scr, vx_scr, s_scr, m_scr, acc_scr, *, tq, lam_init):
    S = q_ref.shape[0]
    nq = S // tq
    scale = HEAD_DIM ** -0.5 * LOG2E
    lane = lax.broadcasted_iota(I32, (tq, V_DIM), 1)
    lam = (jnp.exp(jnp.sum(lq1_ref[...] * lk1_ref[...], keepdims=True))
           - jnp.exp(jnp.sum(lq2_ref[...] * lk2_ref[...], keepdims=True)) + lam_init)
    vx_scr[:, 0:V_DIM] = v_ref[...]
    vx_scr[:, V_DIM:2 * V_DIM] = jnp.ones((S, V_DIM), BF16)

    Z0 = 2

    def prep_q(qi, buf):
        q = (q_ref[pl.ds(pl.multiple_of(qi * tq, tq), tq), :].astype(F32) * scale).astype(BF16)
        zero = jnp.zeros_like(q)
        qs_scr[buf, 0:tq, :] = jnp.where(lane < HEAD_DIM, q, zero)
        qs_scr[buf, tq:2 * tq, :] = jnp.where(lane >= HEAD_DIM, q, zero)

    def scores(buf, j, dst):
        k = k_ref[pl.ds(pl.multiple_of(j * tq, tq), tq), :]
        s_scr[dst] = lax.dot_general(qs_scr[buf], k, (((1,), (1,)), ((), ())),
                                     preferred_element_type=F32)

    def step(j, src, bias_idx, prefetch):
        scores(*prefetch)
        s = s_scr[src]
        if bias_idx is not None:
            b = bias_ref[bias_idx]
            s = s + jnp.concatenate([b, b], axis=0)
        vx = vx_scr[pl.ds(pl.multiple_of(j * tq, tq), tq), :]
        m_prev = m_scr[...]
        m_new = jnp.maximum(m_prev, jnp.max(s, axis=1)[:, None])
        p = jnp.exp2(s - jnp.tile(m_new, (1, tq // LANES)))
        alpha = jnp.exp2(m_prev - m_new)
        acc_scr[...] = (jnp.tile(alpha, (1, 2)) * acc_scr[...]
                        + jnp.dot(p.astype(BF16), vx, preferred_element_type=F32))
        m_scr[...] = m_new

    def begin_block():
        m_scr[...] = jnp.full(m_scr.shape, NEG_INF, F32)
        acc_scr[...] = jnp.zeros(acc_scr.shape, F32)

    def diag_step(qi, par, src):
        nxt = jnp.minimum(qi + 1, nq - 1)
        prep_q(nxt, 1 - par)
        step(qi, src, 0, (1 - par, 0, Z0 + 1 - par))

    def end_block(qi):
        acc = acc_scr[...]
        o_all = acc[:, 0:V_DIM] / acc[:, V_DIM:2 * V_DIM]
        o = o_all[0:tq, :] - lam * o_all[tq:2 * tq, :]
        inv = lax.rsqrt(jnp.mean(o * o, axis=-1, keepdims=True) + EPS)
        y = (o * inv * sg_ref[...]) * (1.0 - lam_init)
        o_ref[pl.ds(pl.multiple_of(qi * tq, tq), tq), :] = y.astype(o_ref.dtype)

    def far_pairs(par, npairs):
        def body(m, c):
            step(2 * m + 1, 0, None, (par, 2 * m + 2, 1))
            step(2 * m + 2, 1, None, (par, 2 * m + 3, 0))
            return c

        lax.fori_loop(0, npairs, body, 0)

    prep_q(0, 0)
    scores(0, 0, Z0)
    begin_block()
    diag_step(0, 0, Z0)
    end_block(0)
    begin_block()
    step(0, Z0 + 1, 1, (1, 1, 0))
    diag_step(1, 1, 0)
    end_block(1)

    def block_pair(i, carry):
        qi = 2 * i
        begin_block()
        step(0, Z0, None, (0, 1, 0))
        far_pairs(0, i - 1)
        step(qi - 1, 0, 1, (0, qi, 1))
        diag_step(qi, 0, 1)
        end_block(qi)
        qi = 2 * i + 1
        begin_block()
        step(0, Z0 + 1, None, (1, 1, 0))
        far_pairs(1, i - 1)
        step(qi - 2, 0, None, (1, qi - 1, 1))
        step(qi - 1, 1, 1, (1, qi, 0))
        diag_step(qi, 1, 0)
        end_block(qi)
        return carry

    lax.fori_loop(1, nq // 2, block_pair, 0)


def _rel_bucket(n):
    n = jnp.maximum(n, 0)
    nf = jnp.maximum(n, MAX_EXACT).astype(F32)
    large = MAX_EXACT + (jnp.log(nf / MAX_EXACT) / math.log(MAX_DISTANCE / MAX_EXACT)
                         * (NUM_BUCKETS - MAX_EXACT)).astype(I32)
    large = jnp.minimum(large, NUM_BUCKETS - 1)
    return jnp.where(n < MAX_EXACT, n, large)


def _bias_tiles(rel_table, tq):
    i = jnp.arange(tq, dtype=I32)[:, None]
    j = jnp.arange(tq, dtype=I32)[None, :]
    table = rel_table.astype(F32) - rel_table[NUM_BUCKETS - 1].astype(F32)[None, :]
    tiles = []
    for delta in (0, tq):
        n = i - j + delta
        onehot = (_rel_bucket(n)[:, :, None] == jnp.arange(NUM_BUCKETS, dtype=I32)).astype(F32)
        b = jnp.einsum('ijb,bh->hij', onehot, table, precision=lax.Precision.HIGHEST) * LOG2E
        tiles.append(jnp.where((n >= 0)[None], b, NEG_INF))
    return jnp.stack(tiles, axis=1)


LOG2E = math.log2(math.e)


def _attention(proj, B, S, lq1, lk1, lq2, lk2, subln_g, rel_table, lam_init):
    T = B * S
    tq = min(ATT_TQ, S // 2)
    assert V_DIM == LANES and MAX_DISTANCE <= tq and S % (2 * tq) == 0
    bias = _bias_tiles(rel_table, tq)
    vec = lambda v: v.reshape(1, -1).astype(F32)
    const2 = lambda b, h: (0, 0)
    kern = functools.partial(_attn_kernel, tq=tq, lam_init=lam_init)
    return pl.pallas_call(
        kern,
        grid=(B, N_HEADS),
        in_specs=[
            pl.BlockSpec((S, V_DIM), lambda b, h: (b, COL_Q * N_HEADS + h)),
            pl.BlockSpec((S, V_DIM), lambda b, h: (b, COL_K * N_HEADS + h)),
            pl.BlockSpec((S, V_DIM), lambda b, h: (b, COL_V * N_HEADS + h)),
            pl.BlockSpec((None, 2, tq, tq), lambda b, h: (h, 0, 0, 0)),
            pl.BlockSpec((1, HEAD_DIM), const2),
            pl.BlockSpec((1, HEAD_DIM), const2),
            pl.BlockSpec((1, HEAD_DIM), const2),
            pl.BlockSpec((1, HEAD_DIM), const2),
            pl.BlockSpec((1, V_DIM), const2),
        ],
        out_specs=pl.BlockSpec((S, V_DIM), lambda b, h: (b, h)),
        out_shape=jax.ShapeDtypeStruct((T, ATT_V), BF16),
        scratch_shapes=[
            pltpu.VMEM((2, 2 * tq, V_DIM), BF16),
            pltpu.VMEM((S, 2 * V_DIM), BF16),
            pltpu.VMEM((4, 2 * tq, tq), F32),
            pltpu.VMEM((2 * tq, LANES), F32),
            pltpu.VMEM((2 * tq, 2 * V_DIM), F32),
        ],
        compiler_params=_params(("parallel", "arbitrary")),
        name="diff_attn",
    )(proj, proj, proj, bias, vec(lq1), vec(lk1), vec(lq2), vec(lk2), vec(subln_g))


def _merge_kernel(x_ref, yr_ref, ya_ref, g0_ref, g1_ref, wr_ref, wa_ref, wo_ref, gf_ref, wrt_ref, brt_ref,
                  x2_ref, hf_ref, idx_ref, wgt_ref):
    pr = jnp.dot(yr_ref[...], wr_ref[...], preferred_element_type=F32)
    pa = jnp.dot(ya_ref[...], wa_ref[...], preferred_element_type=F32)
    merged = (jax.nn.sigmoid(g0_ref[...].astype(F32)) * pr
              + jax.nn.sigmoid(g1_ref[...].astype(F32)) * pa)
    x2 = x_ref[...] + jnp.dot(merged.astype(BF16), wo_ref[...], preferred_element_type=F32)
    x2_ref[...] = x2
    inv = lax.rsqrt(jnp.mean(x2 * x2, axis=-1, keepdims=True) + EPS)
    hf = x2 * inv * gf_ref[...]
    _store_token_major(hf_ref, hf)
    nt = (((1,), (1,)), ((), ()))
    hi = hf.astype(BF16)
    lo = (hf - hi.astype(F32)).astype(BF16)
    hw = lax.dot_general(wrt_ref[...], hi, nt, preferred_element_type=F32)
    lw = lax.dot_general(wrt_ref[0:LANES, :], lo, nt, preferred_element_type=F32)
    logits = (hw[0:N_EXPERTS, :] + hw[LANES:LANES + N_EXPERTS, :] + lw[0:N_EXPERTS, :]
              + brt_ref[...])
    tm = logits.shape[1]
    row = lax.broadcasted_iota(I32, (N_EXPERTS, tm), 0)
    work = logits
    idx_rows, val_rows = [], []
    for k in range(TOP_K):
        mx = jnp.max(work, axis=0, keepdims=True)
        sel = jnp.min(jnp.where(work == mx, row, N_EXPERTS), axis=0, keepdims=True)
        idx_rows.append(sel)
        val_rows.append(mx)
        work = jnp.where(row == sel, -jnp.inf, work)
    es = [jnp.exp(v - val_rows[0]) for v in val_rows]
    inv_sum = 1.0 / functools.reduce(lambda a, b: a + b, es)
    pad_i = [jnp.zeros((1, tm), I32)] * (SUBLANES - TOP_K)
    pad_f = [jnp.zeros((1, tm), F32)] * (SUBLANES - TOP_K)
    idx_ref[...] = jnp.concatenate(idx_rows + pad_i, axis=0)
    wgt_ref[...] = jnp.concatenate([e * inv_sum for e in es] + pad_f, axis=0)


def _merge_router(x2d, y_rnn, y_att, proj, w_pr, w_pa, w_o, g_ffn, w_router, b_router):
    T, D = x2d.shape
    tm = min(TM_MERGE, T)
    w_hi = w_router.astype(BF16)
    w_lo = (w_router - w_hi.astype(F32)).astype(BF16)
    wrt = (jnp.zeros((2 * LANES, D), BF16).at[:N_EXPERTS, :].set(w_hi.T)
           .at[LANES:LANES + N_EXPERTS, :].set(w_lo.T))
    brt = b_router.reshape(N_EXPERTS, 1).astype(F32)
    rowblk = lambda c: pl.BlockSpec((tm, D), lambda i, c=c: (i, c))
    full = lambda a: pl.BlockSpec(a.shape, lambda i: (0,) * a.ndim)
    wr, wa, wo = w_pr.astype(BF16), w_pa.astype(BF16), w_o.astype(BF16)
    gf = g_ffn.reshape(1, D)
    return pl.pallas_call(
        _merge_kernel,
        grid=(T // tm,),
        in_specs=[rowblk(0), rowblk(0), rowblk(0), rowblk(COL_G0), rowblk(COL_G1),
                  full(wr), full(wa), full(wo), full(gf), full(wrt), full(brt)],
        out_specs=[rowblk(0), pl.BlockSpec((tm * ROW_TILES, LANES), lambda i: (i, 0)),
                   pl.BlockSpec((SUBLANES, tm), lambda i: (0, i)),
                   pl.BlockSpec((SUBLANES, tm), lambda i: (0, i))],
        out_shape=[jax.ShapeDtypeStruct((T, D), F32), jax.ShapeDtypeStruct((T * ROW_TILES, LANES), F32),
                   jax.ShapeDtypeStruct((SUBLANES, T), I32), jax.ShapeDtypeStruct((SUBLANES, T), F32)],
        compiler_params=_params(("parallel",)),
        name="merge_router",
    )(x2d, y_rnn, y_att, proj, proj, wr, wa, wo, gf, wrt, brt)


def _route_kernel(idx_ref, dest_ref, cnt_ref, cnt_scr, run_scr, start_scr, *, blk):
    ph = pl.program_id(0)
    i = pl.program_id(1)
    tt = idx_ref.shape[1]
    idx = idx_ref[...]
    row = lax.broadcasted_iota(I32, (N_EXPERTS, tt), 0)
    onehot = jnp.zeros((N_EXPERTS, tt), F32)
    for k in range(TOP_K):
        onehot = onehot + (idx[k:k + 1, :] == row).astype(F32)
    tile_cnt = jnp.sum(onehot, axis=1, keepdims=True)

    @pl.when((ph == 0) & (i == 0))
    def _():
        cnt_scr[...] = jnp.zeros_like(cnt_scr)

    @pl.when(ph == 0)
    def _():
        cnt_scr[...] += tile_cnt

    @pl.when((ph == 1) & (i == 0))
    def _():
        padded = jnp.floor((cnt_scr[...] + (blk - 1)) / blk) * blk
        r = lax.broadcasted_iota(I32, (N_EXPERTS, N_EXPERTS), 0)
        c = lax.broadcasted_iota(I32, (N_EXPERTS, N_EXPERTS), 1)
        start_scr[...] = jnp.dot((c < r).astype(F32), padded, preferred_element_type=F32,
                                 precision=lax.Precision.HIGHEST)
        run_scr[...] = jnp.zeros_like(run_scr)

    @pl.when(ph == 1)
    def _():
        r = lax.broadcasted_iota(I32, (tt, tt), 0)
        c = lax.broadcasted_iota(I32, (tt, tt), 1)
        earlier = (r < c).astype(BF16)
        before = jnp.dot(onehot.astype(BF16), earlier, preferred_element_type=F32)
        base = before + run_scr[:, 0:1] + start_scr[:, 0:1]
        rows = [jnp.sum(jnp.where(idx[k:k + 1, :] == row, base, 0.0), axis=0, keepdims=True)
                for k in range(TOP_K)]
        rows += [jnp.zeros((1, tt), F32)] * (SUBLANES - TOP_K)
        dest_ref[...] = jnp.concatenate(rows, axis=0).astype(I32)
        run_scr[...] += tile_cnt
        cnt_ref[...] = cnt_scr[...]


def _route(top_idx, blk):
    T = top_idx.shape[1]
    tt = min(TT_ROUTE, T)
    kern = functools.partial(_route_kernel, blk=blk)
    return pl.pallas_call(
        kern,
        grid=(2, T // tt),
        in_specs=[pl.BlockSpec((SUBLANES, tt), lambda p, i: (0, i))],
        out_specs=[pl.BlockSpec((SUBLANES, tt), lambda p, i: (0, p * i)),
                   pl.BlockSpec((N_EXPERTS, LANES), lambda p, i: (0, 0))],
        out_shape=[jax.ShapeDtypeStruct((SUBLANES, T), I32),
                   jax.ShapeDtypeStruct((N_EXPERTS, LANES), F32)],
        scratch_shapes=[pltpu.VMEM((N_EXPERTS, LANES), F32), pltpu.VMEM((N_EXPERTS, LANES), F32),
                        pltpu.VMEM((N_EXPERTS, LANES), F32)],
        compiler_params=_params(("arbitrary", "arbitrary")),
        name="route",
    )(top_idx)


def _for_each_token_k(n_tok, fn):
    def body(g, c):
        for u in range(TOKEN_UNROLL):
            for k in range(TOP_K):
                fn(g * TOKEN_UNROLL + u, k)
        return c

    lax.fori_loop(0, n_tok // TOKEN_UNROLL, body, 0)


def _tile_major_ranks(dest, tt):
    T = dest.shape[1]
    return dest[:TOP_K].reshape(TOP_K, T // tt, tt).transpose(1, 0, 2).reshape(-1)


def _row_tile(ref, row):
    return ref.at[pl.ds(pl.multiple_of(row * ROW_TILES, ROW_TILES), ROW_TILES), :]


def _dispatch_kernel(ends_ref, dest_ref, hf_ref, xs_ref, zero_buf, sem, zsem, *, tt, blk):
    @pl.when(pl.program_id(0) == 0)
    def _():
        zero_buf[...] = jnp.zeros_like(zero_buf)

        def tail_fill(e):
            prev_end = ends_ref[e - 1] if e > 0 else 0
            start = pl.multiple_of((ends_ref[e] - blk) * ROW_TILES, ROW_TILES)
            return ends_ref[e] > prev_end, pltpu.make_async_copy(
                zero_buf, xs_ref.at[pl.ds(start, blk * ROW_TILES), :], zsem)

        for e in range(N_EXPERTS):
            nonempty, cp = tail_fill(e)
            pl.when(nonempty)(cp.start)
        for e in range(N_EXPERTS):
            nonempty, cp = tail_fill(e)
            pl.when(nonempty)(cp.wait)

    def row_copy(t, k):
        return pltpu.make_async_copy(_row_tile(hf_ref, t),
                                     _row_tile(xs_ref, dest_ref[k * tt + t]), sem)

    _for_each_token_k(tt, lambda t, k: row_copy(t, k).start(priority=k % 2))
    _for_each_token_k(tt, lambda t, k: row_copy(t, k).wait())


def _dispatch(ends, dest, hf_tm, n_slots, blk):
    T = hf_tm.shape[0] // ROW_TILES
    tt = min(TT_DISPATCH, T)
    kern = functools.partial(_dispatch_kernel, tt=tt, blk=blk)
    grid_spec = pltpu.PrefetchScalarGridSpec(
        num_scalar_prefetch=1,
        grid=(T // tt,),
        in_specs=[pl.BlockSpec((TOP_K * tt,), lambda i, ends: (i,), memory_space=pltpu.SMEM),
                  pl.BlockSpec((tt * ROW_TILES, LANES), lambda i, ends: (i, 0))],
        out_specs=pl.BlockSpec(memory_space=pl.ANY),
        scratch_shapes=[pltpu.VMEM((blk * ROW_TILES, LANES), F32),
                        pltpu.SemaphoreType.DMA(()), pltpu.SemaphoreType.DMA(())],
    )
    return pl.pallas_call(
        kern,
        grid_spec=grid_spec,
        out_shape=jax.ShapeDtypeStruct((n_slots * ROW_TILES, LANES), F32),
        compiler_params=_params(("arbitrary",)),
        name="dispatch",
    )(ends, _tile_major_ranks(dest, tt), hf_tm)


def _expert_kernel(be_ref, bsrc_ref, act_ref, first_ref, x_ref, wg_ref, bg_ref, wu_ref, bu_ref, wd_ref, bd_ref,
                   o_ref, wg_scr, wu_scr, wd_scr, *, blk):
    del be_ref, bsrc_ref
    i = pl.program_id(0)

    @pl.when(first_ref[i] == 1)
    def _():
        wg_scr[...] = wg_ref[0].astype(BF16)
        wu_scr[...] = wu_ref[0].astype(BF16)
        wd_scr[...] = wd_ref[0].astype(BF16)

    @pl.when(act_ref[i] == 1)
    def _():
        x = _load_token_major(x_ref, 0, blk).astype(BF16)
        g = jnp.dot(x, wg_scr[...], preferred_element_type=F32) + bg_ref[0]
        u = jnp.dot(x, wu_scr[...], preferred_element_type=F32) + bu_ref[0]
        g = jnp.minimum(g, SWIGLU_LIMIT)
        u = jnp.clip(u, -SWIGLU_LIMIT, SWIGLU_LIMIT)
        act = (u + 1.0) * (g * jax.nn.sigmoid(SWIGLU_ALPHA * g))
        y = jnp.dot(act.astype(BF16), wd_scr[...], preferred_element_type=F32) + bd_ref[0]
        _store_token_major(o_ref, y)


def _experts(x_slots, blk, block_expert, block_src, block_active, block_first, wg, bg, wu, bu, wd, bd):
    D = D_MODEL
    nb = x_slots.shape[0] // (blk * ROW_TILES)
    wspec = pl.BlockSpec((1, D, D_FF), lambda i, be, bs, ac, fi: (be[i], 0, 0))
    bspec = pl.BlockSpec((1, 1, D_FF), lambda i, be, bs, ac, fi: (be[i], 0, 0))
    xspec = pl.BlockSpec((blk * ROW_TILES, LANES), lambda i, be, bs, ac, fi: (bs[i], 0))
    grid_spec = pltpu.PrefetchScalarGridSpec(
        num_scalar_prefetch=4,
        grid=(nb,),
        in_specs=[xspec, wspec, bspec, wspec, bspec, wspec, bspec],
        out_specs=xspec,
        scratch_shapes=[pltpu.VMEM((D, D_FF), BF16), pltpu.VMEM((D, D_FF), BF16),
                        pltpu.VMEM((D_FF, D), BF16)],
    )
    b3 = lambda b: b.reshape(N_EXPERTS, 1, -1)
    return pl.pallas_call(
        functools.partial(_expert_kernel, blk=blk),
        grid_spec=grid_spec,
        out_shape=jax.ShapeDtypeStruct(x_slots.shape, F32),
        compiler_params=_params(("arbitrary",)),
        name="experts",
    )(block_expert, block_src, block_active, block_first, x_slots,
      wg, b3(bg), wu, b3(bu), wd, b3(bd))


def _combine_kernel(dest_ref, dest_next_ref, ys_ref, w_ref, x2_ref, g_ref, o_ref, buf, sems, *, tt):
    i = pl.program_id(0)
    n = pl.num_programs(0)
    cur = i % 2

    def row_copy(idx_ref, slot, t, k):
        return pltpu.make_async_copy(_row_tile(ys_ref, idx_ref[k * tt + t]),
                                     _row_tile(buf, (slot * TOP_K + k) * tt + t), sems.at[slot])

    def gather(idx_ref, slot):
        _for_each_token_k(tt, lambda t, k: row_copy(idx_ref, slot, t, k).start(priority=k % 2))

    @pl.when(i == 0)
    def _():
        gather(dest_ref, 0)

    @pl.when(i + 1 < n)
    def _():
        gather(dest_next_ref, 1 - cur)

    _for_each_token_k(tt, lambda t, k: row_copy(dest_ref, cur, t, k).wait())
    w = w_ref[...]
    y = x2_ref[...]
    for k in range(TOP_K):
        y = y + w[:, k:k + 1] * _load_token_major(buf, (cur * TOP_K + k) * tt, tt)
    inv = lax.rsqrt(jnp.mean(y * y, axis=-1, keepdims=True) + EPS)
    o_ref[...] = y * inv * g_ref[...]


def _combine(dest, y_slots, top_w, x2, g_final):
    T, D = x2.shape
    tt = min(TT_COMBINE, T)
    nt = T // tt
    kern = functools.partial(_combine_kernel, tt=tt)
    return pl.pallas_call(
        kern,
        grid=(nt,),
        in_specs=[pl.BlockSpec((TOP_K * tt,), lambda i: (i,), memory_space=pltpu.SMEM),
                  pl.BlockSpec((TOP_K * tt,), lambda i: (jnp.minimum(i + 1, nt - 1),),
                               memory_space=pltpu.SMEM),
                  pl.BlockSpec(memory_space=pl.ANY),
                  pl.BlockSpec((tt, SUBLANES), lambda i: (i, 0)),
                  pl.BlockSpec((tt, D), lambda i: (i, 0)),
                  pl.BlockSpec((1, D), lambda i: (0, 0))],
        out_specs=pl.BlockSpec((tt, D), lambda i: (i, 0)),
        out_shape=jax.ShapeDtypeStruct((T, D), F32),
        scratch_shapes=[pltpu.VMEM((2 * TOP_K * tt * ROW_TILES, LANES), F32),
                        pltpu.SemaphoreType.DMA((2,))],
        compiler_params=_params(("arbitrary",)),
        name="combine",
    )(_tile_major_ranks(dest, tt), _tile_major_ranks(dest, tt), y_slots, top_w.T, x2,
      g_final.reshape(1, D))


def _moe_block_size(T):
    return min(MOE_BLOCK, max(SUBLANES, T * TOP_K // N_EXPERTS))


def _block_tables(counts, blk, nb):
    cnt = counts.astype(I32)
    padded = (cnt + blk - 1) // blk * blk
    ends = jnp.cumsum(padded)
    used = ends[-1] // blk
    starts = jnp.arange(nb, dtype=I32) * blk
    src = jnp.minimum(jnp.arange(nb, dtype=I32), jnp.maximum(used - 1, 0))
    expert = jnp.minimum(jnp.sum((starts[:, None] >= ends[None, :]).astype(I32), axis=1), N_EXPERTS - 1)
    expert = expert[src]
    active = (jnp.arange(nb, dtype=I32) < used).astype(I32)
    prev = jnp.concatenate([jnp.full((1,), -1, I32), expert[:-1]])
    first = active * (expert != prev).astype(I32)
    return ends.astype(I32), expert, src, active, first


def kernel(x, norm_mix_g, w_in, conv_w, conv_b, w_rg_a, b_rg_a, w_rg_x, b_rg_x, lru_lambda, diff_lambda_q1, diff_lambda_k1, diff_lambda_q2, diff_lambda_k2, subln_g, rel_bias_table, w_proj_rnn, w_proj_att, w_out, norm_ffn_g, w_router, b_router, w_gate_e, b_gate_e, w_up_e, b_up_e, w_down_e, b_down_e, norm_final_g):
    B, S, D = x.shape
    T = B * S
    assert norm_mix_g.shape[0] == 1, "single-layer block: the final norm is fused into the MoE combine"
    l = 0
    xt = x.reshape(T, D)
    lam_init = 0.8 - 0.6 * math.exp(-0.3 * l)
    proj = _inproj(xt, norm_mix_g[l], w_in[l].astype(BF16))
    y_rnn = _rglru(proj, B, S, conv_w[l], conv_b[l], w_rg_a[l], b_rg_a[l], w_rg_x[l], b_rg_x[l],
                   lru_lambda[l])
    y_att = _attention(proj, B, S, diff_lambda_q1[l], diff_lambda_k1[l], diff_lambda_q2[l],
                       diff_lambda_k2[l], subln_g[l], rel_bias_table, lam_init)
    x2, hf, top_idx, top_w = _merge_router(xt, y_rnn, y_att, proj, w_proj_rnn[l], w_proj_att[l],
                                           w_out[l], norm_ffn_g[l], w_router[l], b_router[l])
    blk = _moe_block_size(T)
    nb = T * TOP_K // blk + N_EXPERTS
    dest, counts = _route(top_idx, blk)
    ends, expert, src, active, first = _block_tables(counts[:, 0], blk, nb)
    x_slots = _dispatch(ends, dest, hf, nb * blk, blk)
    y_slots = _experts(x_slots, blk, expert, src, active, first, w_gate_e[l], b_gate_e[l], w_up_e[l],
                       b_up_e[l], w_down_e[l], b_down_e[l])
    out = _combine(dest, y_slots, top_w, x2, norm_final_g)
    return out.reshape(B, S, D)
```

```python
import functools
import math

import jax
import jax.numpy as jnp
from jax import lax
from jax.experimental import pallas as pl
from jax.experimental.pallas import tpu as pltpu

F32 = jnp.float32
BF16 = jnp.bfloat16
I32 = jnp.int32

D_MODEL = 1024
D_RNN = 1024
RNN_BLOCKS = 16
RNN_BLOCK = D_RNN // RNN_BLOCKS
CONV_W = 4
RGLRU_C = 8.0
N_HEADS = 8
HEAD_DIM = 64
V_DIM = 2 * HEAD_DIM
ATT_QK = N_HEADS * 2 * HEAD_DIM
ATT_V = N_HEADS * V_DIM
D_IN = 2 * D_RNN + 2 * ATT_QK + ATT_V + 2 * D_MODEL
NUM_BUCKETS = 32
MAX_EXACT = NUM_BUCKETS // 2
MAX_DISTANCE = 128
N_EXPERTS = 32
TOP_K = 4
D_FF = D_MODEL
SWIGLU_LIMIT = 7.0
SWIGLU_ALPHA = 1.702
EPS = 1e-6
NEG_INF = -1e30

LANES = 128
SUBLANES = 8
MXU_DIM = 256
VMEM_LIMIT = 56 * 1024 * 1024

TM_INPROJ = 1024
TN_INPROJ = 3584
TS_RGLRU = 256
ATT_TQ = 512
TM_MERGE = 512
TT_ROUTE = 1024
TT_DISPATCH = 1024
TT_COMBINE = 512
MOE_BLOCK = 512
TOKEN_UNROLL = 16

COL_XR, COL_GR, COL_Q, COL_K, COL_V, COL_G0, COL_G1 = range(7)


def _params(sem, vmem=VMEM_LIMIT):
    return pltpu.CompilerParams(dimension_semantics=sem, vmem_limit_bytes=vmem)


ROW_TILES = D_MODEL // LANES
assert ROW_TILES == SUBLANES


def _store_token_major(ref, val, start=0):
    n = val.shape[0]
    for c in range(ROW_TILES):
        ref[pl.ds(start * ROW_TILES + c, n, stride=ROW_TILES), :] = val[:, c * LANES:(c + 1) * LANES]


def _load_token_major(ref, start, n):
    return jnp.concatenate(
        [ref[pl.ds(start * ROW_TILES + c, n, stride=ROW_TILES), :] for c in range(ROW_TILES)], axis=-1)


def _inproj_kernel(x_ref, g_ref, w_ref, o_ref, h_scr):
    @pl.when(pl.program_id(1) == 0)
    def _():
        x = x_ref[...]
        inv = lax.rsqrt(jnp.mean(x * x, axis=-1, keepdims=True) + EPS)
        h_scr[...] = (x * inv * g_ref[...]).astype(BF16)

    o_ref[...] = jnp.dot(h_scr[...], w_ref[...], preferred_element_type=F32).astype(o_ref.dtype)


def _inproj(x2d, g, w_bf16):
    T, D = x2d.shape
    N = w_bf16.shape[1]
    tm = min(TM_INPROJ, T)
    tn = TN_INPROJ if N % TN_INPROJ == 0 else D_MODEL
    return pl.pallas_call(
        _inproj_kernel,
        grid=(T // tm, N // tn),
        in_specs=[
            pl.BlockSpec((tm, D), lambda i, j: (i, 0)),
            pl.BlockSpec((1, D), lambda i, j: (0, 0)),
            pl.BlockSpec((D, tn), lambda i, j: (0, j)),
        ],
        out_specs=pl.BlockSpec((tm, tn), lambda i, j: (i, j)),
        out_shape=jax.ShapeDtypeStruct((T, N), BF16),
        scratch_shapes=[pltpu.VMEM((tm, D), BF16)],
        compiler_params=_params(("parallel", "arbitrary")),
        name="inproj",
    )(x2d, g.reshape(1, D), w_bf16)


def _rglru_kernel(xr_ref, gr_ref, cw_ref, cb_ref, wa_ref, ba_ref, wx_ref, bx_ref, lam_ref,
                  o_ref, tail_scr, a_scr, u_scr, h_scr):
    nb, ts, _ = xr_ref.shape
    pad = SUBLANES
    nslab = D_RNN // LANES

    @pl.when(pl.program_id(0) == 0)
    def _():
        tail_scr[...] = jnp.zeros_like(tail_scr)
        h_scr[...] = jnp.zeros_like(h_scr)

    cw = cw_ref[...]
    z = -lam_ref[...]
    softplus = jnp.maximum(z, 0.0) + jnp.log1p(jnp.exp(-jnp.abs(z)))
    sigmoid = lambda v: 0.5 * jnp.tanh(0.5 * v) + 0.5
    nchunk = D_RNN // MXU_DIM
    nshift = CONV_W - 1
    sr = lax.broadcasted_iota(I32, (nshift * ts, ts), 0)
    sc = lax.broadcasted_iota(I32, (nshift * ts, ts), 1)
    shift_mat = jnp.zeros((nshift * ts, ts), F32)
    for d in range(1, CONV_W):
        hit = (sr >= (d - 1) * ts) & (sr < d * ts) & (sr - (d - 1) * ts - d == sc)
        shift_mat = jnp.where(hit, 1.0, shift_mat)
    shift_mat = shift_mat.astype(BF16)
    row8 = lax.broadcasted_iota(I32, (pad, D_RNN), 0)
    for b in range(nb):
        xb = xr_ref[b]
        x = xb.astype(F32)
        shifted = jnp.dot(shift_mat, xb, preferred_element_type=F32)
        xc = cw[CONV_W - 1:CONV_W, :] * x + cb_ref[...]
        tail = tail_scr[b]
        head_fix = jnp.zeros((pad, D_RNN), F32)
        for d in range(1, CONV_W):
            wd = cw[CONV_W - 1 - d:CONV_W - d, :]
            xc = xc + wd * shifted[(d - 1) * ts:d * ts, :]
            head_fix = head_fix + wd * jnp.where(row8 < d, pltpu.roll(tail, d, 0), 0.0)
        xc = jnp.concatenate([xc[0:pad, :] + head_fix, xc[pad:ts, :]], axis=0)
        tail_scr[b] = x[ts - pad:ts, :]
        xcb = xc.astype(BF16)
        r_pre = jnp.concatenate(
            [jnp.dot(xcb[:, c * MXU_DIM:(c + 1) * MXU_DIM], wa_ref[c], preferred_element_type=F32)
             for c in range(nchunk)], axis=-1)
        i_pre = jnp.concatenate(
            [jnp.dot(xcb[:, c * MXU_DIM:(c + 1) * MXU_DIM], wx_ref[c], preferred_element_type=F32)
             for c in range(nchunk)], axis=-1)
        r = sigmoid(r_pre + ba_ref[...])
        ig = sigmoid(i_pre + bx_ref[...])
        a = jnp.exp((-RGLRU_C) * r * softplus)
        m2 = 1.0 - a * a
        mult = jnp.where(m2 > 0.0, m2 * lax.rsqrt(m2), 0.0)
        u = mult * (ig * xc)
        for c in range(nslab):
            a_scr[c, pl.ds(b, ts, stride=nb), :] = a[:, c * LANES:(c + 1) * LANES]
            u_scr[c, pl.ds(b, ts, stride=nb), :] = u[:, c * LANES:(c + 1) * LANES]

    def body(g, hs):
        for j in range(SCAN_UNROLL):
            off = pl.multiple_of((g * SCAN_UNROLL + j) * nb, nb)
            new = []
            for c in range(nslab):
                h = a_scr[c, pl.ds(off, nb), :] * hs[c] + u_scr[c, pl.ds(off, nb), :]
                u_scr[c, pl.ds(off, nb), :] = h
                new.append(h)
            hs = tuple(new)
        return hs

    hs = lax.fori_loop(0, ts // SCAN_UNROLL, body, tuple(h_scr[c] for c in range(nslab)))
    for c in range(nslab):
        h_scr[c] = hs[c]
    for b in range(nb):
        h = jnp.concatenate([u_scr[c, pl.ds(b, ts, stride=nb), :] for c in range(nslab)], axis=-1)
        gate = jax.nn.gelu(gr_ref[b].astype(F32), approximate=True)
        o_ref[b] = (h * gate).astype(o_ref.dtype)


SCAN_UNROLL = 8


def _block_diag_chunks(w):
    per = MXU_DIM // RNN_BLOCK
    w = w.reshape(D_RNN // MXU_DIM, per, RNN_BLOCK, RNN_BLOCK)
    eye = jnp.eye(per, dtype=w.dtype)
    out = jnp.einsum('gpcd,pq->gpcqd', w, eye)
    return out.reshape(D_RNN // MXU_DIM, MXU_DIM, MXU_DIM)


def _rglru(proj, B, S, conv_w, conv_b, w_a, b_a, w_x, b_x, lru_lambda):
    assert B <= SUBLANES, "all batch rows of a time step share one vreg in the scan"
    ts = min(TS_RGLRU, S)
    ns = S // ts
    wa = _block_diag_chunks(w_a).astype(BF16)
    wx = _block_diag_chunks(w_x).astype(BF16)
    nchunk = D_RNN // MXU_DIM
    nslab = D_RNN // LANES
    row = lambda v: v.reshape(1, D_RNN)
    const2 = lambda s: (0, 0)
    proj3 = proj.reshape(B, S, proj.shape[-1])
    out = pl.pallas_call(
        _rglru_kernel,
        grid=(ns,),
        in_specs=[
            pl.BlockSpec((B, ts, D_RNN), lambda s: (0, s, COL_XR)),
            pl.BlockSpec((B, ts, D_RNN), lambda s: (0, s, COL_GR)),
            pl.BlockSpec((CONV_W, D_RNN), const2),
            pl.BlockSpec((1, D_RNN), const2),
            pl.BlockSpec((nchunk, MXU_DIM, MXU_DIM), lambda s: (0, 0, 0)),
            pl.BlockSpec((1, D_RNN), const2),
            pl.BlockSpec((nchunk, MXU_DIM, MXU_DIM), lambda s: (0, 0, 0)),
            pl.BlockSpec((1, D_RNN), const2),
            pl.BlockSpec((1, D_RNN), const2),
        ],
        out_specs=pl.BlockSpec((B, ts, D_RNN), lambda s: (0, s, 0)),
        out_shape=jax.ShapeDtypeStruct((B, S, D_RNN), BF16),
        scratch_shapes=[
            pltpu.VMEM((B, SUBLANES, D_RNN), F32),
            pltpu.VMEM((nslab, ts * B, LANES), F32),
            pltpu.VMEM((nslab, ts * B, LANES), F32),
            pltpu.VMEM((nslab, B, LANES), F32),
        ],
        compiler_params=_params(("arbitrary",)),
        name="rglru",
    )(proj3, proj3, conv_w, row(conv_b), wa, row(b_a), wx, row(b_x), row(lru_lambda))
    return out.reshape(B * S, D_RNN)


def _attn_kernel(q_ref, k_ref, v_ref, bias_ref, lq1_ref, lk1_ref, lq2_ref, lk2_ref, sg_ref,
                 o_ref, qs_scr, vx_scr, s_scr, m_scr, acc_scr, *, tq, lam_init):
    S = q_ref.shape[0]
    nq = S // tq
    scale = HEAD_DIM ** -0.5 * LOG2E
    lane = lax.broadcasted_iota(I32, (tq, V_DIM), 1)
    lam = (jnp.exp(jnp.sum(lq1_ref[...] * lk1_ref[...], keepdims=True))
           - jnp.exp(jnp.sum(lq2_ref[...] * lk2_ref[...], keepdims=True)) + lam_init)
    vx_scr[:, 0:V_DIM] = v_ref[...]
    vx_scr[:, V_DIM:2 * V_DIM] = jnp.ones((S, V_DIM), BF16)

    Z0 = 2

    def prep_q(qi, buf):
        q = (q_ref[pl.ds(pl.multiple_of(qi * tq, tq), tq), :].astype(F32) * scale).astype(BF16)
        zero = jnp.zeros_like(q)
        qs_scr[buf, 0:tq, :] = jnp.where(lane < HEAD_DIM, q, zero)
        qs_scr[buf, tq:2 * tq, :] = jnp.where(lane >= HEAD_DIM, q, zero)

    def scores(buf, j, dst):
        k = k_ref[pl.ds(pl.multiple_of(j * tq, tq), tq), :]
        s_scr[dst] = lax.dot_general(qs_scr[buf], k, (((1,), (1,)), ((), ())),
                                     preferred_element_type=F32)

    def step(j, src, bias_idx, prefetch):
        scores(*prefetch)
        s = s_scr[src]
        if bias_idx is not None:
            b = bias_ref[bias_idx]
            s = s + jnp.concatenate([b, b], axis=0)
        vx = vx_scr[pl.ds(pl.multiple_of(j * tq, tq), tq), :]
        m_prev = m_scr[...]
        m_new = jnp.maximum(m_prev, jnp.max(s, axis=1)[:, None])
        p = jnp.exp2(s - jnp.tile(m_new, (1, tq // LANES)))
        alpha = jnp.exp2(m_prev - m_new)
        acc_scr[...] = (jnp.tile(alpha, (1, 2)) * acc_scr[...]
                        + jnp.dot(p.astype(BF16), vx, preferred_element_type=F32))
        m_scr[...] = m_new

    def begin_block():
        m_scr[...] = jnp.full(m_scr.shape, NEG_INF, F32)
        acc_scr[...] = jnp.zeros(acc_scr.shape, F32)

    def diag_step(qi, par, src):
        nxt = jnp.minimum(qi + 1, nq - 1)
        prep_q(nxt, 1 - par)
        step(qi, src, 0, (1 - par, 0, Z0 + 1 - par))

    def end_block(qi):
        acc = acc_scr[...]
        o_all = acc[:, 0:V_DIM] / acc[:, V_DIM:2 * V_DIM]
        o = o_all[0:tq, :] - lam * o_all[tq:2 * tq, :]
        inv = lax.rsqrt(jnp.mean(o * o, axis=-1, keepdims=True) + EPS)
        y = (o * inv * sg_ref[...]) * (1.0 - lam_init)
        o_ref[pl.ds(pl.multiple_of(qi * tq, tq), tq), :] = y.astype(o_ref.dtype)

    def far_pairs(par, npairs):
        def body(m, c):
            step(2 * m + 1, 0, None, (par, 2 * m + 2, 1))
            step(2 * m + 2, 1, None, (par, 2 * m + 3, 0))
            return c

        lax.fori_loop(0, npairs, body, 0)

    prep_q(0, 0)
    scores(0, 0, Z0)
    begin_block()
    diag_step(0, 0, Z0)
    end_block(0)
    begin_block()
    step(0, Z0 + 1, 1, (1, 1, 0))
    diag_step(1, 1, 0)
    end_block(1)

    def block_pair(i, carry):
        qi = 2 * i
        begin_block()
        step(0, Z0, None, (0, 1, 0))
        far_pairs(0, i - 1)
        step(qi - 1, 0, 1, (0, qi, 1))
        diag_step(qi, 0, 1)
        end_block(qi)
        qi = 2 * i + 1
        begin_block()
        step(0, Z0 + 1, None, (1, 1, 0))
        far_pairs(1, i - 1)
        step(qi - 2, 0, None, (1, qi - 1, 1))
        step(qi - 1, 1, 1, (1, qi, 0))
        diag_step(qi, 1, 0)
        end_block(qi)
        return carry

    lax.fori_loop(1, nq // 2, block_pair, 0)


def _rel_bucket(n):
    n = jnp.maximum(n, 0)
    nf = jnp.maximum(n, MAX_EXACT).astype(F32)
    large = MAX_EXACT + (jnp.log(nf / MAX_EXACT) / math.log(MAX_DISTANCE / MAX_EXACT)
                         * (NUM_BUCKETS - MAX_EXACT)).astype(I32)
    large = jnp.minimum(large, NUM_BUCKETS - 1)
    return jnp.where(n < MAX_EXACT, n, large)


def _bias_tiles(rel_table, tq):
    i = jnp.arange(tq, dtype=I32)[:, None]
    j = jnp.arange(tq, dtype=I32)[None, :]
    table = rel_table.astype(F32) - rel_table[NUM_BUCKETS - 1].astype(F32)[None, :]
    tiles = []
    for delta in (0, tq):
        n = i - j + delta
        onehot = (_rel_bucket(n)[:, :, None] == jnp.arange(NUM_BUCKETS, dtype=I32)).astype(F32)
        b = jnp.einsum('ijb,bh->hij', onehot, table, precision=lax.Precision.HIGHEST) * LOG2E
        tiles.append(jnp.where((n >= 0)[None], b, NEG_INF))
    return jnp.stack(tiles, axis=1)


LOG2E = math.log2(math.e)


def _attention(proj, B, S, lq1, lk1, lq2, lk2, subln_g, rel_table, lam_init):
    T = B * S
    tq = min(ATT_TQ, S // 2)
    assert V_DIM == LANES and MAX_DISTANCE <= tq and S % (2 * tq) == 0
    bias = _bias_tiles(rel_table, tq)
    vec = lambda v: v.reshape(1, -1).astype(F32)
    const2 = lambda b, h: (0, 0)
    kern = functools.partial(_attn_kernel, tq=tq, lam_init=lam_init)
    return pl.pallas_call(
        kern,
        grid=(B, N_HEADS),
        in_specs=[
            pl.BlockSpec((S, V_DIM), lambda b, h: (b, COL_Q * N_HEADS + h)),
            pl.BlockSpec((S, V_DIM), lambda b, h: (b, COL_K * N_HEADS + h)),
            pl.BlockSpec((S, V_DIM), lambda b, h: (b, COL_V * N_HEADS + h)),
            pl.BlockSpec((None, 2, tq, tq), lambda b, h: (h, 0, 0, 0)),
            pl.BlockSpec((1, HEAD_DIM), const2),
            pl.BlockSpec((1, HEAD_DIM), const2),
            pl.BlockSpec((1, HEAD_DIM), const2),
            pl.BlockSpec((1, HEAD_DIM), const2),
            pl.BlockSpec((1, V_DIM), const2),
        ],
        out_specs=pl.BlockSpec((S, V_DIM), lambda b, h: (b, h)),
        out_shape=jax.ShapeDtypeStruct((T, ATT_V), BF16),
        scratch_shapes=[
            pltpu.VMEM((2, 2 * tq, V_DIM), BF16),
            pltpu.VMEM((S, 2 * V_DIM), BF16),
            pltpu.VMEM((4, 2 * tq, tq), F32),
            pltpu.VMEM((2 * tq, LANES), F32),
            pltpu.VMEM((2 * tq, 2 * V_DIM), F32),
        ],
        compiler_params=_params(("parallel", "arbitrary")),
        name="diff_attn",
    )(proj, proj, proj, bias, vec(lq1), vec(lk1), vec(lq2), vec(lk2), vec(subln_g))


def _merge_kernel(x_ref, yr_ref, ya_ref, g0_ref, g1_ref, wr_ref, wa_ref, wo_ref, gf_ref, wrt_ref, brt_ref,
                  x2_ref, hf_ref, idx_ref, wgt_ref):
    pr = jnp.dot(yr_ref[...], wr_ref[...], preferred_element_type=F32)
    pa = jnp.dot(ya_ref[...], wa_ref[...], preferred_element_type=F32)
    merged = (jax.nn.sigmoid(g0_ref[...].astype(F32)) * pr
              + jax.nn.sigmoid(g1_ref[...].astype(F32)) * pa)
    x2 = x_ref[...] + jnp.dot(merged.astype(BF16), wo_ref[...], preferred_element_type=F32)
    x2_ref[...] = x2
    inv = lax.rsqrt(jnp.mean(x2 * x2, axis=-1, keepdims=True) + EPS)
    hf = x2 * inv * gf_ref[...]
    _store_token_major(hf_ref, hf)
    nt = (((1,), (1,)), ((), ()))
    hi = hf.astype(BF16)
    lo = (hf - hi.astype(F32)).astype(BF16)
    hw = lax.dot_general(wrt_ref[...], hi, nt, preferred_element_type=F32)
    lw = lax.dot_general(wrt_ref[0:LANES, :], lo, nt, preferred_element_type=F32)
    logits = (hw[0:N_EXPERTS, :] + hw[LANES:LANES + N_EXPERTS, :] + lw[0:N_EXPERTS, :]
              + brt_ref[...])
    tm = logits.shape[1]
    row = lax.broadcasted_iota(I32, (N_EXPERTS, tm), 0)
    work = logits
    idx_rows, val_rows = [], []
    for k in range(TOP_K):
        mx = jnp.max(work, axis=0, keepdims=True)
        sel = jnp.min(jnp.where(work == mx, row, N_EXPERTS), axis=0, keepdims=True)
        idx_rows.append(sel)
        val_rows.append(mx)
        work = jnp.where(row == sel, -jnp.inf, work)
    es = [jnp.exp(v - val_rows[0]) for v in val_rows]
    inv_sum = 1.0 / functools.reduce(lambda a, b: a + b, es)
    pad_i = [jnp.zeros((1, tm), I32)] * (SUBLANES - TOP_K)
    pad_f = [jnp.zeros((1, tm), F32)] * (SUBLANES - TOP_K)
    idx_ref[...] = jnp.concatenate(idx_rows + pad_i, axis=0)
    wgt_ref[...] = jnp.concatenate([e * inv_sum for e in es] + pad_f, axis=0)


def _merge_router(x2d, y_rnn, y_att, proj, w_pr, w_pa, w_o, g_ffn, w_router, b_router):
    T, D = x2d.shape
    tm = min(TM_MERGE, T)
    w_hi = w_router.astype(BF16)
    w_lo = (w_router - w_hi.astype(F32)).astype(BF16)
    wrt = (jnp.zeros((2 * LANES, D), BF16).at[:N_EXPERTS, :].set(w_hi.T)
           .at[LANES:LANES + N_EXPERTS, :].set(w_lo.T))
    brt = b_router.reshape(N_EXPERTS, 1).astype(F32)
    rowblk = lambda c: pl.BlockSpec((tm, D), lambda i, c=c: (i, c))
    full = lambda a: pl.BlockSpec(a.shape, lambda i: (0,) * a.ndim)
    wr, wa, wo = w_pr.astype(BF16), w_pa.astype(BF16), w_o.astype(BF16)
    gf = g_ffn.reshape(1, D)
    return pl.pallas_call(
        _merge_kernel,
        grid=(T // tm,),
        in_specs=[rowblk(0), rowblk(0), rowblk(0), rowblk(COL_G0), rowblk(COL_G1),
                  full(wr), full(wa), full(wo), full(gf), full(wrt), full(brt)],
        out_specs=[rowblk(0), pl.BlockSpec((tm * ROW_TILES, LANES), lambda i: (i, 0)),
                   pl.BlockSpec((SUBLANES, tm), lambda i: (0, i)),
                   pl.BlockSpec((SUBLANES, tm), lambda i: (0, i))],
        out_shape=[jax.ShapeDtypeStruct((T, D), F32), jax.ShapeDtypeStruct((T * ROW_TILES, LANES), F32),
                   jax.ShapeDtypeStruct((SUBLANES, T), I32), jax.ShapeDtypeStruct((SUBLANES, T), F32)],
        compiler_params=_params(("parallel",)),
        name="merge_router",
    )(x2d, y_rnn, y_att, proj, proj, wr, wa, wo, gf, wrt, brt)


def _route_kernel(idx_ref, dest_ref, cnt_ref, cnt_scr, run_scr, start_scr, *, blk):
    ph = pl.program_id(0)
    i = pl.program_id(1)
    tt = idx_ref.shape[1]
    idx = idx_ref[...]
    row = lax.broadcasted_iota(I32, (N_EXPERTS, tt), 0)
    onehot = jnp.zeros((N_EXPERTS, tt), F32)
    for k in range(TOP_K):
        onehot = onehot + (idx[k:k + 1, :] == row).astype(F32)
    tile_cnt = jnp.sum(onehot, axis=1, keepdims=True)

    @pl.when((ph == 0) & (i == 0))
    def _():
        cnt_scr[...] = jnp.zeros_like(cnt_scr)

    @pl.when(ph == 0)
    def _():
        cnt_scr[...] += tile_cnt

    @pl.when((ph == 1) & (i == 0))
    def _():
        padded = jnp.floor((cnt_scr[...] + (blk - 1)) / blk) * blk
        r = lax.broadcasted_iota(I32, (N_EXPERTS, N_EXPERTS), 0)
        c = lax.broadcasted_iota(I32, (N_EXPERTS, N_EXPERTS), 1)
        start_scr[...] = jnp.dot((c < r).astype(F32), padded, preferred_element_type=F32,
                                 precision=lax.Precision.HIGHEST)
        run_scr[...] = jnp.zeros_like(run_scr)

    @pl.when(ph == 1)
    def _():
        r = lax.broadcasted_iota(I32, (tt, tt), 0)
        c = lax.broadcasted_iota(I32, (tt, tt), 1)
        earlier = (r < c).astype(BF16)
        before = jnp.dot(onehot.astype(BF16), earlier, preferred_element_type=F32)
        base = before + run_scr[:, 0:1] + start_scr[:, 0:1]
        rows = [jnp.sum(jnp.where(idx[k:k + 1, :] == row, base, 0.0), axis=0, keepdims=True)
                for k in range(TOP_K)]
        rows += [jnp.zeros((1, tt), F32)] * (SUBLANES - TOP_K)
        dest_ref[...] = jnp.concatenate(rows, axis=0).astype(I32)
        run_scr[...] += tile_cnt
        cnt_ref[...] = cnt_scr[...]


def _route(top_idx, blk):
    T = top_idx.shape[1]
    tt = min(TT_ROUTE, T)
    kern = functools.partial(_route_kernel, blk=blk)
    return pl.pallas_call(
        kern,
        grid=(2, T // tt),
        in_specs=[pl.BlockSpec((SUBLANES, tt), lambda p, i: (0, i))],
        out_specs=[pl.BlockSpec((SUBLANES, tt), lambda p, i: (0, p * i)),
                   pl.BlockSpec((N_EXPERTS, LANES), lambda p, i: (0, 0))],
        out_shape=[jax.ShapeDtypeStruct((SUBLANES, T), I32),
                   jax.ShapeDtypeStruct((N_EXPERTS, LANES), F32)],
        scratch_shapes=[pltpu.VMEM((N_EXPERTS, LANES), F32), pltpu.VMEM((N_EXPERTS, LANES), F32),
                        pltpu.VMEM((N_EXPERTS, LANES), F32)],
        compiler_params=_params(("arbitrary", "arbitrary")),
        name="route",
    )(top_idx)


def _for_each_token_k(n_tok, fn):
    def body(g, c):
        for u in range(TOKEN_UNROLL):
            for k in range(TOP_K):
                fn(g * TOKEN_UNROLL + u, k)
        return c

    lax.fori_loop(0, n_tok // TOKEN_UNROLL, body, 0)


def _tile_major_ranks(dest, tt):
    T = dest.shape[1]
    return dest[:TOP_K].reshape(TOP_K, T // tt, tt).transpose(1, 0, 2).reshape(-1)


def _row_tile(ref, row):
    return ref.at[pl.ds(pl.multiple_of(row * ROW_TILES, ROW_TILES), ROW_TILES), :]


def _dispatch_kernel(ends_ref, dest_ref, hf_ref, xs_ref, zero_buf, sem, zsem, *, tt, blk):
    @pl.when(pl.program_id(0) == 0)
    def _():
        zero_buf[...] = jnp.zeros_like(zero_buf)

        def tail_fill(e):
            prev_end = ends_ref[e - 1] if e > 0 else 0
            start = pl.multiple_of((ends_ref[e] - blk) * ROW_TILES, ROW_TILES)
            return ends_ref[e] > prev_end, pltpu.make_async_copy(
                zero_buf, xs_ref.at[pl.ds(start, blk * ROW_TILES), :], zsem)

        for e in range(N_EXPERTS):
            nonempty, cp = tail_fill(e)
            pl.when(nonempty)(cp.start)
        for e in range(N_EXPERTS):
            nonempty, cp = tail_fill(e)
            pl.when(nonempty)(cp.wait)

    def row_copy(t, k):
        return pltpu.make_async_copy(_row_tile(hf_ref, t),
                                     _row_tile(xs_ref, dest_ref[k * tt + t]), sem)

    _for_each_token_k(tt, lambda t, k: row_copy(t, k).start(priority=k % 2))
    _for_each_token_k(tt, lambda t, k: row_copy(t, k).wait())


def _dispatch(ends, dest, hf_tm, n_slots, blk):
    T = hf_tm.shape[0] // ROW_TILES
    tt = min(TT_DISPATCH, T)
    kern = functools.partial(_dispatch_kernel, tt=tt, blk=blk)
    grid_spec = pltpu.PrefetchScalarGridSpec(
        num_scalar_prefetch=1,
        grid=(T // tt,),
        in_specs=[pl.BlockSpec((TOP_K * tt,), lambda i, ends: (i,), memory_space=pltpu.SMEM),
                  pl.BlockSpec((tt * ROW_TILES, LANES), lambda i, ends: (i, 0))],
        out_specs=pl.BlockSpec(memory_space=pl.ANY),
        scratch_shapes=[pltpu.VMEM((blk * ROW_TILES, LANES), F32),
                        pltpu.SemaphoreType.DMA(()), pltpu.SemaphoreType.DMA(())],
    )
    return pl.pallas_call(
        kern,
        grid_spec=grid_spec,
        out_shape=jax.ShapeDtypeStruct((n_slots * ROW_TILES, LANES), F32),
        compiler_params=_params(("arbitrary",)),
        name="dispatch",
    )(ends, _tile_major_ranks(dest, tt), hf_tm)


def _expert_kernel(be_ref, bsrc_ref, act_ref, first_ref, x_ref, wg_ref, bg_ref, wu_ref, bu_ref, wd_ref, bd_ref,
                   o_ref, wg_scr, wu_scr, wd_scr, *, blk):
    del be_ref, bsrc_ref
    i = pl.program_id(0)

    @pl.when(first_ref[i] == 1)
    def _():
        wg_scr[...] = wg_ref[0].astype(BF16)
        wu_scr[...] = wu_ref[0].astype(BF16)
        wd_scr[...] = wd_ref[0].astype(BF16)

    @pl.when(act_ref[i] == 1)
    def _():
        x = _load_token_major(x_ref, 0, blk).astype(BF16)
        g = jnp.dot(x, wg_scr[...], preferred_element_type=F32) + bg_ref[0]
        u = jnp.dot(x, wu_scr[...], preferred_element_type=F32) + bu_ref[0]
        g = jnp.minimum(g, SWIGLU_LIMIT)
        u = jnp.clip(u, -SWIGLU_LIMIT, SWIGLU_LIMIT)
        act = (u + 1.0) * (g * jax.nn.sigmoid(SWIGLU_ALPHA * g))
        y = jnp.dot(act.astype(BF16), wd_scr[...], preferred_element_type=F32) + bd_ref[0]
        _store_token_major(o_ref, y)


def _experts(x_slots, blk, block_expert, block_src, block_active, block_first, wg, bg, wu, bu, wd, bd):
    D = D_MODEL
    nb = x_slots.shape[0] // (blk * ROW_TILES)
    wspec = pl.BlockSpec((1, D, D_FF), lambda i, be, bs, ac, fi: (be[i], 0, 0))
    bspec = pl.BlockSpec((1, 1, D_FF), lambda i, be, bs, ac, fi: (be[i], 0, 0))
    xspec = pl.BlockSpec((blk * ROW_TILES, LANES), lambda i, be, bs, ac, fi: (bs[i], 0))
    grid_spec = pltpu.PrefetchScalarGridSpec(
        num_scalar_prefetch=4,
        grid=(nb,),
        in_specs=[xspec, wspec, bspec, wspec, bspec, wspec, bspec],
        out_specs=xspec,
        scratch_shapes=[pltpu.VMEM((D, D_FF), BF16), pltpu.VMEM((D, D_FF), BF16),
                        pltpu.VMEM((D_FF, D), BF16)],
    )
    b3 = lambda b: b.reshape(N_EXPERTS, 1, -1)
    return pl.pallas_call(
        functools.partial(_expert_kernel, blk=blk),
        grid_spec=grid_spec,
        out_shape=jax.ShapeDtypeStruct(x_slots.shape, F32),
        compiler_params=_params(("arbitrary",)),
        name="experts",
    )(block_expert, block_src, block_active, block_first, x_slots,
      wg, b3(bg), wu, b3(bu), wd, b3(bd))


def _combine_kernel(dest_ref, dest_next_ref, ys_ref, w_ref, x2_ref, g_ref, o_ref, buf, sems, *, tt):
    i = pl.program_id(0)
    n = pl.num_programs(0)
    cur = i % 2

    def row_copy(idx_ref, slot, t, k):
        return pltpu.make_async_copy(_row_tile(ys_ref, idx_ref[k * tt + t]),
                                     _row_tile(buf, (slot * TOP_K + k) * tt + t), sems.at[slot])

    def gather(idx_ref, slot):
        _for_each_token_k(tt, lambda t, k: row_copy(idx_ref, slot, t, k).start(priority=k % 2))

    @pl.when(i == 0)
    def _():
        gather(dest_ref, 0)

    @pl.when(i + 1 < n)
    def _():
        gather(dest_next_ref, 1 - cur)

    _for_each_token_k(tt, lambda t, k: row_copy(dest_ref, cur, t, k).wait())
    w = w_ref[...]
    y = x2_ref[...]
    for k in range(TOP_K):
        y = y + w[:, k:k + 1] * _load_token_major(buf, (cur * TOP_K + k) * tt, tt)
    inv = lax.rsqrt(jnp.mean(y * y, axis=-1, keepdims=True) + EPS)
    o_ref[...] = y * inv * g_ref[...]


def _combine(dest, y_slots, top_w, x2, g_final):
    T, D = x2.shape
    tt = min(TT_COMBINE, T)
    nt = T // tt
    kern = functools.partial(_combine_kernel, tt=tt)
    return pl.pallas_call(
        kern,
        grid=(nt,),
        in_specs=[pl.BlockSpec((TOP_K * tt,), lambda i: (i,), memory_space=pltpu.SMEM),
                  pl.BlockSpec((TOP_K * tt,), lambda i: (jnp.minimum(i + 1, nt - 1),),
                               memory_space=pltpu.SMEM),
                  pl.BlockSpec(memory_space=pl.ANY),
                  pl.BlockSpec((tt, SUBLANES), lambda i: (i, 0)),
                  pl.BlockSpec((tt, D), lambda i: (i, 0)),
                  pl.BlockSpec((1, D), lambda i: (0, 0))],
        out_specs=pl.BlockSpec((tt, D), lambda i: (i, 0)),
        out_shape=jax.ShapeDtypeStruct((T, D), F32),
        scratch_shapes=[pltpu.VMEM((2 * TOP_K * tt * ROW_TILES, LANES), F32),
                        pltpu.SemaphoreType.DMA((2,))],
        compiler_params=_params(("arbitrary",)),
        name="combine",
    )(_tile_major_ranks(dest, tt), _tile_major_ranks(dest, tt), y_slots, top_w.T, x2,
      g_final.reshape(1, D))


def _moe_block_size(T):
    return min(MOE_BLOCK, max(SUBLANES, T * TOP_K // N_EXPERTS))


def _block_tables(counts, blk, nb):
    cnt = counts.astype(I32)
    padded = (cnt + blk - 1) // blk * blk
    ends = jnp.cumsum(padded)
    used = ends[-1] // blk
    starts = jnp.arange(nb, dtype=I32) * blk
    src = jnp.minimum(jnp.arange(nb, dtype=I32), jnp.maximum(used - 1, 0))
    expert = jnp.minimum(jnp.sum((starts[:, None] >= ends[None, :]).astype(I32), axis=1), N_EXPERTS - 1)
    expert = expert[src]
    active = (jnp.arange(nb, dtype=I32) < used).astype(I32)
    prev = jnp.concatenate([jnp.full((1,), -1, I32), expert[:-1]])
    first = active * (expert != prev).astype(I32)
    return ends.astype(I32), expert, src, active, first


def kernel(x, norm_mix_g, w_in, conv_w, conv_b, w_rg_a, b_rg_a, w_rg_x, b_rg_x, lru_lambda, diff_lambda_q1, diff_lambda_k1, diff_lambda_q2, diff_lambda_k2, subln_g, rel_bias_table, w_proj_rnn, w_proj_att, w_out, norm_ffn_g, w_router, b_router, w_gate_e, b_gate_e, w_up_e, b_up_e, w_down_e, b_down_e, norm_final_g):
    B, S, D = x.shape
    T = B * S
    assert norm_mix_g.shape[0] == 1, "single-layer block: the final norm is fused into the MoE combine"
    l = 0
    xt = x.reshape(T, D)
    lam_init = 0.8 - 0.6 * math.exp(-0.3 * l)
    proj = _inproj(xt, norm_mix_g[l], w_in[l].astype(BF16))
    y_rnn = _rglru(proj, B, S, conv_w[l], conv_b[l], w_rg_a[l], b_rg_a[l], w_rg_x[l], b_rg_x[l],
                   lru_lambda[l])
    y_att = _attention(proj, B, S, diff_lambda_q1[l], diff_lambda_k1[l], diff_lambda_q2[l],
                       diff_lambda_k2[l], subln_g[l], rel_bias_table, lam_init)
    x2, hf, top_idx, top_w = _merge_router(xt, y_rnn, y_att, proj, w_proj_rnn[l], w_proj_att[l],
                                           w_out[l], norm_ffn_g[l], w_router[l], b_router[l])
    blk = _moe_block_size(T)
    nb = T * TOP_K // blk + N_EXPERTS
    dest, counts = _route(top_idx, blk)
    ends, expert, src, active, first = _block_tables(counts[:, 0], blk, nb)
    x_slots = _dispatch(ends, dest, hf, nb * blk, blk)
    y_slots = _experts(x_slots, blk, expert, src, active, first, w_gate_e[l], b_gate_e[l], w_up_e[l],
                       b_up_e[l], w_down_e[l], b_down_e[l])
    out = _combine(dest, y_slots, top_w, x2, norm_final_g)
    return out.reshape(B, S, D)
```

```python
import functools
import math

import jax
import jax.numpy as jnp
from jax import lax
from jax.experimental import pallas as pl
from jax.experimental.pallas import tpu as pltpu

F32 = jnp.float32
BF16 = jnp.bfloat16
I32 = jnp.int32

D_MODEL = 1024
D_RNN = 1024
RNN_BLOCKS = 16
RNN_BLOCK = D_RNN // RNN_BLOCKS
CONV_W = 4
RGLRU_C = 8.0
N_HEADS = 8
HEAD_DIM = 64
V_DIM = 2 * HEAD_DIM
ATT_QK = N_HEADS * 2 * HEAD_DIM
ATT_V = N_HEADS * V_DIM
D_IN = 2 * D_RNN + 2 * ATT_QK + ATT_V + 2 * D_MODEL
NUM_BUCKETS = 32
MAX_EXACT = NUM_BUCKETS // 2
MAX_DISTANCE = 128
N_EXPERTS = 32
TOP_K = 4
D_FF = D_MODEL
SWIGLU_LIMIT = 7.0
SWIGLU_ALPHA = 1.702
EPS = 1e-6
NEG_INF = -1e30

LANES = 128
SUBLANES = 8
MXU_DIM = 256
VMEM_LIMIT = 56 * 1024 * 1024

TM_INPROJ = 1024
TN_INPROJ = 3584
TS_RGLRU = 256
ATT_TQ = 512
TM_MERGE = 512
TT_ROUTE = 1024
TT_DISPATCH = 1024
TT_COMBINE = 512
MOE_BLOCK = 512
TOKEN_UNROLL = 32

COL_XR, COL_GR, COL_Q, COL_K, COL_V, COL_G0, COL_G1 = range(7)


def _params(sem, vmem=VMEM_LIMIT):
    return pltpu.CompilerParams(dimension_semantics=sem, vmem_limit_bytes=vmem)


ROW_TILES = D_MODEL // LANES
assert ROW_TILES == SUBLANES


def _store_token_major(ref, val, start=0):
    n = val.shape[0]
    for c in range(ROW_TILES):
        ref[pl.ds(start * ROW_TILES + c, n, stride=ROW_TILES), :] = val[:, c * LANES:(c + 1) * LANES]


def _load_token_major(ref, start, n):
    return jnp.concatenate(
        [ref[pl.ds(start * ROW_TILES + c, n, stride=ROW_TILES), :] for c in range(ROW_TILES)], axis=-1)


def _inproj_kernel(x_ref, g_ref, w_ref, o_ref, h_scr):
    @pl.when(pl.program_id(1) == 0)
    def _():
        x = x_ref[...]
        inv = lax.rsqrt(jnp.mean(x * x, axis=-1, keepdims=True) + EPS)
        h_scr[...] = (x * inv * g_ref[...]).astype(BF16)

    o_ref[...] = jnp.dot(h_scr[...], w_ref[...], preferred_element_type=F32).astype(o_ref.dtype)


def _inproj(x2d, g, w_bf16):
    T, D = x2d.shape
    N = w_bf16.shape[1]
    tm = min(TM_INPROJ, T)
    tn = TN_INPROJ if N % TN_INPROJ == 0 else D_MODEL
    return pl.pallas_call(
        _inproj_kernel,
        grid=(T // tm, N // tn),
        in_specs=[
            pl.BlockSpec((tm, D), lambda i, j: (i, 0)),
            pl.BlockSpec((1, D), lambda i, j: (0, 0)),
            pl.BlockSpec((D, tn), lambda i, j: (0, j)),
        ],
        out_specs=pl.BlockSpec((tm, tn), lambda i, j: (i, j)),
        out_shape=jax.ShapeDtypeStruct((T, N), BF16),
        scratch_shapes=[pltpu.VMEM((tm, D), BF16)],
        compiler_params=_params(("parallel", "arbitrary")),
        name="inproj",
    )(x2d, g.reshape(1, D), w_bf16)


def _rglru_kernel(xr_ref, gr_ref, cw_ref, cb_ref, wa_ref, ba_ref, wx_ref, bx_ref, lam_ref,
                  o_ref, tail_scr, a_scr, u_scr, h_scr):
    nb, ts, _ = xr_ref.shape
    pad = SUBLANES
    nslab = D_RNN // LANES

    @pl.when(pl.program_id(0) == 0)
    def _():
        tail_scr[...] = jnp.zeros_like(tail_scr)
        h_scr[...] = jnp.zeros_like(h_scr)

    cw = cw_ref[...]
    z = -lam_ref[...]
    softplus = jnp.maximum(z, 0.0) + jnp.log1p(jnp.exp(-jnp.abs(z)))
    sigmoid = lambda v: 0.5 * jnp.tanh(0.5 * v) + 0.5
    nchunk = D_RNN // MXU_DIM
    nshift = CONV_W - 1
    sr = lax.broadcasted_iota(I32, (nshift * ts, ts), 0)
    sc = lax.broadcasted_iota(I32, (nshift * ts, ts), 1)
    shift_mat = jnp.zeros((nshift * ts, ts), F32)
    for d in range(1, CONV_W):
        hit = (sr >= (d - 1) * ts) & (sr < d * ts) & (sr - (d - 1) * ts - d == sc)
        shift_mat = jnp.where(hit, 1.0, shift_mat)
    shift_mat = shift_mat.astype(BF16)
    row8 = lax.broadcasted_iota(I32, (pad, D_RNN), 0)
    for b in range(nb):
        xb = xr_ref[b]
        x = xb.astype(F32)
        shifted = jnp.dot(shift_mat, xb, preferred_element_type=F32)
        xc = cw[CONV_W - 1:CONV_W, :] * x + cb_ref[...]
        tail = tail_scr[b]
        head_fix = jnp.zeros((pad, D_RNN), F32)
        for d in range(1, CONV_W):
            wd = cw[CONV_W - 1 - d:CONV_W - d, :]
            xc = xc + wd * shifted[(d - 1) * ts:d * ts, :]
            head_fix = head_fix + wd * jnp.where(row8 < d, pltpu.roll(tail, d, 0), 0.0)
        xc = jnp.concatenate([xc[0:pad, :] + head_fix, xc[pad:ts, :]], axis=0)
        tail_scr[b] = x[ts - pad:ts, :]
        xcb = xc.astype(BF16)
        r_pre = jnp.concatenate(
            [jnp.dot(xcb[:, c * MXU_DIM:(c + 1) * MXU_DIM], wa_ref[c], preferred_element_type=F32)
             for c in range(nchunk)], axis=-1)
        i_pre = jnp.concatenate(
            [jnp.dot(xcb[:, c * MXU_DIM:(c + 1) * MXU_DIM], wx_ref[c], preferred_element_type=F32)
             for c in range(nchunk)], axis=-1)
        r = sigmoid(r_pre + ba_ref[...])
        ig = sigmoid(i_pre + bx_ref[...])
        a = jnp.exp((-RGLRU_C) * r * softplus)
        m2 = 1.0 - a * a
        mult = jnp.where(m2 > 0.0, m2 * lax.rsqrt(m2), 0.0)
        u = mult * (ig * xc)
        for c in range(nslab):
            a_scr[c, pl.ds(b, ts, stride=nb), :] = a[:, c * LANES:(c + 1) * LANES]
            u_scr[c, pl.ds(b, ts, stride=nb), :] = u[:, c * LANES:(c + 1) * LANES]

    def body(g, hs):
        for j in range(SCAN_UNROLL):
            off = pl.multiple_of((g * SCAN_UNROLL + j) * nb, nb)
            new = []
            for c in range(nslab):
                h = a_scr[c, pl.ds(off, nb), :] * hs[c] + u_scr[c, pl.ds(off, nb), :]
                u_scr[c, pl.ds(off, nb), :] = h
                new.append(h)
            hs = tuple(new)
        return hs

    hs = lax.fori_loop(0, ts // SCAN_UNROLL, body, tuple(h_scr[c] for c in range(nslab)))
    for c in range(nslab):
        h_scr[c] = hs[c]
    for b in range(nb):
        h = jnp.concatenate([u_scr[c, pl.ds(b, ts, stride=nb), :] for c in range(nslab)], axis=-1)
        gate = jax.nn.gelu(gr_ref[b].astype(F32), approximate=True)
        o_ref[b] = (h * gate).astype(o_ref.dtype)


SCAN_UNROLL = 8


def _block_diag_chunks(w):
    per = MXU_DIM // RNN_BLOCK
    w = w.reshape(D_RNN // MXU_DIM, per, RNN_BLOCK, RNN_BLOCK)
    eye = jnp.eye(per, dtype=w.dtype)
    out = jnp.einsum('gpcd,pq->gpcqd', w, eye)
    return out.reshape(D_RNN // MXU_DIM, MXU_DIM, MXU_DIM)


def _rglru(proj, B, S, conv_w, conv_b, w_a, b_a, w_x, b_x, lru_lambda):
    assert B <= SUBLANES, "all batch rows of a time step share one vreg in the scan"
    ts = min(TS_RGLRU, S)
    ns = S // ts
    wa = _block_diag_chunks(w_a).astype(BF16)
    wx = _block_diag_chunks(w_x).astype(BF16)
    nchunk = D_RNN // MXU_DIM
    nslab = D_RNN // LANES
    row = lambda v: v.reshape(1, D_RNN)
    const2 = lambda s: (0, 0)
    proj3 = proj.reshape(B, S, proj.shape[-1])
    out = pl.pallas_call(
        _rglru_kernel,
        grid=(ns,),
        in_specs=[
            pl.BlockSpec((B, ts, D_RNN), lambda s: (0, s, COL_XR)),
            pl.BlockSpec((B, ts, D_RNN), lambda s: (0, s, COL_GR)),
            pl.BlockSpec((CONV_W, D_RNN), const2),
            pl.BlockSpec((1, D_RNN), const2),
            pl.BlockSpec((nchunk, MXU_DIM, MXU_DIM), lambda s: (0, 0, 0)),
            pl.BlockSpec((1, D_RNN), const2),
            pl.BlockSpec((nchunk, MXU_DIM, MXU_DIM), lambda s: (0, 0, 0)),
            pl.BlockSpec((1, D_RNN), const2),
            pl.BlockSpec((1, D_RNN), const2),
        ],
        out_specs=pl.BlockSpec((B, ts, D_RNN), lambda s: (0, s, 0)),
        out_shape=jax.ShapeDtypeStruct((B, S, D_RNN), BF16),
        scratch_shapes=[
            pltpu.VMEM((B, SUBLANES, D_RNN), F32),
            pltpu.VMEM((nslab, ts * B, LANES), F32),
            pltpu.VMEM((nslab, ts * B, LANES), F32),
            pltpu.VMEM((nslab, B, LANES), F32),
        ],
        compiler_params=_params(("arbitrary",)),
        name="rglru",
    )(proj3, proj3, conv_w, row(conv_b), wa, row(b_a), wx, row(b_x), row(lru_lambda))
    return out.reshape(B * S, D_RNN)


def _attn_kernel(q_ref, k_ref, v_ref, bias_ref, lq1_ref, lk1_ref, lq2_ref, lk2_ref, sg_ref,
                 o_ref, qs_scr, vx_scr, s_scr, m_scr, acc_scr, *, tq, lam_init):
    S = q_ref.shape[0]
    nq = S // tq
    scale = HEAD_DIM ** -0.5 * LOG2E
    lane = lax.broadcasted_iota(I32, (tq, V_DIM), 1)
    lam = (jnp.exp(jnp.sum(lq1_ref[...] * lk1_ref[...], keepdims=True))
           - jnp.exp(jnp.sum(lq2_ref[...] * lk2_ref[...], keepdims=True)) + lam_init)
    vx_scr[:, 0:V_DIM] = v_ref[...]
    vx_scr[:, V_DIM:2 * V_DIM] = jnp.ones((S, V_DIM), BF16)

    Z0 = 2

    def prep_q(qi, buf):
        q = (q_ref[pl.ds(pl.multiple_of(qi * tq, tq), tq), :].astype(F32) * scale).astype(BF16)
        zero = jnp.zeros_like(q)
        qs_scr[buf, 0:tq, :] = jnp.where(lane < HEAD_DIM, q, zero)
        qs_scr[buf, tq:2 * tq, :] = jnp.where(lane >= HEAD_DIM, q, zero)

    def scores(buf, j, dst):
        k = k_ref[pl.ds(pl.multiple_of(j * tq, tq), tq), :]
        s_scr[dst] = lax.dot_general(qs_scr[buf], k, (((1,), (1,)), ((), ())),
                                     preferred_element_type=F32)

    def step(j, src, bias_idx, prefetch):
        scores(*prefetch)
        s = s_scr[src]
        if bias_idx is not None:
            b = bias_ref[bias_idx]
            s = s + jnp.concatenate([b, b], axis=0)
        vx = vx_scr[pl.ds(pl.multiple_of(j * tq, tq), tq), :]
        m_prev = m_scr[...]
        m_new = jnp.maximum(m_prev, jnp.max(s, axis=1)[:, None])
        p = jnp.exp2(s - jnp.tile(m_new, (1, tq // LANES)))
        alpha = jnp.exp2(m_prev - m_new)
        acc_scr[...] = (jnp.tile(alpha, (1, 2)) * acc_scr[...]
                        + jnp.dot(p.astype(BF16), vx, preferred_element_type=F32))
        m_scr[...] = m_new

    def begin_block():
        m_scr[...] = jnp.full(m_scr.shape, NEG_INF, F32)
        acc_scr[...] = jnp.zeros(acc_scr.shape, F32)

    def diag_step(qi, par, src):
        nxt = jnp.minimum(qi + 1, nq - 1)
        prep_q(nxt, 1 - par)
        step(qi, src, 0, (1 - par, 0, Z0 + 1 - par))

    def end_block(qi):
        acc = acc_scr[...]
        o_all = acc[:, 0:V_DIM] / acc[:, V_DIM:2 * V_DIM]
        o = o_all[0:tq, :] - lam * o_all[tq:2 * tq, :]
        inv = lax.rsqrt(jnp.mean(o * o, axis=-1, keepdims=True) + EPS)
        y = (o * inv * sg_ref[...]) * (1.0 - lam_init)
        o_ref[pl.ds(pl.multiple_of(qi * tq, tq), tq), :] = y.astype(o_ref.dtype)

    def far_pairs(par, npairs):
        def body(m, c):
            step(2 * m + 1, 0, None, (par, 2 * m + 2, 1))
            step(2 * m + 2, 1, None, (par, 2 * m + 3, 0))
            return c

        lax.fori_loop(0, npairs, body, 0)

    prep_q(0, 0)
    scores(0, 0, Z0)
    begin_block()
    diag_step(0, 0, Z0)
    end_block(0)
    begin_block()
    step(0, Z0 + 1, 1, (1, 1, 0))
    diag_step(1, 1, 0)
    end_block(1)

    def block_pair(i, carry):
        qi = 2 * i
        begin_block()
        step(0, Z0, None, (0, 1, 0))
        far_pairs(0, i - 1)
        step(qi - 1, 0, 1, (0, qi, 1))
        diag_step(qi, 0, 1)
        end_block(qi)
        qi = 2 * i + 1
        begin_block()
        step(0, Z0 + 1, None, (1, 1, 0))
        far_pairs(1, i - 1)
        step(qi - 2, 0, None, (1, qi - 1, 1))
        step(qi - 1, 1, 1, (1, qi, 0))
        diag_step(qi, 1, 0)
        end_block(qi)
        return carry

    lax.fori_loop(1, nq // 2, block_pair, 0)


def _rel_bucket(n):
    n = jnp.maximum(n, 0)
    nf = jnp.maximum(n, MAX_EXACT).astype(F32)
    large = MAX_EXACT + (jnp.log(nf / MAX_EXACT) / math.log(MAX_DISTANCE / MAX_EXACT)
                         * (NUM_BUCKETS - MAX_EXACT)).astype(I32)
    large = jnp.minimum(large, NUM_BUCKETS - 1)
    return jnp.where(n < MAX_EXACT, n, large)


def _bias_tiles(rel_table, tq):
    i = jnp.arange(tq, dtype=I32)[:, None]
    j = jnp.arange(tq, dtype=I32)[None, :]
    table = rel_table.astype(F32) - rel_table[NUM_BUCKETS - 1].astype(F32)[None, :]
    tiles = []
    for delta in (0, tq):
        n = i - j + delta
        onehot = (_rel_bucket(n)[:, :, None] == jnp.arange(NUM_BUCKETS, dtype=I32)).astype(F32)
        b = jnp.einsum('ijb,bh->hij', onehot, table, precision=lax.Precision.HIGHEST) * LOG2E
        tiles.append(jnp.where((n >= 0)[None], b, NEG_INF))
    return jnp.stack(tiles, axis=1)


LOG2E = math.log2(math.e)


def _attention(proj, B, S, lq1, lk1, lq2, lk2, subln_g, rel_table, lam_init):
    T = B * S
    tq = min(ATT_TQ, S // 2)
    assert V_DIM == LANES and MAX_DISTANCE <= tq and S % (2 * tq) == 0
    bias = _bias_tiles(rel_table, tq)
    vec = lambda v: v.reshape(1, -1).astype(F32)
    const2 = lambda b, h: (0, 0)
    kern = functools.partial(_attn_kernel, tq=tq, lam_init=lam_init)
    return pl.pallas_call(
        kern,
        grid=(B, N_HEADS),
        in_specs=[
            pl.BlockSpec((S, V_DIM), lambda b, h: (b, COL_Q * N_HEADS + h)),
            pl.BlockSpec((S, V_DIM), lambda b, h: (b, COL_K * N_HEADS + h)),
            pl.BlockSpec((S, V_DIM), lambda b, h: (b, COL_V * N_HEADS + h)),
            pl.BlockSpec((None, 2, tq, tq), lambda b, h: (h, 0, 0, 0)),
            pl.BlockSpec((1, HEAD_DIM), const2),
            pl.BlockSpec((1, HEAD_DIM), const2),
            pl.BlockSpec((1, HEAD_DIM), const2),
            pl.BlockSpec((1, HEAD_DIM), const2),
            pl.BlockSpec((1, V_DIM), const2),
        ],
        out_specs=pl.BlockSpec((S, V_DIM), lambda b, h: (b, h)),
        out_shape=jax.ShapeDtypeStruct((T, ATT_V), BF16),
        scratch_shapes=[
            pltpu.VMEM((2, 2 * tq, V_DIM), BF16),
            pltpu.VMEM((S, 2 * V_DIM), BF16),
            pltpu.VMEM((4, 2 * tq, tq), F32),
            pltpu.VMEM((2 * tq, LANES), F32),
            pltpu.VMEM((2 * tq, 2 * V_DIM), F32),
        ],
        compiler_params=_params(("parallel", "arbitrary")),
        name="diff_attn",
    )(proj, proj, proj, bias, vec(lq1), vec(lk1), vec(lq2), vec(lk2), vec(subln_g))


def _merge_kernel(x_ref, yr_ref, ya_ref, g0_ref, g1_ref, wr_ref, wa_ref, wo_ref, gf_ref, wrt_ref, brt_ref,
                  x2_ref, hf_ref, idx_ref, wgt_ref):
    pr = jnp.dot(yr_ref[...], wr_ref[...], preferred_element_type=F32)
    pa = jnp.dot(ya_ref[...], wa_ref[...], preferred_element_type=F32)
    merged = (jax.nn.sigmoid(g0_ref[...].astype(F32)) * pr
              + jax.nn.sigmoid(g1_ref[...].astype(F32)) * pa)
    x2 = x_ref[...] + jnp.dot(merged.astype(BF16), wo_ref[...], preferred_element_type=F32)
    x2_ref[...] = x2
    inv = lax.rsqrt(jnp.mean(x2 * x2, axis=-1, keepdims=True) + EPS)
    hf = x2 * inv * gf_ref[...]
    _store_token_major(hf_ref, hf)
    nt = (((1,), (1,)), ((), ()))
    hi = hf.astype(BF16)
    lo = (hf - hi.astype(F32)).astype(BF16)
    hw = lax.dot_general(wrt_ref[...], hi, nt, preferred_element_type=F32)
    lw = lax.dot_general(wrt_ref[0:LANES, :], lo, nt, preferred_element_type=F32)
    logits = (hw[0:N_EXPERTS, :] + hw[LANES:LANES + N_EXPERTS, :] + lw[0:N_EXPERTS, :]
              + brt_ref[...])
    tm = logits.shape[1]
    row = lax.broadcasted_iota(I32, (N_EXPERTS, tm), 0)
    work = logits
    idx_rows, val_rows = [], []
    for k in range(TOP_K):
        mx = jnp.max(work, axis=0, keepdims=True)
        sel = jnp.min(jnp.where(work == mx, row, N_EXPERTS), axis=0, keepdims=True)
        idx_rows.append(sel)
        val_rows.append(mx)
        work = jnp.where(row == sel, -jnp.inf, work)
    es = [jnp.exp(v - val_rows[0]) for v in val_rows]
    inv_sum = 1.0 / functools.reduce(lambda a, b: a + b, es)
    pad_i = [jnp.zeros((1, tm), I32)] * (SUBLANES - TOP_K)
    pad_f = [jnp.zeros((1, tm), F32)] * (SUBLANES - TOP_K)
    idx_ref[...] = jnp.concatenate(idx_rows + pad_i, axis=0)
    wgt_ref[...] = jnp.concatenate([e * inv_sum for e in es] + pad_f, axis=0)


def _merge_router(x2d, y_rnn, y_att, proj, w_pr, w_pa, w_o, g_ffn, w_router, b_router):
    T, D = x2d.shape
    tm = min(TM_MERGE, T)
    w_hi = w_router.astype(BF16)
    w_lo = (w_router - w_hi.astype(F32)).astype(BF16)
    wrt = (jnp.zeros((2 * LANES, D), BF16).at[:N_EXPERTS, :].set(w_hi.T)
           .at[LANES:LANES + N_EXPERTS, :].set(w_lo.T))
    brt = b_router.reshape(N_EXPERTS, 1).astype(F32)
    rowblk = lambda c: pl.BlockSpec((tm, D), lambda i, c=c: (i, c))
    full = lambda a: pl.BlockSpec(a.shape, lambda i: (0,) * a.ndim)
    wr, wa, wo = w_pr.astype(BF16), w_pa.astype(BF16), w_o.astype(BF16)
    gf = g_ffn.reshape(1, D)
    return pl.pallas_call(
        _merge_kernel,
        grid=(T // tm,),
        in_specs=[rowblk(0), rowblk(0), rowblk(0), rowblk(COL_G0), rowblk(COL_G1),
                  full(wr), full(wa), full(wo), full(gf), full(wrt), full(brt)],
        out_specs=[rowblk(0), pl.BlockSpec((tm * ROW_TILES, LANES), lambda i: (i, 0)),
                   pl.BlockSpec((SUBLANES, tm), lambda i: (0, i)),
                   pl.BlockSpec((SUBLANES, tm), lambda i: (0, i))],
        out_shape=[jax.ShapeDtypeStruct((T, D), F32), jax.ShapeDtypeStruct((T * ROW_TILES, LANES), F32),
                   jax.ShapeDtypeStruct((SUBLANES, T), I32), jax.ShapeDtypeStruct((SUBLANES, T), F32)],
        compiler_params=_params(("parallel",)),
        name="merge_router",
    )(x2d, y_rnn, y_att, proj, proj, wr, wa, wo, gf, wrt, brt)


def _route_kernel(idx_ref, dest_ref, cnt_ref, cnt_scr, run_scr, start_scr, *, blk):
    ph = pl.program_id(0)
    i = pl.program_id(1)
    tt = idx_ref.shape[1]
    idx = idx_ref[...]
    row = lax.broadcasted_iota(I32, (N_EXPERTS, tt), 0)
    onehot = jnp.zeros((N_EXPERTS, tt), F32)
    for k in range(TOP_K):
        onehot = onehot + (idx[k:k + 1, :] == row).astype(F32)
    tile_cnt = jnp.sum(onehot, axis=1, keepdims=True)

    @pl.when((ph == 0) & (i == 0))
    def _():
        cnt_scr[...] = jnp.zeros_like(cnt_scr)

    @pl.when(ph == 0)
    def _():
        cnt_scr[...] += tile_cnt

    @pl.when((ph == 1) & (i == 0))
    def _():
        padded = jnp.floor((cnt_scr[...] + (blk - 1)) / blk) * blk
        r = lax.broadcasted_iota(I32, (N_EXPERTS, N_EXPERTS), 0)
        c = lax.broadcasted_iota(I32, (N_EXPERTS, N_EXPERTS), 1)
        start_scr[...] = jnp.dot((c < r).astype(F32), padded, preferred_element_type=F32,
                                 precision=lax.Precision.HIGHEST)
        run_scr[...] = jnp.zeros_like(run_scr)

    @pl.when(ph == 1)
    def _():
        r = lax.broadcasted_iota(I32, (tt, tt), 0)
        c = lax.broadcasted_iota(I32, (tt, tt), 1)
        earlier = (r < c).astype(BF16)
        before = jnp.dot(onehot.astype(BF16), earlier, preferred_element_type=F32)
        base = before + run_scr[:, 0:1] + start_scr[:, 0:1]
        rows = [jnp.sum(jnp.where(idx[k:k + 1, :] == row, base, 0.0), axis=0, keepdims=True)
                for k in range(TOP_K)]
        rows += [jnp.zeros((1, tt), F32)] * (SUBLANES - TOP_K)
        dest_ref[...] = jnp.concatenate(rows, axis=0).astype(I32)
        run_scr[...] += tile_cnt
        cnt_ref[...] = cnt_scr[...]


def _route(top_idx, blk):
    T = top_idx.shape[1]
    tt = min(TT_ROUTE, T)
    kern = functools.partial(_route_kernel, blk=blk)
    return pl.pallas_call(
        kern,
        grid=(2, T // tt),
        in_specs=[pl.BlockSpec((SUBLANES, tt), lambda p, i: (0, i))],
        out_specs=[pl.BlockSpec((SUBLANES, tt), lambda p, i: (0, p * i)),
                   pl.BlockSpec((N_EXPERTS, LANES), lambda p, i: (0, 0))],
        out_shape=[jax.ShapeDtypeStruct((SUBLANES, T), I32),
                   jax.ShapeDtypeStruct((N_EXPERTS, LANES), F32)],
        scratch_shapes=[pltpu.VMEM((N_EXPERTS, LANES), F32), pltpu.VMEM((N_EXPERTS, LANES), F32),
                        pltpu.VMEM((N_EXPERTS, LANES), F32)],
        compiler_params=_params(("arbitrary", "arbitrary")),
        name="route",
    )(top_idx)


def _for_each_token_k(n_tok, fn):
    def body(g, c):
        for u in range(TOKEN_UNROLL):
            for k in range(TOP_K):
                fn(g * TOKEN_UNROLL + u, k)
        return c

    lax.fori_loop(0, n_tok // TOKEN_UNROLL, body, 0)


def _tile_major_ranks(dest, tt):
    T = dest.shape[1]
    return dest[:TOP_K].reshape(TOP_K, T // tt, tt).transpose(1, 0, 2).reshape(-1)


def _row_tile(ref, row):
    return ref.at[pl.ds(pl.multiple_of(row * ROW_TILES, ROW_TILES), ROW_TILES), :]


def _dispatch_kernel(ends_ref, dest_ref, hf_ref, xs_ref, zero_buf, sem, zsem, *, tt, blk):
    @pl.when(pl.program_id(0) == 0)
    def _():
        zero_buf[...] = jnp.zeros_like(zero_buf)

        def tail_fill(e):
            prev_end = ends_ref[e - 1] if e > 0 else 0
            start = pl.multiple_of((ends_ref[e] - blk) * ROW_TILES, ROW_TILES)
            return ends_ref[e] > prev_end, pltpu.make_async_copy(
                zero_buf, xs_ref.at[pl.ds(start, blk * ROW_TILES), :], zsem)

        for e in range(N_EXPERTS):
            nonempty, cp = tail_fill(e)
            pl.when(nonempty)(cp.start)
        for e in range(N_EXPERTS):
            nonempty, cp = tail_fill(e)
            pl.when(nonempty)(cp.wait)

    def row_copy(t, k):
        return pltpu.make_async_copy(_row_tile(hf_ref, t),
                                     _row_tile(xs_ref, dest_ref[k * tt + t]), sem)

    _for_each_token_k(tt, lambda t, k: row_copy(t, k).start(priority=k % 2))
    _for_each_token_k(tt, lambda t, k: row_copy(t, k).wait())


def _dispatch(ends, dest, hf_tm, n_slots, blk):
    T = hf_tm.shape[0] // ROW_TILES
    tt = min(TT_DISPATCH, T)
    kern = functools.partial(_dispatch_kernel, tt=tt, blk=blk)
    grid_spec = pltpu.PrefetchScalarGridSpec(
        num_scalar_prefetch=1,
        grid=(T // tt,),
        in_specs=[pl.BlockSpec((TOP_K * tt,), lambda i, ends: (i,), memory_space=pltpu.SMEM),
                  pl.BlockSpec((tt * ROW_TILES, LANES), lambda i, ends: (i, 0))],
        out_specs=pl.BlockSpec(memory_space=pl.ANY),
        scratch_shapes=[pltpu.VMEM((blk * ROW_TILES, LANES), F32),
                        pltpu.SemaphoreType.DMA(()), pltpu.SemaphoreType.DMA(())],
    )
    return pl.pallas_call(
        kern,
        grid_spec=grid_spec,
        out_shape=jax.ShapeDtypeStruct((n_slots * ROW_TILES, LANES), F32),
        compiler_params=_params(("arbitrary",)),
        name="dispatch",
    )(ends, _tile_major_ranks(dest, tt), hf_tm)


def _expert_kernel(be_ref, bsrc_ref, act_ref, first_ref, x_ref, wg_ref, bg_ref, wu_ref, bu_ref, wd_ref, bd_ref,
                   o_ref, wg_scr, wu_scr, wd_scr, *, blk):
    del be_ref, bsrc_ref
    i = pl.program_id(0)

    @pl.when(first_ref[i] == 1)
    def _():
        wg_scr[...] = wg_ref[0].astype(BF16)
        wu_scr[...] = wu_ref[0].astype(BF16)
        wd_scr[...] = wd_ref[0].astype(BF16)

    @pl.when(act_ref[i] == 1)
    def _():
        x = _load_token_major(x_ref, 0, blk).astype(BF16)
        g = jnp.dot(x, wg_scr[...], preferred_element_type=F32) + bg_ref[0]
        u = jnp.dot(x, wu_scr[...], preferred_element_type=F32) + bu_ref[0]
        g = jnp.minimum(g, SWIGLU_LIMIT)
        u = jnp.clip(u, -SWIGLU_LIMIT, SWIGLU_LIMIT)
        act = (u + 1.0) * (g * jax.nn.sigmoid(SWIGLU_ALPHA * g))
        y = jnp.dot(act.astype(BF16), wd_scr[...], preferred_element_type=F32) + bd_ref[0]
        _store_token_major(o_ref, y)


def _experts(x_slots, blk, block_expert, block_src, block_active, block_first, wg, bg, wu, bu, wd, bd):
    D = D_MODEL
    nb = x_slots.shape[0] // (blk * ROW_TILES)
    wspec = pl.BlockSpec((1, D, D_FF), lambda i, be, bs, ac, fi: (be[i], 0, 0))
    bspec = pl.BlockSpec((1, 1, D_FF), lambda i, be, bs, ac, fi: (be[i], 0, 0))
    xspec = pl.BlockSpec((blk * ROW_TILES, LANES), lambda i, be, bs, ac, fi: (bs[i], 0))
    grid_spec = pltpu.PrefetchScalarGridSpec(
        num_scalar_prefetch=4,
        grid=(nb,),
        in_specs=[xspec, wspec, bspec, wspec, bspec, wspec, bspec],
        out_specs=xspec,
        scratch_shapes=[pltpu.VMEM((D, D_FF), BF16), pltpu.VMEM((D, D_FF), BF16),
                        pltpu.VMEM((D_FF, D), BF16)],
    )
    b3 = lambda b: b.reshape(N_EXPERTS, 1, -1)
    return pl.pallas_call(
        functools.partial(_expert_kernel, blk=blk),
        grid_spec=grid_spec,
        out_shape=jax.ShapeDtypeStruct(x_slots.shape, F32),
        compiler_params=_params(("arbitrary",)),
        name="experts",
    )(block_expert, block_src, block_active, block_first, x_slots,
      wg, b3(bg), wu, b3(bu), wd, b3(bd))


def _combine_kernel(dest_ref, dest_next_ref, ys_ref, w_ref, x2_ref, g_ref, o_ref, buf, sems, *, tt):
    i = pl.program_id(0)
    n = pl.num_programs(0)
    cur = i % 2

    def row_copy(idx_ref, slot, t, k):
        return pltpu.make_async_copy(_row_tile(ys_ref, idx_ref[k * tt + t]),
                                     _row_tile(buf, (slot * TOP_K + k) * tt + t), sems.at[slot])

    def gather(idx_ref, slot):
        _for_each_token_k(tt, lambda t, k: row_copy(idx_ref, slot, t, k).start(priority=k % 2))

    @pl.when(i == 0)
    def _():
        gather(dest_ref, 0)

    @pl.when(i + 1 < n)
    def _():
        gather(dest_next_ref, 1 - cur)

    _for_each_token_k(tt, lambda t, k: row_copy(dest_ref, cur, t, k).wait())
    w = w_ref[...]
    y = x2_ref[...]
    for k in range(TOP_K):
        y = y + w[:, k:k + 1] * _load_token_major(buf, (cur * TOP_K + k) * tt, tt)
    inv = lax.rsqrt(jnp.mean(y * y, axis=-1, keepdims=True) + EPS)
    o_ref[...] = y * inv * g_ref[...]


def _combine(dest, y_slots, top_w, x2, g_final):
    T, D = x2.shape
    tt = min(TT_COMBINE, T)
    nt = T // tt
    kern = functools.partial(_combine_kernel, tt=tt)
    return pl.pallas_call(
        kern,
        grid=(nt,),
        in_specs=[pl.BlockSpec((TOP_K * tt,), lambda i: (i,), memory_space=pltpu.SMEM),
                  pl.BlockSpec((TOP_K * tt,), lambda i: (jnp.minimum(i + 1, nt - 1),),
                               memory_space=pltpu.SMEM),
                  pl.BlockSpec(memory_space=pl.ANY),
                  pl.BlockSpec((tt, SUBLANES), lambda i: (i, 0)),
                  pl.BlockSpec((tt, D), lambda i: (i, 0)),
                  pl.BlockSpec((1, D), lambda i: (0, 0))],
        out_specs=pl.BlockSpec((tt, D), lambda i: (i, 0)),
        out_shape=jax.ShapeDtypeStruct((T, D), F32),
        scratch_shapes=[pltpu.VMEM((2 * TOP_K * tt * ROW_TILES, LANES), F32),
                        pltpu.SemaphoreType.DMA((2,))],
        compiler_params=_params(("arbitrary",)),
        name="combine",
    )(_tile_major_ranks(dest, tt), _tile_major_ranks(dest, tt), y_slots, top_w.T, x2,
      g_final.reshape(1, D))


def _moe_block_size(T):
    return min(MOE_BLOCK, max(SUBLANES, T * TOP_K // N_EXPERTS))


def _block_tables(counts, blk, nb):
    cnt = counts.astype(I32)
    padded = (cnt + blk - 1) // blk * blk
    ends = jnp.cumsum(padded)
    used = ends[-1] // blk
    starts = jnp.arange(nb, dtype=I32) * blk
    src = jnp.minimum(jnp.arange(nb, dtype=I32), jnp.maximum(used - 1, 0))
    expert = jnp.minimum(jnp.sum((starts[:, None] >= ends[None, :]).astype(I32), axis=1), N_EXPERTS - 1)
    expert = expert[src]
    active = (jnp.arange(nb, dtype=I32) < used).astype(I32)
    prev = jnp.concatenate([jnp.full((1,), -1, I32), expert[:-1]])
    first = active * (expert != prev).astype(I32)
    return ends.astype(I32), expert, src, active, first


def kernel(x, norm_mix_g, w_in, conv_w, conv_b, w_rg_a, b_rg_a, w_rg_x, b_rg_x, lru_lambda, diff_lambda_q1, diff_lambda_k1, diff_lambda_q2, diff_lambda_k2, subln_g, rel_bias_table, w_proj_rnn, w_proj_att, w_out, norm_ffn_g, w_router, b_router, w_gate_e, b_gate_e, w_up_e, b_up_e, w_down_e, b_down_e, norm_final_g):
    B, S, D = x.shape
    T = B * S
    assert norm_mix_g.shape[0] == 1, "single-layer block: the final norm is fused into the MoE combine"
    l = 0
    xt = x.reshape(T, D)
    lam_init = 0.8 - 0.6 * math.exp(-0.3 * l)
    proj = _inproj(xt, norm_mix_g[l], w_in[l].astype(BF16))
    y_rnn = _rglru(proj, B, S, conv_w[l], conv_b[l], w_rg_a[l], b_rg_a[l], w_rg_x[l], b_rg_x[l],
                   lru_lambda[l])
    y_att = _attention(proj, B, S, diff_lambda_q1[l], diff_lambda_k1[l], diff_lambda_q2[l],
                       diff_lambda_k2[l], subln_g[l], rel_bias_table, lam_init)
    x2, hf, top_idx, top_w = _merge_router(xt, y_rnn, y_att, proj, w_proj_rnn[l], w_proj_att[l],
                                           w_out[l], norm_ffn_g[l], w_router[l], b_router[l])
    blk = _moe_block_size(T)
    nb = T * TOP_K // blk + N_EXPERTS
    dest, counts = _route(top_idx, blk)
    ends, expert, src, active, first = _block_tables(counts[:, 0], blk, nb)
    x_slots = _dispatch(ends, dest, hf, nb * blk, blk)
    y_slots = _experts(x_slots, blk, expert, src, active, first, w_gate_e[l], b_gate_e[l], w_up_e[l],
                       b_up_e[l], w_down_e[l], b_down_e[l])
    out = _combine(dest, y_slots, top_w, x2, norm_final_g)
    return out.reshape(B, S, D)
```

```python
import functools
import math

import jax
import jax.numpy as jnp
from jax import lax
from jax.experimental import pallas as pl
from jax.experimental.pallas import tpu as pltpu

F32 = jnp.float32
BF16 = jnp.bfloat16
I32 = jnp.int32

D_MODEL = 1024
D_RNN = 1024
RNN_BLOCKS = 16
RNN_BLOCK = D_RNN // RNN_BLOCKS
CONV_W = 4
RGLRU_C = 8.0
N_HEADS = 8
HEAD_DIM = 64
V_DIM = 2 * HEAD_DIM
ATT_QK = N_HEADS * 2 * HEAD_DIM
ATT_V = N_HEADS * V_DIM
D_IN = 2 * D_RNN + 2 * ATT_QK + ATT_V + 2 * D_MODEL
NUM_BUCKETS = 32
MAX_EXACT = NUM_BUCKETS // 2
MAX_DISTANCE = 128
N_EXPERTS = 32
TOP_K = 4
D_FF = D_MODEL
SWIGLU_LIMIT = 7.0
SWIGLU_ALPHA = 1.702
EPS = 1e-6
NEG_INF = -1e30

LANES = 128
SUBLANES = 8
MXU_DIM = 256
VMEM_LIMIT = 56 * 1024 * 1024

TM_INPROJ = 1024
TN_INPROJ = 3584
TS_RGLRU = 256
ATT_TQ = 512
TM_MERGE = 512
TT_ROUTE = 1024
TT_DISPATCH = 1024
TT_COMBINE = 512
MOE_BLOCK = 512
TOKEN_UNROLL = 16

COL_XR, COL_GR, COL_Q, COL_K, COL_V, COL_G0, COL_G1 = range(7)


def _params(sem, vmem=VMEM_LIMIT):
    return pltpu.CompilerParams(dimension_semantics=sem, vmem_limit_bytes=vmem)


ROW_TILES = D_MODEL // LANES
assert ROW_TILES == SUBLANES


def _store_token_major(ref, val, start=0):
    n = val.shape[0]
    for c in range(ROW_TILES):
        ref[pl.ds(start * ROW_TILES + c, n, stride=ROW_TILES), :] = val[:, c * LANES:(c + 1) * LANES]


def _load_token_major(ref, start, n):
    return jnp.concatenate(
        [ref[pl.ds(start * ROW_TILES + c, n, stride=ROW_TILES), :] for c in range(ROW_TILES)], axis=-1)


def _inproj_kernel(x_ref, g_ref, w_ref, o_ref, h_scr):
    @pl.when(pl.program_id(1) == 0)
    def _():
        x = x_ref[...]
        inv = lax.rsqrt(jnp.mean(x * x, axis=-1, keepdims=True) + EPS)
        h_scr[...] = (x * inv * g_ref[...]).astype(BF16)

    o_ref[...] = jnp.dot(h_scr[...], w_ref[...], preferred_element_type=F32).astype(o_ref.dtype)


def _inproj(x2d, g, w_bf16):
    T, D = x2d.shape
    N = w_bf16.shape[1]
    tm = min(TM_INPROJ, T)
    tn = TN_INPROJ if N % TN_INPROJ == 0 else D_MODEL
    return pl.pallas_call(
        _inproj_kernel,
        grid=(T // tm, N // tn),
        in_specs=[
            pl.BlockSpec((tm, D), lambda i, j: (i, 0)),
            pl.BlockSpec((1, D), lambda i, j: (0, 0)),
            pl.BlockSpec((D, tn), lambda i, j: (0, j)),
        ],
        out_specs=pl.BlockSpec((tm, tn), lambda i, j: (i, j)),
        out_shape=jax.ShapeDtypeStruct((T, N), BF16),
        scratch_shapes=[pltpu.VMEM((tm, D), BF16)],
        compiler_params=_params(("parallel", "arbitrary")),
        name="inproj",
    )(x2d, g.reshape(1, D), w_bf16)


def _rglru_kernel(xr_ref, gr_ref, cw_ref, cb_ref, wa_ref, ba_ref, wx_ref, bx_ref, lam_ref,
                  o_ref, tail_scr, a_scr, u_scr, h_scr):
    nb, ts, _ = xr_ref.shape
    pad = SUBLANES
    nslab = D_RNN // LANES

    @pl.when(pl.program_id(0) == 0)
    def _():
        tail_scr[...] = jnp.zeros_like(tail_scr)
        h_scr[...] = jnp.zeros_like(h_scr)

    cw = cw_ref[...]
    z = -lam_ref[...]
    softplus = jnp.maximum(z, 0.0) + jnp.log1p(jnp.exp(-jnp.abs(z)))
    sigmoid = lambda v: 0.5 * jnp.tanh(0.5 * v) + 0.5
    nchunk = D_RNN // MXU_DIM
    nshift = CONV_W - 1
    sr = lax.broadcasted_iota(I32, (nshift * ts, ts), 0)
    sc = lax.broadcasted_iota(I32, (nshift * ts, ts), 1)
    shift_mat = jnp.zeros((nshift * ts, ts), F32)
    for d in range(1, CONV_W):
        hit = (sr >= (d - 1) * ts) & (sr < d * ts) & (sr - (d - 1) * ts - d == sc)
        shift_mat = jnp.where(hit, 1.0, shift_mat)
    shift_mat = shift_mat.astype(BF16)
    row8 = lax.broadcasted_iota(I32, (pad, D_RNN), 0)
    for b in range(nb):
        xb = xr_ref[b]
        x = xb.astype(F32)
        shifted = jnp.dot(shift_mat, xb, preferred_element_type=F32)
        xc = cw[CONV_W - 1:CONV_W, :] * x + cb_ref[...]
        tail = tail_scr[b]
        head_fix = jnp.zeros((pad, D_RNN), F32)
        for d in range(1, CONV_W):
            wd = cw[CONV_W - 1 - d:CONV_W - d, :]
            xc = xc + wd * shifted[(d - 1) * ts:d * ts, :]
            head_fix = head_fix + wd * jnp.where(row8 < d, pltpu.roll(tail, d, 0), 0.0)
        xc = jnp.concatenate([xc[0:pad, :] + head_fix, xc[pad:ts, :]], axis=0)
        tail_scr[b] = x[ts - pad:ts, :]
        xcb = xc.astype(BF16)
        r_pre = jnp.concatenate(
            [jnp.dot(xcb[:, c * MXU_DIM:(c + 1) * MXU_DIM], wa_ref[c], preferred_element_type=F32)
             for c in range(nchunk)], axis=-1)
        i_pre = jnp.concatenate(
            [jnp.dot(xcb[:, c * MXU_DIM:(c + 1) * MXU_DIM], wx_ref[c], preferred_element_type=F32)
             for c in range(nchunk)], axis=-1)
        r = sigmoid(r_pre + ba_ref[...])
        ig = sigmoid(i_pre + bx_ref[...])
        a = jnp.exp((-RGLRU_C) * r * softplus)
        m2 = 1.0 - a * a
        mult = jnp.where(m2 > 0.0, m2 * lax.rsqrt(m2), 0.0)
        u = mult * (ig * xc)
        for c in range(nslab):
            a_scr[c, pl.ds(b, ts, stride=nb), :] = a[:, c * LANES:(c + 1) * LANES]
            u_scr[c, pl.ds(b, ts, stride=nb), :] = u[:, c * LANES:(c + 1) * LANES]

    def body(g, hs):
        for j in range(SCAN_UNROLL):
            off = pl.multiple_of((g * SCAN_UNROLL + j) * nb, nb)
            new = []
            for c in range(nslab):
                h = a_scr[c, pl.ds(off, nb), :] * hs[c] + u_scr[c, pl.ds(off, nb), :]
                u_scr[c, pl.ds(off, nb), :] = h
                new.append(h)
            hs = tuple(new)
        return hs

    hs = lax.fori_loop(0, ts // SCAN_UNROLL, body, tuple(h_scr[c] for c in range(nslab)))
    for c in range(nslab):
        h_scr[c] = hs[c]
    for b in range(nb):
        h = jnp.concatenate([u_scr[c, pl.ds(b, ts, stride=nb), :] for c in range(nslab)], axis=-1)
        gate = jax.nn.gelu(gr_ref[b].astype(F32), approximate=True)
        o_ref[b] = (h * gate).astype(o_ref.dtype)


SCAN_UNROLL = 8


def _block_diag_chunks(w):
    per = MXU_DIM // RNN_BLOCK
    w = w.reshape(D_RNN // MXU_DIM, per, RNN_BLOCK, RNN_BLOCK)
    eye = jnp.eye(per, dtype=w.dtype)
    out = jnp.einsum('gpcd,pq->gpcqd', w, eye)
    return out.reshape(D_RNN // MXU_DIM, MXU_DIM, MXU_DIM)


def _rglru(proj, B, S, conv_w, conv_b, w_a, b_a, w_x, b_x, lru_lambda):
    assert B <= SUBLANES, "all batch rows of a time step share one vreg in the scan"
    ts = min(TS_RGLRU, S)
    ns = S // ts
    wa = _block_diag_chunks(w_a).astype(BF16)
    wx = _block_diag_chunks(w_x).astype(BF16)
    nchunk = D_RNN // MXU_DIM
    nslab = D_RNN // LANES
    row = lambda v: v.reshape(1, D_RNN)
    const2 = lambda s: (0, 0)
    proj3 = proj.reshape(B, S, proj.shape[-1])
    out = pl.pallas_call(
        _rglru_kernel,
        grid=(ns,),
        in_specs=[
            pl.BlockSpec((B, ts, D_RNN), lambda s: (0, s, COL_XR)),
            pl.BlockSpec((B, ts, D_RNN), lambda s: (0, s, COL_GR)),
            pl.BlockSpec((CONV_W, D_RNN), const2),
            pl.BlockSpec((1, D_RNN), const2),
            pl.BlockSpec((nchunk, MXU_DIM, MXU_DIM), lambda s: (0, 0, 0)),
            pl.BlockSpec((1, D_RNN), const2),
            pl.BlockSpec((nchunk, MXU_DIM, MXU_DIM), lambda s: (0, 0, 0)),
            pl.BlockSpec((1, D_RNN), const2),
            pl.BlockSpec((1, D_RNN), const2),
        ],
        out_specs=pl.BlockSpec((B, ts, D_RNN), lambda s: (0, s, 0)),
        out_shape=jax.ShapeDtypeStruct((B, S, D_RNN), BF16),
        scratch_shapes=[
            pltpu.VMEM((B, SUBLANES, D_RNN), F32),
            pltpu.VMEM((nslab, ts * B, LANES), F32),
            pltpu.VMEM((nslab, ts * B, LANES), F32),
            pltpu.VMEM((nslab, B, LANES), F32),
        ],
        compiler_params=_params(("arbitrary",)),
        name="rglru",
    )(proj3, proj3, conv_w, row(conv_b), wa, row(b_a), wx, row(b_x), row(lru_lambda))
    return out.reshape(B * S, D_RNN)


def _attn_kernel(q_ref, k_ref, v_ref, bias_ref, lq1_ref, lk1_ref, lq2_ref, lk2_ref, sg_ref,
                 o_ref, qs_scr, vx_scr, s_scr, m_scr, acc_scr, *, tq, lam_init):
    S = q_ref.shape[0]
    nq = S // tq
    scale = HEAD_DIM ** -0.5 * LOG2E
    lane = lax.broadcasted_iota(I32, (tq, V_DIM), 1)
    lam = (jnp.exp(jnp.sum(lq1_ref[...] * lk1_ref[...], keepdims=True))
           - jnp.exp(jnp.sum(lq2_ref[...] * lk2_ref[...], keepdims=True)) + lam_init)
    vx_scr[:, 0:V_DIM] = v_ref[...]
    vx_scr[:, V_DIM:2 * V_DIM] = jnp.ones((S, V_DIM), BF16)

    Z0 = 2

    def prep_q(qi, buf):
        q = (q_ref[pl.ds(pl.multiple_of(qi * tq, tq), tq), :].astype(F32) * scale).astype(BF16)
        zero = jnp.zeros_like(q)
        qs_scr[buf, 0:tq, :] = jnp.where(lane < HEAD_DIM, q, zero)
        qs_scr[buf, tq:2 * tq, :] = jnp.where(lane >= HEAD_DIM, q, zero)

    def scores(buf, j, dst):
        k = k_ref[pl.ds(pl.multiple_of(j * tq, tq), tq), :]
        s_scr[dst] = lax.dot_general(qs_scr[buf], k, (((1,), (1,)), ((), ())),
                                     preferred_element_type=F32)

    def step(j, src, bias_idx, prefetch):
        scores(*prefetch)
        s = s_scr[src]
        if bias_idx is not None:
            b = bias_ref[bias_idx]
            s = s + jnp.concatenate([b, b], axis=0)
        vx = vx_scr[pl.ds(pl.multiple_of(j * tq, tq), tq), :]
        m_prev = m_scr[...]
        m_new = jnp.maximum(m_prev, jnp.max(s, axis=1)[:, None])
        p = jnp.exp2(s - jnp.tile(m_new, (1, tq // LANES)))
        alpha = jnp.exp2(m_prev - m_new)
        acc_scr[...] = (jnp.tile(alpha, (1, 2)) * acc_scr[...]
                        + jnp.dot(p.astype(BF16), vx, preferred_element_type=F32))
        m_scr[...] = m_new

    def begin_block():
        m_scr[...] = jnp.full(m_scr.shape, NEG_INF, F32)
        acc_scr[...] = jnp.zeros(acc_scr.shape, F32)

    def diag_step(qi, par, src):
        nxt = jnp.minimum(qi + 1, nq - 1)
        prep_q(nxt, 1 - par)
        step(qi, src, 0, (1 - par, 0, Z0 + 1 - par))

    def end_block(qi):
        acc = acc_scr[...]
        o_all = acc[:, 0:V_DIM] / acc[:, V_DIM:2 * V_DIM]
        o = o_all[0:tq, :] - lam * o_all[tq:2 * tq, :]
        inv = lax.rsqrt(jnp.mean(o * o, axis=-1, keepdims=True) + EPS)
        y = (o * inv * sg_ref[...]) * (1.0 - lam_init)
        o_ref[pl.ds(pl.multiple_of(qi * tq, tq), tq), :] = y.astype(o_ref.dtype)

    def far_pairs(par, npairs):
        def body(m, c):
            step(2 * m + 1, 0, None, (par, 2 * m + 2, 1))
            step(2 * m + 2, 1, None, (par, 2 * m + 3, 0))
            return c

        lax.fori_loop(0, npairs, body, 0)

    prep_q(0, 0)
    scores(0, 0, Z0)
    begin_block()
    diag_step(0, 0, Z0)
    end_block(0)
    begin_block()
    step(0, Z0 + 1, 1, (1, 1, 0))
    diag_step(1, 1, 0)
    end_block(1)

    def block_pair(i, carry):
        qi = 2 * i
        begin_block()
        step(0, Z0, None, (0, 1, 0))
        far_pairs(0, i - 1)
        step(qi - 1, 0, 1, (0, qi, 1))
        diag_step(qi, 0, 1)
        end_block(qi)
        qi = 2 * i + 1
        begin_block()
        step(0, Z0 + 1, None, (1, 1, 0))
        far_pairs(1, i - 1)
        step(qi - 2, 0, None, (1, qi - 1, 1))
        step(qi - 1, 1, 1, (1, qi, 0))
        diag_step(qi, 1, 0)
        end_block(qi)
        return carry

    lax.fori_loop(1, nq // 2, block_pair, 0)


def _rel_bucket(n):
    n = jnp.maximum(n, 0)
    nf = jnp.maximum(n, MAX_EXACT).astype(F32)
    large = MAX_EXACT + (jnp.log(nf / MAX_EXACT) / math.log(MAX_DISTANCE / MAX_EXACT)
                         * (NUM_BUCKETS - MAX_EXACT)).astype(I32)
    large = jnp.minimum(large, NUM_BUCKETS - 1)
    return jnp.where(n < MAX_EXACT, n, large)


def _bias_tiles(rel_table, tq):
    i = jnp.arange(tq, dtype=I32)[:, None]
    j = jnp.arange(tq, dtype=I32)[None, :]
    table = rel_table.astype(F32) - rel_table[NUM_BUCKETS - 1].astype(F32)[None, :]
    tiles = []
    for delta in (0, tq):
        n = i - j + delta
        onehot = (_rel_bucket(n)[:, :, None] == jnp.arange(NUM_BUCKETS, dtype=I32)).astype(F32)
        b = jnp.einsum('ijb,bh->hij', onehot, table, precision=lax.Precision.HIGHEST) * LOG2E
        tiles.append(jnp.where((n >= 0)[None], b, NEG_INF))
    return jnp.stack(tiles, axis=1)


LOG2E = math.log2(math.e)


def _attention(proj, B, S, lq1, lk1, lq2, lk2, subln_g, rel_table, lam_init):
    T = B * S
    tq = min(ATT_TQ, S // 2)
    assert V_DIM == LANES and MAX_DISTANCE <= tq and S % (2 * tq) == 0
    bias = _bias_tiles(rel_table, tq)
    vec = lambda v: v.reshape(1, -1).astype(F32)
    const2 = lambda b, h: (0, 0)
    kern = functools.partial(_attn_kernel, tq=tq, lam_init=lam_init)
    return pl.pallas_call(
        kern,
        grid=(B, N_HEADS),
        in_specs=[
            pl.BlockSpec((S, V_DIM), lambda b, h: (b, COL_Q * N_HEADS + h)),
            pl.BlockSpec((S, V_DIM), lambda b, h: (b, COL_K * N_HEADS + h)),
            pl.BlockSpec((S, V_DIM), lambda b, h: (b, COL_V * N_HEADS + h)),
            pl.BlockSpec((None, 2, tq, tq), lambda b, h: (h, 0, 0, 0)),
            pl.BlockSpec((1, HEAD_DIM), const2),
            pl.BlockSpec((1, HEAD_DIM), const2),
            pl.BlockSpec((1, HEAD_DIM), const2),
            pl.BlockSpec((1, HEAD_DIM), const2),
            pl.BlockSpec((1, V_DIM), const2),
        ],
        out_specs=pl.BlockSpec((S, V_DIM), lambda b, h: (b, h)),
        out_shape=jax.ShapeDtypeStruct((T, ATT_V), BF16),
        scratch_shapes=[
            pltpu.VMEM((2, 2 * tq, V_DIM), BF16),
            pltpu.VMEM((S, 2 * V_DIM), BF16),
            pltpu.VMEM((4, 2 * tq, tq), F32),
            pltpu.VMEM((2 * tq, LANES), F32),
            pltpu.VMEM((2 * tq, 2 * V_DIM), F32),
        ],
        compiler_params=_params(("parallel", "arbitrary")),
        name="diff_attn",
    )(proj, proj, proj, bias, vec(lq1), vec(lk1), vec(lq2), vec(lk2), vec(subln_g))


def _merge_kernel(x_ref, yr_ref, ya_ref, g0_ref, g1_ref, wr_ref, wa_ref, wo_ref, gf_ref, wrt_ref, brt_ref,
                  x2_ref, hf_ref, idx_ref, wgt_ref):
    pr = jnp.dot(yr_ref[...], wr_ref[...], preferred_element_type=F32)
    pa = jnp.dot(ya_ref[...], wa_ref[...], preferred_element_type=F32)
    merged = (jax.nn.sigmoid(g0_ref[...].astype(F32)) * pr
              + jax.nn.sigmoid(g1_ref[...].astype(F32)) * pa)
    x2 = x_ref[...] + jnp.dot(merged.astype(BF16), wo_ref[...], preferred_element_type=F32)
    x2_ref[...] = x2
    inv = lax.rsqrt(jnp.mean(x2 * x2, axis=-1, keepdims=True) + EPS)
    hf = x2 * inv * gf_ref[...]
    _store_token_major(hf_ref, hf)
    nt = (((1,), (1,)), ((), ()))
    hi = hf.astype(BF16)
    lo = (hf - hi.astype(F32)).astype(BF16)
    hw = lax.dot_general(wrt_ref[...], hi, nt, preferred_element_type=F32)
    lw = lax.dot_general(wrt_ref[0:LANES, :], lo, nt, preferred_element_type=F32)
    logits = (hw[0:N_EXPERTS, :] + hw[LANES:LANES + N_EXPERTS, :] + lw[0:N_EXPERTS, :]
              + brt_ref[...])
    tm = logits.shape[1]
    row = lax.broadcasted_iota(I32, (N_EXPERTS, tm), 0)
    work = logits
    idx_rows, val_rows = [], []
    for k in range(TOP_K):
        mx = jnp.max(work, axis=0, keepdims=True)
        sel = jnp.min(jnp.where(work == mx, row, N_EXPERTS), axis=0, keepdims=True)
        idx_rows.append(sel)
        val_rows.append(mx)
        work = jnp.where(row == sel, -jnp.inf, work)
    es = [jnp.exp(v - val_rows[0]) for v in val_rows]
    inv_sum = 1.0 / functools.reduce(lambda a, b: a + b, es)
    pad_i = [jnp.zeros((1, tm), I32)] * (SUBLANES - TOP_K)
    pad_f = [jnp.zeros((1, tm), F32)] * (SUBLANES - TOP_K)
    idx_ref[...] = jnp.concatenate(idx_rows + pad_i, axis=0)
    wgt_ref[...] = jnp.concatenate([e * inv_sum for e in es] + pad_f, axis=0)


def _merge_router(x2d, y_rnn, y_att, proj, w_pr, w_pa, w_o, g_ffn, w_router, b_router):
    T, D = x2d.shape
    tm = min(TM_MERGE, T)
    w_hi = w_router.astype(BF16)
    w_lo = (w_router - w_hi.astype(F32)).astype(BF16)
    wrt = (jnp.zeros((2 * LANES, D), BF16).at[:N_EXPERTS, :].set(w_hi.T)
           .at[LANES:LANES + N_EXPERTS, :].set(w_lo.T))
    brt = b_router.reshape(N_EXPERTS, 1).astype(F32)
    rowblk = lambda c: pl.BlockSpec((tm, D), lambda i, c=c: (i, c))
    full = lambda a: pl.BlockSpec(a.shape, lambda i: (0,) * a.ndim)
    wr, wa, wo = w_pr.astype(BF16), w_pa.astype(BF16), w_o.astype(BF16)
    gf = g_ffn.reshape(1, D)
    return pl.pallas_call(
        _merge_kernel,
        grid=(T // tm,),
        in_specs=[rowblk(0), rowblk(0), rowblk(0), rowblk(COL_G0), rowblk(COL_G1),
                  full(wr), full(wa), full(wo), full(gf), full(wrt), full(brt)],
        out_specs=[rowblk(0), pl.BlockSpec((tm * ROW_TILES, LANES), lambda i: (i, 0)),
                   pl.BlockSpec((SUBLANES, tm), lambda i: (0, i)),
                   pl.BlockSpec((SUBLANES, tm), lambda i: (0, i))],
        out_shape=[jax.ShapeDtypeStruct((T, D), F32), jax.ShapeDtypeStruct((T * ROW_TILES, LANES), F32),
                   jax.ShapeDtypeStruct((SUBLANES, T), I32), jax.ShapeDtypeStruct((SUBLANES, T), F32)],
        compiler_params=_params(("parallel",)),
        name="merge_router",
    )(x2d, y_rnn, y_att, proj, proj, wr, wa, wo, gf, wrt, brt)


def _route_kernel(idx_ref, dest_ref, cnt_ref, cnt_scr, run_scr, start_scr, *, blk):
    ph = pl.program_id(0)
    i = pl.program_id(1)
    tt = idx_ref.shape[1]
    idx = idx_ref[...]
    row = lax.broadcasted_iota(I32, (N_EXPERTS, tt), 0)
    onehot = jnp.zeros((N_EXPERTS, tt), F32)
    for k in range(TOP_K):
        onehot = onehot + (idx[k:k + 1, :] == row).astype(F32)
    tile_cnt = jnp.sum(onehot, axis=1, keepdims=True)

    @pl.when((ph == 0) & (i == 0))
    def _():
        cnt_scr[...] = jnp.zeros_like(cnt_scr)

    @pl.when(ph == 0)
    def _():
        cnt_scr[...] += tile_cnt

    @pl.when((ph == 1) & (i == 0))
    def _():
        padded = jnp.floor((cnt_scr[...] + (blk - 1)) / blk) * blk
        r = lax.broadcasted_iota(I32, (N_EXPERTS, N_EXPERTS), 0)
        c = lax.broadcasted_iota(I32, (N_EXPERTS, N_EXPERTS), 1)
        start_scr[...] = jnp.dot((c < r).astype(F32), padded, preferred_element_type=F32,
                                 precision=lax.Precision.HIGHEST)
        run_scr[...] = jnp.zeros_like(run_scr)

    @pl.when(ph == 1)
    def _():
        r = lax.broadcasted_iota(I32, (tt, tt), 0)
        c = lax.broadcasted_iota(I32, (tt, tt), 1)
        earlier = (r < c).astype(BF16)
        before = jnp.dot(onehot.astype(BF16), earlier, preferred_element_type=F32)
        base = before + run_scr[:, 0:1] + start_scr[:, 0:1]
        rows = [jnp.sum(jnp.where(idx[k:k + 1, :] == row, base, 0.0), axis=0, keepdims=True)
                for k in range(TOP_K)]
        rows += [jnp.zeros((1, tt), F32)] * (SUBLANES - TOP_K)
        dest_ref[...] = jnp.concatenate(rows, axis=0).astype(I32)
        run_scr[...] += tile_cnt
        cnt_ref[...] = cnt_scr[...]


def _route(top_idx, blk):
    T = top_idx.shape[1]
    tt = min(TT_ROUTE, T)
    kern = functools.partial(_route_kernel, blk=blk)
    return pl.pallas_call(
        kern,
        grid=(2, T // tt),
        in_specs=[pl.BlockSpec((SUBLANES, tt), lambda p, i: (0, i))],
        out_specs=[pl.BlockSpec((SUBLANES, tt), lambda p, i: (0, p * i)),
                   pl.BlockSpec((N_EXPERTS, LANES), lambda p, i: (0, 0))],
        out_shape=[jax.ShapeDtypeStruct((SUBLANES, T), I32),
                   jax.ShapeDtypeStruct((N_EXPERTS, LANES), F32)],
        scratch_shapes=[pltpu.VMEM((N_EXPERTS, LANES), F32), pltpu.VMEM((N_EXPERTS, LANES), F32),
                        pltpu.VMEM((N_EXPERTS, LANES), F32)],
        compiler_params=_params(("arbitrary", "arbitrary")),
        name="route",
    )(top_idx)


def _for_each_token_k(n_tok, fn):
    def body(g, c):
        for u in range(TOKEN_UNROLL):
            for k in range(TOP_K):
                fn(g * TOKEN_UNROLL + u, k)
        return c

    lax.fori_loop(0, n_tok // TOKEN_UNROLL, body, 0)


def _tile_major_ranks(dest, tt):
    T = dest.shape[1]
    return dest[:TOP_K].reshape(TOP_K, T // tt, tt).transpose(1, 0, 2).reshape(-1)


def _row_tile(ref, row):
    return ref.at[pl.ds(pl.multiple_of(row * ROW_TILES, ROW_TILES), ROW_TILES), :]


def _dispatch_kernel(ends_ref, dest_ref, hf_ref, xs_ref, zero_buf, sem, zsem, *, tt, blk):
    @pl.when(pl.program_id(0) == 0)
    def _():
        zero_buf[...] = jnp.zeros_like(zero_buf)

        def tail_fill(e):
            prev_end = ends_ref[e - 1] if e > 0 else 0
            start = pl.multiple_of((ends_ref[e] - blk) * ROW_TILES, ROW_TILES)
            return ends_ref[e] > prev_end, pltpu.make_async_copy(
                zero_buf, xs_ref.at[pl.ds(start, blk * ROW_TILES), :], zsem)

        for e in range(N_EXPERTS):
            nonempty, cp = tail_fill(e)
            pl.when(nonempty)(cp.start)
        for e in range(N_EXPERTS):
            nonempty, cp = tail_fill(e)
            pl.when(nonempty)(cp.wait)

    def row_copy(t, k):
        return pltpu.make_async_copy(_row_tile(hf_ref, t),
                                     _row_tile(xs_ref, dest_ref[k * tt + t]), sem)

    _for_each_token_k(tt, lambda t, k: row_copy(t, k).start(priority=k % 2))
    _for_each_token_k(tt, lambda t, k: row_copy(t, k).wait())


def _dispatch(ends, dest, hf_tm, n_slots, blk):
    T = hf_tm.shape[0] // ROW_TILES
    tt = min(TT_DISPATCH, T)
    kern = functools.partial(_dispatch_kernel, tt=tt, blk=blk)
    grid_spec = pltpu.PrefetchScalarGridSpec(
        num_scalar_prefetch=1,
        grid=(T // tt,),
        in_specs=[pl.BlockSpec((TOP_K * tt,), lambda i, ends: (i,), memory_space=pltpu.SMEM),
                  pl.BlockSpec((tt * ROW_TILES, LANES), lambda i, ends: (i, 0))],
        out_specs=pl.BlockSpec(memory_space=pl.ANY),
        scratch_shapes=[pltpu.VMEM((blk * ROW_TILES, LANES), F32),
                        pltpu.SemaphoreType.DMA(()), pltpu.SemaphoreType.DMA(())],
    )
    return pl.pallas_call(
        kern,
        grid_spec=grid_spec,
        out_shape=jax.ShapeDtypeStruct((n_slots * ROW_TILES, LANES), F32),
        compiler_params=_params(("arbitrary",)),
        name="dispatch",
    )(ends, _tile_major_ranks(dest, tt), hf_tm)


def _expert_kernel(be_ref, bsrc_ref, act_ref, first_ref, x_ref, wg_ref, bg_ref, wu_ref, bu_ref, wd_ref, bd_ref,
                   o_ref, wg_scr, wu_scr, wd_scr, *, blk):
    del be_ref, bsrc_ref
    i = pl.program_id(0)

    @pl.when(first_ref[i] == 1)
    def _():
        wg_scr[...] = wg_ref[0].astype(BF16)
        wu_scr[...] = wu_ref[0].astype(BF16)
        wd_scr[...] = wd_ref[0].astype(BF16)

    @pl.when(act_ref[i] == 1)
    def _():
        x = _load_token_major(x_ref, 0, blk).astype(BF16)
        g = jnp.dot(x, wg_scr[...], preferred_element_type=F32) + bg_ref[0]
        u = jnp.dot(x, wu_scr[...], preferred_element_type=F32) + bu_ref[0]
        g = jnp.minimum(g, SWIGLU_LIMIT)
        u = jnp.clip(u, -SWIGLU_LIMIT, SWIGLU_LIMIT)
        act = (u + 1.0) * (g * jax.nn.sigmoid(SWIGLU_ALPHA * g))
        y = jnp.dot(act.astype(BF16), wd_scr[...], preferred_element_type=F32) + bd_ref[0]
        _store_token_major(o_ref, y)


def _experts(x_slots, blk, block_expert, block_src, block_active, block_first, wg, bg, wu, bu, wd, bd):
    D = D_MODEL
    nb = x_slots.shape[0] // (blk * ROW_TILES)
    wspec = pl.BlockSpec((1, D, D_FF), lambda i, be, bs, ac, fi: (be[i], 0, 0))
    bspec = pl.BlockSpec((1, 1, D_FF), lambda i, be, bs, ac, fi: (be[i], 0, 0))
    xspec = pl.BlockSpec((blk * ROW_TILES, LANES), lambda i, be, bs, ac, fi: (bs[i], 0))
    grid_spec = pltpu.PrefetchScalarGridSpec(
        num_scalar_prefetch=4,
        grid=(nb,),
        in_specs=[xspec, wspec, bspec, wspec, bspec, wspec, bspec],
        out_specs=xspec,
        scratch_shapes=[pltpu.VMEM((D, D_FF), BF16), pltpu.VMEM((D, D_FF), BF16),
                        pltpu.VMEM((D_FF, D), BF16)],
    )
    b3 = lambda b: b.reshape(N_EXPERTS, 1, -1)
    return pl.pallas_call(
        functools.partial(_expert_kernel, blk=blk),
        grid_spec=grid_spec,
        out_shape=jax.ShapeDtypeStruct(x_slots.shape, F32),
        compiler_params=_params(("arbitrary",)),
        name="experts",
    )(block_expert, block_src, block_active, block_first, x_slots,
      wg, b3(bg), wu, b3(bu), wd, b3(bd))


def _combine_kernel(dest_ref, dest_next_ref, ys_ref, w_ref, x2_ref, g_ref, o_ref, buf, sems, *, tt):
    i = pl.program_id(0)
    n = pl.num_programs(0)
    cur = i % 2

    def row_copy(idx_ref, slot, t, k):
        return pltpu.make_async_copy(_row_tile(ys_ref, idx_ref[k * tt + t]),
                                     _row_tile(buf, (slot * TOP_K + k) * tt + t), sems.at[slot])

    def gather(idx_ref, slot):
        _for_each_token_k(tt, lambda t, k: row_copy(idx_ref, slot, t, k).start(priority=k % 2))

    @pl.when(i == 0)
    def _():
        gather(dest_ref, 0)

    _for_each_token_k(tt, lambda t, k: row_copy(dest_ref, cur, t, k).wait())
    for t in range(tt):
        for k in range(TOP_K):
            row_copy(dest_next_ref, 1 - cur, t, k).start(priority=k % 2)
    w = w_ref[...]
    y = x2_ref[...]
    for k in range(TOP_K):
        y = y + w[:, k:k + 1] * _load_token_major(buf, (cur * TOP_K + k) * tt, tt)
    inv = lax.rsqrt(jnp.mean(y * y, axis=-1, keepdims=True) + EPS)
    o_ref[...] = y * inv * g_ref[...]

    @pl.when(i == n - 1)
    def _():
        _for_each_token_k(tt, lambda t, k: row_copy(dest_next_ref, 1 - cur, t, k).wait())


def _combine(dest, y_slots, top_w, x2, g_final):
    T, D = x2.shape
    tt = min(TT_COMBINE, T)
    nt = T // tt
    kern = functools.partial(_combine_kernel, tt=tt)
    return pl.pallas_call(
        kern,
        grid=(nt,),
        in_specs=[pl.BlockSpec((TOP_K * tt,), lambda i: (i,), memory_space=pltpu.SMEM),
                  pl.BlockSpec((TOP_K * tt,), lambda i: (jnp.minimum(i + 1, nt - 1),),
                               memory_space=pltpu.SMEM),
                  pl.BlockSpec(memory_space=pl.ANY),
                  pl.BlockSpec((tt, SUBLANES), lambda i: (i, 0)),
                  pl.BlockSpec((tt, D), lambda i: (i, 0)),
                  pl.BlockSpec((1, D), lambda i: (0, 0))],
        out_specs=pl.BlockSpec((tt, D), lambda i: (i, 0)),
        out_shape=jax.ShapeDtypeStruct((T, D), F32),
        scratch_shapes=[pltpu.VMEM((2 * TOP_K * tt * ROW_TILES, LANES), F32),
                        pltpu.SemaphoreType.DMA((2,))],
        compiler_params=_params(("arbitrary",)),
        name="combine",
    )(_tile_major_ranks(dest, tt), _tile_major_ranks(dest, tt), y_slots, top_w.T, x2,
      g_final.reshape(1, D))


def _moe_block_size(T):
    return min(MOE_BLOCK, max(SUBLANES, T * TOP_K // N_EXPERTS))


def _block_tables(counts, blk, nb):
    cnt = counts.astype(I32)
    padded = (cnt + blk - 1) // blk * blk
    ends = jnp.cumsum(padded)
    used = ends[-1] // blk
    starts = jnp.arange(nb, dtype=I32) * blk
    src = jnp.minimum(jnp.arange(nb, dtype=I32), jnp.maximum(used - 1, 0))
    expert = jnp.minimum(jnp.sum((starts[:, None] >= ends[None, :]).astype(I32), axis=1), N_EXPERTS - 1)
    expert = expert[src]
    active = (jnp.arange(nb, dtype=I32) < used).astype(I32)
    prev = jnp.concatenate([jnp.full((1,), -1, I32), expert[:-1]])
    first = active * (expert != prev).astype(I32)
    return ends.astype(I32), expert, src, active, first


def kernel(x, norm_mix_g, w_in, conv_w, conv_b, w_rg_a, b_rg_a, w_rg_x, b_rg_x, lru_lambda, diff_lambda_q1, diff_lambda_k1, diff_lambda_q2, diff_lambda_k2, subln_g, rel_bias_table, w_proj_rnn, w_proj_att, w_out, norm_ffn_g, w_router, b_router, w_gate_e, b_gate_e, w_up_e, b_up_e, w_down_e, b_down_e, norm_final_g):
    B, S, D = x.shape
    T = B * S
    assert norm_mix_g.shape[0] == 1, "single-layer block: the final norm is fused into the MoE combine"
    l = 0
    xt = x.reshape(T, D)
    lam_init = 0.8 - 0.6 * math.exp(-0.3 * l)
    proj = _inproj(xt, norm_mix_g[l], w_in[l].astype(BF16))
    y_rnn = _rglru(proj, B, S, conv_w[l], conv_b[l], w_rg_a[l], b_rg_a[l], w_rg_x[l], b_rg_x[l],
                   lru_lambda[l])
    y_att = _attention(proj, B, S, diff_lambda_q1[l], diff_lambda_k1[l], diff_lambda_q2[l],
                       diff_lambda_k2[l], subln_g[l], rel_bias_table, lam_init)
    x2, hf, top_idx, top_w = _merge_router(xt, y_rnn, y_att, proj, w_proj_rnn[l], w_proj_att[l],
                                           w_out[l], norm_ffn_g[l], w_router[l], b_router[l])
    blk = _moe_block_size(T)
    nb = T * TOP_K // blk + N_EXPERTS
    dest, counts = _route(top_idx, blk)
    ends, expert, src, active, first = _block_tables(counts[:, 0], blk, nb)
    x_slots = _dispatch(ends, dest, hf, nb * blk, blk)
    y_slots = _experts(x_slots, blk, expert, src, active, first, w_gate_e[l], b_gate_e[l], w_up_e[l],
                       b_up_e[l], w_down_e[l], b_down_e[l])
    out = _combine(dest, y_slots, top_w, x2, norm_final_g)
    return out.reshape(B, S, D)
```
